```python
import math
import jax, jax.numpy as jnp
from jax import lax
import numpy as np

D_MODEL = 2048
BATCH = 8
SEQ = 2048
DEPTH = 4

HEAD_DIM_A = 128
N_HEADS_A = D_MODEL // (2 * HEAD_DIM_A)
WIDTH_A = N_HEADS_A * HEAD_DIM_A
CONV_K = 5
CHUNK = 64
WIDTH_B = D_MODEL // 2
S5_GROUP_CH = 16
N_GROUPS_B = WIDTH_B // S5_GROUP_CH
S5_STATE = 64
DT_MIN = 0.001
DT_MAX = 0.1
RMS_EPS = 1e-6

COL_QKV = 3 * WIDTH_A
COL_ZA = WIDTH_A
COL_BETA = 2 * N_HEADS_A
COL_ALPHA = 2 * N_HEADS_A
COL_U = WIDTH_B
COL_ZB = WIDTH_B
COL_GATES = 2 * D_MODEL
PROJ_WIDTH = COL_QKV + COL_ZA + COL_BETA + COL_ALPHA + COL_U + COL_ZB + COL_GATES
SPLIT_POINTS = list(np.cumsum([COL_QKV, COL_ZA, COL_BETA, COL_ALPHA, COL_U, COL_ZB]).tolist())

kernel_name = "hybrid_gdn_s5_bidir_encoder"


def rmsnorm(x, g):
    xf = x.astype(jnp.float32)
    y = xf * lax.rsqrt(jnp.mean(xf * xf, axis=-1, keepdims=True) + RMS_EPS)
    return (y * g.astype(jnp.float32)).astype(x.dtype)


def l2norm(x):
    xf = x.astype(jnp.float32)
    return xf * lax.rsqrt(jnp.sum(xf * xf, axis=-1, keepdims=True) + RMS_EPS)


def centred_dwconv(x, w):
    pad = (CONV_K - 1) // 2
    L = x.shape[1]
    xp = jnp.pad(x, ((0, 0), (pad, pad), (0, 0)))
    return sum(xp[:, i:i + L] * w[i] for i in range(CONV_K))


def _to_chunks(t):
    b, l, h = t.shape[:3]
    t = t.reshape((b, l // CHUNK, CHUNK, h) + t.shape[3:])
    return jnp.moveaxis(jnp.moveaxis(t, 1, 0), 3, 2)


def gated_delta_chunked(q, k, v, g, beta):
    b_, l_, h_, dv = v.shape
    qc, kc, vc = _to_chunks(q), _to_chunks(k), _to_chunks(v)
    gc = jnp.cumsum(_to_chunks(g), axis=-1)
    bc = _to_chunks(beta)
    idx = jnp.arange(CHUNK)
    incl = idx[:, None] >= idx[None, :]
    strict = idx[:, None] > idx[None, :]
    decay = jnp.exp(jnp.where(incl, gc[..., :, None] - gc[..., None, :], -jnp.inf))
    kb = kc * bc[..., None]
    vb = vc * bc[..., None]
    lmat = jnp.where(strict, jnp.einsum('nbhck,nbhsk->nbhcs', kb, kc) * decay, 0.0)
    u = lax.linalg.triangular_solve(lmat, vb, left_side=True, lower=True, unit_diagonal=True)
    w = lax.linalg.triangular_solve(lmat, kb * jnp.exp(gc)[..., None], left_side=True, lower=True, unit_diagonal=True)
    qk = jnp.einsum('nbhck,nbhsk->nbhcs', qc, kc) * decay

    def step(S, xs):
        q_c, k_c, u_c, w_c, g_c, qk_c = xs
        v_new = u_c - jnp.einsum('bhck,bhkv->bhcv', w_c, S)
        o_c = (jnp.einsum('bhck,bhkv->bhcv', q_c * jnp.exp(g_c)[..., None], S)
               + jnp.einsum('bhcs,bhsv->bhcv', qk_c, v_new))
        g_last = g_c[..., -1:]
        S = (S * jnp.exp(g_last)[..., None]
             + jnp.einsum('bhck,bhcv->bhkv', k_c * jnp.exp(g_last - g_c)[..., None], v_new))
        return S, o_c

    s0 = jnp.zeros((b_, h_, q.shape[-1], dv), jnp.float32)
    _, o = lax.scan(step, s0, (qc, kc, u, w, gc, qk))
    o = jnp.moveaxis(jnp.moveaxis(o, 2, 3), 0, 1)
    return o.reshape(b_, l_, h_, dv)


def bidir_gated_delta(q, k, v, g, beta):
    flip = lambda t: jnp.flip(t, axis=1)
    fwd = gated_delta_chunked(q, k, v, g[:, :, 0], beta[:, :, 0])
    bwd = flip(gated_delta_chunked(flip(q), flip(k), flip(v), flip(g[:, :, 1]), flip(beta[:, :, 1])))
    return fwd + bwd


def _ssm_combine(left, right):
    a_i, b_i = left
    a_j, b_j = right
    return a_j * a_i, a_j * b_i + b_j


def s5_bidirectional(u, lam_re, lam_im, log_dt, b_re, b_im, c_re, c_im, d_skip):
    f32 = jnp.float32
    bsz, L, _ = u.shape
    ug = u.astype(f32).reshape(bsz, L, N_GROUPS_B, S5_GROUP_CH)
    ugc = ug.astype(jnp.complex64)
    lam = lax.complex(lam_re.astype(f32), lam_im.astype(f32))
    dt = jnp.exp(log_dt.astype(f32))[..., None]
    lam_bar = jnp.exp(lam * dt)
    b_bar = ((lam_bar - 1.0) / lam)[..., None] * lax.complex(b_re.astype(f32), b_im.astype(f32))
    c = lax.complex(c_re.astype(f32), c_im.astype(f32))

    def one_direction(d, reverse):
        bu = jnp.einsum('gpc,blgc->blgp', b_bar[d], ugc)
        a = jnp.broadcast_to(lam_bar[d], bu.shape)
        _, states = lax.associative_scan(_ssm_combine, (a, bu), axis=1, reverse=reverse)
        return jnp.einsum('gcp,blgp->blgc', c[d], states).real

    y = (one_direction(0, False) + one_direction(1, True)
         + ug * d_skip.astype(f32).reshape(N_GROUPS_B, S5_GROUP_CH))
    return y.reshape(bsz, L, WIDTH_B).astype(u.dtype)


def hybrid_layer(x, ln_g, w_in, conv_w, a_log, dt_bias, head_norm_g, lam_re, lam_im, log_dt,
                 b_re, b_im, c_re, c_im, d_skip, w_glu, b_glu, w_pa, w_pb, b_gate, w_out):
    bsz, L, _ = x.shape
    h = rmsnorm(x, ln_g)
    proj = h @ w_in
    qkv, z_a, beta_logit, alpha_logit, u, z_b, gate_logit = jnp.split(proj, SPLIT_POINTS, axis=-1)

    qkv = jax.nn.silu(centred_dwconv(qkv, conv_w))
    q, k, v = jnp.split(qkv, 3, axis=-1)
    q = l2norm(q.reshape(bsz, L, N_HEADS_A, HEAD_DIM_A)) * (HEAD_DIM_A ** -0.5)
    k = l2norm(k.reshape(bsz, L, N_HEADS_A, HEAD_DIM_A))
    v = v.reshape(bsz, L, N_HEADS_A, HEAD_DIM_A).astype(jnp.float32)
    beta = jax.nn.sigmoid(beta_logit.astype(jnp.float32).reshape(bsz, L, 2, N_HEADS_A))
    g = -jnp.exp(a_log.astype(jnp.float32)) * jax.nn.softplus(
        alpha_logit.astype(jnp.float32).reshape(bsz, L, 2, N_HEADS_A) + dt_bias.astype(jnp.float32))
    o_a = bidir_gated_delta(q, k, v, g, beta)
    o_a = rmsnorm(o_a, head_norm_g).reshape(bsz, L, WIDTH_A).astype(x.dtype)
    y_a = (o_a * jax.nn.silu(z_a)) @ w_pa

    y_s = jax.nn.gelu(s5_bidirectional(u, lam_re, lam_im, log_dt, b_re, b_im, c_re, c_im, d_skip))
    y_s = y_s * jax.nn.sigmoid(y_s @ w_glu + b_glu)
    y_b = (y_s * jax.nn.silu(z_b)) @ w_pb

    gate_a, gate_b = jnp.split(jax.nn.sigmoid(gate_logit + b_gate), 2, axis=-1)
    merged = gate_a * y_a + gate_b * y_b
    return x + merged @ w_out


def _fwd_setup_inputs(seed: int = 0) -> dict:
    key = jax.random.key(seed)
    ks = jax.random.split(key, 24)
    f32 = jnp.float32
    nrm = lambda k, shape, scale: scale * jax.random.normal(k, shape, f32)
    x = jax.random.normal(ks[0], (BATCH, SEQ, D_MODEL), f32)
    ln_g = 1.0 + nrm(ks[1], (DEPTH, D_MODEL), 0.02)
    w_in = nrm(ks[2], (DEPTH, D_MODEL, PROJ_WIDTH), D_MODEL ** -0.5)
    conv_w = nrm(ks[3], (DEPTH, CONV_K, 3 * WIDTH_A), CONV_K ** -0.5)
    a_log = jnp.log(jax.random.uniform(ks[4], (DEPTH, 2, N_HEADS_A), f32, 1.0, 16.0))
    dt = jnp.exp(jax.random.uniform(ks[5], (DEPTH, 2, N_HEADS_A), f32, math.log(DT_MIN), math.log(DT_MAX)))
    dt_bias = dt + jnp.log(-jnp.expm1(-dt))
    head_norm_g = 1.0 + nrm(ks[6], (DEPTH, HEAD_DIM_A), 0.02)
    ssm_shape = (DEPTH, 2, N_GROUPS_B, S5_STATE)
    lam_re = -0.5 + nrm(ks[7], ssm_shape, 0.01)
    lam_im = jnp.pi * jnp.arange(S5_STATE, dtype=f32) + nrm(ks[8], ssm_shape, 0.01)
    log_dt = jax.random.uniform(ks[9], (DEPTH, 2, N_GROUPS_B), f32, math.log(DT_MIN), math.log(DT_MAX))
    b_shape = (DEPTH, 2, N_GROUPS_B, S5_STATE, S5_GROUP_CH)
    b_re = nrm(ks[10], b_shape, (2 * S5_GROUP_CH) ** -0.5)
    b_im = nrm(ks[11], b_shape, (2 * S5_GROUP_CH) ** -0.5)
    c_shape = (DEPTH, 2, N_GROUPS_B, S5_GROUP_CH, S5_STATE)
    c_re = nrm(ks[12], c_shape, (2 * S5_STATE) ** -0.5)
    c_im = nrm(ks[13], c_shape, (2 * S5_STATE) ** -0.5)
    d_skip = nrm(ks[14], (DEPTH, WIDTH_B), 1.0)
    w_glu = nrm(ks[15], (DEPTH, WIDTH_B, WIDTH_B), WIDTH_B ** -0.5)
    b_glu = nrm(ks[16], (DEPTH, WIDTH_B), 0.01)
    w_pa = nrm(ks[17], (DEPTH, WIDTH_A, D_MODEL), WIDTH_A ** -0.5)
    w_pb = nrm(ks[18], (DEPTH, WIDTH_B, D_MODEL), WIDTH_B ** -0.5)
    b_gate = nrm(ks[19], (DEPTH, 2 * D_MODEL), 0.01)
    w_out = nrm(ks[20], (DEPTH, D_MODEL, D_MODEL), D_MODEL ** -0.5)
    final_g = 1.0 + nrm(ks[21], (D_MODEL,), 0.02)
    return {"x": x, "ln_g": ln_g, "w_in": w_in, "conv_w": conv_w, "a_log": a_log,
            "dt_bias": dt_bias, "head_norm_g": head_norm_g, "lam_re": lam_re, "lam_im": lam_im,
            "log_dt": log_dt, "b_re": b_re, "b_im": b_im, "c_re": c_re, "c_im": c_im,
            "d_skip": d_skip, "w_glu": w_glu, "b_glu": b_glu, "w_pa": w_pa, "w_pb": w_pb,
            "b_gate": b_gate, "w_out": w_out, "final_g": final_g}


def _fwd_reference(x, ln_g, w_in, conv_w, a_log, dt_bias, head_norm_g, lam_re, lam_im, log_dt,
              b_re, b_im, c_re, c_im, d_skip, w_glu, b_glu, w_pa, w_pb, b_gate, w_out, final_g):
    for l in range(DEPTH):
        x = hybrid_layer(x, ln_g[l], w_in[l], conv_w[l], a_log[l], dt_bias[l], head_norm_g[l],
                         lam_re[l], lam_im[l], log_dt[l], b_re[l], b_im[l], c_re[l], c_im[l],
                         d_skip[l], w_glu[l], b_glu[l], w_pa[l], w_pb[l], b_gate[l], w_out[l])
    return rmsnorm(x, final_g)


import jax as _jax
import jax.numpy as _jnp

TWIN_FORMAT = 'train_step'
FWD_PARAMS = ['x', 'ln_g', 'w_in', 'conv_w', 'a_log', 'dt_bias', 'head_norm_g', 'lam_re', 'lam_im', 'log_dt', 'b_re', 'b_im', 'c_re', 'c_im', 'd_skip', 'w_glu', 'b_glu', 'w_pa', 'w_pb', 'b_gate', 'w_out', 'final_g']
TWIN_WEIGHTS = ['ln_g', 'w_in', 'conv_w', 'a_log', 'dt_bias', 'head_norm_g', 'lam_re', 'lam_im', 'log_dt', 'b_re', 'b_im', 'c_re', 'c_im', 'd_skip', 'w_glu', 'b_glu', 'w_pa', 'w_pb', 'b_gate', 'w_out', 'final_g']
TWIN_DIFF_INPUT = 'x'
TWIN_INPUTS = ['x', 'ln_g', 'w_in', 'conv_w', 'a_log', 'dt_bias', 'head_norm_g', 'lam_re', 'lam_im', 'log_dt', 'b_re', 'b_im', 'c_re', 'c_im', 'd_skip', 'w_glu', 'b_glu', 'w_pa', 'w_pb', 'b_gate', 'w_out', 'final_g', 'loss_target', 'm_ln_g', 'm_w_in', 'm_conv_w', 'm_a_log', 'm_dt_bias', 'm_head_norm_g', 'm_lam_re', 'm_lam_im', 'm_log_dt', 'm_b_re', 'm_b_im', 'm_c_re', 'm_c_im', 'm_d_skip', 'm_w_glu', 'm_b_glu', 'm_w_pa', 'm_w_pb', 'm_b_gate', 'm_w_out', 'm_final_g', 'v_ln_g', 'v_w_in', 'v_conv_w', 'v_a_log', 'v_dt_bias', 'v_head_norm_g', 'v_lam_re', 'v_lam_im', 'v_log_dt', 'v_b_re', 'v_b_im', 'v_c_re', 'v_c_im', 'v_d_skip', 'v_w_glu', 'v_b_glu', 'v_w_pa', 'v_w_pb', 'v_b_gate', 'v_w_out', 'v_final_g']
TWIN_OUTPUTS = ['loss', 'grad_x', 'grad_ln_g', 'grad_w_in', 'grad_conv_w', 'grad_a_log', 'grad_dt_bias', 'grad_head_norm_g', 'grad_lam_re', 'grad_lam_im', 'grad_log_dt', 'grad_b_re', 'grad_b_im', 'grad_c_re', 'grad_c_im', 'grad_d_skip', 'grad_w_glu', 'grad_b_glu', 'grad_w_pa', 'grad_w_pb', 'grad_b_gate', 'grad_w_out', 'grad_final_g', 'delta_ln_g', 'delta_w_in', 'delta_conv_w', 'delta_a_log', 'delta_dt_bias', 'delta_head_norm_g', 'delta_lam_re', 'delta_lam_im', 'delta_log_dt', 'delta_b_re', 'delta_b_im', 'delta_c_re', 'delta_c_im', 'delta_d_skip', 'delta_w_glu', 'delta_b_glu', 'delta_w_pa', 'delta_w_pb', 'delta_b_gate', 'delta_w_out', 'delta_final_g', 'new_m_ln_g', 'new_m_w_in', 'new_m_conv_w', 'new_m_a_log', 'new_m_dt_bias', 'new_m_head_norm_g', 'new_m_lam_re', 'new_m_lam_im', 'new_m_log_dt', 'new_m_b_re', 'new_m_b_im', 'new_m_c_re', 'new_m_c_im', 'new_m_d_skip', 'new_m_w_glu', 'new_m_b_glu', 'new_m_w_pa', 'new_m_w_pb', 'new_m_b_gate', 'new_m_w_out', 'new_m_final_g', 'new_v_ln_g', 'new_v_w_in', 'new_v_conv_w', 'new_v_a_log', 'new_v_dt_bias', 'new_v_head_norm_g', 'new_v_lam_re', 'new_v_lam_im', 'new_v_log_dt', 'new_v_b_re', 'new_v_b_im', 'new_v_c_re', 'new_v_c_im', 'new_v_d_skip', 'new_v_w_glu', 'new_v_b_glu', 'new_v_w_pa', 'new_v_w_pb', 'new_v_b_gate', 'new_v_w_out', 'new_v_final_g']
TWIN_LEAF_KINDS = {'loss': 'loss', 'grad_x': 'grad_x', 'grad_ln_g': 'grad_w', 'grad_w_in': 'grad_w', 'grad_conv_w': 'grad_w', 'grad_a_log': 'grad_w', 'grad_dt_bias': 'grad_w', 'grad_head_norm_g': 'grad_w', 'grad_lam_re': 'grad_w', 'grad_lam_im': 'grad_w', 'grad_log_dt': 'grad_w', 'grad_b_re': 'grad_w', 'grad_b_im': 'grad_w', 'grad_c_re': 'grad_w', 'grad_c_im': 'grad_w', 'grad_d_skip': 'grad_w', 'grad_w_glu': 'grad_w', 'grad_b_glu': 'grad_w', 'grad_w_pa': 'grad_w', 'grad_w_pb': 'grad_w', 'grad_b_gate': 'grad_w', 'grad_w_out': 'grad_w', 'grad_final_g': 'grad_w', 'delta_ln_g': 'delta_w', 'delta_w_in': 'delta_w', 'delta_conv_w': 'delta_w', 'delta_a_log': 'delta_w', 'delta_dt_bias': 'delta_w', 'delta_head_norm_g': 'delta_w', 'delta_lam_re': 'delta_w', 'delta_lam_im': 'delta_w', 'delta_log_dt': 'delta_w', 'delta_b_re': 'delta_w', 'delta_b_im': 'delta_w', 'delta_c_re': 'delta_w', 'delta_c_im': 'delta_w', 'delta_d_skip': 'delta_w', 'delta_w_glu': 'delta_w', 'delta_b_glu': 'delta_w', 'delta_w_pa': 'delta_w', 'delta_w_pb': 'delta_w', 'delta_b_gate': 'delta_w', 'delta_w_out': 'delta_w', 'delta_final_g': 'delta_w', 'new_m_ln_g': 'new_m', 'new_m_w_in': 'new_m', 'new_m_conv_w': 'new_m', 'new_m_a_log': 'new_m', 'new_m_dt_bias': 'new_m', 'new_m_head_norm_g': 'new_m', 'new_m_lam_re': 'new_m', 'new_m_lam_im': 'new_m', 'new_m_log_dt': 'new_m', 'new_m_b_re': 'new_m', 'new_m_b_im': 'new_m', 'new_m_c_re': 'new_m', 'new_m_c_im': 'new_m', 'new_m_d_skip': 'new_m', 'new_m_w_glu': 'new_m', 'new_m_b_glu': 'new_m', 'new_m_w_pa': 'new_m', 'new_m_w_pb': 'new_m', 'new_m_b_gate': 'new_m', 'new_m_w_out': 'new_m', 'new_m_final_g': 'new_m', 'new_v_ln_g': 'new_v', 'new_v_w_in': 'new_v', 'new_v_conv_w': 'new_v', 'new_v_a_log': 'new_v', 'new_v_dt_bias': 'new_v', 'new_v_head_norm_g': 'new_v', 'new_v_lam_re': 'new_v', 'new_v_lam_im': 'new_v', 'new_v_log_dt': 'new_v', 'new_v_b_re': 'new_v', 'new_v_b_im': 'new_v', 'new_v_c_re': 'new_v', 'new_v_c_im': 'new_v', 'new_v_d_skip': 'new_v', 'new_v_w_glu': 'new_v', 'new_v_b_glu': 'new_v', 'new_v_w_pa': 'new_v', 'new_v_w_pb': 'new_v', 'new_v_b_gate': 'new_v', 'new_v_w_out': 'new_v', 'new_v_final_g': 'new_v'}


def _forward(args):
    return _fwd_reference(*[args[k] for k in FWD_PARAMS])


def _output_shape():
    out = _jax.eval_shape(lambda: _forward(_fwd_setup_inputs(0)))
    return out.shape, out.dtype

N_MICROBATCH = 1
ADAM_LR = 0.001
ADAM_B1 = 0.9
ADAM_B2 = 0.999
ADAM_EPS = 1e-08
ADAM_WD = 0.01
ADAM_STEP = 10
PER_EXAMPLE_BATCH_AXIS = {'x': 0, 'loss_target': 0}
SHARED_INPUTS = []
_WEIGHT_DTYPES = {'ln_g': _jnp.float32, 'w_in': _jnp.float32, 'conv_w': _jnp.float32, 'a_log': _jnp.float32, 'dt_bias': _jnp.float32, 'head_norm_g': _jnp.float32, 'lam_re': _jnp.float32, 'lam_im': _jnp.float32, 'log_dt': _jnp.float32, 'b_re': _jnp.float32, 'b_im': _jnp.float32, 'c_re': _jnp.float32, 'c_im': _jnp.float32, 'd_skip': _jnp.float32, 'w_glu': _jnp.float32, 'b_glu': _jnp.float32, 'w_pa': _jnp.float32, 'w_pb': _jnp.float32, 'b_gate': _jnp.float32, 'w_out': _jnp.float32, 'final_g': _jnp.float32}
MOMENT_SCALE = {'ln_g': 3.507995e-02, 'w_in': 1.556519e-02, 'conv_w': 2.048054e-02, 'a_log': 7.673941e-02, 'dt_bias': 7.432405e-02, 'head_norm_g': 8.194779e-02, 'lam_re': 5.291353e-04, 'lam_im': 5.372440e-04, 'log_dt': 3.568975e-01, 'b_re': 3.459826e-04, 'b_im': 3.467941e-04, 'c_re': 6.990330e-04, 'c_im': 6.862769e-04, 'd_skip': 1.094965e-02, 'w_glu': 2.989870e-03, 'b_glu': 4.435500e-03, 'w_pa': 2.019460e-02, 'w_pb': 7.146443e-03, 'b_gate': 5.981035e-03, 'w_out': 2.146261e-02, 'final_g': 8.003082e+00}


def _to_microbatches(a, axis):
    t = _jnp.moveaxis(a, axis, 0)
    t = t.reshape((N_MICROBATCH, t.shape[0] // N_MICROBATCH) + t.shape[1:])
    return _jnp.moveaxis(t, 1, axis + 1)


def setup_inputs(seed: int = 0) -> dict:
    inp = _fwd_setup_inputs(seed)
    key = _jax.random.fold_in(_jax.random.key(seed), 7919)
    shape, _ = _output_shape()
    out = dict(inp)
    out["loss_target"] = _jax.random.normal(_jax.random.fold_in(key, 0), shape, _jnp.float32)
    for i, name in enumerate(TWIN_WEIGHTS):
        w = inp[name].astype(_jnp.float32)
        if MOMENT_SCALE is None:
            s = _jnp.sqrt(_jnp.mean(_jnp.square(w)) + 1e-30)
        else:
            s = MOMENT_SCALE[name]
        km, kv = _jax.random.split(_jax.random.fold_in(key, i + 1))
        out[name] = w
        out["m_" + name] = s * _jax.random.normal(km, w.shape, _jnp.float32)
        out["v_" + name] = (s * s) * _jax.random.uniform(kv, w.shape, _jnp.float32, 0.5, 1.5)
    if N_MICROBATCH > 1:
        for name, axis in PER_EXAMPLE_BATCH_AXIS.items():
            out[name] = _to_microbatches(out[name], axis)
    return {'x': out['x'], 'ln_g': out['ln_g'], 'w_in': out['w_in'], 'conv_w': out['conv_w'], 'a_log': out['a_log'], 'dt_bias': out['dt_bias'], 'head_norm_g': out['head_norm_g'], 'lam_re': out['lam_re'], 'lam_im': out['lam_im'], 'log_dt': out['log_dt'], 'b_re': out['b_re'], 'b_im': out['b_im'], 'c_re': out['c_re'], 'c_im': out['c_im'], 'd_skip': out['d_skip'], 'w_glu': out['w_glu'], 'b_glu': out['b_glu'], 'w_pa': out['w_pa'], 'w_pb': out['w_pb'], 'b_gate': out['b_gate'], 'w_out': out['w_out'], 'final_g': out['final_g'], 'loss_target': out['loss_target'], 'm_ln_g': out['m_ln_g'], 'm_w_in': out['m_w_in'], 'm_conv_w': out['m_conv_w'], 'm_a_log': out['m_a_log'], 'm_dt_bias': out['m_dt_bias'], 'm_head_norm_g': out['m_head_norm_g'], 'm_lam_re': out['m_lam_re'], 'm_lam_im': out['m_lam_im'], 'm_log_dt': out['m_log_dt'], 'm_b_re': out['m_b_re'], 'm_b_im': out['m_b_im'], 'm_c_re': out['m_c_re'], 'm_c_im': out['m_c_im'], 'm_d_skip': out['m_d_skip'], 'm_w_glu': out['m_w_glu'], 'm_b_glu': out['m_b_glu'], 'm_w_pa': out['m_w_pa'], 'm_w_pb': out['m_w_pb'], 'm_b_gate': out['m_b_gate'], 'm_w_out': out['m_w_out'], 'm_final_g': out['m_final_g'], 'v_ln_g': out['v_ln_g'], 'v_w_in': out['v_w_in'], 'v_conv_w': out['v_conv_w'], 'v_a_log': out['v_a_log'], 'v_dt_bias': out['v_dt_bias'], 'v_head_norm_g': out['v_head_norm_g'], 'v_lam_re': out['v_lam_re'], 'v_lam_im': out['v_lam_im'], 'v_log_dt': out['v_log_dt'], 'v_b_re': out['v_b_re'], 'v_b_im': out['v_b_im'], 'v_c_re': out['v_c_re'], 'v_c_im': out['v_c_im'], 'v_d_skip': out['v_d_skip'], 'v_w_glu': out['v_w_glu'], 'v_b_glu': out['v_b_glu'], 'v_w_pa': out['v_w_pa'], 'v_w_pb': out['v_w_pb'], 'v_b_gate': out['v_b_gate'], 'v_w_out': out['v_w_out'], 'v_final_g': out['v_final_g']}


def _loss(weights, diff, rest, loss_target):
    with _jax.named_scope("forward"):
        args = {**rest, TWIN_DIFF_INPUT: diff, **{k: w.astype(_WEIGHT_DTYPES[k]) for k, w in weights.items()}}
        y = _forward(args)
    with _jax.named_scope("loss_head"):
        err = _jnp.square(y.astype(_jnp.float32) - loss_target)
        return 0.5 * _jnp.sum(_jnp.mean(err, axis=-1)) if err.ndim else 0.5 * err


def _adamw(w, g, m, v):
    m = ADAM_B1 * m + (1.0 - ADAM_B1) * g
    v = ADAM_B2 * v + (1.0 - ADAM_B2) * _jnp.square(g)
    m_hat = m / (1.0 - ADAM_B1 ** ADAM_STEP)
    v_hat = v / (1.0 - ADAM_B2 ** ADAM_STEP)
    delta = -ADAM_LR * (m_hat / (_jnp.sqrt(v_hat) + ADAM_EPS) + ADAM_WD * w)
    return delta, m, v


def reference(x, ln_g, w_in, conv_w, a_log, dt_bias, head_norm_g, lam_re, lam_im, log_dt, b_re, b_im, c_re, c_im, d_skip, w_glu, b_glu, w_pa, w_pb, b_gate, w_out, final_g, loss_target, m_ln_g, m_w_in, m_conv_w, m_a_log, m_dt_bias, m_head_norm_g, m_lam_re, m_lam_im, m_log_dt, m_b_re, m_b_im, m_c_re, m_c_im, m_d_skip, m_w_glu, m_b_glu, m_w_pa, m_w_pb, m_b_gate, m_w_out, m_final_g, v_ln_g, v_w_in, v_conv_w, v_a_log, v_dt_bias, v_head_norm_g, v_lam_re, v_lam_im, v_log_dt, v_b_re, v_b_im, v_c_re, v_c_im, v_d_skip, v_w_glu, v_b_glu, v_w_pa, v_w_pb, v_b_gate, v_w_out, v_final_g):
    given = dict(x=x, ln_g=ln_g, w_in=w_in, conv_w=conv_w, a_log=a_log, dt_bias=dt_bias, head_norm_g=head_norm_g, lam_re=lam_re, lam_im=lam_im, log_dt=log_dt, b_re=b_re, b_im=b_im, c_re=c_re, c_im=c_im, d_skip=d_skip, w_glu=w_glu, b_glu=b_glu, w_pa=w_pa, w_pb=w_pb, b_gate=b_gate, w_out=w_out, final_g=final_g, loss_target=loss_target, m_ln_g=m_ln_g, m_w_in=m_w_in, m_conv_w=m_conv_w, m_a_log=m_a_log, m_dt_bias=m_dt_bias, m_head_norm_g=m_head_norm_g, m_lam_re=m_lam_re, m_lam_im=m_lam_im, m_log_dt=m_log_dt, m_b_re=m_b_re, m_b_im=m_b_im, m_c_re=m_c_re, m_c_im=m_c_im, m_d_skip=m_d_skip, m_w_glu=m_w_glu, m_b_glu=m_b_glu, m_w_pa=m_w_pa, m_w_pb=m_w_pb, m_b_gate=m_b_gate, m_w_out=m_w_out, m_final_g=m_final_g, v_ln_g=v_ln_g, v_w_in=v_w_in, v_conv_w=v_conv_w, v_a_log=v_a_log, v_dt_bias=v_dt_bias, v_head_norm_g=v_head_norm_g, v_lam_re=v_lam_re, v_lam_im=v_lam_im, v_log_dt=v_log_dt, v_b_re=v_b_re, v_b_im=v_b_im, v_c_re=v_c_re, v_c_im=v_c_im, v_d_skip=v_d_skip, v_w_glu=v_w_glu, v_b_glu=v_b_glu, v_w_pa=v_w_pa, v_w_pb=v_w_pb, v_b_gate=v_b_gate, v_w_out=v_w_out, v_final_g=v_final_g)
    weights = {n: given[n] for n in TWIN_WEIGHTS}
    shared = {n: given[n] for n in SHARED_INPUTS}
    per_example = {n: given[n] for n in ['x']}
    grad_fn = _jax.value_and_grad(_loss, argnums=(0, 1))

    def one_microbatch(ex, loss_target):
        ex = dict(ex)
        diff = ex.pop(TWIN_DIFF_INPUT)
        return grad_fn(weights, diff, {**shared, **ex}, loss_target)

    if N_MICROBATCH == 1:
        loss, (grad_w, grad_x) = one_microbatch(per_example, given["loss_target"])
    else:
        def body(carry, xs):
            loss_sum, grad_sum = carry
            l_k, (gw_k, gx_k) = one_microbatch(xs[0], xs[1])
            with _jax.named_scope("update"):
                return (loss_sum + l_k, _jax.tree.map(_jnp.add, grad_sum, gw_k)), gx_k

        init = (_jnp.zeros((), _jnp.float32), _jax.tree.map(_jnp.zeros_like, weights))
        (loss, grad_w), grad_x = _jax.lax.scan(body, init, (per_example, given["loss_target"]))
    with _jax.named_scope("update"):
        delta_w, new_m, new_v = {}, {}, {}
        for n in TWIN_WEIGHTS:
            delta_w[n], new_m[n], new_v[n] = _adamw(weights[n], grad_w[n], given["m_" + n], given["v_" + n])
    return (loss, grad_x, *[grad_w[n] for n in TWIN_WEIGHTS], *[delta_w[n] for n in TWIN_WEIGHTS],
            *[new_m[n] for n in TWIN_WEIGHTS], *[new_v[n] for n in TWIN_WEIGHTS])
```

```python
import functools
import math

import jax
import jax.numpy as jnp
from jax import lax
from jax.experimental import pallas as pl
from jax.experimental.pallas import tpu as pltpu

F32 = jnp.float32
MXU_DT = jnp.bfloat16
WIRE_DT = jnp.bfloat16

HEAD_DIM = 128
CHUNK = 64
CONV_K = 5
S5_GROUP_CH = 16
S5_STATE = 64
S5_BLOCK_GROUPS = 8
S5_BLOCK_STATE = S5_BLOCK_GROUPS * S5_STATE
RMS_EPS = 1e-6
LANES = 128
VMEM_LIMIT = 56 * 1024 * 1024
ROW_TILE = 256
COMM_COLS = 1024

ADAM_LR = 0.001
ADAM_B1 = 0.9
ADAM_B2 = 0.999
ADAM_EPS = 1e-08
ADAM_WD = 0.01
ADAM_STEP = 10

WEIGHTS = ['ln_g', 'w_in', 'conv_w', 'a_log', 'dt_bias', 'head_norm_g', 'lam_re', 'lam_im', 'log_dt',
           'b_re', 'b_im', 'c_re', 'c_im', 'd_skip', 'w_glu', 'b_glu', 'w_pa', 'w_pb', 'b_gate',
           'w_out', 'final_g']
SHARDED = ['w_in', 'w_glu', 'w_pa', 'w_pb', 'w_out', 'conv_w']
REPLICATED = [n for n in WEIGHTS if n not in SHARDED]
MESH = pl.DeviceIdType.MESH


def _params(sem=None):
    return pltpu.CompilerParams(dimension_semantics=sem, vmem_limit_bytes=VMEM_LIMIT)


def _tile(n, cap, q=LANES):
    t = (min(n, cap) // q) * q
    while t > q and n % t:
        t -= q
    return t if t > 0 and n % t == 0 else n


def mm(a, b, *, ta=False, tb=False, add=None, name):
    m = a.shape[1] if ta else a.shape[0]
    k = a.shape[0] if ta else a.shape[1]
    n = b.shape[0] if tb else b.shape[1]
    tm, tn, tk = _tile(m, 1024), _tile(n, 512), _tile(k, 2048)
    nk = k // tk
    dn = (((0 if ta else 1,), (1 if tb else 0,)), ((), ()))

    def body(*refs):
        if add is None:
            a_ref, b_ref, o_ref, acc = refs
        else:
            a_ref, b_ref, add_ref, o_ref, acc = refs
        kk = pl.program_id(2)

        @pl.when(kk == 0)
        def _():
            acc[...] = jnp.zeros_like(acc)

        acc[...] += lax.dot_general(a_ref[...].astype(MXU_DT), b_ref[...].astype(MXU_DT), dn,
                                    preferred_element_type=F32)

        @pl.when(kk == nk - 1)
        def _():
            r = acc[...]
            if add is not None:
                r = r + add_ref[...]
            o_ref[...] = r

    a_spec = pl.BlockSpec((tk, tm), lambda i, j, kk: (kk, i)) if ta else pl.BlockSpec((tm, tk), lambda i, j, kk: (i, kk))
    b_spec = pl.BlockSpec((tn, tk), lambda i, j, kk: (j, kk)) if tb else pl.BlockSpec((tk, tn), lambda i, j, kk: (kk, j))
    o_spec = pl.BlockSpec((tm, tn), lambda i, j, kk: (i, j))
    ins, specs = [a, b], [a_spec, b_spec]
    if add is not None:
        ins.append(add)
        specs.append(o_spec)
    return pl.pallas_call(
        body, grid=(m // tm, n // tn, nk), in_specs=specs, out_specs=o_spec,
        out_shape=jax.ShapeDtypeStruct((m, n), F32), scratch_shapes=[pltpu.VMEM((tm, tn), F32)],
        compiler_params=_params(("parallel", "parallel", "arbitrary")), name=name)(*ins)


def rows(arr, t, width=None, base=0, per_j=False):
    width = arr.shape[1] if width is None else width
    return (arr, (t, width), lambda j, i: (i, base + (j if per_j else 0)))


def rows3(arr, lead, t, width, per_j=True):
    return (arr, (None, t, width), lambda j, i: (lead, i, j if per_j else 0))


def bcast(arr, width=None, base=0, per_j=False):
    width = arr.shape[1] if width is None else width
    return (arr, (arr.shape[0], width), lambda j, i: (0, base + (j if per_j else 0)))


def _specs(items):
    return [pl.BlockSpec(bs, im) for (_, bs, im) in items]


def rw_fwd(fn, ins, outs, *, nrow, t, ncol=1, name):
    out_specs = [pl.BlockSpec((t, w), (lambda j, i: (i, j)) if pj else (lambda j, i: (i, 0))) for (w, pj) in outs]
    out_shape = [jax.ShapeDtypeStruct((nrow, w * (ncol if pj else 1)), F32) for (w, pj) in outs]
    nin = len(ins)

    def body(*refs):
        j = pl.program_id(0)
        res = fn(j, *[r[...] for r in refs[:nin]])
        for o_ref, r in zip(refs[nin:], res):
            o_ref[...] = r

    return pl.pallas_call(
        body, grid=(ncol, nrow // t), in_specs=_specs(ins), out_specs=out_specs, out_shape=out_shape,
        compiler_params=_params(("parallel", "parallel")), name=name)(*[x[0] for x in ins])


def rw_bwd(fn, ins, cots, *, nrow, t, ncol=1, row_grads, bc_grads, residual=None, name):
    nin = len(ins)
    flat_cots = [c for group in cots for c in group]
    extra = [residual] if residual is not None else []
    out_specs, out_shape = [], []
    for idx, pj in row_grads:
        w = ins[idx][1][-1]
        out_specs.append(pl.BlockSpec((t, w), (lambda j, i: (i, j)) if pj else (lambda j, i: (i, 0))))
        out_shape.append(jax.ShapeDtypeStruct((nrow, w * (ncol if pj else 1)), F32))
    for idx in bc_grads:
        r, w = ins[idx][1]
        out_specs.append(pl.BlockSpec((None, r, w), lambda j, i: (j, 0, 0)))
        out_shape.append(jax.ShapeDtypeStruct((ncol, r, w), F32))

    def body(*refs):
        j = pl.program_id(0)
        i = pl.program_id(1)
        vals = [r[...] for r in refs[:nin]]
        pos = nin
        cts = []
        for group in cots:
            c = refs[pos][...]
            for q in range(1, len(group)):
                c = c + refs[pos + q][...]
            pos += len(group)
            cts.append(c)
        res_ref = refs[pos] if residual is not None else None
        pos += len(extra)
        outs = refs[pos:]
        _, vjp = jax.vjp(lambda *a: tuple(fn(j, *a)), *vals)
        grads = vjp(tuple(cts))
        for q, (idx, _) in enumerate(row_grads):
            g = grads[idx]
            if q == 0 and res_ref is not None:
                g = g + res_ref[...]
            outs[q][...] = g
        for q, idx in enumerate(bc_grads):
            o_ref = outs[len(row_grads) + q]

            @pl.when(i == 0)
            def _(o_ref=o_ref):
                o_ref[...] = jnp.zeros_like(o_ref)

            o_ref[...] += grads[idx]

    all_in = list(ins) + flat_cots + extra
    return pl.pallas_call(
        body, grid=(ncol, nrow // t), in_specs=_specs(all_in), out_specs=out_specs, out_shape=out_shape,
        compiler_params=_params(("parallel", "arbitrary")), name=name)(*[x[0] for x in all_in])


def _silu(x):
    return x * jax.nn.sigmoid(x)


@jax.custom_vjp
def _softplus(x):
    return jnp.maximum(x, 0.0) + jnp.log1p(jnp.exp(-jnp.abs(x)))


def _softplus_fwd(x):
    return _softplus(x), x


def _softplus_bwd(x, ct):
    return (ct * jax.nn.sigmoid(x),)


_softplus.defvjp(_softplus_fwd, _softplus_bwd)


def _gelu(x):
    return 0.5 * x * (1.0 + jnp.tanh(math.sqrt(2.0 / math.pi) * (x + 0.044715 * (x * x * x))))


def _row_shift_impl(x, s):
    n = x.shape[0]
    if s == 0:
        return x
    rolled = pltpu.roll(x, (-s) % n, 0)
    t = lax.broadcasted_iota(jnp.int32, x.shape, 0)
    ok = (t + s >= 0) & (t + s < n)
    return jnp.where(ok, rolled, 0.0)


@functools.partial(jax.custom_vjp, nondiff_argnums=(1,))
def _row_shift(x, s):
    return _row_shift_impl(x, s)


def _row_shift_fwd(x, s):
    return _row_shift_impl(x, s), None


def _row_shift_bwd(s, _, ct):
    return (_row_shift_impl(ct, -s),)


_row_shift.defvjp(_row_shift_fwd, _row_shift_bwd)


def fn_rms(j, x, g):
    return (x * lax.rsqrt(jnp.mean(x * x, axis=-1, keepdims=True) + RMS_EPS) * g,)


def make_fn_prep(n_heads):
    pad = (CONV_K - 1) // 2

    def fn_prep(j, x, w):
        y = _row_shift(x, -pad) * w[0:1, :]
        for i in range(1, CONV_K):
            y = y + _row_shift(x, i - pad) * w[i:i + 1, :]
        a = _silu(y)
        scale = jnp.where(j < n_heads, HEAD_DIM ** -0.5, 1.0).astype(F32)
        nrm = a * lax.rsqrt(jnp.sum(a * a, axis=-1, keepdims=True) + RMS_EPS) * scale
        return (jnp.where(j < 2 * n_heads, nrm, a),)

    return fn_prep


def fn_headnorm(j, o0, o1, z, g):
    o = o0 + o1
    n = o * lax.rsqrt(jnp.mean(o * o, axis=-1, keepdims=True) + RMS_EPS) * g
    return (n * _silu(z),)


def fn_gelu(j, y0, y1, u, dsk):
    return (_gelu(y0 + y1 + u * dsk),)


def fn_glu(j, ys, logit, z, b):
    return (ys * jax.nn.sigmoid(logit + b) * _silu(z),)


def fn_merge(j, la, lb, ya, yb, ba, bb):
    return (jax.nn.sigmoid(la + ba) * ya + jax.nn.sigmoid(lb + bb) * yb,)


def fn_loss(j, x, t, g):
    y = x * lax.rsqrt(jnp.mean(x * x, axis=-1, keepdims=True) + RMS_EPS) * g
    e = y - t
    return (0.5 * jnp.mean(e * e, axis=-1, keepdims=True),)


def make_fn_sum(n):
    def fn_sum(j, *xs):
        s = xs[0]
        for q in range(1, n):
            s = s + xs[q]
        return (s,)

    return fn_sum


def fn_adamw(j, w, g, m, v):
    m2 = ADAM_B1 * m + (1.0 - ADAM_B1) * g
    v2 = ADAM_B2 * v + (1.0 - ADAM_B2) * (g * g)
    m_hat = m2 / (1.0 - ADAM_B1 ** ADAM_STEP)
    v_hat = v2 / (1.0 - ADAM_B2 ** ADAM_STEP)
    delta = -ADAM_LR * (m_hat / (jnp.sqrt(v_hat) + ADAM_EPS) + ADAM_WD * w)
    return delta, m2, v2


def _dot(a, b, dims, precision=None):
    return lax.dot_general(a, b, (dims, ((), ())), precision=precision, preferred_element_type=F32)


_NN = ((1,), (0,))
_NT = ((1,), (1,))
_TN = ((0,), (0,))
_HI = lax.Precision.HIGHEST


def gdn_chunk(s, q, k, v, ba, alog, dtb, *, d, colb, cola):
    c = q.shape[0]
    lane = lax.broadcasted_iota(jnp.int32, ba.shape, 1)
    beta = jnp.sum(jnp.where(lane == colb, jax.nn.sigmoid(ba), 0.0), axis=1, keepdims=True)
    gl = -jnp.exp(alog) * _softplus(ba + dtb)
    g = jnp.sum(jnp.where(lane == cola, gl, 0.0), axis=1, keepdims=True)
    ii = lax.broadcasted_iota(jnp.int32, (c, c), 0)
    jj = lax.broadcasted_iota(jnp.int32, (c, c), 1)
    sgn = 1 - 2 * d
    incl = (ii - jj) * sgn >= 0
    strict = (ii - jj) * sgn > 0
    g_row = jnp.sum(jnp.where(ii == jj, g, 0.0), axis=0, keepdims=True)
    gc_col = jnp.sum(jnp.where(incl, g_row, 0.0), axis=1, keepdims=True)
    incl_t = (jj - ii) * sgn >= 0
    gc_row = jnp.sum(jnp.where(incl_t, g, 0.0), axis=0, keepdims=True)
    decay = jnp.exp(jnp.where(incl, gc_col - gc_row, -1e30))
    kb = k * beta
    vb = v * beta
    lmat = jnp.where(strict, _dot(kb, k, _NT) * decay, 0.0)
    eye = jnp.where(ii == jj, 1.0, 0.0).astype(F32)
    pw = -lmat
    tinv = eye + pw
    steps = max(1, int(math.ceil(math.log2(c))) - 1)
    for _ in range(steps):
        pw = _dot(pw, pw, _NN, _HI)
        tinv = tinv + _dot(tinv, pw, _NN, _HI)
    u = _dot(tinv, vb, _NN, _HI)
    w = _dot(tinv, kb * jnp.exp(gc_col), _NN, _HI)
    qk = _dot(q, k, _NT) * decay
    v_new = u - _dot(w, s, _NN)
    o = _dot(q * jnp.exp(gc_col), s, _NN) + _dot(qk, v_new, _NN)
    g_tot = jnp.sum(g, axis=0, keepdims=True)
    s_new = s * jnp.exp(g_tot) + _dot(k * jnp.exp(g_tot - gc_col), v_new, _TN)
    return s_new, o


def _gdn_maps(n_heads, n_chunks):
    def chunk_of(p, step):
        d = p // n_heads
        return step + d * (n_chunks - 1 - 2 * step)
    return chunk_of


def gdn_fwd(qkv, proj, ba_blk, alog_row, dtb_row, *, n_heads, name):
    nrow = qkv.shape[0]
    nc = nrow // CHUNK
    h = n_heads
    chunk_of = _gdn_maps(h, nc)

    def body(q_ref, k_ref, v_ref, ba_ref, al_ref, dt_ref, o_ref, sh_ref, s_scr):
        p = pl.program_id(0)
        n = pl.program_id(1)
        d = p // h
        hh = p % h

        @pl.when(n == 0)
        def _():
            s_scr[...] = jnp.zeros_like(s_scr)

        s = s_scr[...]
        sh_ref[...] = s
        s2, o = gdn_chunk(s, q_ref[...], k_ref[...], v_ref[...], ba_ref[...], al_ref[...], dt_ref[...],
                          d=d, colb=d * h + hh, cola=2 * h + d * h + hh)
        o_ref[...] = o
        s_scr[...] = s2

    blk = (CHUNK, HEAD_DIM)
    in_specs = [
        pl.BlockSpec(blk, lambda p, n: (chunk_of(p, n), p % h)),
        pl.BlockSpec(blk, lambda p, n: (chunk_of(p, n), h + p % h)),
        pl.BlockSpec(blk, lambda p, n: (chunk_of(p, n), 2 * h + p % h)),
        pl.BlockSpec(blk, lambda p, n: (chunk_of(p, n), ba_blk)),
        pl.BlockSpec((1, LANES), lambda p, n: (0, 0)),
        pl.BlockSpec((1, LANES), lambda p, n: (0, 0)),
    ]
    out_specs = [
        pl.BlockSpec((None, CHUNK, HEAD_DIM), lambda p, n: (p // h, chunk_of(p, n), p % h)),
        pl.BlockSpec((None, None, HEAD_DIM, HEAD_DIM), lambda p, n: (p, n, 0, 0)),
    ]
    out_shape = [jax.ShapeDtypeStruct((2, nrow, h * HEAD_DIM), F32),
                 jax.ShapeDtypeStruct((2 * h, nc, HEAD_DIM, HEAD_DIM), F32)]
    return pl.pallas_call(
        body, grid=(2 * h, nc), in_specs=in_specs, out_specs=out_specs, out_shape=out_shape,
        scratch_shapes=[pltpu.VMEM((HEAD_DIM, HEAD_DIM), F32)],
        compiler_params=_params(("parallel", "arbitrary")), name=name)(qkv, qkv, qkv, proj, alog_row, dtb_row)


def gdn_bwd(qkv, proj, ba_blk, alog_row, dtb_row, s_hist, do, *, n_heads, name):
    nrow = qkv.shape[0]
    nc = nrow // CHUNK
    h = n_heads
    chunk_of = _gdn_maps(h, nc)

    def cb(p, n):
        return chunk_of(p, nc - 1 - n)

    def body(q_ref, k_ref, v_ref, ba_ref, al_ref, dt_ref, sh_ref, do_ref,
             dqkv_ref, dba_ref, dal_ref, ddt_ref, ds_scr):
        p = pl.program_id(0)
        n = pl.program_id(1)
        d = p // h
        hh = p % h

        @pl.when(n == 0)
        def _():
            ds_scr[...] = jnp.zeros_like(ds_scr)
            dal_ref[...] = jnp.zeros_like(dal_ref)
            ddt_ref[...] = jnp.zeros_like(ddt_ref)

        f = functools.partial(gdn_chunk, d=d, colb=d * h + hh, cola=2 * h + d * h + hh)
        _, vjp = jax.vjp(f, sh_ref[...], q_ref[...], k_ref[...], v_ref[...], ba_ref[...], al_ref[...], dt_ref[...])
        ds, dq, dk, dv, dba, dal, ddt = vjp((ds_scr[...], do_ref[...]))
        ds_scr[...] = ds
        dqkv_ref[0] = dq
        dqkv_ref[1] = dk
        dqkv_ref[2] = dv
        dba_ref[...] = dba
        dal_ref[...] += dal
        ddt_ref[...] += ddt

    blk = (CHUNK, HEAD_DIM)
    in_specs = [
        pl.BlockSpec(blk, lambda p, n: (cb(p, n), p % h)),
        pl.BlockSpec(blk, lambda p, n: (cb(p, n), h + p % h)),
        pl.BlockSpec(blk, lambda p, n: (cb(p, n), 2 * h + p % h)),
        pl.BlockSpec(blk, lambda p, n: (cb(p, n), ba_blk)),
        pl.BlockSpec((1, LANES), lambda p, n: (0, 0)),
        pl.BlockSpec((1, LANES), lambda p, n: (0, 0)),
        pl.BlockSpec((None, None, HEAD_DIM, HEAD_DIM), lambda p, n: (p, nc - 1 - n, 0, 0)),
        pl.BlockSpec(blk, lambda p, n: (cb(p, n), p % h)),
    ]
    out_specs = [
        pl.BlockSpec((None, 3, CHUNK, HEAD_DIM), lambda p, n: (p // h, 0, cb(p, n), p % h)),
        pl.BlockSpec((None, CHUNK, LANES), lambda p, n: (p, cb(p, n), 0)),
        pl.BlockSpec((None, 1, LANES), lambda p, n: (p, 0, 0)),
        pl.BlockSpec((None, 1, LANES), lambda p, n: (p, 0, 0)),
    ]
    out_shape = [jax.ShapeDtypeStruct((2, 3, nrow, h * HEAD_DIM), F32),
                 jax.ShapeDtypeStruct((2 * h, nrow, LANES), F32),
                 jax.ShapeDtypeStruct((2 * h, 1, LANES), F32),
                 jax.ShapeDtypeStruct((2 * h, 1, LANES), F32)]
    return pl.pallas_call(
        body, grid=(2 * h, nc), in_specs=in_specs, out_specs=out_specs, out_shape=out_shape,
        scratch_shapes=[pltpu.VMEM((HEAD_DIM, HEAD_DIM), F32)],
        compiler_params=_params(("parallel", "arbitrary")), name=name)(
            qkv, qkv, qkv, proj, alog_row, dtb_row, s_hist, do)


def s5_mats(lam_re, lam_im, log_dt, b_re, b_im, c_re, c_im):
    g = lam_re.shape[1]
    nb = g // S5_BLOCK_GROUPS
    dt = jnp.exp(log_dt)[..., None]
    mag = jnp.exp(lam_re * dt)
    ar = mag * jnp.cos(lam_im * dt)
    ai = mag * jnp.sin(lam_im * dt)
    den = lam_re * lam_re + lam_im * lam_im
    fr = ((ar - 1.0) * lam_re + ai * lam_im) / den
    fi = (ai * lam_re - (ar - 1.0) * lam_im) / den
    bbr = fr[..., None] * b_re - fi[..., None] * b_im
    bbi = fr[..., None] * b_im + fi[..., None] * b_re
    eye = jnp.eye(S5_BLOCK_GROUPS, dtype=F32)
    shp = (2, nb, S5_BLOCK_GROUPS, S5_STATE, S5_GROUP_CH)
    w_r = jnp.einsum('dsjpc,jk->dsjckp', bbr.reshape(shp), eye).reshape(2, nb, LANES, S5_BLOCK_STATE)
    w_i = jnp.einsum('dsjpc,jk->dsjckp', bbi.reshape(shp), eye).reshape(2, nb, LANES, S5_BLOCK_STATE)
    w = jnp.concatenate([w_r, w_i], axis=-1)
    shc = (2, nb, S5_BLOCK_GROUPS, S5_GROUP_CH, S5_STATE)
    c_r = jnp.einsum('dsjcp,jk->dsjpkc', c_re.reshape(shc), eye).reshape(2, nb, S5_BLOCK_STATE, LANES)
    c_i = jnp.einsum('dsjcp,jk->dsjpkc', c_im.reshape(shc), eye).reshape(2, nb, S5_BLOCK_STATE, LANES)
    cm = jnp.concatenate([c_r, -c_i], axis=-2)
    return (ar.reshape(2, nb, 1, S5_BLOCK_STATE), ai.reshape(2, nb, 1, S5_BLOCK_STATE), w, cm)


_S5_ROWS = 512


def s5_fwd(proj, u_blk, ar, ai, w, cm, *, name):
    nrow = proj.shape[0]
    nb = w.shape[1]
    nt = nrow // 8
    hs = S5_BLOCK_STATE
    rs = min(_S5_ROWS, nrow)

    def body(u_ref, ar_ref, ai_ref, w_ref, cm_ref, y_ref, x_ref):
        d = pl.program_id(0)
        wb = w_ref[...].astype(MXU_DT)
        for r0 in range(0, nrow, rs):
            x_ref[pl.ds(r0, rs), :] = _dot(u_ref[pl.ds(r0, rs), :].astype(MXU_DT), wb, _NN)
        a_r = ar_ref[...]
        a_i = ai_ref[...]
        rid = lax.broadcasted_iota(jnp.int32, (8, hs), 0)

        def run(reverse):
            def tile_step(tt, carry):
                s_r, s_i = carry
                base = pl.multiple_of((nt - 1 - tt if reverse else tt) * 8, 8)
                tile = x_ref[pl.ds(base, 8), :]
                o_r = jnp.zeros((8, hs), F32)
                o_i = jnp.zeros((8, hs), F32)
                for q in range(8):
                    rr = 7 - q if reverse else q
                    n_r = a_r * s_r - a_i * s_i + tile[rr:rr + 1, :hs]
                    n_i = a_r * s_i + a_i * s_r + tile[rr:rr + 1, hs:]
                    s_r, s_i = n_r, n_i
                    o_r = jnp.where(rid == rr, s_r, o_r)
                    o_i = jnp.where(rid == rr, s_i, o_i)
                x_ref[pl.ds(base, 8), pl.ds(0, hs)] = o_r
                x_ref[pl.ds(base, 8), pl.ds(hs, hs)] = o_i
                return s_r, s_i

            z = jnp.zeros((1, hs), F32)
            lax.fori_loop(0, nt, tile_step, (z, z))

        @pl.when(d == 0)
        def _():
            run(False)

        @pl.when(d == 1)
        def _():
            run(True)

        cb = cm_ref[...].astype(MXU_DT)
        for r0 in range(0, nrow, rs):
            y_ref[pl.ds(r0, rs), :] = _dot(x_ref[pl.ds(r0, rs), :].astype(MXU_DT), cb, _NN)

    in_specs = [
        pl.BlockSpec((nrow, LANES), lambda d, s: (0, u_blk + s)),
        pl.BlockSpec((None, None, 1, hs), lambda d, s: (d, s, 0, 0)),
        pl.BlockSpec((None, None, 1, hs), lambda d, s: (d, s, 0, 0)),
        pl.BlockSpec((None, None, LANES, 2 * hs), lambda d, s: (d, s, 0, 0)),
        pl.BlockSpec((None, None, 2 * hs, LANES), lambda d, s: (d, s, 0, 0)),
    ]
    out_specs = [
        pl.BlockSpec((None, nrow, LANES), lambda d, s: (d, 0, s)),
        pl.BlockSpec((None, nrow, 2 * hs), lambda d, s: (d, 0, s)),
    ]
    out_shape = [jax.ShapeDtypeStruct((2, nrow, nb * LANES), F32),
                 jax.ShapeDtypeStruct((2, nrow, nb * 2 * hs), F32)]
    return pl.pallas_call(
        body, grid=(2, nb), in_specs=in_specs, out_specs=out_specs, out_shape=out_shape,
        compiler_params=_params(("parallel", "parallel")), name=name)(proj, ar, ai, w, cm)


def s5_bwd(proj, u_blk, ar, ai, w, cm, xs, dy, du_in, *, name):
    nrow = proj.shape[0]
    nb = w.shape[1]
    nt = nrow // 8
    hs = S5_BLOCK_STATE
    rs = min(_S5_ROWS, nrow)

    def body(u_ref, dy_ref, dui_ref, x_ref, ar_ref, ai_ref, w_ref, cm_ref,
             du_ref, dw_ref, dcm_ref, dar_ref, dai_ref, g_ref):
        d = pl.program_id(1)
        cb = cm_ref[...].astype(MXU_DT)
        for r0 in range(0, nrow, rs):
            g_ref[pl.ds(r0, rs), :] = _dot(dy_ref[pl.ds(r0, rs), :].astype(MXU_DT), cb, _NT)
        a_r = ar_ref[...]
        a_i = ai_ref[...]
        rid = lax.broadcasted_iota(jnp.int32, (8, hs), 0)

        def run(descending):
            def tile_step(tt, carry):
                g_r, g_i, acc_r, acc_i = carry
                tidx = nt - 1 - tt if descending else tt
                base = pl.multiple_of(tidx * 8, 8)
                tile = g_ref[pl.ds(base, 8), :]
                o_r = jnp.zeros((8, hs), F32)
                o_i = jnp.zeros((8, hs), F32)
                for q in range(8):
                    rr = 7 - q if descending else q
                    n_r = tile[rr:rr + 1, :hs] + (a_r * g_r + a_i * g_i)
                    n_i = tile[rr:rr + 1, hs:] + (a_r * g_i - a_i * g_r)
                    g_r, g_i = n_r, n_i
                    o_r = jnp.where(rid == rr, g_r, o_r)
                    o_i = jnp.where(rid == rr, g_i, o_i)
                g_ref[pl.ds(base, 8), pl.ds(0, hs)] = o_r
                g_ref[pl.ds(base, 8), pl.ds(hs, hs)] = o_i
                xt = x_ref[pl.ds(base, 8), :]
                if descending:
                    nbase = pl.multiple_of(jnp.maximum(tidx - 1, 0) * 8, 8)
                    edge = x_ref[pl.ds(nbase, 8), :][7:8, :] * jnp.where(tidx > 0, 1.0, 0.0).astype(F32)
                    prev = jnp.where(lax.broadcasted_iota(jnp.int32, xt.shape, 0) == 0, edge, pltpu.roll(xt, 1, 0))
                else:
                    nbase = pl.multiple_of(jnp.minimum(tidx + 1, nt - 1) * 8, 8)
                    edge = x_ref[pl.ds(nbase, 8), :][0:1, :] * jnp.where(tidx < nt - 1, 1.0, 0.0).astype(F32)
                    prev = jnp.where(lax.broadcasted_iota(jnp.int32, xt.shape, 0) == 7, edge, pltpu.roll(xt, 7, 0))
                p_r = prev[:, :hs]
                p_i = prev[:, hs:]
                acc_r = acc_r + o_r * p_r + o_i * p_i
                acc_i = acc_i + o_i * p_r - o_r * p_i
                return g_r, g_i, acc_r, acc_i

            z = jnp.zeros((1, hs), F32)
            z8 = jnp.zeros((8, hs), F32)
            _, _, acc_r, acc_i = lax.fori_loop(0, nt, tile_step, (z, z, z8, z8))
            dar_ref[...] = jnp.sum(acc_r, axis=0, keepdims=True)
            dai_ref[...] = jnp.sum(acc_i, axis=0, keepdims=True)

        @pl.when(d == 0)
        def _():
            run(True)

        @pl.when(d == 1)
        def _():
            run(False)

        wb = w_ref[...].astype(MXU_DT)
        for r0 in range(0, nrow, rs):
            part = _dot(g_ref[pl.ds(r0, rs), :].astype(MXU_DT), wb, _NT)

            @pl.when(d == 0)
            def _(part=part, r0=r0):
                du_ref[pl.ds(r0, rs), :] = dui_ref[pl.ds(r0, rs), :] + part

            @pl.when(d == 1)
            def _(part=part, r0=r0):
                du_ref[pl.ds(r0, rs), :] += part

        dw_ref[...] = _dot(u_ref[...].astype(MXU_DT), g_ref[...].astype(MXU_DT), _TN)
        dcm_ref[...] = _dot(x_ref[...].astype(MXU_DT), dy_ref[...].astype(MXU_DT), _TN)

    in_specs = [
        pl.BlockSpec((nrow, LANES), lambda s, d: (0, u_blk + s)),
        pl.BlockSpec((nrow, LANES), lambda s, d: (0, s)),
        pl.BlockSpec((nrow, LANES), lambda s, d: (0, s)),
        pl.BlockSpec((None, nrow, 2 * hs), lambda s, d: (d, 0, s)),
        pl.BlockSpec((None, None, 1, hs), lambda s, d: (d, s, 0, 0)),
        pl.BlockSpec((None, None, 1, hs), lambda s, d: (d, s, 0, 0)),
        pl.BlockSpec((None, None, LANES, 2 * hs), lambda s, d: (d, s, 0, 0)),
        pl.BlockSpec((None, None, 2 * hs, LANES), lambda s, d: (d, s, 0, 0)),
    ]
    out_specs = [
        pl.BlockSpec((nrow, LANES), lambda s, d: (0, s)),
        pl.BlockSpec((None, None, LANES, 2 * hs), lambda s, d: (d, s, 0, 0)),
        pl.BlockSpec((None, None, 2 * hs, LANES), lambda s, d: (d, s, 0, 0)),
        pl.BlockSpec((None, None, 1, hs), lambda s, d: (d, s, 0, 0)),
        pl.BlockSpec((None, None, 1, hs), lambda s, d: (d, s, 0, 0)),
    ]
    out_shape = [jax.ShapeDtypeStruct((nrow, nb * LANES), F32),
                 jax.ShapeDtypeStruct(w.shape, F32), jax.ShapeDtypeStruct(cm.shape, F32),
                 jax.ShapeDtypeStruct(ar.shape, F32), jax.ShapeDtypeStruct(ai.shape, F32)]
    return pl.pallas_call(
        body, grid=(nb, 2), in_specs=in_specs, out_specs=out_specs, out_shape=out_shape,
        scratch_shapes=[pltpu.VMEM((nrow, 2 * hs), F32)],
        compiler_params=_params(("parallel", "arbitrary")), name=name)(proj, dy, du_in, xs, ar, ai, w, cm)


def _me():
    return lax.axis_index("x"), lax.axis_index("y"), lax.axis_index("c")


def all_gather(shards, *, name):
    na = len(shards)
    hbm = pl.BlockSpec(memory_space=pl.ANY)

    def body(*refs):
        x_refs, out_refs = refs[:na], refs[na:2 * na]
        send_sems, recv_sems, local_sems = refs[2 * na:]
        x, y, c = _me()
        me, sibling = (x, y, c), (x, y, 1 - c)
        chips = [(1 - x, y), (x, 1 - y), (1 - x, 1 - y)]

        def slot(a, px, py, pc):
            return out_refs[a].at[4 * px + 2 * py + pc]

        def copy(a, k, block, to, src=None):
            return pltpu.make_async_remote_copy(
                src_ref=slot(a, *block) if src is None else src, dst_ref=slot(a, *block),
                send_sem=send_sems.at[a, k], recv_sem=recv_sems.at[a, k], device_id=to, device_id_type=MESH)

        mine = [pltpu.make_async_copy(x_refs[a], slot(a, *me), local_sems.at[a]) for a in range(na)]
        for cp in mine:
            cp.start()
        first = []
        for a in range(na):
            first.append(copy(a, 0, me, sibling, src=x_refs[a]))
            first += [copy(a, 1 + j, me, (*chip, c), src=x_refs[a]) for j, chip in enumerate(chips)]
        for cp in first:
            cp.start()
        passed = []
        for j, chip in enumerate(chips):
            for a in range(na):
                copy(a, 1 + j, (*chip, c), me).wait_recv()
                fwd = copy(a, 4 + j, (*chip, c), sibling)
                fwd.start()
                passed.append(fwd)
        for a in range(na):
            copy(a, 0, sibling, me).wait_recv()
            for j, chip in enumerate(chips):
                copy(a, 4 + j, (*chip, 1 - c), me).wait_recv()
        for cp in first + passed:
            cp.wait_send()
        for cp in mine:
            cp.wait()

    return pl.pallas_call(
        body, out_shape=[jax.ShapeDtypeStruct((8,) + s.shape, s.dtype) for s in shards],
        in_specs=[hbm] * na, out_specs=[hbm] * na,
        scratch_shapes=[pltpu.SemaphoreType.DMA((na, 7)), pltpu.SemaphoreType.DMA((na, 7)),
                        pltpu.SemaphoreType.DMA((na,))],
        name=name)(*shards)


_AXES = ("x", "y", "c")


def exchange(bufs, axis, *, half, name):
    na = len(bufs)
    out_sd = [jax.ShapeDtypeStruct(b.shape[1:] if half else b.shape, b.dtype) for b in bufs]
    hbm = pl.BlockSpec(memory_space=pl.ANY)

    def body(*refs):
        in_refs, out_refs = refs[:na], refs[na:2 * na]
        send_sems, recv_sems = refs[2 * na:]
        me = _me()
        bit = me[_AXES.index(axis)]
        peer = tuple(1 - v if a == axis else v for a, v in zip(_AXES, me))
        cps = [pltpu.make_async_remote_copy(
            src_ref=in_refs[a].at[1 - bit] if half else in_refs[a], dst_ref=out_refs[a],
            send_sem=send_sems.at[a], recv_sem=recv_sems.at[a], device_id=peer, device_id_type=MESH)
            for a in range(na)]
        for cp in cps:
            cp.start()
        for cp in cps:
            cp.wait()

    return pl.pallas_call(
        body, out_shape=out_sd, in_specs=[hbm] * na, out_specs=[hbm] * na,
        scratch_shapes=[pltpu.SemaphoreType.DMA((na,)), pltpu.SemaphoreType.DMA((na,))], name=name)(*bufs)


def add_half(buf, recv, bit, *, name):
    c = recv.shape[-1]
    r = math.prod(recv.shape[:-1])
    t = _tile(r, 512, 8)

    def body(bit_ref, a_ref, b_ref, o_ref):
        o_ref[...] = a_ref[...] + b_ref[...]

    grid_spec = pltpu.PrefetchScalarGridSpec(
        num_scalar_prefetch=1, grid=(r // t,),
        in_specs=[pl.BlockSpec((None, t, c), lambda i, b: (b[0], i, 0)), pl.BlockSpec((t, c), lambda i, b: (i, 0))],
        out_specs=pl.BlockSpec((t, c), lambda i, b: (i, 0)))
    out = pl.pallas_call(body, grid_spec=grid_spec, out_shape=jax.ShapeDtypeStruct((r, c), F32),
                         compiler_params=_params(("parallel",)), name=name)(
                             bit, buf.reshape(2, r, c), recv.reshape(r, c))
    return out.reshape(recv.shape)


def _pack(arrs, cols, mult):
    flat = jnp.concatenate([a.reshape(-1) for a in arrs])
    n = flat.shape[0]
    padded = -(-n // (cols * mult)) * (cols * mult)
    return jnp.pad(flat, (0, padded - n)).reshape(padded // cols, cols)


def _unpack(flat, shapes):
    out, off = [], 0
    for s in shapes:
        n = math.prod(s)
        out.append(flat[off:off + n].reshape(s))
        off += n
    return out


def _elementwise(fn, arrs, nout, name):
    r, c = arrs[0].shape
    t = _tile(r, ROW_TILE, 8)
    return rw_fwd(fn, [rows(a, t) for a in arrs], [(c, False)] * nout, nrow=r, t=t, name=name)


def kernel(x, ln_g, w_in, conv_w, a_log, dt_bias, head_norm_g, lam_re, lam_im, log_dt, b_re, b_im, c_re, c_im, d_skip, w_glu, b_glu, w_pa, w_pb, b_gate, w_out, final_g, loss_target, m_ln_g, m_w_in, m_conv_w, m_a_log, m_dt_bias, m_head_norm_g, m_lam_re, m_lam_im, m_log_dt, m_b_re, m_b_im, m_c_re, m_c_im, m_d_skip, m_w_glu, m_b_glu, m_w_pa, m_w_pb, m_b_gate, m_w_out, m_final_g, v_ln_g, v_w_in, v_conv_w, v_a_log, v_dt_bias, v_head_norm_g, v_lam_re, v_lam_im, v_log_dt, v_b_re, v_b_im, v_c_re, v_c_im, v_d_skip, v_w_glu, v_b_glu, v_w_pa, v_w_pb, v_b_gate, v_w_out, v_final_g):
    env = dict(locals())
    wts = {n: env[n] for n in WEIGHTS}
    mom_m = {n: env["m_" + n] for n in WEIGHTS}
    mom_v = {n: env["v_" + n] for n in WEIGHTS}

    xin = x[0]
    tgt = loss_target[0]
    nrow, dm = xin.shape
    depth = ln_g.shape[0]
    nh = dm // (2 * HEAD_DIM)
    wa = nh * HEAD_DIM
    wb = dm // 2
    ngrp = wb // S5_GROUP_CH
    pw = w_in.shape[-1]
    t = min(ROW_TILE, nrow)

    o_ba = 4 * wa
    o_u = o_ba + 4 * nh
    n_main = 4 * wa + 2 * wb + 2 * dm
    projp = -(-(n_main + LANES) // 512) * 512
    blk_za = 3 * wa // HEAD_DIM
    blk_u = 4 * wa // LANES
    ba_blk = n_main // LANES
    cx, cy, cc = _me()

    g_in, g_glu, g_pa, g_pb, g_out, g_conv = all_gather(
        [w_in.astype(WIRE_DT), w_glu.astype(WIRE_DT), w_pa.astype(WIRE_DT), w_pb.astype(WIRE_DT),
         w_out.astype(WIRE_DT), conv_w], name="gather_weights")

    def cat(g, l):
        return jnp.concatenate([g[j, l] for j in range(8)], axis=1)

    full = []
    for l in range(depth):
        w_full = cat(g_in, l)
        w_perm = jnp.concatenate(
            [w_full[:, :o_ba], w_full[:, o_u:], w_full[:, o_ba:o_u],
             jnp.zeros((dm, projp - n_main - 4 * nh), WIRE_DT)], axis=1)
        full.append(dict(
            w_in=w_perm, w_glu=g_glu[:, l].reshape(wb, wb), w_pa=cat(g_pa, l), w_pb=cat(g_pb, l),
            w_out=g_out[:, l].reshape(dm, dm), conv_w=cat(g_conv, l)))

    def small(l):
        z = jnp.zeros((1, LANES - 4 * nh), F32)
        alog_row = jnp.concatenate([jnp.zeros((1, 2 * nh), F32), a_log[l].reshape(1, 2 * nh), z], axis=1)
        dtb_row = jnp.concatenate([jnp.zeros((1, 2 * nh), F32), dt_bias[l].reshape(1, 2 * nh), z], axis=1)
        return alog_row, dtb_row

    saved = []
    cur = xin
    for l in range(depth):
        fw = full[l]
        tag = f"l{l}_"
        (hh,) = rw_fwd(fn_rms, [rows(cur, t), bcast(ln_g[l][None])], [(dm, False)], nrow=nrow, t=t, name="rms_fwd")
        proj = mm(hh, fw["w_in"], name="proj_fwd")
        (qkv,) = rw_fwd(make_fn_prep(nh), [rows(proj, nrow, HEAD_DIM, 0, True), bcast(fw["conv_w"], HEAD_DIM, 0, True)],
                        [(HEAD_DIM, True)], nrow=nrow, t=nrow, ncol=3 * nh, name="prep_fwd")
        alog_row, dtb_row = small(l)
        o_dir, s_hist = gdn_fwd(qkv, proj, ba_blk, alog_row, dtb_row, n_heads=nh, name="gdn_fwd")
        hn_ins = [rows3(o_dir, 0, t, HEAD_DIM), rows3(o_dir, 1, t, HEAD_DIM), rows(proj, t, HEAD_DIM, blk_za, True),
                  bcast(head_norm_g[l][None])]
        (ya_in,) = rw_fwd(fn_headnorm, hn_ins, [(HEAD_DIM, True)], nrow=nrow, t=t, ncol=nh, name="headnorm_fwd")
        mats, mats_vjp = jax.vjp(s5_mats, lam_re[l], lam_im[l], log_dt[l], b_re[l], b_im[l], c_re[l], c_im[l])
        yd, xs = s5_fwd(proj, blk_u, *mats, name="s5_fwd")
        ge_ins = [rows3(yd, 0, t, wb, False), rows3(yd, 1, t, wb, False), rows(proj, t, wb, 4 * wa // wb),
                  bcast(d_skip[l][None])]
        (ys,) = rw_fwd(fn_gelu, ge_ins, [(wb, False)], nrow=nrow, t=t, name="gelu_fwd")
        glu = mm(ys, fw["w_glu"], name="glu_fwd")
        gl_ins = [rows(ys, t), rows(glu, t), rows(proj, t, wb, (4 * wa + wb) // wb), bcast(b_glu[l][None])]
        (yb_in,) = rw_fwd(fn_glu, gl_ins, [(wb, False)], nrow=nrow, t=t, name="glugate_fwd")
        y_a = mm(ya_in, fw["w_pa"], name="pa_fwd")
        y_b = mm(yb_in, fw["w_pb"], name="pb_fwd")
        bg = b_gate[l][None]
        mg_ins = [rows(proj, t, dm, 3), rows(proj, t, dm, 4), rows(y_a, t), rows(y_b, t),
                  bcast(bg, dm, 0), bcast(bg, dm, 1)]
        (merged,) = rw_fwd(fn_merge, mg_ins, [(dm, False)], nrow=nrow, t=t, name="merge_fwd")
        nxt = mm(merged, fw["w_out"], add=cur, name="out_fwd")
        saved.append(dict(x=cur, h=hh, proj=proj, qkv=qkv, o_dir=o_dir, s_hist=s_hist, hn_ins=hn_ins,
                          mats=mats, mats_vjp=mats_vjp, xs=xs, ge_ins=ge_ins, ys=ys, gl_ins=gl_ins,
                          ya_in=ya_in, yb_in=yb_in, mg_ins=mg_ins, merged=merged,
                          alog_row=alog_row, dtb_row=dtb_row))
        cur = nxt

    loss_ins = [rows(cur, t), rows(tgt, t), bcast(final_g[None])]
    (row_loss,) = rw_fwd(fn_loss, loss_ins, [(1, False)], nrow=nrow, t=t, name="loss_fwd")
    ones = jnp.ones((nrow, 1), F32)
    dcur, dfinal = rw_bwd(fn_loss, loss_ins, [[rows(ones, t)]], nrow=nrow, t=t,
                          row_grads=[(0, False)], bc_grads=[2], name="loss_bwd")
    loss = lax.psum(jnp.sum(row_loss), ("x", "y", "c"))

    gsh = {n: [None] * depth for n in SHARDED}
    grep = {n: [None] * depth for n in REPLICATED if n != "final_g"}
    for l in reversed(range(depth)):
        fw, sv = full[l], saved[l]
        dmerged = mm(dcur, fw["w_out"], tb=True, name="out_bwd_x")
        gsh["w_out"][l] = mm(sv["merged"], dcur, ta=True, name="out_bwd_w")
        dla, dlb, dya, dyb, dbga, dbgb = rw_bwd(
            fn_merge, sv["mg_ins"], [[rows(dmerged, t)]], nrow=nrow, t=t,
            row_grads=[(0, False), (1, False), (2, False), (3, False)], bc_grads=[4, 5], name="merge_bwd")
        grep["b_gate"][l] = jnp.concatenate([dbga.reshape(dm), dbgb.reshape(dm)])
        dya_in = mm(dya, fw["w_pa"], tb=True, name="pa_bwd_x")
        gsh["w_pa"][l] = mm(sv["ya_in"], dya, ta=True, name="pa_bwd_w")
        dyb_in = mm(dyb, fw["w_pb"], tb=True, name="pb_bwd_x")
        gsh["w_pb"][l] = mm(sv["yb_in"], dyb, ta=True, name="pb_bwd_w")
        dys1, dglu, dzb, dbglu = rw_bwd(
            fn_glu, sv["gl_ins"], [[rows(dyb_in, t)]], nrow=nrow, t=t,
            row_grads=[(0, False), (1, False), (2, False)], bc_grads=[3], name="glugate_bwd")
        grep["b_glu"][l] = dbglu.reshape(wb)
        dys2 = mm(dglu, fw["w_glu"], tb=True, name="glu_bwd_x")
        gsh["w_glu"][l] = mm(sv["ys"], dglu, ta=True, name="glu_bwd_w")
        dyd, du1, ddskip = rw_bwd(
            fn_gelu, sv["ge_ins"], [[rows(dys1, t), rows(dys2, t)]], nrow=nrow, t=t,
            row_grads=[(0, False), (2, False)], bc_grads=[3], name="gelu_bwd")
        grep["d_skip"][l] = ddskip.reshape(wb)
        du, dw_s5, dcm_s5, dar, dai = s5_bwd(sv["proj"], blk_u, *sv["mats"], sv["xs"], dyd, du1, name="s5_bwd")
        g_lr, g_li, g_ldt, g_br, g_bi, g_cr, g_ci = sv["mats_vjp"]((dar, dai, dw_s5, dcm_s5))
        for nme, val in zip(["lam_re", "lam_im", "log_dt", "b_re", "b_im", "c_re", "c_im"],
                            [g_lr, g_li, g_ldt, g_br, g_bi, g_cr, g_ci]):
            grep[nme][l] = val
        do, dza, dhn = rw_bwd(fn_headnorm, sv["hn_ins"], [[rows(dya_in, t, HEAD_DIM, 0, True)]], nrow=nrow, t=t,
                              ncol=nh, row_grads=[(0, True), (2, True)], bc_grads=[3], name="headnorm_bwd")
        grep["head_norm_g"][l] = jnp.sum(dhn, axis=0).reshape(HEAD_DIM)
        dqkv, dba_all, dal, ddt = gdn_bwd(sv["qkv"], sv["proj"], ba_blk, sv["alog_row"], sv["dtb_row"],
                                          sv["s_hist"], do, n_heads=nh, name="gdn_bwd")
        grep["a_log"][l] = jnp.sum(dal, axis=(0, 1))[2 * nh:4 * nh].reshape(2, nh)
        grep["dt_bias"][l] = jnp.sum(ddt, axis=(0, 1))[2 * nh:4 * nh].reshape(2, nh)
        (dba,) = rw_fwd(make_fn_sum(2 * nh), [rows3(dba_all, p, t, LANES, False) for p in range(2 * nh)],
                        [(LANES, False)], nrow=nrow, t=t, name="dba_sum")
        prep_ins = [rows(sv["proj"], nrow, HEAD_DIM, 0, True), bcast(fw["conv_w"], HEAD_DIM, 0, True)]
        dq_cots = [(dqkv.reshape(2, 3 * nrow, wa), (None, nrow, HEAD_DIM),
                    (lambda j, i, dd=dd: (dd, j // nh, j % nh))) for dd in range(2)]
        dqkv_raw, dconv = rw_bwd(make_fn_prep(nh), prep_ins, [dq_cots], nrow=nrow, t=nrow, ncol=3 * nh,
                                 row_grads=[(0, True)], bc_grads=[1], name="prep_bwd")
        gsh["conv_w"][l] = jnp.transpose(dconv, (1, 0, 2)).reshape(CONV_K, 3 * wa)
        dproj = jnp.concatenate([dqkv_raw, dza, du, dzb, dla, dlb, dba,
                                 jnp.zeros((nrow, projp - n_main - LANES), F32)], axis=1)
        dh = mm(dproj, fw["w_in"], tb=True, name="proj_bwd_x")
        dwp = mm(sv["h"], dproj, ta=True, name="proj_bwd_w")
        gsh["w_in"][l] = jnp.concatenate([dwp[:, :o_ba], dwp[:, n_main:n_main + 4 * nh], dwp[:, o_ba:n_main]], axis=1)
        dcur, dlng = rw_bwd(fn_rms, [rows(sv["x"], t), bcast(ln_g[l][None])], [[rows(dh, t)]], nrow=nrow, t=t,
                            row_grads=[(0, False)], bc_grads=[1], residual=rows(dcur, t), name="rms_bwd")
        grep["ln_g"][l] = dlng.reshape(dm)
    grad_x = dcur[None]

    order = [4 * ((q // 2) % 2) + 2 * (q % 2) + q // 4 for q in range(8)]

    def by_cols(g):
        n = g.shape[1] // 8
        return jnp.stack([g[:, d * n:(d + 1) * n] for d in order])

    def by_rows(g):
        n = g.shape[0] // 8
        return jnp.stack([g[d * n:(d + 1) * n] for d in order])

    g_final = {n: [None] * depth for n in SHARDED}
    for l in range(depth):
        bufs = [by_cols(gsh["w_in"][l]), by_rows(gsh["w_glu"][l]), by_cols(gsh["w_pa"][l]),
                by_cols(gsh["w_pb"][l]), by_rows(gsh["w_out"][l]), by_cols(gsh["conv_w"][l])]
        for axis, coord in (("c", cc), ("x", cx), ("y", cy)):
            bufs = [b.reshape((2, b.shape[0] // 2) + b.shape[1:]) for b in bufs]
            recv = exchange(bufs, axis, half=True, name="rs_exchange_" + axis)
            bit = coord.astype(jnp.int32).reshape(1)
            bufs = [add_half(b, r, bit, name="rs_add_" + axis) for b, r in zip(bufs, recv)]
        for n, b in zip(["w_in", "w_glu", "w_pa", "w_pb", "w_out", "conv_w"], bufs):
            g_final[n][l] = b.reshape(b.shape[1:])
    grads = {n: jnp.stack(g_final[n]) for n in SHARDED}

    rep_list = [jnp.stack(grep[n]) for n in REPLICATED if n != "final_g"] + [dfinal.reshape(dm)]
    rep_shapes = [a.shape for a in rep_list]
    rbuf = _pack(rep_list, COMM_COLS, ROW_TILE)
    for axis in ("c", "x", "y"):
        (recv,) = exchange([rbuf], axis, half=False, name="ar_exchange_" + axis)
        (rbuf,) = _elementwise(make_fn_sum(2), [rbuf, recv], 1, "ar_add_" + axis)
    for n, val in zip(REPLICATED, _unpack(rbuf.reshape(-1), rep_shapes)):
        grads[n] = val

    deltas, new_m, new_v = {}, {}, {}
    big = ["w_in", "w_glu", "w_pa", "w_pb", "w_out"]
    for n in big:
        shp = wts[n].shape
        two = [a.reshape(-1, shp[-1]) for a in (wts[n], grads[n], mom_m[n], mom_v[n])]
        d_, m_, v_ = _elementwise(fn_adamw, two, 3, "adamw_" + n)
        deltas[n], new_m[n], new_v[n] = d_.reshape(shp), m_.reshape(shp), v_.reshape(shp)
    rest = [n for n in WEIGHTS if n not in big]
    rest_shapes = [wts[n].shape for n in rest]
    packed = [_pack([src[n] for n in rest], COMM_COLS, ROW_TILE) for src in (wts, grads, mom_m, mom_v)]
    outs = _elementwise(fn_adamw, packed, 3, "adamw_small")
    for dst, arr in zip((deltas, new_m, new_v), outs):
        for n, val in zip(rest, _unpack(arr.reshape(-1), rest_shapes)):
            dst[n] = val

    return (loss, grad_x, *[grads[n] for n in WEIGHTS], *[deltas[n] for n in WEIGHTS],
            *[new_m[n] for n in WEIGHTS], *[new_v[n] for n in WEIGHTS])
```

```python
import functools
import math

import jax
import jax.numpy as jnp
from jax import lax
from jax.experimental import pallas as pl
from jax.experimental.pallas import tpu as pltpu

F32 = jnp.float32
MXU_DT = jnp.bfloat16
WIRE_DT = jnp.bfloat16

HEAD_DIM = 128
CHUNK = 64
CONV_K = 5
S5_GROUP_CH = 16
S5_STATE = 64
S5_BLOCK_GROUPS = 8
S5_BLOCK_STATE = S5_BLOCK_GROUPS * S5_STATE
RMS_EPS = 1e-6
LANES = 128
VMEM_LIMIT = 56 * 1024 * 1024
ROW_TILE = 256
COMM_COLS = 1024
ADD_BLOCK_BYTES = 2 * 1024 * 1024

ADAM_LR = 0.001
ADAM_B1 = 0.9
ADAM_B2 = 0.999
ADAM_EPS = 1e-08
ADAM_WD = 0.01
ADAM_STEP = 10

WEIGHTS = ['ln_g', 'w_in', 'conv_w', 'a_log', 'dt_bias', 'head_norm_g', 'lam_re', 'lam_im', 'log_dt',
           'b_re', 'b_im', 'c_re', 'c_im', 'd_skip', 'w_glu', 'b_glu', 'w_pa', 'w_pb', 'b_gate',
           'w_out', 'final_g']
SHARDED = ['w_in', 'w_glu', 'w_pa', 'w_pb', 'w_out', 'conv_w']
REPLICATED = [n for n in WEIGHTS if n not in SHARDED]
MESH = pl.DeviceIdType.MESH


def _params(sem=None):
    return pltpu.CompilerParams(dimension_semantics=sem, vmem_limit_bytes=VMEM_LIMIT)


def _tile(n, cap, q=LANES):
    t = (min(n, cap) // q) * q
    while t > q and n % t:
        t -= q
    return t if t > 0 and n % t == 0 else n


def slot_cxy(d):
    return 4 * (d % 2) + 2 * (d // 4) + (d // 2) % 2


def slot_cyx(d):
    return 4 * (d % 2) + 2 * ((d // 2) % 2) + d // 4


def mm(a, b, *, ta=False, tb=False, add=None, m_part=None, scatter=None, name):
    m = a.shape[1] if ta else a.shape[0]
    k = a.shape[0] if ta else a.shape[1]
    n = b.shape[0] if tb else b.shape[1]
    m_off = 0
    if m_part is not None:
        m = m // m_part[1]
        m_off = m_part[0]
    tm, tn, tk = _tile(m, 1024), _tile(n, 512), _tile(k, 2048)
    if scatter is not None and scatter[0] == "rows":
        tm = m // 8
    if scatter is not None and scatter[0] == "cols":
        tn = n // 8
    if m_part is not None:
        assert tm == m
    nk = k // tk
    dn = (((0 if ta else 1,), (1 if tb else 0,)), ((), ()))

    def body(*refs):
        if add is None:
            a_ref, b_ref, o_ref, acc = refs
        else:
            a_ref, b_ref, add_ref, o_ref, acc = refs
        kk = pl.program_id(2)

        @pl.when(kk == 0)
        def _():
            acc[...] = jnp.zeros_like(acc)

        acc[...] += lax.dot_general(a_ref[...].astype(MXU_DT), b_ref[...].astype(MXU_DT), dn,
                                    preferred_element_type=F32)

        @pl.when(kk == nk - 1)
        def _():
            r = acc[...]
            if add is not None:
                r = r + add_ref[...]
            o_ref[...] = r

    if ta:
        a_spec = pl.BlockSpec((tk, tm), lambda i, j, kk: (kk, i + m_off))
    else:
        a_spec = pl.BlockSpec((tm, tk), lambda i, j, kk: (i + m_off, kk))
    b_spec = pl.BlockSpec((tn, tk), lambda i, j, kk: (j, kk)) if tb else pl.BlockSpec((tk, tn), lambda i, j, kk: (kk, j))
    out_sd = jax.ShapeDtypeStruct((m, n), F32)
    if scatter is None:
        o_spec = pl.BlockSpec((tm, tn), lambda i, j, kk: (i, j))
    elif scatter[0] == "rows":
        o_spec = pl.BlockSpec((None, tm, tn), lambda i, j, kk: (scatter[1](i), 0, j))
        out_sd = jax.ShapeDtypeStruct((8, tm, n), F32)
    else:
        o_spec = pl.BlockSpec((None, tm, tn), lambda i, j, kk: (scatter[1](j), i, 0))
        out_sd = jax.ShapeDtypeStruct((8, m, tn), F32)
    ins, specs = [a, b], [a_spec, b_spec]
    if add is not None:
        ins.append(add)
        specs.append(o_spec)
    return pl.pallas_call(
        body, grid=(m // tm, n // tn, nk), in_specs=specs, out_specs=o_spec,
        out_shape=out_sd, scratch_shapes=[pltpu.VMEM((tm, tn), F32)],
        compiler_params=_params(("parallel", "parallel", "arbitrary")), name=name)(*ins)


def rows(arr, t, width=None, base=0, per_j=False):
    width = arr.shape[1] if width is None else width
    return (arr, (t, width), lambda j, i: (i, base + (j if per_j else 0)))


def rows3(arr, lead, t, width, per_j=True):
    return (arr, (None, t, width), lambda j, i: (lead, i, j if per_j else 0))


def bcast(arr, width=None, base=0, per_j=False):
    width = arr.shape[1] if width is None else width
    return (arr, (arr.shape[0], width), lambda j, i: (0, base + (j if per_j else 0)))


def _specs(items):
    return [pl.BlockSpec(bs, im) for (_, bs, im) in items]


def rw_fwd(fn, ins, outs, *, nrow, t, ncol=1, name):
    out_specs = [pl.BlockSpec((t, w), (lambda j, i: (i, j)) if pj else (lambda j, i: (i, 0))) for (w, pj) in outs]
    out_shape = [jax.ShapeDtypeStruct((nrow, w * (ncol if pj else 1)), F32) for (w, pj) in outs]
    nin = len(ins)

    def body(*refs):
        j = pl.program_id(0)
        res = fn(j, *[r[...] for r in refs[:nin]])
        for o_ref, r in zip(refs[nin:], res):
            o_ref[...] = r

    return pl.pallas_call(
        body, grid=(ncol, nrow // t), in_specs=_specs(ins), out_specs=out_specs, out_shape=out_shape,
        compiler_params=_params(("parallel", "parallel")), name=name)(*[x[0] for x in ins])


def rw_bwd(fn, ins, cots, *, nrow, t, ncol=1, row_grads, bc_grads, residual=None, name):
    nin = len(ins)
    flat_cots = [c for group in cots for c in group]
    extra = [residual] if residual is not None else []
    out_specs, out_shape = [], []
    for idx, pj in row_grads:
        w = ins[idx][1][-1]
        out_specs.append(pl.BlockSpec((t, w), (lambda j, i: (i, j)) if pj else (lambda j, i: (i, 0))))
        out_shape.append(jax.ShapeDtypeStruct((nrow, w * (ncol if pj else 1)), F32))
    for idx in bc_grads:
        r, w = ins[idx][1]
        out_specs.append(pl.BlockSpec((None, r, w), lambda j, i: (j, 0, 0)))
        out_shape.append(jax.ShapeDtypeStruct((ncol, r, w), F32))

    def body(*refs):
        j = pl.program_id(0)
        i = pl.program_id(1)
        vals = [r[...] for r in refs[:nin]]
        pos = nin
        cts = []
        for group in cots:
            c = refs[pos][...]
            for q in range(1, len(group)):
                c = c + refs[pos + q][...]
            pos += len(group)
            cts.append(c)
        res_ref = refs[pos] if residual is not None else None
        pos += len(extra)
        outs = refs[pos:]
        _, vjp = jax.vjp(lambda *a: tuple(fn(j, *a)), *vals)
        grads = vjp(tuple(cts))
        for q, (idx, _) in enumerate(row_grads):
            g = grads[idx]
            if q == 0 and res_ref is not None:
                g = g + res_ref[...]
            outs[q][...] = g
        for q, idx in enumerate(bc_grads):
            o_ref = outs[len(row_grads) + q]

            @pl.when(i == 0)
            def _(o_ref=o_ref):
                o_ref[...] = jnp.zeros_like(o_ref)

            o_ref[...] += grads[idx]

    all_in = list(ins) + flat_cots + extra
    return pl.pallas_call(
        body, grid=(ncol, nrow // t), in_specs=_specs(all_in), out_specs=out_specs, out_shape=out_shape,
        compiler_params=_params(("parallel", "arbitrary")), name=name)(*[x[0] for x in all_in])


def _silu(x):
    return x * jax.nn.sigmoid(x)


@jax.custom_vjp
def _softplus(x):
    return jnp.maximum(x, 0.0) + jnp.log1p(jnp.exp(-jnp.abs(x)))


def _softplus_fwd(x):
    return _softplus(x), x


def _softplus_bwd(x, ct):
    return (ct * jax.nn.sigmoid(x),)


_softplus.defvjp(_softplus_fwd, _softplus_bwd)


def _gelu(x):
    return 0.5 * x * (1.0 + jnp.tanh(math.sqrt(2.0 / math.pi) * (x + 0.044715 * (x * x * x))))


def _row_shift_impl(x, s):
    n = x.shape[0]
    if s == 0:
        return x
    rolled = pltpu.roll(x, (-s) % n, 0)
    t = lax.broadcasted_iota(jnp.int32, x.shape, 0)
    ok = (t + s >= 0) & (t + s < n)
    return jnp.where(ok, rolled, 0.0)


@functools.partial(jax.custom_vjp, nondiff_argnums=(1,))
def _row_shift(x, s):
    return _row_shift_impl(x, s)


def _row_shift_fwd(x, s):
    return _row_shift_impl(x, s), None


def _row_shift_bwd(s, _, ct):
    return (_row_shift_impl(ct, -s),)


_row_shift.defvjp(_row_shift_fwd, _row_shift_bwd)


def fn_rms(j, x, g):
    return (x * lax.rsqrt(jnp.mean(x * x, axis=-1, keepdims=True) + RMS_EPS) * g,)


def make_fn_prep(n_heads):
    pad = (CONV_K - 1) // 2

    def fn_prep(j, x, w):
        y = _row_shift(x, -pad) * w[0:1, :]
        for i in range(1, CONV_K):
            y = y + _row_shift(x, i - pad) * w[i:i + 1, :]
        a = _silu(y)
        scale = jnp.where(j < n_heads, HEAD_DIM ** -0.5, 1.0).astype(F32)
        nrm = a * lax.rsqrt(jnp.sum(a * a, axis=-1, keepdims=True) + RMS_EPS) * scale
        return (jnp.where(j < 2 * n_heads, nrm, a),)

    return fn_prep


def fn_headnorm(j, o0, o1, z, g):
    o = o0 + o1
    n = o * lax.rsqrt(jnp.mean(o * o, axis=-1, keepdims=True) + RMS_EPS) * g
    return (n * _silu(z),)


def fn_gelu(j, y0, y1, u, dsk):
    return (_gelu(y0 + y1 + u * dsk),)


def fn_glu(j, ys, logit, z, b):
    return (ys * jax.nn.sigmoid(logit + b) * _silu(z),)


def fn_merge(j, la, lb, ya, yb, ba, bb):
    return (jax.nn.sigmoid(la + ba) * ya + jax.nn.sigmoid(lb + bb) * yb,)


def fn_loss(j, x, t, g):
    y = x * lax.rsqrt(jnp.mean(x * x, axis=-1, keepdims=True) + RMS_EPS) * g
    e = y - t
    return (0.5 * jnp.mean(e * e, axis=-1, keepdims=True),)


def make_fn_sum(n):
    def fn_sum(j, *xs):
        s = xs[0]
        for q in range(1, n):
            s = s + xs[q]
        return (s,)

    return fn_sum


def fn_adamw(j, w, g, m, v):
    m2 = ADAM_B1 * m + (1.0 - ADAM_B1) * g
    v2 = ADAM_B2 * v + (1.0 - ADAM_B2) * (g * g)
    m_hat = m2 / (1.0 - ADAM_B1 ** ADAM_STEP)
    v_hat = v2 / (1.0 - ADAM_B2 ** ADAM_STEP)
    delta = -ADAM_LR * (m_hat / (jnp.sqrt(v_hat) + ADAM_EPS) + ADAM_WD * w)
    return delta, m2, v2


def _dot(a, b, dims, precision=None):
    return lax.dot_general(a, b, (dims, ((), ())), precision=precision, preferred_element_type=F32)


_NN = ((1,), (0,))
_NT = ((1,), (1,))
_TN = ((0,), (0,))
GDN_HEAD_BLOCK = 4


def _split(a):
    hi = a.astype(MXU_DT)
    return hi, (a - hi.astype(F32)).astype(MXU_DT)


def _dot3(a, b, dims):
    ah, al = _split(a)
    bh, bl = _split(b)
    return _dot(ah, bh, dims) + (_dot(al, bh, dims) + _dot(ah, bl, dims))


def _unit_inverse(lmat):
    r = lmat.shape[0]
    eye = (lax.broadcasted_iota(jnp.int32, (r, r), 0) == lax.broadcasted_iota(jnp.int32, (r, r), 1)).astype(F32)
    pw = -lmat
    tinv = eye + pw
    for _ in range(int(math.ceil(math.log2(CHUNK))) - 1):
        pw = _dot3(pw, pw, _NN)
        tinv = tinv + _dot3(tinv, pw, _NN)
    return tinv


@jax.custom_vjp
def tri_solve(lmat, rhs):
    return _dot3(_unit_inverse(lmat), rhs, _NN)


def _tri_solve_fwd(lmat, rhs):
    tinv = _unit_inverse(lmat)
    x = _dot3(tinv, rhs, _NN)
    return x, (tinv, x)


def _tri_solve_bwd(res, dx):
    tinv, x = res
    drhs = _dot3(tinv, dx, _TN)
    return -_dot3(drhs, x, _NT), drhs


tri_solve.defvjp(_tri_solve_fwd, _tri_solve_bwd)


def gdn_group(s, q, k, v, ba, alog, dtb, *, d, head0, n_heads):
    hb = len(s)
    c = q.shape[0]
    r = hb * c

    def stack(x):
        return jnp.concatenate([x[:, i * HEAD_DIM:(i + 1) * HEAD_DIM] for i in range(hb)], axis=0)

    def pick(x, lane0):
        lane = lax.broadcasted_iota(jnp.int32, x.shape, 1)
        return jnp.concatenate(
            [jnp.sum(jnp.where(lane == lane0 + i, x, 0.0), axis=1, keepdims=True) for i in range(hb)], axis=0)

    q4, k4, v4 = stack(q), stack(k), stack(v)
    beta = pick(jax.nn.sigmoid(ba), d * n_heads + head0)
    g = pick(-jnp.exp(alog) * _softplus(ba + dtb), 2 * n_heads + d * n_heads + head0)
    ii = lax.broadcasted_iota(jnp.int32, (r, r), 0)
    jj = lax.broadcasted_iota(jnp.int32, (r, r), 1)
    same = (ii // c) == (jj // c)
    rel = (ii - jj) * (1 - 2 * d)
    incl = same & (rel >= 0)
    strict = same & (rel > 0)
    incl_t = same & (rel <= 0)
    g_row = jnp.sum(jnp.where(ii == jj, g, 0.0), axis=0, keepdims=True)
    gc_col = jnp.sum(jnp.where(incl, g_row, 0.0), axis=1, keepdims=True)
    gc_row = jnp.sum(jnp.where(incl_t, g, 0.0), axis=0, keepdims=True)
    g_tot = jnp.sum(jnp.where(same, g_row, 0.0), axis=1, keepdims=True)
    decay = jnp.exp(jnp.where(incl, gc_col - gc_row, -1e30))
    kb = k4 * beta
    vb = v4 * beta
    lmat = jnp.where(strict, _dot(kb, k4, _NT) * decay, 0.0)
    uw = tri_solve(lmat, jnp.concatenate([vb, kb * jnp.exp(gc_col)], axis=1))
    u, w = uw[:, :HEAD_DIM], uw[:, HEAD_DIM:]
    qk = _dot(q4, k4, _NT) * decay
    qe = q4 * jnp.exp(gc_col)
    kd = k4 * jnp.exp(g_tot - gc_col)
    eg = jnp.exp(g_tot)
    v_new, o_s = [], []
    for i in range(hb):
        rs = slice(i * c, (i + 1) * c)
        ws = _dot(jnp.concatenate([w[rs], qe[rs]], axis=0), s[i], _NN)
        v_new.append(u[rs] - ws[:c])
        o_s.append(ws[c:])
    o4 = jnp.concatenate(o_s, axis=0) + _dot(qk, jnp.concatenate(v_new, axis=0), _NN)
    s_new = tuple(s[i] * eg[i * c:i * c + 1, :] + _dot(kd[i * c:(i + 1) * c], v_new[i], _TN) for i in range(hb))
    o = jnp.concatenate([o4[i * c:(i + 1) * c] for i in range(hb)], axis=1)
    return s_new, o


def _gdn_maps(n_chunks):
    def chunk_of(d, step):
        return step + d * (n_chunks - 1 - 2 * step)
    return chunk_of


def _heads(ref, h0, n):
    return ref[:, h0 * HEAD_DIM:(h0 + n) * HEAD_DIM]


def gdn_fwd(qkv, proj, ba_blk, alog_row, dtb_row, *, n_heads, name):
    nrow = qkv.shape[0]
    nc = nrow // CHUNK
    h = n_heads
    wa = h * HEAD_DIM
    hb = math.gcd(h, GDN_HEAD_BLOCK)
    chunk_of = _gdn_maps(nc)

    def body(q_ref, k_ref, v_ref, ba_ref, al_ref, dt_ref, o_ref, sh_ref, s_scr):
        d = pl.program_id(0)
        n = pl.program_id(1)

        @pl.when(n == 0)
        def _():
            s_scr[...] = jnp.zeros_like(s_scr)

        ba, al, dt = ba_ref[...], al_ref[...], dt_ref[...]
        for h0 in range(0, h, hb):
            s = tuple(s_scr[h0 + i] for i in range(hb))
            for i in range(hb):
                sh_ref[h0 + i] = s[i]
            s2, o = gdn_group(s, _heads(q_ref, h0, hb), _heads(k_ref, h0, hb), _heads(v_ref, h0, hb), ba, al, dt,
                              d=d, head0=h0, n_heads=h)
            o_ref[:, h0 * HEAD_DIM:(h0 + hb) * HEAD_DIM] = o
            for i in range(hb):
                s_scr[h0 + i] = s2[i]

    blk = (CHUNK, wa)
    in_specs = [
        pl.BlockSpec(blk, lambda d, n: (chunk_of(d, n), 0)),
        pl.BlockSpec(blk, lambda d, n: (chunk_of(d, n), 1)),
        pl.BlockSpec(blk, lambda d, n: (chunk_of(d, n), 2)),
        pl.BlockSpec((CHUNK, LANES), lambda d, n: (chunk_of(d, n), ba_blk)),
        pl.BlockSpec((1, LANES), lambda d, n: (0, 0)),
        pl.BlockSpec((1, LANES), lambda d, n: (0, 0)),
    ]
    out_specs = [
        pl.BlockSpec((None, CHUNK, wa), lambda d, n: (d, chunk_of(d, n), 0)),
        pl.BlockSpec((h, None, HEAD_DIM, HEAD_DIM), lambda d, n: (d, n, 0, 0)),
    ]
    out_shape = [jax.ShapeDtypeStruct((2, nrow, wa), F32),
                 jax.ShapeDtypeStruct((2 * h, nc, HEAD_DIM, HEAD_DIM), F32)]
    return pl.pallas_call(
        body, grid=(2, nc), in_specs=in_specs, out_specs=out_specs, out_shape=out_shape,
        scratch_shapes=[pltpu.VMEM((h, HEAD_DIM, HEAD_DIM), F32)],
        compiler_params=_params(("parallel", "arbitrary")), name=name)(qkv, qkv, qkv, proj, alog_row, dtb_row)


def gdn_bwd(qkv, proj, ba_blk, alog_row, dtb_row, s_hist, do, *, n_heads, name):
    nrow = qkv.shape[0]
    nc = nrow // CHUNK
    h = n_heads
    wa = h * HEAD_DIM
    hb = math.gcd(h, GDN_HEAD_BLOCK)
    chunk_of = _gdn_maps(nc)

    def cb(d, n):
        return chunk_of(d, nc - 1 - n)

    def body(q_ref, k_ref, v_ref, ba_ref, al_ref, dt_ref, sh_ref, do_ref,
             dqkv_ref, dba_ref, dal_ref, ddt_ref, ds_scr):
        d = pl.program_id(0)
        n = pl.program_id(1)

        @pl.when(n == 0)
        def _():
            ds_scr[...] = jnp.zeros_like(ds_scr)
            dal_ref[...] = jnp.zeros_like(dal_ref)
            ddt_ref[...] = jnp.zeros_like(ddt_ref)

        ba, al, dt = ba_ref[...], al_ref[...], dt_ref[...]
        dba_sum = jnp.zeros_like(ba)
        dal_sum = jnp.zeros_like(al)
        ddt_sum = jnp.zeros_like(dt)
        for h0 in range(0, h, hb):
            f = functools.partial(gdn_group, d=d, head0=h0, n_heads=h)
            s = tuple(sh_ref[h0 + i] for i in range(hb))
            _, vjp = jax.vjp(f, s, _heads(q_ref, h0, hb), _heads(k_ref, h0, hb), _heads(v_ref, h0, hb), ba, al, dt)
            ds, dq, dk, dv, dba, dal, ddt = vjp((tuple(ds_scr[h0 + i] for i in range(hb)), _heads(do_ref, h0, hb)))
            for i in range(hb):
                ds_scr[h0 + i] = ds[i]
            cols = slice(h0 * HEAD_DIM, (h0 + hb) * HEAD_DIM)
            dqkv_ref[0, :, cols] = dq
            dqkv_ref[1, :, cols] = dk
            dqkv_ref[2, :, cols] = dv
            dba_sum = dba_sum + dba
            dal_sum = dal_sum + dal
            ddt_sum = ddt_sum + ddt
        dba_ref[...] = dba_sum
        dal_ref[...] += dal_sum
        ddt_ref[...] += ddt_sum

    blk = (CHUNK, wa)
    in_specs = [
        pl.BlockSpec(blk, lambda d, n: (cb(d, n), 0)),
        pl.BlockSpec(blk, lambda d, n: (cb(d, n), 1)),
        pl.BlockSpec(blk, lambda d, n: (cb(d, n), 2)),
        pl.BlockSpec((CHUNK, LANES), lambda d, n: (cb(d, n), ba_blk)),
        pl.BlockSpec((1, LANES), lambda d, n: (0, 0)),
        pl.BlockSpec((1, LANES), lambda d, n: (0, 0)),
        pl.BlockSpec((h, None, HEAD_DIM, HEAD_DIM), lambda d, n: (d, nc - 1 - n, 0, 0)),
        pl.BlockSpec(blk, lambda d, n: (cb(d, n), 0)),
    ]
    out_specs = [
        pl.BlockSpec((None, 3, CHUNK, wa), lambda d, n: (d, 0, cb(d, n), 0)),
        pl.BlockSpec((None, CHUNK, LANES), lambda d, n: (d, cb(d, n), 0)),
        pl.BlockSpec((None, 1, LANES), lambda d, n: (d, 0, 0)),
        pl.BlockSpec((None, 1, LANES), lambda d, n: (d, 0, 0)),
    ]
    out_shape = [jax.ShapeDtypeStruct((2, 3, nrow, wa), F32),
                 jax.ShapeDtypeStruct((2, nrow, LANES), F32),
                 jax.ShapeDtypeStruct((2, 1, LANES), F32),
                 jax.ShapeDtypeStruct((2, 1, LANES), F32)]
    return pl.pallas_call(
        body, grid=(2, nc), in_specs=in_specs, out_specs=out_specs, out_shape=out_shape,
        scratch_shapes=[pltpu.VMEM((h, HEAD_DIM, HEAD_DIM), F32)],
        compiler_params=_params(("parallel", "arbitrary")), name=name)(
            qkv, qkv, qkv, proj, alog_row, dtb_row, s_hist, do)


def s5_mats(lam_re, lam_im, log_dt, b_re, b_im, c_re, c_im):
    g = lam_re.shape[1]
    nb = g // S5_BLOCK_GROUPS
    dt = jnp.exp(log_dt)[..., None]
    mag = jnp.exp(lam_re * dt)
    ar = mag * jnp.cos(lam_im * dt)
    ai = mag * jnp.sin(lam_im * dt)
    den = lam_re * lam_re + lam_im * lam_im
    fr = ((ar - 1.0) * lam_re + ai * lam_im) / den
    fi = (ai * lam_re - (ar - 1.0) * lam_im) / den
    bbr = fr[..., None] * b_re - fi[..., None] * b_im
    bbi = fr[..., None] * b_im + fi[..., None] * b_re
    eye = jnp.eye(S5_BLOCK_GROUPS, dtype=F32)
    shp = (2, nb, S5_BLOCK_GROUPS, S5_STATE, S5_GROUP_CH)
    w_r = jnp.einsum('dsjpc,jk->dsjckp', bbr.reshape(shp), eye).reshape(2, nb, LANES, S5_BLOCK_STATE)
    w_i = jnp.einsum('dsjpc,jk->dsjckp', bbi.reshape(shp), eye).reshape(2, nb, LANES, S5_BLOCK_STATE)
    w = jnp.concatenate([w_r, w_i], axis=-1)
    shc = (2, nb, S5_BLOCK_GROUPS, S5_GROUP_CH, S5_STATE)
    c_r = jnp.einsum('dsjcp,jk->dsjpkc', c_re.reshape(shc), eye).reshape(2, nb, S5_BLOCK_STATE, LANES)
    c_i = jnp.einsum('dsjcp,jk->dsjpkc', c_im.reshape(shc), eye).reshape(2, nb, S5_BLOCK_STATE, LANES)
    cm = jnp.concatenate([c_r, -c_i], axis=-2)
    return (ar.reshape(2, nb, 1, S5_BLOCK_STATE), ai.reshape(2, nb, 1, S5_BLOCK_STATE), w, cm)


_S5_ROWS = 512


def s5_fwd(proj, u_blk, ar, ai, w, cm, *, name):
    nrow = proj.shape[0]
    nb = w.shape[1]
    nt = nrow // 8
    hs = S5_BLOCK_STATE
    rs = min(_S5_ROWS, nrow)

    def body(u_ref, ar_ref, ai_ref, w_ref, cm_ref, y_ref, x_ref):
        d = pl.program_id(0)
        wb = w_ref[...].astype(MXU_DT)
        for r0 in range(0, nrow, rs):
            x_ref[pl.ds(r0, rs), :] = _dot(u_ref[pl.ds(r0, rs), :].astype(MXU_DT), wb, _NN)
        a_r = ar_ref[...]
        a_i = ai_ref[...]
        rid = lax.broadcasted_iota(jnp.int32, (8, hs), 0)

        def run(reverse):
            def tile_step(tt, carry):
                s_r, s_i = carry
                base = pl.multiple_of((nt - 1 - tt if reverse else tt) * 8, 8)
                tile = x_ref[pl.ds(base, 8), :]
                o_r = jnp.zeros((8, hs), F32)
                o_i = jnp.zeros((8, hs), F32)
                for q in range(8):
                    rr = 7 - q if reverse else q
                    n_r = a_r * s_r - a_i * s_i + tile[rr:rr + 1, :hs]
                    n_i = a_r * s_i + a_i * s_r + tile[rr:rr + 1, hs:]
                    s_r, s_i = n_r, n_i
                    o_r = jnp.where(rid == rr, s_r, o_r)
                    o_i = jnp.where(rid == rr, s_i, o_i)
                x_ref[pl.ds(base, 8), pl.ds(0, hs)] = o_r
                x_ref[pl.ds(base, 8), pl.ds(hs, hs)] = o_i
                return s_r, s_i

            z = jnp.zeros((1, hs), F32)
            lax.fori_loop(0, nt, tile_step, (z, z))

        @pl.when(d == 0)
        def _():
            run(False)

        @pl.when(d == 1)
        def _():
            run(True)

        cb = cm_ref[...].astype(MXU_DT)
        for r0 in range(0, nrow, rs):
            y_ref[pl.ds(r0, rs), :] = _dot(x_ref[pl.ds(r0, rs), :].astype(MXU_DT), cb, _NN)

    in_specs = [
        pl.BlockSpec((nrow, LANES), lambda d, s: (0, u_blk + s)),
        pl.BlockSpec((None, None, 1, hs), lambda d, s: (d, s, 0, 0)),
        pl.BlockSpec((None, None, 1, hs), lambda d, s: (d, s, 0, 0)),
        pl.BlockSpec((None, None, LANES, 2 * hs), lambda d, s: (d, s, 0, 0)),
        pl.BlockSpec((None, None, 2 * hs, LANES), lambda d, s: (d, s, 0, 0)),
    ]
    out_specs = [
        pl.BlockSpec((None, nrow, LANES), lambda d, s: (d, 0, s)),
        pl.BlockSpec((None, nrow, 2 * hs), lambda d, s: (d, 0, s)),
    ]
    out_shape = [jax.ShapeDtypeStruct((2, nrow, nb * LANES), F32),
                 jax.ShapeDtypeStruct((2, nrow, nb * 2 * hs), F32)]
    return pl.pallas_call(
        body, grid=(2, nb), in_specs=in_specs, out_specs=out_specs, out_shape=out_shape,
        compiler_params=_params(("parallel", "parallel")), name=name)(proj, ar, ai, w, cm)


def s5_bwd(proj, u_blk, ar, ai, w, cm, xs, dy, du_in, *, name):
    nrow = proj.shape[0]
    nb = w.shape[1]
    nt = nrow // 8
    hs = S5_BLOCK_STATE
    rs = min(_S5_ROWS, nrow)

    def body(u_ref, dy_ref, dui_ref, x_ref, ar_ref, ai_ref, w_ref, cm_ref,
             du_ref, dw_ref, dcm_ref, dar_ref, dai_ref, g_ref):
        d = pl.program_id(1)
        cb = cm_ref[...].astype(MXU_DT)
        for r0 in range(0, nrow, rs):
            g_ref[pl.ds(r0, rs), :] = _dot(dy_ref[pl.ds(r0, rs), :].astype(MXU_DT), cb, _NT)
        a_r = ar_ref[...]
        a_i = ai_ref[...]
        rid = lax.broadcasted_iota(jnp.int32, (8, hs), 0)

        def run(descending):
            def tile_step(tt, carry):
                g_r, g_i, acc_r, acc_i = carry
                tidx = nt - 1 - tt if descending else tt
                base = pl.multiple_of(tidx * 8, 8)
                tile = g_ref[pl.ds(base, 8), :]
                o_r = jnp.zeros((8, hs), F32)
                o_i = jnp.zeros((8, hs), F32)
                for q in range(8):
                    rr = 7 - q if descending else q
                    n_r = tile[rr:rr + 1, :hs] + (a_r * g_r + a_i * g_i)
                    n_i = tile[rr:rr + 1, hs:] + (a_r * g_i - a_i * g_r)
                    g_r, g_i = n_r, n_i
                    o_r = jnp.where(rid == rr, g_r, o_r)
                    o_i = jnp.where(rid == rr, g_i, o_i)
                g_ref[pl.ds(base, 8), pl.ds(0, hs)] = o_r
                g_ref[pl.ds(base, 8), pl.ds(hs, hs)] = o_i
                xt = x_ref[pl.ds(base, 8), :]
                if descending:
                    nbase = pl.multiple_of(jnp.maximum(tidx - 1, 0) * 8, 8)
                    edge = x_ref[pl.ds(nbase, 8), :][7:8, :] * jnp.where(tidx > 0, 1.0, 0.0).astype(F32)
                    prev = jnp.where(lax.broadcasted_iota(jnp.int32, xt.shape, 0) == 0, edge, pltpu.roll(xt, 1, 0))
                else:
                    nbase = pl.multiple_of(jnp.minimum(tidx + 1, nt - 1) * 8, 8)
                    edge = x_ref[pl.ds(nbase, 8), :][0:1, :] * jnp.where(tidx < nt - 1, 1.0, 0.0).astype(F32)
                    prev = jnp.where(lax.broadcasted_iota(jnp.int32, xt.shape, 0) == 7, edge, pltpu.roll(xt, 7, 0))
                p_r = prev[:, :hs]
                p_i = prev[:, hs:]
                acc_r = acc_r + o_r * p_r + o_i * p_i
                acc_i = acc_i + o_i * p_r - o_r * p_i
                return g_r, g_i, acc_r, acc_i

            z = jnp.zeros((1, hs), F32)
            z8 = jnp.zeros((8, hs), F32)
            _, _, acc_r, acc_i = lax.fori_loop(0, nt, tile_step, (z, z, z8, z8))
            dar_ref[...] = jnp.sum(acc_r, axis=0, keepdims=True)
            dai_ref[...] = jnp.sum(acc_i, axis=0, keepdims=True)

        @pl.when(d == 0)
        def _():
            run(True)

        @pl.when(d == 1)
        def _():
            run(False)

        wb = w_ref[...].astype(MXU_DT)
        for r0 in range(0, nrow, rs):
            part = _dot(g_ref[pl.ds(r0, rs), :].astype(MXU_DT), wb, _NT)

            @pl.when(d == 0)
            def _(part=part, r0=r0):
                du_ref[pl.ds(r0, rs), :] = dui_ref[pl.ds(r0, rs), :] + part

            @pl.when(d == 1)
            def _(part=part, r0=r0):
                du_ref[pl.ds(r0, rs), :] += part

        dw_ref[...] = _dot(u_ref[...].astype(MXU_DT), g_ref[...].astype(MXU_DT), _TN)
        dcm_ref[...] = _dot(x_ref[...].astype(MXU_DT), dy_ref[...].astype(MXU_DT), _TN)

    in_specs = [
        pl.BlockSpec((nrow, LANES), lambda s, d: (0, u_blk + s)),
        pl.BlockSpec((nrow, LANES), lambda s, d: (0, s)),
        pl.BlockSpec((nrow, LANES), lambda s, d: (0, s)),
        pl.BlockSpec((None, nrow, 2 * hs), lambda s, d: (d, 0, s)),
        pl.BlockSpec((None, None, 1, hs), lambda s, d: (d, s, 0, 0)),
        pl.BlockSpec((None, None, 1, hs), lambda s, d: (d, s, 0, 0)),
        pl.BlockSpec((None, None, LANES, 2 * hs), lambda s, d: (d, s, 0, 0)),
        pl.BlockSpec((None, None, 2 * hs, LANES), lambda s, d: (d, s, 0, 0)),
    ]
    out_specs = [
        pl.BlockSpec((nrow, LANES), lambda s, d: (0, s)),
        pl.BlockSpec((None, None, LANES, 2 * hs), lambda s, d: (d, s, 0, 0)),
        pl.BlockSpec((None, None, 2 * hs, LANES), lambda s, d: (d, s, 0, 0)),
        pl.BlockSpec((None, None, 1, hs), lambda s, d: (d, s, 0, 0)),
        pl.BlockSpec((None, None, 1, hs), lambda s, d: (d, s, 0, 0)),
    ]
    out_shape = [jax.ShapeDtypeStruct((nrow, nb * LANES), F32),
                 jax.ShapeDtypeStruct(w.shape, F32), jax.ShapeDtypeStruct(cm.shape, F32),
                 jax.ShapeDtypeStruct(ar.shape, F32), jax.ShapeDtypeStruct(ai.shape, F32)]
    return pl.pallas_call(
        body, grid=(nb, 2), in_specs=in_specs, out_specs=out_specs, out_shape=out_shape,
        scratch_shapes=[pltpu.VMEM((nrow, 2 * hs), F32)],
        compiler_params=_params(("parallel", "arbitrary")), name=name)(proj, dy, du_in, xs, ar, ai, w, cm)


def _me():
    return lax.axis_index("x"), lax.axis_index("y"), lax.axis_index("c")


def all_gather(shards, *, name):
    na = len(shards)
    hbm = pl.BlockSpec(memory_space=pl.ANY)

    def body(*refs):
        x_refs, out_refs = refs[:na], refs[na:2 * na]
        send_sems, recv_sems, local_sems = refs[2 * na:]
        x, y, c = _me()
        me, sibling = (x, y, c), (x, y, 1 - c)
        chips = [(1 - x, y), (x, 1 - y), (1 - x, 1 - y)]

        def slot(a, px, py, pc):
            return out_refs[a].at[4 * px + 2 * py + pc]

        def copy(a, k, block, to, src=None):
            return pltpu.make_async_remote_copy(
                src_ref=slot(a, *block) if src is None else src, dst_ref=slot(a, *block),
                send_sem=send_sems.at[a, k], recv_sem=recv_sems.at[a, k], device_id=to, device_id_type=MESH)

        mine = [pltpu.make_async_copy(x_refs[a], slot(a, *me), local_sems.at[a]) for a in range(na)]
        for cp in mine:
            cp.start()
        first = []
        for a in range(na):
            first.append(copy(a, 0, me, sibling, src=x_refs[a]))
            first += [copy(a, 1 + j, me, (*chip, c), src=x_refs[a]) for j, chip in enumerate(chips)]
        for cp in first:
            cp.start()
        passed = []
        for j, chip in enumerate(chips):
            for a in range(na):
                copy(a, 1 + j, (*chip, c), me).wait_recv()
                fwd = copy(a, 4 + j, (*chip, c), sibling)
                fwd.start()
                passed.append(fwd)
        for a in range(na):
            copy(a, 0, sibling, me).wait_recv()
            for j, chip in enumerate(chips):
                copy(a, 4 + j, (*chip, 1 - c), me).wait_recv()
        for cp in first + passed:
            cp.wait_send()
        for cp in mine:
            cp.wait()

    return pl.pallas_call(
        body, out_shape=[jax.ShapeDtypeStruct((8,) + s.shape, s.dtype) for s in shards],
        in_specs=[hbm] * na, out_specs=[hbm] * na,
        scratch_shapes=[pltpu.SemaphoreType.DMA((na, 7)), pltpu.SemaphoreType.DMA((na, 7)),
                        pltpu.SemaphoreType.DMA((na,))],
        name=name)(*shards)


_AXES = ("x", "y", "c")


def exchange(bufs, axes, *, half, name):
    na = len(bufs)
    out_sd = [jax.ShapeDtypeStruct(b.shape[1:] if half else b.shape, b.dtype) for b in bufs]
    hbm = pl.BlockSpec(memory_space=pl.ANY)

    def body(*refs):
        in_refs, out_refs = refs[:na], refs[na:2 * na]
        send_sems, recv_sems = refs[2 * na:]
        me = _me()
        cps = []
        for a in range(na):
            bit = me[_AXES.index(axes[a])]
            peer = tuple(1 - v if ax == axes[a] else v for ax, v in zip(_AXES, me))
            cps.append(pltpu.make_async_remote_copy(
                src_ref=in_refs[a].at[1 - bit] if half else in_refs[a], dst_ref=out_refs[a],
                send_sem=send_sems.at[a], recv_sem=recv_sems.at[a], device_id=peer, device_id_type=MESH))
        for cp in cps:
            cp.start()
        for cp in cps:
            cp.wait()

    return pl.pallas_call(
        body, out_shape=out_sd, in_specs=[hbm] * na, out_specs=[hbm] * na,
        scratch_shapes=[pltpu.SemaphoreType.DMA((na,)), pltpu.SemaphoreType.DMA((na,))], name=name)(*bufs)


def all_to_all(pieces, slot_fns, *, name):
    na = len(pieces)
    hbm = pl.BlockSpec(memory_space=pl.ANY)

    def body(*refs):
        in_refs, out_refs = refs[:na], refs[na:2 * na]
        send_sems, recv_sems, local_sems = refs[2 * na:]
        x, y, c = _me()
        cps = []
        for a in range(na):
            mine = slot_fns[a](x, y, c)
            for k in range(8):
                tx, ty, tc = x ^ (k // 4), y ^ ((k // 2) % 2), c ^ (k % 2)
                src = in_refs[a].at[4 * tx + 2 * ty + tc]
                dst = out_refs[a].at[mine]
                if k == 0:
                    cp = pltpu.make_async_copy(src, dst, local_sems.at[a])
                else:
                    cp = pltpu.make_async_remote_copy(
                        src_ref=src, dst_ref=dst, send_sem=send_sems.at[a, k - 1], recv_sem=recv_sems.at[a, k - 1],
                        device_id=(tx, ty, tc), device_id_type=MESH)
                cp.start()
                cps.append(cp)
        for cp in cps:
            cp.wait()

    return pl.pallas_call(
        body, out_shape=[jax.ShapeDtypeStruct(p.shape, p.dtype) for p in pieces],
        in_specs=[hbm] * na, out_specs=[hbm] * na,
        scratch_shapes=[pltpu.SemaphoreType.DMA((na, 7)), pltpu.SemaphoreType.DMA((na, 7)),
                        pltpu.SemaphoreType.DMA((na,))],
        name=name)(*pieces)


def add_half(buf, recv, bit, *, name):
    c = recv.shape[-1]
    r = math.prod(recv.shape[:-1])
    t = _tile(r, max(8, (ADD_BLOCK_BYTES // (4 * c)) // 8 * 8), 8)

    def body(bit_ref, a_ref, b_ref, o_ref):
        o_ref[...] = a_ref[...] + b_ref[...]

    grid_spec = pltpu.PrefetchScalarGridSpec(
        num_scalar_prefetch=1, grid=(r // t,),
        in_specs=[pl.BlockSpec((None, t, c), lambda i, b: (b[0], i, 0)), pl.BlockSpec((t, c), lambda i, b: (i, 0))],
        out_specs=pl.BlockSpec((t, c), lambda i, b: (i, 0)))
    out = pl.pallas_call(body, grid_spec=grid_spec, out_shape=jax.ShapeDtypeStruct((r, c), F32),
                         compiler_params=_params(("parallel",)), name=name)(
                             bit, buf.reshape(2, r, c), recv.reshape(r, c))
    return out.reshape(recv.shape)


def _pack(arrs, cols, mult):
    flat = jnp.concatenate([a.reshape(-1) for a in arrs])
    n = flat.shape[0]
    padded = -(-n // (cols * mult)) * (cols * mult)
    return jnp.pad(flat, (0, padded - n)).reshape(padded // cols, cols)


def _unpack(flat, shapes):
    out, off = [], 0
    for s in shapes:
        n = math.prod(s)
        out.append(flat[off:off + n].reshape(s))
        off += n
    return out


def _elementwise(fn, arrs, nout, name):
    r, c = arrs[0].shape
    t = _tile(r, ROW_TILE, 8)
    return rw_fwd(fn, [rows(a, t) for a in arrs], [(c, False)] * nout, nrow=r, t=t, name=name)


def kernel(x, ln_g, w_in, conv_w, a_log, dt_bias, head_norm_g, lam_re, lam_im, log_dt, b_re, b_im, c_re, c_im, d_skip, w_glu, b_glu, w_pa, w_pb, b_gate, w_out, final_g, loss_target, m_ln_g, m_w_in, m_conv_w, m_a_log, m_dt_bias, m_head_norm_g, m_lam_re, m_lam_im, m_log_dt, m_b_re, m_b_im, m_c_re, m_c_im, m_d_skip, m_w_glu, m_b_glu, m_w_pa, m_w_pb, m_b_gate, m_w_out, m_final_g, v_ln_g, v_w_in, v_conv_w, v_a_log, v_dt_bias, v_head_norm_g, v_lam_re, v_lam_im, v_log_dt, v_b_re, v_b_im, v_c_re, v_c_im, v_d_skip, v_w_glu, v_b_glu, v_w_pa, v_w_pb, v_b_gate, v_w_out, v_final_g):
    env = dict(locals())
    wts = {n: env[n] for n in WEIGHTS}
    mom_m = {n: env["m_" + n] for n in WEIGHTS}
    mom_v = {n: env["v_" + n] for n in WEIGHTS}

    xin = x[0]
    tgt = loss_target[0]
    nrow, dm = xin.shape
    depth = ln_g.shape[0]
    nh = dm // (2 * HEAD_DIM)
    wa = nh * HEAD_DIM
    wb = dm // 2
    ngrp = wb // S5_GROUP_CH
    pw = w_in.shape[-1]
    t = min(ROW_TILE, nrow)

    o_ba = 4 * wa
    o_u = o_ba + 4 * nh
    n_main = 4 * wa + 2 * wb + 2 * dm
    projp = -(-(n_main + LANES) // 512) * 512
    blk_za = 3 * wa // HEAD_DIM
    blk_u = 4 * wa // LANES
    ba_blk = n_main // LANES
    cx, cy, cc = _me()

    g_in, g_glu, g_pa, g_pb, g_out, g_conv = all_gather(
        [w_in.astype(WIRE_DT), w_glu.astype(WIRE_DT), w_pa.astype(WIRE_DT), w_pb.astype(WIRE_DT),
         w_out.astype(WIRE_DT), conv_w], name="gather_weights")

    def cat(g, l):
        return jnp.concatenate([g[j, l] for j in range(8)], axis=1)

    full = []
    for l in range(depth):
        w_full = cat(g_in, l)
        w_perm = jnp.concatenate(
            [w_full[:, :o_ba], w_full[:, o_u:], w_full[:, o_ba:o_u],
             jnp.zeros((dm, projp - n_main - 4 * nh), WIRE_DT)], axis=1)
        full.append(dict(
            w_in=w_perm, w_glu=g_glu[:, l].reshape(wb, wb), w_pa=cat(g_pa, l), w_pb=cat(g_pb, l),
            w_out=g_out[:, l].reshape(dm, dm), conv_w=cat(g_conv, l)))

    def small(l):
        z = jnp.zeros((1, LANES - 4 * nh), F32)
        alog_row = jnp.concatenate([jnp.zeros((1, 2 * nh), F32), a_log[l].reshape(1, 2 * nh), z], axis=1)
        dtb_row = jnp.concatenate([jnp.zeros((1, 2 * nh), F32), dt_bias[l].reshape(1, 2 * nh), z], axis=1)
        return alog_row, dtb_row

    saved = []
    cur = xin
    for l in range(depth):
        fw = full[l]
        (hh,) = rw_fwd(fn_rms, [rows(cur, t), bcast(ln_g[l][None])], [(dm, False)], nrow=nrow, t=t, name="rms_fwd")
        proj = mm(hh, fw["w_in"], name="proj_fwd")
        (qkv,) = rw_fwd(make_fn_prep(nh), [rows(proj, nrow, HEAD_DIM, 0, True), bcast(fw["conv_w"], HEAD_DIM, 0, True)],
                        [(HEAD_DIM, True)], nrow=nrow, t=nrow, ncol=3 * nh, name="prep_fwd")
        alog_row, dtb_row = small(l)
        o_dir, s_hist = gdn_fwd(qkv, proj, ba_blk, alog_row, dtb_row, n_heads=nh, name="gdn_fwd")
        hn_ins = [rows3(o_dir, 0, t, HEAD_DIM), rows3(o_dir, 1, t, HEAD_DIM), rows(proj, t, HEAD_DIM, blk_za, True),
                  bcast(head_norm_g[l][None])]
        (ya_in,) = rw_fwd(fn_headnorm, hn_ins, [(HEAD_DIM, True)], nrow=nrow, t=t, ncol=nh, name="headnorm_fwd")
        mats, mats_vjp = jax.vjp(s5_mats, lam_re[l], lam_im[l], log_dt[l], b_re[l], b_im[l], c_re[l], c_im[l])
        yd, xs = s5_fwd(proj, blk_u, *mats, name="s5_fwd")
        ge_ins = [rows3(yd, 0, t, wb, False), rows3(yd, 1, t, wb, False), rows(proj, t, wb, 4 * wa // wb),
                  bcast(d_skip[l][None])]
        (ys,) = rw_fwd(fn_gelu, ge_ins, [(wb, False)], nrow=nrow, t=t, name="gelu_fwd")
        glu = mm(ys, fw["w_glu"], name="glu_fwd")
        gl_ins = [rows(ys, t), rows(glu, t), rows(proj, t, wb, (4 * wa + wb) // wb), bcast(b_glu[l][None])]
        (yb_in,) = rw_fwd(fn_glu, gl_ins, [(wb, False)], nrow=nrow, t=t, name="glugate_fwd")
        y_a = mm(ya_in, fw["w_pa"], name="pa_fwd")
        y_b = mm(yb_in, fw["w_pb"], name="pb_fwd")
        bg = b_gate[l][None]
        mg_ins = [rows(proj, t, dm, 3), rows(proj, t, dm, 4), rows(y_a, t), rows(y_b, t),
                  bcast(bg, dm, 0), bcast(bg, dm, 1)]
        (merged,) = rw_fwd(fn_merge, mg_ins, [(dm, False)], nrow=nrow, t=t, name="merge_fwd")
        nxt = mm(merged, fw["w_out"], add=cur, name="out_fwd")
        saved.append(dict(x=cur, h=hh, proj=proj, qkv=qkv, o_dir=o_dir, s_hist=s_hist, hn_ins=hn_ins,
                          mats=mats, mats_vjp=mats_vjp, xs=xs, ge_ins=ge_ins, ys=ys, gl_ins=gl_ins,
                          ya_in=ya_in, yb_in=yb_in, mg_ins=mg_ins, merged=merged,
                          alog_row=alog_row, dtb_row=dtb_row))
        cur = nxt

    loss_ins = [rows(cur, t), rows(tgt, t), bcast(final_g[None])]
    (row_loss,) = rw_fwd(fn_loss, loss_ins, [(1, False)], nrow=nrow, t=t, name="loss_fwd")
    ones = jnp.ones((nrow, 1), F32)
    dcur, dfinal = rw_bwd(fn_loss, loss_ins, [[rows(ones, t)]], nrow=nrow, t=t,
                          row_grads=[(0, False)], bc_grads=[2], name="loss_bwd")
    loss = lax.psum(jnp.sum(row_loss), ("x", "y", "c"))

    gsh = {n: [None] * depth for n in SHARDED}
    grep = {n: [None] * depth for n in REPLICATED if n != "final_g"}
    for l in reversed(range(depth)):
        fw, sv = full[l], saved[l]
        dmerged = mm(dcur, fw["w_out"], tb=True, name="out_bwd_x")
        gsh["w_out"][l] = mm(sv["merged"], dcur, ta=True, scatter=("rows", slot_cxy), name="out_bwd_w")
        dla, dlb, dya, dyb, dbga, dbgb = rw_bwd(
            fn_merge, sv["mg_ins"], [[rows(dmerged, t)]], nrow=nrow, t=t,
            row_grads=[(0, False), (1, False), (2, False), (3, False)], bc_grads=[4, 5], name="merge_bwd")
        grep["b_gate"][l] = jnp.concatenate([dbga.reshape(dm), dbgb.reshape(dm)])
        dya_in = mm(dya, fw["w_pa"], tb=True, name="pa_bwd_x")
        gsh["w_pa"][l] = mm(sv["ya_in"], dya, ta=True, scatter=("cols", slot_cyx), name="pa_bwd_w")
        dyb_in = mm(dyb, fw["w_pb"], tb=True, name="pb_bwd_x")
        gsh["w_pb"][l] = mm(sv["yb_in"], dyb, ta=True, scatter=("cols", slot_cyx), name="pb_bwd_w")
        dys1, dglu, dzb, dbglu = rw_bwd(
            fn_glu, sv["gl_ins"], [[rows(dyb_in, t)]], nrow=nrow, t=t,
            row_grads=[(0, False), (1, False), (2, False)], bc_grads=[3], name="glugate_bwd")
        grep["b_glu"][l] = dbglu.reshape(wb)
        dys2 = mm(dglu, fw["w_glu"], tb=True, name="glu_bwd_x")
        gsh["w_glu"][l] = mm(sv["ys"], dglu, ta=True, scatter=("rows", slot_cxy), name="glu_bwd_w")
        dyd, du1, ddskip = rw_bwd(
            fn_gelu, sv["ge_ins"], [[rows(dys1, t), rows(dys2, t)]], nrow=nrow, t=t,
            row_grads=[(0, False), (2, False)], bc_grads=[3], name="gelu_bwd")
        grep["d_skip"][l] = ddskip.reshape(wb)
        du, dw_s5, dcm_s5, dar, dai = s5_bwd(sv["proj"], blk_u, *sv["mats"], sv["xs"], dyd, du1, name="s5_bwd")
        g_lr, g_li, g_ldt, g_br, g_bi, g_cr, g_ci = sv["mats_vjp"]((dar, dai, dw_s5, dcm_s5))
        for nme, val in zip(["lam_re", "lam_im", "log_dt", "b_re", "b_im", "c_re", "c_im"],
                            [g_lr, g_li, g_ldt, g_br, g_bi, g_cr, g_ci]):
            grep[nme][l] = val
        do, dza, dhn = rw_bwd(fn_headnorm, sv["hn_ins"], [[rows(dya_in, t, HEAD_DIM, 0, True)]], nrow=nrow, t=t,
                              ncol=nh, row_grads=[(0, True), (2, True)], bc_grads=[3], name="headnorm_bwd")
        grep["head_norm_g"][l] = jnp.sum(dhn, axis=0).reshape(HEAD_DIM)
        dqkv, dba_all, dal, ddt = gdn_bwd(sv["qkv"], sv["proj"], ba_blk, sv["alog_row"], sv["dtb_row"],
                                          sv["s_hist"], do, n_heads=nh, name="gdn_bwd")
        grep["a_log"][l] = jnp.sum(dal, axis=(0, 1))[2 * nh:4 * nh].reshape(2, nh)
        grep["dt_bias"][l] = jnp.sum(ddt, axis=(0, 1))[2 * nh:4 * nh].reshape(2, nh)
        (dba,) = rw_fwd(make_fn_sum(2), [rows3(dba_all, p, t, LANES, False) for p in range(2)],
                        [(LANES, False)], nrow=nrow, t=t, name="dba_sum")
        prep_ins = [rows(sv["proj"], nrow, HEAD_DIM, 0, True), bcast(fw["conv_w"], HEAD_DIM, 0, True)]
        dq_cots = [(dqkv.reshape(2, 3 * nrow, wa), (None, nrow, HEAD_DIM),
                    (lambda j, i, dd=dd: (dd, j // nh, j % nh))) for dd in range(2)]
        dqkv_raw, dconv = rw_bwd(make_fn_prep(nh), prep_ins, [dq_cots], nrow=nrow, t=nrow, ncol=3 * nh,
                                 row_grads=[(0, True)], bc_grads=[1], name="prep_bwd")
        gsh["conv_w"][l] = jnp.transpose(dconv, (1, 0, 2)).reshape(CONV_K, 3 * wa)
        dproj = jnp.concatenate([dqkv_raw, dza, du, dzb, dla, dlb, dba,
                                 jnp.zeros((nrow, projp - n_main - LANES), F32)], axis=1)
        dh = mm(dproj, fw["w_in"], tb=True, name="proj_bwd_x")
        gsh["w_in"][l] = [mm(sv["h"], dproj, ta=True, m_part=(part, 2), name="proj_bwd_w") for part in range(2)]
        dcur, dlng = rw_bwd(fn_rms, [rows(sv["x"], t), bcast(ln_g[l][None])], [[rows(dh, t)]], nrow=nrow, t=t,
                            row_grads=[(0, False)], bc_grads=[1], residual=rows(dcur, t), name="rms_bwd")
        grep["ln_g"][l] = dlng.reshape(dm)
    grad_x = dcur[None]

    coord = {"x": cx, "y": cy, "c": cc}
    routes = {"cxy": ("c", "x", "y"), "cyx": ("c", "y", "x")}
    conv_order = [4 * ((q // 2) % 2) + 2 * (q % 2) + q // 4 for q in range(8)]
    cw = 3 * wa // 8

    def own_columns(z):
        un = jnp.concatenate([z[:, :o_ba], z[:, n_main:n_main + 4 * nh], z[:, o_ba:n_main]], axis=1)
        return jnp.stack([un[:, d * pw:(d + 1) * pw] for d in range(8)])

    g_final = {n: [None] * depth for n in SHARDED}
    for l in range(depth):
        in_a, in_b = gsh["w_in"][l]
        conv_buf = jnp.stack([gsh["conv_w"][l][:, d * cw:(d + 1) * cw] for d in conv_order])
        bufs = [in_a.reshape(8, dm // 16, projp), in_b.reshape(8, dm // 16, projp), gsh["w_out"][l],
                gsh["w_glu"][l], gsh["w_pa"][l], gsh["w_pb"][l], conv_buf]
        route = ["cxy", "cyx", "cxy", "cxy", "cyx", "cyx", "cxy"]
        for rnd in range(3):
            axes = [routes[r][rnd] for r in route]
            bufs = [b.reshape((2, b.shape[0] // 2) + b.shape[1:]) for b in bufs]
            recv = exchange(bufs, axes, half=True, name=f"rs_exchange_{rnd}")
            bufs = [add_half(b, r, coord[ax].astype(jnp.int32).reshape(1), name=f"rs_add_{rnd}")
                    for b, r, ax in zip(bufs, recv, axes)]
        bufs = [b.reshape(b.shape[1:]) for b in bufs]
        got_a, got_b = all_to_all(
            [own_columns(bufs[0]), own_columns(bufs[1])],
            [lambda x, y, c: 4 * c + 2 * x + y, lambda x, y, c: 4 * c + 2 * y + x], name="w_in_all_to_all")
        g_final["w_in"][l] = jnp.concatenate([got_a.reshape(dm // 2, pw), got_b.reshape(dm // 2, pw)], axis=0)
        for n, b in zip(["w_out", "w_glu", "w_pa", "w_pb", "conv_w"], bufs[2:]):
            g_final[n][l] = b
    grads = {n: jnp.stack(g_final[n]) for n in SHARDED}

    rep_list = [jnp.stack(grep[n]) for n in REPLICATED if n != "final_g"] + [dfinal.reshape(dm)]
    rep_shapes = [a.shape for a in rep_list]
    rbuf = _pack(rep_list, COMM_COLS, ROW_TILE)
    for axis in ("c", "x", "y"):
        (recv,) = exchange([rbuf], [axis], half=False, name="ar_exchange_" + axis)
        (rbuf,) = _elementwise(make_fn_sum(2), [rbuf, recv], 1, "ar_add_" + axis)
    for n, val in zip(REPLICATED, _unpack(rbuf.reshape(-1), rep_shapes)):
        grads[n] = val

    deltas, new_m, new_v = {}, {}, {}
    big = ["w_in", "w_glu", "w_pa", "w_pb", "w_out"]
    for n in big:
        shp = wts[n].shape
        two = [a.reshape(-1, shp[-1]) for a in (wts[n], grads[n], mom_m[n], mom_v[n])]
        d_, m_, v_ = _elementwise(fn_adamw, two, 3, "adamw_" + n)
        deltas[n], new_m[n], new_v[n] = d_.reshape(shp), m_.reshape(shp), v_.reshape(shp)
    rest = [n for n in WEIGHTS if n not in big]
    rest_shapes = [wts[n].shape for n in rest]
    packed = [_pack([src[n] for n in rest], COMM_COLS, ROW_TILE) for src in (wts, grads, mom_m, mom_v)]
    outs = _elementwise(fn_adamw, packed, 3, "adamw_small")
    for dst, arr in zip((deltas, new_m, new_v), outs):
        for n, val in zip(rest, _unpack(arr.reshape(-1), rest_shapes)):
            dst[n] = val

    return (loss, grad_x, *[grads[n] for n in WEIGHTS], *[deltas[n] for n in WEIGHTS],
            *[new_m[n] for n in WEIGHTS], *[new_v[n] for n in WEIGHTS])
```

```python
import functools
import math

import jax
import jax.numpy as jnp
from jax import lax
from jax.experimental import pallas as pl
from jax.experimental.pallas import tpu as pltpu

F32 = jnp.float32
MXU_DT = jnp.bfloat16
WIRE_DT = jnp.bfloat16

HEAD_DIM = 128
CHUNK = 64
CONV_K = 5
S5_GROUP_CH = 16
S5_STATE = 64
S5_BLOCK_GROUPS = 8
S5_BLOCK_STATE = S5_BLOCK_GROUPS * S5_STATE
RMS_EPS = 1e-6
LANES = 128
VMEM_LIMIT = 56 * 1024 * 1024
ROW_TILE = 256
COMM_COLS = 1024
ADD_BLOCK_BYTES = 2 * 1024 * 1024
RS_STAGES = 4

ADAM_LR = 0.001
ADAM_B1 = 0.9
ADAM_B2 = 0.999
ADAM_EPS = 1e-08
ADAM_WD = 0.01
ADAM_STEP = 10

WEIGHTS = ['ln_g', 'w_in', 'conv_w', 'a_log', 'dt_bias', 'head_norm_g', 'lam_re', 'lam_im', 'log_dt',
           'b_re', 'b_im', 'c_re', 'c_im', 'd_skip', 'w_glu', 'b_glu', 'w_pa', 'w_pb', 'b_gate',
           'w_out', 'final_g']
SHARDED = ['w_in', 'w_glu', 'w_pa', 'w_pb', 'w_out', 'conv_w']
REPLICATED = [n for n in WEIGHTS if n not in SHARDED]
MESH = pl.DeviceIdType.MESH


def _params(sem=None):
    return pltpu.CompilerParams(dimension_semantics=sem, vmem_limit_bytes=VMEM_LIMIT)


def _tile(n, cap, q=LANES):
    t = (min(n, cap) // q) * q
    while t > q and n % t:
        t -= q
    return t if t > 0 and n % t == 0 else n


class Side:
    def __init__(self, ins, out_sd, sems, start, finish):
        self.ins, self.out_sd, self.sems, self.start, self.finish = list(ins), list(out_sd), list(sems), start, finish
        self.result = None


def _call(body, *, grid, in_specs, out_specs, out_shape, scratch_shapes=(), sem, name, args, side=None):
    in_specs, out_specs, out_shape = list(in_specs), list(out_specs), list(out_shape)
    scratch_shapes = list(scratch_shapes)
    if side is None:
        return pl.pallas_call(body, grid=grid, in_specs=in_specs, out_specs=out_specs, out_shape=out_shape,
                              scratch_shapes=scratch_shapes, compiler_params=_params(sem), name=name)(*args)
    hbm = pl.BlockSpec(memory_space=pl.ANY)
    n_in, n_out, n_scr = len(in_specs), len(out_specs), len(scratch_shapes)
    s_in, s_out = len(side.ins), len(side.out_sd)

    def hosted(*refs):
        main_in, rest = refs[:n_in], refs[n_in:]
        side_in, rest = rest[:s_in], rest[s_in:]
        main_out, rest = rest[:n_out], rest[n_out:]
        side_out, rest = rest[:s_out], rest[s_out:]
        main_scr, sems = rest[:n_scr], rest[n_scr:]
        ids = [pl.program_id(k) for k in range(len(grid))]
        first = functools.reduce(jnp.logical_and, [i == 0 for i in ids])
        last = functools.reduce(jnp.logical_and, [i == g - 1 for i, g in zip(ids, grid)])

        @pl.when(first)
        def _():
            side.start(side_in, side_out, sems)

        body(*main_in, *main_out, *main_scr)

        @pl.when(last)
        def _():
            side.finish(side_in, side_out, sems)

    outs = pl.pallas_call(
        hosted, grid=grid, in_specs=in_specs + [hbm] * s_in, out_specs=out_specs + [hbm] * s_out,
        out_shape=out_shape + side.out_sd, scratch_shapes=scratch_shapes + side.sems,
        compiler_params=_params(("arbitrary",) * len(grid)), name=name)(*args, *side.ins)
    side.result = list(outs[n_out:])
    return list(outs[:n_out])


def run_side(side, *, name):
    hbm = pl.BlockSpec(memory_space=pl.ANY)
    s_in, s_out = len(side.ins), len(side.out_sd)

    def body(*refs):
        side_in, side_out, sems = refs[:s_in], refs[s_in:s_in + s_out], refs[s_in + s_out:]
        side.start(side_in, side_out, sems)
        side.finish(side_in, side_out, sems)

    side.result = list(pl.pallas_call(body, out_shape=side.out_sd, in_specs=[hbm] * s_in, out_specs=[hbm] * s_out,
                                      scratch_shapes=side.sems, name=name)(*side.ins))
    return side.result


def slot_cxy(d):
    return 4 * (d % 2) + 2 * (d // 4) + (d // 2) % 2


def slot_cyx(d):
    return 4 * (d % 2) + 2 * ((d // 2) % 2) + d // 4


def mm(a, b, *, ta=False, tb=False, add=None, m_part=None, scatter=None, side=None, name):
    m = a.shape[1] if ta else a.shape[0]
    k = a.shape[0] if ta else a.shape[1]
    n = b.shape[0] if tb else b.shape[1]
    m_off = 0
    if m_part is not None:
        m = m // m_part[1]
        m_off = m_part[0]
    tm, tn, tk = _tile(m, 1024), _tile(n, 512), _tile(k, 2048)
    if scatter is not None and scatter[0] == "rows":
        tm = m // 8
    if scatter is not None and scatter[0] == "cols":
        tn = n // 8
    if m_part is not None:
        assert tm == m
    nk = k // tk
    dn = (((0 if ta else 1,), (1 if tb else 0,)), ((), ()))

    def body(*refs):
        if add is None:
            a_ref, b_ref, o_ref, acc = refs
        else:
            a_ref, b_ref, add_ref, o_ref, acc = refs
        kk = pl.program_id(2)

        @pl.when(kk == 0)
        def _():
            acc[...] = jnp.zeros_like(acc)

        acc[...] += lax.dot_general(a_ref[...].astype(MXU_DT), b_ref[...].astype(MXU_DT), dn,
                                    preferred_element_type=F32)

        @pl.when(kk == nk - 1)
        def _():
            r = acc[...]
            if add is not None:
                r = r + add_ref[...]
            o_ref[...] = r

    if ta:
        a_spec = pl.BlockSpec((tk, tm), lambda i, j, kk: (kk, i + m_off))
    else:
        a_spec = pl.BlockSpec((tm, tk), lambda i, j, kk: (i + m_off, kk))
    b_spec = pl.BlockSpec((tn, tk), lambda i, j, kk: (j, kk)) if tb else pl.BlockSpec((tk, tn), lambda i, j, kk: (kk, j))
    out_sd = jax.ShapeDtypeStruct((m, n), F32)
    if scatter is None:
        o_spec = pl.BlockSpec((tm, tn), lambda i, j, kk: (i, j))
    elif scatter[0] == "rows":
        o_spec = pl.BlockSpec((None, tm, tn), lambda i, j, kk: (scatter[1](i), 0, j))
        out_sd = jax.ShapeDtypeStruct((8, tm, n), F32)
    else:
        o_spec = pl.BlockSpec((None, tm, tn), lambda i, j, kk: (scatter[1](j), i, 0))
        out_sd = jax.ShapeDtypeStruct((8, m, tn), F32)
    ins, specs = [a, b], [a_spec, b_spec]
    if add is not None:
        ins.append(add)
        specs.append(o_spec)
    return _call(body, grid=(m // tm, n // tn, nk), in_specs=specs, out_specs=[o_spec], out_shape=[out_sd],
                 scratch_shapes=[pltpu.VMEM((tm, tn), F32)], sem=("parallel", "parallel", "arbitrary"),
                 name=name, args=ins, side=side)[0]


def rows(arr, t, width=None, base=0, per_j=False):
    width = arr.shape[1] if width is None else width
    return (arr, (t, width), lambda j, i: (i, base + (j if per_j else 0)))


def rows3(arr, lead, t, width, per_j=True):
    return (arr, (None, t, width), lambda j, i: (lead, i, j if per_j else 0))


def bcast(arr, width=None, base=0, per_j=False):
    width = arr.shape[1] if width is None else width
    return (arr, (arr.shape[0], width), lambda j, i: (0, base + (j if per_j else 0)))


def _specs(items):
    return [pl.BlockSpec(bs, im) for (_, bs, im) in items]


def rw_fwd(fn, ins, outs, *, nrow, t, ncol=1, name):
    out_specs = [pl.BlockSpec((t, w), (lambda j, i: (i, j)) if pj else (lambda j, i: (i, 0))) for (w, pj) in outs]
    out_shape = [jax.ShapeDtypeStruct((nrow, w * (ncol if pj else 1)), F32) for (w, pj) in outs]
    nin = len(ins)

    def body(*refs):
        j = pl.program_id(0)
        res = fn(j, *[r[...] for r in refs[:nin]])
        for o_ref, r in zip(refs[nin:], res):
            o_ref[...] = r

    return pl.pallas_call(
        body, grid=(ncol, nrow // t), in_specs=_specs(ins), out_specs=out_specs, out_shape=out_shape,
        compiler_params=_params(("parallel", "parallel")), name=name)(*[x[0] for x in ins])


def rw_bwd(fn, ins, cots, *, nrow, t, ncol=1, row_grads, bc_grads, residual=None, side=None, name):
    nin = len(ins)
    flat_cots = [c for group in cots for c in group]
    extra = [residual] if residual is not None else []
    out_specs, out_shape = [], []
    for idx, pj in row_grads:
        w = ins[idx][1][-1]
        out_specs.append(pl.BlockSpec((t, w), (lambda j, i: (i, j)) if pj else (lambda j, i: (i, 0))))
        out_shape.append(jax.ShapeDtypeStruct((nrow, w * (ncol if pj else 1)), F32))
    for idx in bc_grads:
        r, w = ins[idx][1]
        out_specs.append(pl.BlockSpec((None, r, w), lambda j, i: (j, 0, 0)))
        out_shape.append(jax.ShapeDtypeStruct((ncol, r, w), F32))

    def body(*refs):
        j = pl.program_id(0)
        i = pl.program_id(1)
        vals = [r[...] for r in refs[:nin]]
        pos = nin
        cts = []
        for group in cots:
            c = refs[pos][...]
            for q in range(1, len(group)):
                c = c + refs[pos + q][...]
            pos += len(group)
            cts.append(c)
        res_ref = refs[pos] if residual is not None else None
        pos += len(extra)
        outs = refs[pos:]
        _, vjp = jax.vjp(lambda *a: tuple(fn(j, *a)), *vals)
        grads = vjp(tuple(cts))
        for q, (idx, _) in enumerate(row_grads):
            g = grads[idx]
            if q == 0 and res_ref is not None:
                g = g + res_ref[...]
            outs[q][...] = g
        for q, idx in enumerate(bc_grads):
            o_ref = outs[len(row_grads) + q]

            @pl.when(i == 0)
            def _(o_ref=o_ref):
                o_ref[...] = jnp.zeros_like(o_ref)

            o_ref[...] += grads[idx]

    all_in = list(ins) + flat_cots + extra
    return _call(body, grid=(ncol, nrow // t), in_specs=_specs(all_in), out_specs=out_specs, out_shape=out_shape,
                 sem=("parallel", "arbitrary"), name=name, args=[x[0] for x in all_in], side=side)


def _silu(x):
    return x * jax.nn.sigmoid(x)


@jax.custom_vjp
def _softplus(x):
    return jnp.maximum(x, 0.0) + jnp.log1p(jnp.exp(-jnp.abs(x)))


def _softplus_fwd(x):
    return _softplus(x), x


def _softplus_bwd(x, ct):
    return (ct * jax.nn.sigmoid(x),)


_softplus.defvjp(_softplus_fwd, _softplus_bwd)


def _gelu(x):
    return 0.5 * x * (1.0 + jnp.tanh(math.sqrt(2.0 / math.pi) * (x + 0.044715 * (x * x * x))))


def _row_shift_impl(x, s):
    n = x.shape[0]
    if s == 0:
        return x
    rolled = pltpu.roll(x, (-s) % n, 0)
    t = lax.broadcasted_iota(jnp.int32, x.shape, 0)
    ok = (t + s >= 0) & (t + s < n)
    return jnp.where(ok, rolled, 0.0)


@functools.partial(jax.custom_vjp, nondiff_argnums=(1,))
def _row_shift(x, s):
    return _row_shift_impl(x, s)


def _row_shift_fwd(x, s):
    return _row_shift_impl(x, s), None


def _row_shift_bwd(s, _, ct):
    return (_row_shift_impl(ct, -s),)


_row_shift.defvjp(_row_shift_fwd, _row_shift_bwd)


def fn_rms(j, x, g):
    return (x * lax.rsqrt(jnp.mean(x * x, axis=-1, keepdims=True) + RMS_EPS) * g,)


def make_fn_prep(n_heads):
    pad = (CONV_K - 1) // 2

    def fn_prep(j, x, w):
        y = _row_shift(x, -pad) * w[0:1, :]
        for i in range(1, CONV_K):
            y = y + _row_shift(x, i - pad) * w[i:i + 1, :]
        a = _silu(y)
        scale = jnp.where(j < n_heads, HEAD_DIM ** -0.5, 1.0).astype(F32)
        nrm = a * lax.rsqrt(jnp.sum(a * a, axis=-1, keepdims=True) + RMS_EPS) * scale
        return (jnp.where(j < 2 * n_heads, nrm, a),)

    return fn_prep


def fn_headnorm(j, o0, o1, z, g):
    o = o0 + o1
    n = o * lax.rsqrt(jnp.mean(o * o, axis=-1, keepdims=True) + RMS_EPS) * g
    return (n * _silu(z),)


def fn_gelu(j, y0, y1, u, dsk):
    return (_gelu(y0 + y1 + u * dsk),)


def fn_glu(j, ys, logit, z, b):
    return (ys * jax.nn.sigmoid(logit + b) * _silu(z),)


def fn_merge(j, la, lb, ya, yb, ba, bb):
    return (jax.nn.sigmoid(la + ba) * ya + jax.nn.sigmoid(lb + bb) * yb,)


def fn_loss(j, x, t, g):
    y = x * lax.rsqrt(jnp.mean(x * x, axis=-1, keepdims=True) + RMS_EPS) * g
    e = y - t
    return (0.5 * jnp.mean(e * e, axis=-1, keepdims=True),)


def make_fn_sum(n):
    def fn_sum(j, *xs):
        s = xs[0]
        for q in range(1, n):
            s = s + xs[q]
        return (s,)

    return fn_sum


def fn_adamw(j, w, g, m, v):
    m2 = ADAM_B1 * m + (1.0 - ADAM_B1) * g
    v2 = ADAM_B2 * v + (1.0 - ADAM_B2) * (g * g)
    m_hat = m2 / (1.0 - ADAM_B1 ** ADAM_STEP)
    v_hat = v2 / (1.0 - ADAM_B2 ** ADAM_STEP)
    delta = -ADAM_LR * (m_hat / (jnp.sqrt(v_hat) + ADAM_EPS) + ADAM_WD * w)
    return delta, m2, v2


def _dot(a, b, dims, precision=None):
    return lax.dot_general(a, b, (dims, ((), ())), precision=precision, preferred_element_type=F32)


_NN = ((1,), (0,))
_NT = ((1,), (1,))
_TN = ((0,), (0,))
GDN_HEAD_BLOCK = 4


def _split(a):
    hi = a.astype(MXU_DT)
    return hi, (a - hi.astype(F32)).astype(MXU_DT)


def _dot3(a, b, dims):
    ah, al = _split(a)
    bh, bl = _split(b)
    return _dot(ah, bh, dims) + (_dot(al, bh, dims) + _dot(ah, bl, dims))


def _unit_inverse(lmat):
    r = lmat.shape[0]
    eye = (lax.broadcasted_iota(jnp.int32, (r, r), 0) == lax.broadcasted_iota(jnp.int32, (r, r), 1)).astype(F32)
    pw = -lmat
    tinv = eye + pw
    for _ in range(int(math.ceil(math.log2(CHUNK))) - 1):
        pw = _dot3(pw, pw, _NN)
        tinv = tinv + _dot3(tinv, pw, _NN)
    return tinv


@jax.custom_vjp
def tri_solve(lmat, rhs):
    return _dot3(_unit_inverse(lmat), rhs, _NN)


def _tri_solve_fwd(lmat, rhs):
    tinv = _unit_inverse(lmat)
    x = _dot3(tinv, rhs, _NN)
    return x, (tinv, x)


def _tri_solve_bwd(res, dx):
    tinv, x = res
    drhs = _dot3(tinv, dx, _TN)
    return -_dot3(drhs, x, _NT), drhs


tri_solve.defvjp(_tri_solve_fwd, _tri_solve_bwd)


def gdn_group(s, q, k, v, ba, alog, dtb, *, d, head0, n_heads):
    hb = len(s)
    c = q.shape[0]
    r = hb * c

    def stack(x):
        return jnp.concatenate([x[:, i * HEAD_DIM:(i + 1) * HEAD_DIM] for i in range(hb)], axis=0)

    def pick(x, lane0):
        lane = lax.broadcasted_iota(jnp.int32, x.shape, 1)
        return jnp.concatenate(
            [jnp.sum(jnp.where(lane == lane0 + i, x, 0.0), axis=1, keepdims=True) for i in range(hb)], axis=0)

    q4, k4, v4 = stack(q), stack(k), stack(v)
    beta = pick(jax.nn.sigmoid(ba), d * n_heads + head0)
    g = pick(-jnp.exp(alog) * _softplus(ba + dtb), 2 * n_heads + d * n_heads + head0)
    ii = lax.broadcasted_iota(jnp.int32, (r, r), 0)
    jj = lax.broadcasted_iota(jnp.int32, (r, r), 1)
    same = (ii // c) == (jj // c)
    rel = (ii - jj) * (1 - 2 * d)
    incl = same & (rel >= 0)
    strict = same & (rel > 0)
    incl_t = same & (rel <= 0)
    g_row = jnp.sum(jnp.where(ii == jj, g, 0.0), axis=0, keepdims=True)
    gc_col = jnp.sum(jnp.where(incl, g_row, 0.0), axis=1, keepdims=True)
    gc_row = jnp.sum(jnp.where(incl_t, g, 0.0), axis=0, keepdims=True)
    g_tot = jnp.sum(jnp.where(same, g_row, 0.0), axis=1, keepdims=True)
    decay = jnp.exp(jnp.where(incl, gc_col - gc_row, -1e30))
    kb = k4 * beta
    vb = v4 * beta
    lmat = jnp.where(strict, _dot(kb, k4, _NT) * decay, 0.0)
    uw = tri_solve(lmat, jnp.concatenate([vb, kb * jnp.exp(gc_col)], axis=1))
    u, w = uw[:, :HEAD_DIM], uw[:, HEAD_DIM:]
    qk = _dot(q4, k4, _NT) * decay
    qe = q4 * jnp.exp(gc_col)
    kd = k4 * jnp.exp(g_tot - gc_col)
    eg = jnp.exp(g_tot)
    v_new, o_s = [], []
    for i in range(hb):
        rs = slice(i * c, (i + 1) * c)
        ws = _dot(jnp.concatenate([w[rs], qe[rs]], axis=0), s[i], _NN)
        v_new.append(u[rs] - ws[:c])
        o_s.append(ws[c:])
    o4 = jnp.concatenate(o_s, axis=0) + _dot(qk, jnp.concatenate(v_new, axis=0), _NN)
    s_new = tuple(s[i] * eg[i * c:i * c + 1, :] + _dot(kd[i * c:(i + 1) * c], v_new[i], _TN) for i in range(hb))
    o = jnp.concatenate([o4[i * c:(i + 1) * c] for i in range(hb)], axis=1)
    return s_new, o


def _gdn_maps(n_chunks):
    def chunk_of(d, step):
        return step + d * (n_chunks - 1 - 2 * step)
    return chunk_of


def _heads(ref, h0, n):
    return ref[:, h0 * HEAD_DIM:(h0 + n) * HEAD_DIM]


def gdn_fwd(qkv, proj, ba_blk, alog_row, dtb_row, *, n_heads, side=None, name):
    nrow = qkv.shape[0]
    nc = nrow // CHUNK
    h = n_heads
    wa = h * HEAD_DIM
    hb = math.gcd(h, GDN_HEAD_BLOCK)
    chunk_of = _gdn_maps(nc)

    def body(q_ref, k_ref, v_ref, ba_ref, al_ref, dt_ref, o_ref, sh_ref, s_scr):
        d = pl.program_id(0)
        n = pl.program_id(1)

        @pl.when(n == 0)
        def _():
            s_scr[...] = jnp.zeros_like(s_scr)

        ba, al, dt = ba_ref[...], al_ref[...], dt_ref[...]
        for h0 in range(0, h, hb):
            s = tuple(s_scr[h0 + i] for i in range(hb))
            for i in range(hb):
                sh_ref[h0 + i] = s[i]
            s2, o = gdn_group(s, _heads(q_ref, h0, hb), _heads(k_ref, h0, hb), _heads(v_ref, h0, hb), ba, al, dt,
                              d=d, head0=h0, n_heads=h)
            o_ref[:, h0 * HEAD_DIM:(h0 + hb) * HEAD_DIM] = o
            for i in range(hb):
                s_scr[h0 + i] = s2[i]

    blk = (CHUNK, wa)
    in_specs = [
        pl.BlockSpec(blk, lambda d, n: (chunk_of(d, n), 0)),
        pl.BlockSpec(blk, lambda d, n: (chunk_of(d, n), 1)),
        pl.BlockSpec(blk, lambda d, n: (chunk_of(d, n), 2)),
        pl.BlockSpec((CHUNK, LANES), lambda d, n: (chunk_of(d, n), ba_blk)),
        pl.BlockSpec((1, LANES), lambda d, n: (0, 0)),
        pl.BlockSpec((1, LANES), lambda d, n: (0, 0)),
    ]
    out_specs = [
        pl.BlockSpec((None, CHUNK, wa), lambda d, n: (d, chunk_of(d, n), 0)),
        pl.BlockSpec((h, None, HEAD_DIM, HEAD_DIM), lambda d, n: (d, n, 0, 0)),
    ]
    out_shape = [jax.ShapeDtypeStruct((2, nrow, wa), F32),
                 jax.ShapeDtypeStruct((2 * h, nc, HEAD_DIM, HEAD_DIM), F32)]
    return _call(body, grid=(2, nc), in_specs=in_specs, out_specs=out_specs, out_shape=out_shape,
                 scratch_shapes=[pltpu.VMEM((h, HEAD_DIM, HEAD_DIM), F32)], sem=("parallel", "arbitrary"),
                 name=name, args=(qkv, qkv, qkv, proj, alog_row, dtb_row), side=side)


def gdn_bwd(qkv, proj, ba_blk, alog_row, dtb_row, s_hist, do, *, n_heads, side=None, name):
    nrow = qkv.shape[0]
    nc = nrow // CHUNK
    h = n_heads
    wa = h * HEAD_DIM
    hb = math.gcd(h, GDN_HEAD_BLOCK)
    chunk_of = _gdn_maps(nc)

    def cb(d, n):
        return chunk_of(d, nc - 1 - n)

    def body(q_ref, k_ref, v_ref, ba_ref, al_ref, dt_ref, sh_ref, do_ref,
             dqkv_ref, dba_ref, dal_ref, ddt_ref, ds_scr):
        d = pl.program_id(0)
        n = pl.program_id(1)

        @pl.when(n == 0)
        def _():
            ds_scr[...] = jnp.zeros_like(ds_scr)
            dal_ref[...] = jnp.zeros_like(dal_ref)
            ddt_ref[...] = jnp.zeros_like(ddt_ref)

        ba, al, dt = ba_ref[...], al_ref[...], dt_ref[...]
        dba_sum = jnp.zeros_like(ba)
        dal_sum = jnp.zeros_like(al)
        ddt_sum = jnp.zeros_like(dt)
        for h0 in range(0, h, hb):
            f = functools.partial(gdn_group, d=d, head0=h0, n_heads=h)
            s = tuple(sh_ref[h0 + i] for i in range(hb))
            _, vjp = jax.vjp(f, s, _heads(q_ref, h0, hb), _heads(k_ref, h0, hb), _heads(v_ref, h0, hb), ba, al, dt)
            ds, dq, dk, dv, dba, dal, ddt = vjp((tuple(ds_scr[h0 + i] for i in range(hb)), _heads(do_ref, h0, hb)))
            for i in range(hb):
                ds_scr[h0 + i] = ds[i]
            cols = slice(h0 * HEAD_DIM, (h0 + hb) * HEAD_DIM)
            dqkv_ref[0, :, cols] = dq
            dqkv_ref[1, :, cols] = dk
            dqkv_ref[2, :, cols] = dv
            dba_sum = dba_sum + dba
            dal_sum = dal_sum + dal
            ddt_sum = ddt_sum + ddt
        dba_ref[...] = dba_sum
        dal_ref[...] += dal_sum
        ddt_ref[...] += ddt_sum

    blk = (CHUNK, wa)
    in_specs = [
        pl.BlockSpec(blk, lambda d, n: (cb(d, n), 0)),
        pl.BlockSpec(blk, lambda d, n: (cb(d, n), 1)),
        pl.BlockSpec(blk, lambda d, n: (cb(d, n), 2)),
        pl.BlockSpec((CHUNK, LANES), lambda d, n: (cb(d, n), ba_blk)),
        pl.BlockSpec((1, LANES), lambda d, n: (0, 0)),
        pl.BlockSpec((1, LANES), lambda d, n: (0, 0)),
        pl.BlockSpec((h, None, HEAD_DIM, HEAD_DIM), lambda d, n: (d, nc - 1 - n, 0, 0)),
        pl.BlockSpec(blk, lambda d, n: (cb(d, n), 0)),
    ]
    out_specs = [
        pl.BlockSpec((None, 3, CHUNK, wa), lambda d, n: (d, 0, cb(d, n), 0)),
        pl.BlockSpec((None, CHUNK, LANES), lambda d, n: (d, cb(d, n), 0)),
        pl.BlockSpec((None, 1, LANES), lambda d, n: (d, 0, 0)),
        pl.BlockSpec((None, 1, LANES), lambda d, n: (d, 0, 0)),
    ]
    out_shape = [jax.ShapeDtypeStruct((2, 3, nrow, wa), F32),
                 jax.ShapeDtypeStruct((2, nrow, LANES), F32),
                 jax.ShapeDtypeStruct((2, 1, LANES), F32),
                 jax.ShapeDtypeStruct((2, 1, LANES), F32)]
    return _call(body, grid=(2, nc), in_specs=in_specs, out_specs=out_specs, out_shape=out_shape,
                 scratch_shapes=[pltpu.VMEM((h, HEAD_DIM, HEAD_DIM), F32)], sem=("parallel", "arbitrary"),
                 name=name, args=(qkv, qkv, qkv, proj, alog_row, dtb_row, s_hist, do), side=side)


def s5_mats(lam_re, lam_im, log_dt, b_re, b_im, c_re, c_im):
    g = lam_re.shape[1]
    nb = g // S5_BLOCK_GROUPS
    dt = jnp.exp(log_dt)[..., None]
    mag = jnp.exp(lam_re * dt)
    ar = mag * jnp.cos(lam_im * dt)
    ai = mag * jnp.sin(lam_im * dt)
    den = lam_re * lam_re + lam_im * lam_im
    fr = ((ar - 1.0) * lam_re + ai * lam_im) / den
    fi = (ai * lam_re - (ar - 1.0) * lam_im) / den
    bbr = fr[..., None] * b_re - fi[..., None] * b_im
    bbi = fr[..., None] * b_im + fi[..., None] * b_re
    eye = jnp.eye(S5_BLOCK_GROUPS, dtype=F32)
    shp = (2, nb, S5_BLOCK_GROUPS, S5_STATE, S5_GROUP_CH)
    w_r = jnp.einsum('dsjpc,jk->dsjckp', bbr.reshape(shp), eye).reshape(2, nb, LANES, S5_BLOCK_STATE)
    w_i = jnp.einsum('dsjpc,jk->dsjckp', bbi.reshape(shp), eye).reshape(2, nb, LANES, S5_BLOCK_STATE)
    w = jnp.concatenate([w_r, w_i], axis=-1)
    shc = (2, nb, S5_BLOCK_GROUPS, S5_GROUP_CH, S5_STATE)
    c_r = jnp.einsum('dsjcp,jk->dsjpkc', c_re.reshape(shc), eye).reshape(2, nb, S5_BLOCK_STATE, LANES)
    c_i = jnp.einsum('dsjcp,jk->dsjpkc', c_im.reshape(shc), eye).reshape(2, nb, S5_BLOCK_STATE, LANES)
    cm = jnp.concatenate([c_r, -c_i], axis=-2)
    return (ar.reshape(2, nb, 1, S5_BLOCK_STATE), ai.reshape(2, nb, 1, S5_BLOCK_STATE), w, cm)


_S5_ROWS = 512


def s5_fwd(proj, u_blk, ar, ai, w, cm, *, name):
    nrow = proj.shape[0]
    nb = w.shape[1]
    nt = nrow // 8
    hs = S5_BLOCK_STATE
    rs = min(_S5_ROWS, nrow)

    def body(u_ref, ar_ref, ai_ref, w_ref, cm_ref, y_ref, x_ref):
        d = pl.program_id(0)
        wb = w_ref[...].astype(MXU_DT)
        for r0 in range(0, nrow, rs):
            x_ref[pl.ds(r0, rs), :] = _dot(u_ref[pl.ds(r0, rs), :].astype(MXU_DT), wb, _NN)
        a_r = ar_ref[...]
        a_i = ai_ref[...]
        rid = lax.broadcasted_iota(jnp.int32, (8, hs), 0)

        def run(reverse):
            def tile_step(tt, carry):
                s_r, s_i = carry
                base = pl.multiple_of((nt - 1 - tt if reverse else tt) * 8, 8)
                tile = x_ref[pl.ds(base, 8), :]
                o_r = jnp.zeros((8, hs), F32)
                o_i = jnp.zeros((8, hs), F32)
                for q in range(8):
                    rr = 7 - q if reverse else q
                    n_r = a_r * s_r - a_i * s_i + tile[rr:rr + 1, :hs]
                    n_i = a_r * s_i + a_i * s_r + tile[rr:rr + 1, hs:]
                    s_r, s_i = n_r, n_i
                    o_r = jnp.where(rid == rr, s_r, o_r)
                    o_i = jnp.where(rid == rr, s_i, o_i)
                x_ref[pl.ds(base, 8), pl.ds(0, hs)] = o_r
                x_ref[pl.ds(base, 8), pl.ds(hs, hs)] = o_i
                return s_r, s_i

            z = jnp.zeros((1, hs), F32)
            lax.fori_loop(0, nt, tile_step, (z, z))

        @pl.when(d == 0)
        def _():
            run(False)

        @pl.when(d == 1)
        def _():
            run(True)

        cb = cm_ref[...].astype(MXU_DT)
        for r0 in range(0, nrow, rs):
            y_ref[pl.ds(r0, rs), :] = _dot(x_ref[pl.ds(r0, rs), :].astype(MXU_DT), cb, _NN)

    in_specs = [
        pl.BlockSpec((nrow, LANES), lambda d, s: (0, u_blk + s)),
        pl.BlockSpec((None, None, 1, hs), lambda d, s: (d, s, 0, 0)),
        pl.BlockSpec((None, None, 1, hs), lambda d, s: (d, s, 0, 0)),
        pl.BlockSpec((None, None, LANES, 2 * hs), lambda d, s: (d, s, 0, 0)),
        pl.BlockSpec((None, None, 2 * hs, LANES), lambda d, s: (d, s, 0, 0)),
    ]
    out_specs = [
        pl.BlockSpec((None, nrow, LANES), lambda d, s: (d, 0, s)),
        pl.BlockSpec((None, nrow, 2 * hs), lambda d, s: (d, 0, s)),
    ]
    out_shape = [jax.ShapeDtypeStruct((2, nrow, nb * LANES), F32),
                 jax.ShapeDtypeStruct((2, nrow, nb * 2 * hs), F32)]
    return pl.pallas_call(
        body, grid=(2, nb), in_specs=in_specs, out_specs=out_specs, out_shape=out_shape,
        compiler_params=_params(("parallel", "parallel")), name=name)(proj, ar, ai, w, cm)


def s5_bwd(proj, u_blk, ar, ai, w, cm, xs, dy, du_in, *, side=None, name):
    nrow = proj.shape[0]
    nb = w.shape[1]
    nt = nrow // 8
    hs = S5_BLOCK_STATE
    rs = min(_S5_ROWS, nrow)

    def body(u_ref, dy_ref, dui_ref, x_ref, ar_ref, ai_ref, w_ref, cm_ref,
             du_ref, dw_ref, dcm_ref, dar_ref, dai_ref, g_ref):
        d = pl.program_id(1)
        cb = cm_ref[...].astype(MXU_DT)
        for r0 in range(0, nrow, rs):
            g_ref[pl.ds(r0, rs), :] = _dot(dy_ref[pl.ds(r0, rs), :].astype(MXU_DT), cb, _NT)
        a_r = ar_ref[...]
        a_i = ai_ref[...]
        rid = lax.broadcasted_iota(jnp.int32, (8, hs), 0)

        def run(descending):
            def tile_step(tt, carry):
                g_r, g_i, acc_r, acc_i = carry
                tidx = nt - 1 - tt if descending else tt
                base = pl.multiple_of(tidx * 8, 8)
                tile = g_ref[pl.ds(base, 8), :]
                o_r = jnp.zeros((8, hs), F32)
                o_i = jnp.zeros((8, hs), F32)
                for q in range(8):
                    rr = 7 - q if descending else q
                    n_r = tile[rr:rr + 1, :hs] + (a_r * g_r + a_i * g_i)
                    n_i = tile[rr:rr + 1, hs:] + (a_r * g_i - a_i * g_r)
                    g_r, g_i = n_r, n_i
                    o_r = jnp.where(rid == rr, g_r, o_r)
                    o_i = jnp.where(rid == rr, g_i, o_i)
                g_ref[pl.ds(base, 8), pl.ds(0, hs)] = o_r
                g_ref[pl.ds(base, 8), pl.ds(hs, hs)] = o_i
                xt = x_ref[pl.ds(base, 8), :]
                if descending:
                    nbase = pl.multiple_of(jnp.maximum(tidx - 1, 0) * 8, 8)
                    edge = x_ref[pl.ds(nbase, 8), :][7:8, :] * jnp.where(tidx > 0, 1.0, 0.0).astype(F32)
                    prev = jnp.where(lax.broadcasted_iota(jnp.int32, xt.shape, 0) == 0, edge, pltpu.roll(xt, 1, 0))
                else:
                    nbase = pl.multiple_of(jnp.minimum(tidx + 1, nt - 1) * 8, 8)
                    edge = x_ref[pl.ds(nbase, 8), :][0:1, :] * jnp.where(tidx < nt - 1, 1.0, 0.0).astype(F32)
                    prev = jnp.where(lax.broadcasted_iota(jnp.int32, xt.shape, 0) == 7, edge, pltpu.roll(xt, 7, 0))
                p_r = prev[:, :hs]
                p_i = prev[:, hs:]
                acc_r = acc_r + o_r * p_r + o_i * p_i
                acc_i = acc_i + o_i * p_r - o_r * p_i
                return g_r, g_i, acc_r, acc_i

            z = jnp.zeros((1, hs), F32)
            z8 = jnp.zeros((8, hs), F32)
            _, _, acc_r, acc_i = lax.fori_loop(0, nt, tile_step, (z, z, z8, z8))
            dar_ref[...] = jnp.sum(acc_r, axis=0, keepdims=True)
            dai_ref[...] = jnp.sum(acc_i, axis=0, keepdims=True)

        @pl.when(d == 0)
        def _():
            run(True)

        @pl.when(d == 1)
        def _():
            run(False)

        wb = w_ref[...].astype(MXU_DT)
        for r0 in range(0, nrow, rs):
            part = _dot(g_ref[pl.ds(r0, rs), :].astype(MXU_DT), wb, _NT)

            @pl.when(d == 0)
            def _(part=part, r0=r0):
                du_ref[pl.ds(r0, rs), :] = dui_ref[pl.ds(r0, rs), :] + part

            @pl.when(d == 1)
            def _(part=part, r0=r0):
                du_ref[pl.ds(r0, rs), :] += part

        dw_ref[...] = _dot(u_ref[...].astype(MXU_DT), g_ref[...].astype(MXU_DT), _TN)
        dcm_ref[...] = _dot(x_ref[...].astype(MXU_DT), dy_ref[...].astype(MXU_DT), _TN)

    in_specs = [
        pl.BlockSpec((nrow, LANES), lambda s, d: (0, u_blk + s)),
        pl.BlockSpec((nrow, LANES), lambda s, d: (0, s)),
        pl.BlockSpec((nrow, LANES), lambda s, d: (0, s)),
        pl.BlockSpec((None, nrow, 2 * hs), lambda s, d: (d, 0, s)),
        pl.BlockSpec((None, None, 1, hs), lambda s, d: (d, s, 0, 0)),
        pl.BlockSpec((None, None, 1, hs), lambda s, d: (d, s, 0, 0)),
        pl.BlockSpec((None, None, LANES, 2 * hs), lambda s, d: (d, s, 0, 0)),
        pl.BlockSpec((None, None, 2 * hs, LANES), lambda s, d: (d, s, 0, 0)),
    ]
    out_specs = [
        pl.BlockSpec((nrow, LANES), lambda s, d: (0, s)),
        pl.BlockSpec((None, None, LANES, 2 * hs), lambda s, d: (d, s, 0, 0)),
        pl.BlockSpec((None, None, 2 * hs, LANES), lambda s, d: (d, s, 0, 0)),
        pl.BlockSpec((None, None, 1, hs), lambda s, d: (d, s, 0, 0)),
        pl.BlockSpec((None, None, 1, hs), lambda s, d: (d, s, 0, 0)),
    ]
    out_shape = [jax.ShapeDtypeStruct((nrow, nb * LANES), F32),
                 jax.ShapeDtypeStruct(w.shape, F32), jax.ShapeDtypeStruct(cm.shape, F32),
                 jax.ShapeDtypeStruct(ar.shape, F32), jax.ShapeDtypeStruct(ai.shape, F32)]
    return _call(body, grid=(nb, 2), in_specs=in_specs, out_specs=out_specs, out_shape=out_shape,
                 scratch_shapes=[pltpu.VMEM((nrow, 2 * hs), F32)], sem=("parallel", "arbitrary"),
                 name=name, args=(proj, dy, du_in, xs, ar, ai, w, cm), side=side)


def _me():
    return lax.axis_index("x"), lax.axis_index("y"), lax.axis_index("c")


def all_gather(shards):
    na = len(shards)

    def plan(x_refs, out_refs, sems):
        send_sems, recv_sems, local_sems = sems
        x, y, c = _me()
        me, sibling = (x, y, c), (x, y, 1 - c)
        chips = [(1 - x, y), (x, 1 - y), (1 - x, 1 - y)]

        def slot(a, px, py, pc):
            return out_refs[a].at[4 * px + 2 * py + pc]

        def copy(a, k, block, to, src=None):
            return pltpu.make_async_remote_copy(
                src_ref=slot(a, *block) if src is None else src, dst_ref=slot(a, *block),
                send_sem=send_sems.at[a, k], recv_sem=recv_sems.at[a, k], device_id=to, device_id_type=MESH)

        mine = [pltpu.make_async_copy(x_refs[a], slot(a, *me), local_sems.at[a]) for a in range(na)]
        first = []
        for a in range(na):
            first.append(copy(a, 0, me, sibling, src=x_refs[a]))
            first += [copy(a, 1 + j, me, (*chip, c), src=x_refs[a]) for j, chip in enumerate(chips)]
        return me, sibling, chips, c, copy, mine, first

    def start(x_refs, out_refs, sems):
        _, _, _, _, _, mine, first = plan(x_refs, out_refs, sems)
        for cp in mine + first:
            cp.start()

    def finish(x_refs, out_refs, sems):
        me, sibling, chips, c, copy, mine, first = plan(x_refs, out_refs, sems)
        passed = []
        for j, chip in enumerate(chips):
            for a in range(na):
                copy(a, 1 + j, (*chip, c), me).wait_recv()
                fwd = copy(a, 4 + j, (*chip, c), sibling)
                fwd.start()
                passed.append(fwd)
        for a in range(na):
            copy(a, 0, sibling, me).wait_recv()
            for j, chip in enumerate(chips):
                copy(a, 4 + j, (*chip, 1 - c), me).wait_recv()
        for cp in first + passed:
            cp.wait_send()
        for cp in mine:
            cp.wait()

    return Side(shards, [jax.ShapeDtypeStruct((8,) + s.shape, s.dtype) for s in shards],
                [pltpu.SemaphoreType.DMA((na, 7)), pltpu.SemaphoreType.DMA((na, 7)), pltpu.SemaphoreType.DMA((na,))],
                start, finish)


_AXES = ("x", "y", "c")


def exchange(bufs, axes, *, half):
    na = len(bufs)

    def copies(in_refs, out_refs, sems):
        send_sems, recv_sems = sems
        me = _me()
        cps = []
        for a in range(na):
            bit = me[_AXES.index(axes[a])]
            peer = tuple(1 - v if ax == axes[a] else v for ax, v in zip(_AXES, me))
            cps.append(pltpu.make_async_remote_copy(
                src_ref=in_refs[a].at[1 - bit] if half else in_refs[a], dst_ref=out_refs[a],
                send_sem=send_sems.at[a], recv_sem=recv_sems.at[a], device_id=peer, device_id_type=MESH))
        return cps

    def start(*refs):
        for cp in copies(*refs):
            cp.start()

    def finish(*refs):
        for cp in copies(*refs):
            cp.wait()

    return Side(bufs, [jax.ShapeDtypeStruct(b.shape[1:] if half else b.shape, b.dtype) for b in bufs],
                [pltpu.SemaphoreType.DMA((na,)), pltpu.SemaphoreType.DMA((na,))], start, finish)


def all_to_all(pieces, slot_fns):
    na = len(pieces)

    def copies(in_refs, out_refs, sems):
        send_sems, recv_sems, local_sems = sems
        x, y, c = _me()
        cps = []
        for a in range(na):
            mine = slot_fns[a](x, y, c)
            for k in range(8):
                tx, ty, tc = x ^ (k // 4), y ^ ((k // 2) % 2), c ^ (k % 2)
                src = in_refs[a].at[4 * tx + 2 * ty + tc]
                dst = out_refs[a].at[mine]
                if k == 0:
                    cps.append(pltpu.make_async_copy(src, dst, local_sems.at[a]))
                else:
                    cps.append(pltpu.make_async_remote_copy(
                        src_ref=src, dst_ref=dst, send_sem=send_sems.at[a, k - 1], recv_sem=recv_sems.at[a, k - 1],
                        device_id=(tx, ty, tc), device_id_type=MESH))
        return cps

    def start(*refs):
        for cp in copies(*refs):
            cp.start()

    def finish(*refs):
        for cp in copies(*refs):
            cp.wait()

    return Side(pieces, [jax.ShapeDtypeStruct(p.shape, p.dtype) for p in pieces],
                [pltpu.SemaphoreType.DMA((na, 7)), pltpu.SemaphoreType.DMA((na, 7)), pltpu.SemaphoreType.DMA((na,))],
                start, finish)


def add_half(buf, recv, bit, *, name):
    c = recv.shape[-1]
    r = math.prod(recv.shape[:-1])
    t = _tile(r, max(8, (ADD_BLOCK_BYTES // (4 * c)) // 8 * 8), 8)

    def body(bit_ref, a_ref, b_ref, o_ref):
        o_ref[...] = a_ref[...] + b_ref[...]

    grid_spec = pltpu.PrefetchScalarGridSpec(
        num_scalar_prefetch=1, grid=(r // t,),
        in_specs=[pl.BlockSpec((None, t, c), lambda i, b: (b[0], i, 0)), pl.BlockSpec((t, c), lambda i, b: (i, 0))],
        out_specs=pl.BlockSpec((t, c), lambda i, b: (i, 0)))
    out = pl.pallas_call(body, grid_spec=grid_spec, out_shape=jax.ShapeDtypeStruct((r, c), F32),
                         compiler_params=_params(("parallel",)), name=name)(
                             bit, buf.reshape(2, r, c), recv.reshape(r, c))
    return out.reshape(recv.shape)


def _pack(arrs, cols, mult):
    flat = jnp.concatenate([a.reshape(-1) for a in arrs])
    n = flat.shape[0]
    padded = -(-n // (cols * mult)) * (cols * mult)
    return jnp.pad(flat, (0, padded - n)).reshape(padded // cols, cols)


def _unpack(flat, shapes):
    out, off = [], 0
    for s in shapes:
        n = math.prod(s)
        out.append(flat[off:off + n].reshape(s))
        off += n
    return out


def _elementwise(fn, arrs, nout, name):
    r, c = arrs[0].shape
    t = _tile(r, ROW_TILE, 8)
    return rw_fwd(fn, [rows(a, t) for a in arrs], [(c, False)] * nout, nrow=r, t=t, name=name)


def kernel(x, ln_g, w_in, conv_w, a_log, dt_bias, head_norm_g, lam_re, lam_im, log_dt, b_re, b_im, c_re, c_im, d_skip, w_glu, b_glu, w_pa, w_pb, b_gate, w_out, final_g, loss_target, m_ln_g, m_w_in, m_conv_w, m_a_log, m_dt_bias, m_head_norm_g, m_lam_re, m_lam_im, m_log_dt, m_b_re, m_b_im, m_c_re, m_c_im, m_d_skip, m_w_glu, m_b_glu, m_w_pa, m_w_pb, m_b_gate, m_w_out, m_final_g, v_ln_g, v_w_in, v_conv_w, v_a_log, v_dt_bias, v_head_norm_g, v_lam_re, v_lam_im, v_log_dt, v_b_re, v_b_im, v_c_re, v_c_im, v_d_skip, v_w_glu, v_b_glu, v_w_pa, v_w_pb, v_b_gate, v_w_out, v_final_g):
    env = dict(locals())
    wts = {n: env[n] for n in WEIGHTS}
    mom_m = {n: env["m_" + n] for n in WEIGHTS}
    mom_v = {n: env["v_" + n] for n in WEIGHTS}

    xin = x[0]
    tgt = loss_target[0]
    nrow, dm = xin.shape
    depth = ln_g.shape[0]
    nh = dm // (2 * HEAD_DIM)
    wa = nh * HEAD_DIM
    wb = dm // 2
    ngrp = wb // S5_GROUP_CH
    pw = w_in.shape[-1]
    t = min(ROW_TILE, nrow)

    o_ba = 4 * wa
    o_u = o_ba + 4 * nh
    n_main = 4 * wa + 2 * wb + 2 * dm
    projp = -(-(n_main + LANES) // 512) * 512
    blk_za = 3 * wa // HEAD_DIM
    blk_u = 4 * wa // LANES
    ba_blk = n_main // LANES
    cx, cy, cc = _me()

    def gather_side(l):
        return all_gather([w_in[l].astype(WIRE_DT), w_glu[l].astype(WIRE_DT), w_pa[l].astype(WIRE_DT),
                           w_pb[l].astype(WIRE_DT), w_out[l].astype(WIRE_DT), conv_w[l]])

    def cat(g):
        return jnp.concatenate([g[j] for j in range(8)], axis=1)

    def columns(g, lo, hi):
        out = []
        for j in range(8):
            a, b = max(lo, j * pw), min(hi, (j + 1) * pw)
            if a < b:
                out.append(g[j][:, a - j * pw:b - j * pw])
        return out

    def assemble(gathered):
        g_in, g_glu, g_pa, g_pb, g_out, g_conv = gathered
        w_perm = jnp.concatenate(
            columns(g_in, 0, o_ba) + columns(g_in, o_u, 8 * pw) + columns(g_in, o_ba, o_u)
            + [jnp.zeros((dm, projp - n_main - 4 * nh), WIRE_DT)], axis=1)
        return dict(w_in=w_perm, w_glu=g_glu.reshape(wb, wb), w_pa=cat(g_pa), w_pb=cat(g_pb),
                    w_out=g_out.reshape(dm, dm), conv_w=cat(g_conv))

    full = [assemble(run_side(gather_side(0), name="gather_weights"))]

    def small(l):
        z = jnp.zeros((1, LANES - 4 * nh), F32)
        alog_row = jnp.concatenate([jnp.zeros((1, 2 * nh), F32), a_log[l].reshape(1, 2 * nh), z], axis=1)
        dtb_row = jnp.concatenate([jnp.zeros((1, 2 * nh), F32), dt_bias[l].reshape(1, 2 * nh), z], axis=1)
        return alog_row, dtb_row

    saved = []
    cur = xin
    for l in range(depth):
        fw = full[l]
        (hh,) = rw_fwd(fn_rms, [rows(cur, t), bcast(ln_g[l][None])], [(dm, False)], nrow=nrow, t=t, name="rms_fwd")
        proj = mm(hh, fw["w_in"], name="proj_fwd")
        (qkv,) = rw_fwd(make_fn_prep(nh), [rows(proj, nrow, HEAD_DIM, 0, True), bcast(fw["conv_w"], HEAD_DIM, 0, True)],
                        [(HEAD_DIM, True)], nrow=nrow, t=nrow, ncol=3 * nh, name="prep_fwd")
        alog_row, dtb_row = small(l)
        nxt_gather = gather_side(l + 1) if l + 1 < depth else None
        o_dir, s_hist = gdn_fwd(qkv, proj, ba_blk, alog_row, dtb_row, n_heads=nh, side=nxt_gather, name="gdn_fwd")
        if nxt_gather is not None:
            full.append(assemble(nxt_gather.result))
        hn_ins = [rows3(o_dir, 0, t, HEAD_DIM), rows3(o_dir, 1, t, HEAD_DIM), rows(proj, t, HEAD_DIM, blk_za, True),
                  bcast(head_norm_g[l][None])]
        (ya_in,) = rw_fwd(fn_headnorm, hn_ins, [(HEAD_DIM, True)], nrow=nrow, t=t, ncol=nh, name="headnorm_fwd")
        mats, mats_vjp = jax.vjp(s5_mats, lam_re[l], lam_im[l], log_dt[l], b_re[l], b_im[l], c_re[l], c_im[l])
        yd, xs = s5_fwd(proj, blk_u, *mats, name="s5_fwd")
        ge_ins = [rows3(yd, 0, t, wb, False), rows3(yd, 1, t, wb, False), rows(proj, t, wb, 4 * wa // wb),
                  bcast(d_skip[l][None])]
        (ys,) = rw_fwd(fn_gelu, ge_ins, [(wb, False)], nrow=nrow, t=t, name="gelu_fwd")
        glu = mm(ys, fw["w_glu"], name="glu_fwd")
        gl_ins = [rows(ys, t), rows(glu, t), rows(proj, t, wb, (4 * wa + wb) // wb), bcast(b_glu[l][None])]
        (yb_in,) = rw_fwd(fn_glu, gl_ins, [(wb, False)], nrow=nrow, t=t, name="glugate_fwd")
        y_a = mm(ya_in, fw["w_pa"], name="pa_fwd")
        y_b = mm(yb_in, fw["w_pb"], name="pb_fwd")
        bg = b_gate[l][None]
        mg_ins = [rows(proj, t, dm, 3), rows(proj, t, dm, 4), rows(y_a, t), rows(y_b, t),
                  bcast(bg, dm, 0), bcast(bg, dm, 1)]
        (merged,) = rw_fwd(fn_merge, mg_ins, [(dm, False)], nrow=nrow, t=t, name="merge_fwd")
        nxt = mm(merged, fw["w_out"], add=cur, name="out_fwd")
        saved.append(dict(x=cur, h=hh, proj=proj, qkv=qkv, o_dir=o_dir, s_hist=s_hist, hn_ins=hn_ins,
                          mats=mats, mats_vjp=mats_vjp, xs=xs, ge_ins=ge_ins, ys=ys, gl_ins=gl_ins,
                          ya_in=ya_in, yb_in=yb_in, mg_ins=mg_ins, merged=merged,
                          alog_row=alog_row, dtb_row=dtb_row))
        cur = nxt

    loss_ins = [rows(cur, t), rows(tgt, t), bcast(final_g[None])]
    (row_loss,) = rw_fwd(fn_loss, loss_ins, [(1, False)], nrow=nrow, t=t, name="loss_fwd")
    ones = jnp.ones((nrow, 1), F32)
    dcur, dfinal = rw_bwd(fn_loss, loss_ins, [[rows(ones, t)]], nrow=nrow, t=t,
                          row_grads=[(0, False)], bc_grads=[2], name="loss_bwd")
    loss = lax.psum(jnp.sum(row_loss), ("x", "y", "c"))

    coord = {"x": cx, "y": cy, "c": cc}
    routes = {"cxy": ("c", "x", "y"), "cyx": ("c", "y", "x")}
    route = ["cxy", "cyx", "cxy", "cxy", "cyx", "cyx", "cxy"]
    conv_order = [4 * ((q // 2) % 2) + 2 * (q % 2) + q // 4 for q in range(8)]
    cw = 3 * wa // 8
    g_final = {n: [None] * depth for n in SHARDED}
    gsh = {n: [None] * depth for n in SHARDED}

    def own_columns(z):
        un = jnp.concatenate([z[:, :o_ba], z[:, n_main:n_main + 4 * nh], z[:, o_ba:n_main]], axis=1)
        return jnp.stack([un[:, d * pw:(d + 1) * pw] for d in range(8)])

    def rs_begin(l):
        in_a, in_b = gsh["w_in"][l]
        conv_buf = jnp.stack([gsh["conv_w"][l][:, d * cw:(d + 1) * cw] for d in conv_order])
        return dict(layer=l, stage=0, bufs=[
            in_a.reshape(8, dm // 16, projp), in_b.reshape(8, dm // 16, projp), gsh["w_out"][l],
            gsh["w_glu"][l], gsh["w_pa"][l], gsh["w_pb"][l], conv_buf])

    def rs_side(st):
        if st is None:
            return None
        if st["stage"] < 3:
            st["axes"] = [routes[r][st["stage"]] for r in route]
            st["bufs"] = [b.reshape((2, b.shape[0] // 2) + b.shape[1:]) for b in st["bufs"]]
            return exchange(st["bufs"], st["axes"], half=True)
        st["bufs"] = [b.reshape(b.shape[1:]) for b in st["bufs"]]
        return all_to_all([own_columns(st["bufs"][0]), own_columns(st["bufs"][1])],
                          [lambda x, y, c: 4 * c + 2 * x + y, lambda x, y, c: 4 * c + 2 * y + x])

    def rs_absorb(st, side):
        if st is None:
            return
        if st["stage"] < 3:
            st["bufs"] = [add_half(b, r, coord[ax].astype(jnp.int32).reshape(1), name=f"rs_add_{st['stage']}")
                          for b, r, ax in zip(st["bufs"], side.result, st["axes"])]
        else:
            got_a, got_b = side.result
            g_final["w_in"][st["layer"]] = jnp.concatenate(
                [got_a.reshape(dm // 2, pw), got_b.reshape(dm // 2, pw)], axis=0)
            for n, b in zip(["w_out", "w_glu", "w_pa", "w_pb", "conv_w"], st["bufs"][2:]):
                g_final[n][st["layer"]] = b
        st["stage"] += 1

    grep = {n: [None] * depth for n in REPLICATED if n != "final_g"}
    pending = None
    for l in reversed(range(depth)):
        fw, sv = full[l], saved[l]
        dmerged = mm(dcur, fw["w_out"], tb=True, name="out_bwd_x")
        gsh["w_out"][l] = mm(sv["merged"], dcur, ta=True, scatter=("rows", slot_cxy), name="out_bwd_w")
        side = rs_side(pending)
        dla, dlb, dya, dyb, dbga, dbgb = rw_bwd(
            fn_merge, sv["mg_ins"], [[rows(dmerged, t)]], nrow=nrow, t=t,
            row_grads=[(0, False), (1, False), (2, False), (3, False)], bc_grads=[4, 5], side=side, name="merge_bwd")
        rs_absorb(pending, side)
        grep["b_gate"][l] = jnp.concatenate([dbga.reshape(dm), dbgb.reshape(dm)])
        dya_in = mm(dya, fw["w_pa"], tb=True, name="pa_bwd_x")
        gsh["w_pa"][l] = mm(sv["ya_in"], dya, ta=True, scatter=("cols", slot_cyx), name="pa_bwd_w")
        dyb_in = mm(dyb, fw["w_pb"], tb=True, name="pb_bwd_x")
        gsh["w_pb"][l] = mm(sv["yb_in"], dyb, ta=True, scatter=("cols", slot_cyx), name="pb_bwd_w")
        dys1, dglu, dzb, dbglu = rw_bwd(
            fn_glu, sv["gl_ins"], [[rows(dyb_in, t)]], nrow=nrow, t=t,
            row_grads=[(0, False), (1, False), (2, False)], bc_grads=[3], name="glugate_bwd")
        grep["b_glu"][l] = dbglu.reshape(wb)
        dys2 = mm(dglu, fw["w_glu"], tb=True, name="glu_bwd_x")
        gsh["w_glu"][l] = mm(sv["ys"], dglu, ta=True, scatter=("rows", slot_cxy), name="glu_bwd_w")
        dyd, du1, ddskip = rw_bwd(
            fn_gelu, sv["ge_ins"], [[rows(dys1, t), rows(dys2, t)]], nrow=nrow, t=t,
            row_grads=[(0, False), (2, False)], bc_grads=[3], name="gelu_bwd")
        grep["d_skip"][l] = ddskip.reshape(wb)
        side = rs_side(pending)
        du, dw_s5, dcm_s5, dar, dai = s5_bwd(sv["proj"], blk_u, *sv["mats"], sv["xs"], dyd, du1, side=side,
                                             name="s5_bwd")
        rs_absorb(pending, side)
        g_lr, g_li, g_ldt, g_br, g_bi, g_cr, g_ci = sv["mats_vjp"]((dar, dai, dw_s5, dcm_s5))
        for nme, val in zip(["lam_re", "lam_im", "log_dt", "b_re", "b_im", "c_re", "c_im"],
                            [g_lr, g_li, g_ldt, g_br, g_bi, g_cr, g_ci]):
            grep[nme][l] = val
        do, dza, dhn = rw_bwd(fn_headnorm, sv["hn_ins"], [[rows(dya_in, t, HEAD_DIM, 0, True)]], nrow=nrow, t=t,
                              ncol=nh, row_grads=[(0, True), (2, True)], bc_grads=[3], name="headnorm_bwd")
        grep["head_norm_g"][l] = jnp.sum(dhn, axis=0).reshape(HEAD_DIM)
        side = rs_side(pending)
        dqkv, dba_all, dal, ddt = gdn_bwd(sv["qkv"], sv["proj"], ba_blk, sv["alog_row"], sv["dtb_row"],
                                          sv["s_hist"], do, n_heads=nh, side=side, name="gdn_bwd")
        rs_absorb(pending, side)
        grep["a_log"][l] = jnp.sum(dal, axis=(0, 1))[2 * nh:4 * nh].reshape(2, nh)
        grep["dt_bias"][l] = jnp.sum(ddt, axis=(0, 1))[2 * nh:4 * nh].reshape(2, nh)
        (dba,) = rw_fwd(make_fn_sum(2), [rows3(dba_all, p, t, LANES, False) for p in range(2)],
                        [(LANES, False)], nrow=nrow, t=t, name="dba_sum")
        prep_ins = [rows(sv["proj"], nrow, HEAD_DIM, 0, True), bcast(fw["conv_w"], HEAD_DIM, 0, True)]
        dq_cots = [(dqkv.reshape(2, 3 * nrow, wa), (None, nrow, HEAD_DIM),
                    (lambda j, i, dd=dd: (dd, j // nh, j % nh))) for dd in range(2)]
        dqkv_raw, dconv = rw_bwd(make_fn_prep(nh), prep_ins, [dq_cots], nrow=nrow, t=nrow, ncol=3 * nh,
                                 row_grads=[(0, True)], bc_grads=[1], name="prep_bwd")
        gsh["conv_w"][l] = jnp.transpose(dconv, (1, 0, 2)).reshape(CONV_K, 3 * wa)
        dproj = jnp.concatenate([dqkv_raw, dza, du, dzb, dla, dlb, dba,
                                 jnp.zeros((nrow, projp - n_main - LANES), F32)], axis=1)
        side = rs_side(pending)
        dh = mm(dproj, fw["w_in"], tb=True, side=side, name="proj_bwd_x")
        rs_absorb(pending, side)
        gsh["w_in"][l] = [mm(sv["h"], dproj, ta=True, m_part=(part, 2), name="proj_bwd_w") for part in range(2)]
        dcur, dlng = rw_bwd(fn_rms, [rows(sv["x"], t), bcast(ln_g[l][None])], [[rows(dh, t)]], nrow=nrow, t=t,
                            row_grads=[(0, False)], bc_grads=[1], residual=rows(dcur, t), name="rms_bwd")
        grep["ln_g"][l] = dlng.reshape(dm)
        pending = rs_begin(l)
    grad_x = dcur[None]
    for stage in range(RS_STAGES):
        side = rs_side(pending)
        run_side(side, name=f"rs_stage_{stage}")
        rs_absorb(pending, side)
    grads = {n: jnp.stack(g_final[n]) for n in SHARDED}

    rep_list = [jnp.stack(grep[n]) for n in REPLICATED if n != "final_g"] + [dfinal.reshape(dm)]
    rep_shapes = [a.shape for a in rep_list]
    rbuf = _pack(rep_list, COMM_COLS, ROW_TILE)
    for axis in ("c", "x", "y"):
        (recv,) = run_side(exchange([rbuf], [axis], half=False), name="ar_exchange_" + axis)
        (rbuf,) = _elementwise(make_fn_sum(2), [rbuf, recv], 1, "ar_add_" + axis)
    for n, val in zip(REPLICATED, _unpack(rbuf.reshape(-1), rep_shapes)):
        grads[n] = val

    deltas, new_m, new_v = {}, {}, {}
    big = ["w_in", "w_glu", "w_pa", "w_pb", "w_out"]
    for n in big:
        shp = wts[n].shape
        two = [a.reshape(-1, shp[-1]) for a in (wts[n], grads[n], mom_m[n], mom_v[n])]
        d_, m_, v_ = _elementwise(fn_adamw, two, 3, "adamw_" + n)
        deltas[n], new_m[n], new_v[n] = d_.reshape(shp), m_.reshape(shp), v_.reshape(shp)
    rest = [n for n in WEIGHTS if n not in big]
    rest_shapes = [wts[n].shape for n in rest]
    packed = [_pack([src[n] for n in rest], COMM_COLS, ROW_TILE) for src in (wts, grads, mom_m, mom_v)]
    outs = _elementwise(fn_adamw, packed, 3, "adamw_small")
    for dst, arr in zip((deltas, new_m, new_v), outs):
        for n, val in zip(rest, _unpack(arr.reshape(-1), rest_shapes)):
            dst[n] = val

    return (loss, grad_x, *[grads[n] for n in WEIGHTS], *[deltas[n] for n in WEIGHTS],
            *[new_m[n] for n in WEIGHTS], *[new_v[n] for n in WEIGHTS])
```

```python
import functools
import math

import jax
import jax.numpy as jnp
from jax import lax
from jax.experimental import pallas as pl
from jax.experimental.pallas import tpu as pltpu

F32 = jnp.float32
MXU_DT = jnp.bfloat16
WIRE_DT = jnp.bfloat16

HEAD_DIM = 128
CHUNK = 64
CONV_K = 5
S5_GROUP_CH = 16
S5_STATE = 64
S5_BLOCK_GROUPS = 8
S5_BLOCK_STATE = S5_BLOCK_GROUPS * S5_STATE
RMS_EPS = 1e-6
LANES = 128
VMEM_LIMIT = 56 * 1024 * 1024
ROW_TILE = 256
COMM_COLS = 1024
ADD_BLOCK_BYTES = 2 * 1024 * 1024
RS_STAGES = 4

ADAM_LR = 0.001
ADAM_B1 = 0.9
ADAM_B2 = 0.999
ADAM_EPS = 1e-08
ADAM_WD = 0.01
ADAM_STEP = 10

WEIGHTS = ['ln_g', 'w_in', 'conv_w', 'a_log', 'dt_bias', 'head_norm_g', 'lam_re', 'lam_im', 'log_dt',
           'b_re', 'b_im', 'c_re', 'c_im', 'd_skip', 'w_glu', 'b_glu', 'w_pa', 'w_pb', 'b_gate',
           'w_out', 'final_g']
SHARDED = ['w_in', 'w_glu', 'w_pa', 'w_pb', 'w_out', 'conv_w']
REPLICATED = [n for n in WEIGHTS if n not in SHARDED]
MESH = pl.DeviceIdType.MESH


def _params(sem=None):
    return pltpu.CompilerParams(dimension_semantics=sem, vmem_limit_bytes=VMEM_LIMIT)


def _tile(n, cap, q=LANES):
    t = (min(n, cap) // q) * q
    while t > q and n % t:
        t -= q
    return t if t > 0 and n % t == 0 else n


class Side:
    def __init__(self, ins, out_sd, sems, start, finish):
        self.ins, self.out_sd, self.sems, self.start, self.finish = list(ins), list(out_sd), list(sems), start, finish
        self.result = None


def _call(body, *, grid, in_specs, out_specs, out_shape, scratch_shapes=(), sem, name, args, side=None):
    in_specs, out_specs, out_shape = list(in_specs), list(out_specs), list(out_shape)
    scratch_shapes = list(scratch_shapes)
    if side is None:
        return pl.pallas_call(body, grid=grid, in_specs=in_specs, out_specs=out_specs, out_shape=out_shape,
                              scratch_shapes=scratch_shapes, compiler_params=_params(sem), name=name)(*args)
    hbm = pl.BlockSpec(memory_space=pl.ANY)
    n_in, n_out, n_scr = len(in_specs), len(out_specs), len(scratch_shapes)
    s_in, s_out = len(side.ins), len(side.out_sd)

    def hosted(*refs):
        main_in, rest = refs[:n_in], refs[n_in:]
        side_in, rest = rest[:s_in], rest[s_in:]
        main_out, rest = rest[:n_out], rest[n_out:]
        side_out, rest = rest[:s_out], rest[s_out:]
        main_scr, sems = rest[:n_scr], rest[n_scr:]
        ids = [pl.program_id(k) for k in range(len(grid))]
        first = functools.reduce(jnp.logical_and, [i == 0 for i in ids])
        last = functools.reduce(jnp.logical_and, [i == g - 1 for i, g in zip(ids, grid)])

        @pl.when(first)
        def _():
            side.start(side_in, side_out, sems)

        body(*main_in, *main_out, *main_scr)

        @pl.when(last)
        def _():
            side.finish(side_in, side_out, sems)

    outs = pl.pallas_call(
        hosted, grid=grid, in_specs=in_specs + [hbm] * s_in, out_specs=out_specs + [hbm] * s_out,
        out_shape=out_shape + side.out_sd, scratch_shapes=scratch_shapes + side.sems,
        compiler_params=_params(("arbitrary",) * len(grid)), name=name)(*args, *side.ins)
    side.result = list(outs[n_out:])
    return list(outs[:n_out])


def run_side(side, *, name):
    hbm = pl.BlockSpec(memory_space=pl.ANY)
    s_in, s_out = len(side.ins), len(side.out_sd)

    def body(*refs):
        side_in, side_out, sems = refs[:s_in], refs[s_in:s_in + s_out], refs[s_in + s_out:]
        side.start(side_in, side_out, sems)
        side.finish(side_in, side_out, sems)

    side.result = list(pl.pallas_call(body, out_shape=side.out_sd, in_specs=[hbm] * s_in, out_specs=[hbm] * s_out,
                                      scratch_shapes=side.sems, name=name)(*side.ins))
    return side.result


def slot_cxy(d):
    return 4 * (d % 2) + 2 * (d // 4) + (d // 2) % 2


def slot_cyx(d):
    return 4 * (d % 2) + 2 * ((d // 2) % 2) + d // 4


def mm(a, b, *, ta=False, tb=False, add=None, m_part=None, scatter=None, side=None, name):
    m = a.shape[1] if ta else a.shape[0]
    k = a.shape[0] if ta else a.shape[1]
    n = b.shape[0] if tb else b.shape[1]
    m_off = 0
    if m_part is not None:
        m = m // m_part[1]
        m_off = m_part[0]
    tm, tn, tk = _tile(m, 1024), _tile(n, 512), _tile(k, 2048)
    if scatter is not None and scatter[0] == "rows":
        tm = m // 8
    if scatter is not None and scatter[0] == "cols":
        tn = n // 8
    if m_part is not None:
        assert tm == m
    nk = k // tk
    dn = (((0 if ta else 1,), (1 if tb else 0,)), ((), ()))

    def body(*refs):
        if add is None:
            a_ref, b_ref, o_ref, acc = refs
        else:
            a_ref, b_ref, add_ref, o_ref, acc = refs
        kk = pl.program_id(2)

        @pl.when(kk == 0)
        def _():
            acc[...] = jnp.zeros_like(acc)

        acc[...] += lax.dot_general(a_ref[...].astype(MXU_DT), b_ref[...].astype(MXU_DT), dn,
                                    preferred_element_type=F32)

        @pl.when(kk == nk - 1)
        def _():
            r = acc[...]
            if add is not None:
                r = r + add_ref[...]
            o_ref[...] = r

    if ta:
        a_spec = pl.BlockSpec((tk, tm), lambda i, j, kk: (kk, i + m_off))
    else:
        a_spec = pl.BlockSpec((tm, tk), lambda i, j, kk: (i + m_off, kk))
    b_spec = pl.BlockSpec((tn, tk), lambda i, j, kk: (j, kk)) if tb else pl.BlockSpec((tk, tn), lambda i, j, kk: (kk, j))
    out_sd = jax.ShapeDtypeStruct((m, n), F32)
    if scatter is None:
        o_spec = pl.BlockSpec((tm, tn), lambda i, j, kk: (i, j))
    elif scatter[0] == "rows":
        o_spec = pl.BlockSpec((None, tm, tn), lambda i, j, kk: (scatter[1](i), 0, j))
        out_sd = jax.ShapeDtypeStruct((8, tm, n), F32)
    else:
        o_spec = pl.BlockSpec((None, tm, tn), lambda i, j, kk: (scatter[1](j), i, 0))
        out_sd = jax.ShapeDtypeStruct((8, m, tn), F32)
    ins, specs = [a, b], [a_spec, b_spec]
    if add is not None:
        ins.append(add)
        specs.append(o_spec)
    return _call(body, grid=(m // tm, n // tn, nk), in_specs=specs, out_specs=[o_spec], out_shape=[out_sd],
                 scratch_shapes=[pltpu.VMEM((tm, tn), F32)], sem=("parallel", "parallel", "arbitrary"),
                 name=name, args=ins, side=side)[0]


def rows(arr, t, width=None, base=0, per_j=False):
    width = arr.shape[1] if width is None else width
    return (arr, (t, width), lambda j, i: (i, base + (j if per_j else 0)))


def rows3(arr, lead, t, width, per_j=True):
    return (arr, (None, t, width), lambda j, i: (lead, i, j if per_j else 0))


def bcast(arr, width=None, base=0, per_j=False):
    width = arr.shape[1] if width is None else width
    return (arr, (arr.shape[0], width), lambda j, i: (0, base + (j if per_j else 0)))


def _specs(items):
    return [pl.BlockSpec(bs, im) for (_, bs, im) in items]


def rw_fwd(fn, ins, outs, *, nrow, t, ncol=1, name):
    out_specs = [pl.BlockSpec((t, w), (lambda j, i: (i, j)) if pj else (lambda j, i: (i, 0))) for (w, pj) in outs]
    out_shape = [jax.ShapeDtypeStruct((nrow, w * (ncol if pj else 1)), F32) for (w, pj) in outs]
    nin = len(ins)

    def body(*refs):
        j = pl.program_id(0)
        res = fn(j, *[r[...] for r in refs[:nin]])
        for o_ref, r in zip(refs[nin:], res):
            o_ref[...] = r

    return pl.pallas_call(
        body, grid=(ncol, nrow // t), in_specs=_specs(ins), out_specs=out_specs, out_shape=out_shape,
        compiler_params=_params(("parallel", "parallel")), name=name)(*[x[0] for x in ins])


def rw_bwd(fn, ins, cots, *, nrow, t, ncol=1, row_grads, bc_grads, residual=None, side=None, name):
    nin = len(ins)
    flat_cots = [c for group in cots for c in group]
    extra = [residual] if residual is not None else []
    out_specs, out_shape = [], []
    for idx, pj in row_grads:
        w = ins[idx][1][-1]
        out_specs.append(pl.BlockSpec((t, w), (lambda j, i: (i, j)) if pj else (lambda j, i: (i, 0))))
        out_shape.append(jax.ShapeDtypeStruct((nrow, w * (ncol if pj else 1)), F32))
    for idx in bc_grads:
        r, w = ins[idx][1]
        out_specs.append(pl.BlockSpec((None, r, w), lambda j, i: (j, 0, 0)))
        out_shape.append(jax.ShapeDtypeStruct((ncol, r, w), F32))

    def body(*refs):
        j = pl.program_id(0)
        i = pl.program_id(1)
        vals = [r[...] for r in refs[:nin]]
        pos = nin
        cts = []
        for group in cots:
            c = refs[pos][...]
            for q in range(1, len(group)):
                c = c + refs[pos + q][...]
            pos += len(group)
            cts.append(c)
        res_ref = refs[pos] if residual is not None else None
        pos += len(extra)
        outs = refs[pos:]
        _, vjp = jax.vjp(lambda *a: tuple(fn(j, *a)), *vals)
        grads = vjp(tuple(cts))
        for q, (idx, _) in enumerate(row_grads):
            g = grads[idx]
            if q == 0 and res_ref is not None:
                g = g + res_ref[...]
            outs[q][...] = g
        for q, idx in enumerate(bc_grads):
            o_ref = outs[len(row_grads) + q]

            @pl.when(i == 0)
            def _(o_ref=o_ref):
                o_ref[...] = jnp.zeros_like(o_ref)

            o_ref[...] += grads[idx]

    all_in = list(ins) + flat_cots + extra
    return _call(body, grid=(ncol, nrow // t), in_specs=_specs(all_in), out_specs=out_specs, out_shape=out_shape,
                 sem=("parallel", "arbitrary"), name=name, args=[x[0] for x in all_in], side=side)


def _silu(x):
    return x * jax.nn.sigmoid(x)


@jax.custom_vjp
def _softplus(x):
    return jnp.maximum(x, 0.0) + jnp.log1p(jnp.exp(-jnp.abs(x)))


def _softplus_fwd(x):
    return _softplus(x), x


def _softplus_bwd(x, ct):
    return (ct * jax.nn.sigmoid(x),)


_softplus.defvjp(_softplus_fwd, _softplus_bwd)


def _gelu(x):
    return 0.5 * x * (1.0 + jnp.tanh(math.sqrt(2.0 / math.pi) * (x + 0.044715 * (x * x * x))))


def _row_shift_impl(x, s):
    n = x.shape[0]
    if s == 0:
        return x
    rolled = pltpu.roll(x, (-s) % n, 0)
    t = lax.broadcasted_iota(jnp.int32, x.shape, 0)
    ok = (t + s >= 0) & (t + s < n)
    return jnp.where(ok, rolled, 0.0)


@functools.partial(jax.custom_vjp, nondiff_argnums=(1,))
def _row_shift(x, s):
    return _row_shift_impl(x, s)


def _row_shift_fwd(x, s):
    return _row_shift_impl(x, s), None


def _row_shift_bwd(s, _, ct):
    return (_row_shift_impl(ct, -s),)


_row_shift.defvjp(_row_shift_fwd, _row_shift_bwd)


def fn_rms(j, x, g):
    return (x * lax.rsqrt(jnp.mean(x * x, axis=-1, keepdims=True) + RMS_EPS) * g,)


def make_fn_prep(n_heads):
    pad = (CONV_K - 1) // 2

    def fn_prep(j, x, w):
        y = _row_shift(x, -pad) * w[0:1, :]
        for i in range(1, CONV_K):
            y = y + _row_shift(x, i - pad) * w[i:i + 1, :]
        a = _silu(y)
        scale = jnp.where(j < n_heads, HEAD_DIM ** -0.5, 1.0).astype(F32)
        nrm = a * lax.rsqrt(jnp.sum(a * a, axis=-1, keepdims=True) + RMS_EPS) * scale
        return (jnp.where(j < 2 * n_heads, nrm, a),)

    return fn_prep


def fn_headnorm(j, o0, o1, z, g):
    o = o0 + o1
    n = o * lax.rsqrt(jnp.mean(o * o, axis=-1, keepdims=True) + RMS_EPS) * g
    return (n * _silu(z),)


def fn_gelu(j, y0, y1, u, dsk):
    return (_gelu(y0 + y1 + u * dsk),)


def fn_glu(j, ys, logit, z, b):
    return (ys * jax.nn.sigmoid(logit + b) * _silu(z),)


def fn_merge(j, la, lb, ya, yb, ba, bb):
    return (jax.nn.sigmoid(la + ba) * ya + jax.nn.sigmoid(lb + bb) * yb,)


def fn_loss(j, x, t, g):
    y = x * lax.rsqrt(jnp.mean(x * x, axis=-1, keepdims=True) + RMS_EPS) * g
    e = y - t
    return (0.5 * jnp.mean(e * e, axis=-1, keepdims=True),)


def make_fn_sum(n):
    def fn_sum(j, *xs):
        s = xs[0]
        for q in range(1, n):
            s = s + xs[q]
        return (s,)

    return fn_sum


def fn_adamw(j, w, g, m, v):
    m2 = ADAM_B1 * m + (1.0 - ADAM_B1) * g
    v2 = ADAM_B2 * v + (1.0 - ADAM_B2) * (g * g)
    m_hat = m2 / (1.0 - ADAM_B1 ** ADAM_STEP)
    v_hat = v2 / (1.0 - ADAM_B2 ** ADAM_STEP)
    delta = -ADAM_LR * (m_hat / (jnp.sqrt(v_hat) + ADAM_EPS) + ADAM_WD * w)
    return delta, m2, v2


def _dot(a, b, dims, precision=None):
    return lax.dot_general(a, b, (dims, ((), ())), precision=precision, preferred_element_type=F32)


_NN = ((1,), (0,))
_NT = ((1,), (1,))
_TN = ((0,), (0,))
GDN_HEAD_BLOCK = 4


def _split(a):
    hi = a.astype(MXU_DT)
    return hi, (a - hi.astype(F32)).astype(MXU_DT)


def _dot3(a, b, dims):
    ah, al = _split(a)
    bh, bl = _split(b)
    return _dot(ah, bh, dims) + (_dot(al, bh, dims) + _dot(ah, bl, dims))


def _unit_inverse(lmat):
    r = lmat.shape[0]
    eye = (lax.broadcasted_iota(jnp.int32, (r, r), 0) == lax.broadcasted_iota(jnp.int32, (r, r), 1)).astype(F32)
    pw = -lmat
    tinv = eye + pw
    for _ in range(int(math.ceil(math.log2(CHUNK))) - 1):
        pw = _dot3(pw, pw, _NN)
        tinv = tinv + _dot3(tinv, pw, _NN)
    return tinv


@jax.custom_vjp
def tri_solve(lmat, rhs):
    return _dot3(_unit_inverse(lmat), rhs, _NN)


def _tri_solve_fwd(lmat, rhs):
    tinv = _unit_inverse(lmat)
    x = _dot3(tinv, rhs, _NN)
    return x, (tinv, x)


def _tri_solve_bwd(res, dx):
    tinv, x = res
    drhs = _dot3(tinv, dx, _TN)
    return -_dot3(drhs, x, _NT), drhs


tri_solve.defvjp(_tri_solve_fwd, _tri_solve_bwd)


def gdn_group(s, q, k, v, ba, alog, dtb, *, d, head0, n_heads):
    hb = len(s)
    c = q.shape[0]
    r = hb * c

    def stack(x):
        return jnp.concatenate([x[:, i * HEAD_DIM:(i + 1) * HEAD_DIM] for i in range(hb)], axis=0)

    def pick(x, lane0):
        lane = lax.broadcasted_iota(jnp.int32, x.shape, 1)
        return jnp.concatenate(
            [jnp.sum(jnp.where(lane == lane0 + i, x, 0.0), axis=1, keepdims=True) for i in range(hb)], axis=0)

    q4, k4, v4 = stack(q), stack(k), stack(v)
    beta = pick(jax.nn.sigmoid(ba), d * n_heads + head0)
    g = pick(-jnp.exp(alog) * _softplus(ba + dtb), 2 * n_heads + d * n_heads + head0)
    ii = lax.broadcasted_iota(jnp.int32, (r, r), 0)
    jj = lax.broadcasted_iota(jnp.int32, (r, r), 1)
    same = (ii // c) == (jj // c)
    rel = (ii - jj) * (1 - 2 * d)
    incl = same & (rel >= 0)
    strict = same & (rel > 0)
    incl_t = same & (rel <= 0)
    g_row = jnp.sum(jnp.where(ii == jj, g, 0.0), axis=0, keepdims=True)
    gc_col = jnp.sum(jnp.where(incl, g_row, 0.0), axis=1, keepdims=True)
    gc_row = jnp.sum(jnp.where(incl_t, g, 0.0), axis=0, keepdims=True)
    g_tot = jnp.sum(jnp.where(same, g_row, 0.0), axis=1, keepdims=True)
    decay = jnp.exp(jnp.where(incl, gc_col - gc_row, -1e30))
    kb = k4 * beta
    vb = v4 * beta
    lmat = jnp.where(strict, _dot(kb, k4, _NT) * decay, 0.0)
    uw = tri_solve(lmat, jnp.concatenate([vb, kb * jnp.exp(gc_col)], axis=1))
    u, w = uw[:, :HEAD_DIM], uw[:, HEAD_DIM:]
    qk = _dot(q4, k4, _NT) * decay
    qe = q4 * jnp.exp(gc_col)
    kd = k4 * jnp.exp(g_tot - gc_col)
    eg = jnp.exp(g_tot)
    v_new, o_s = [], []
    for i in range(hb):
        rs = slice(i * c, (i + 1) * c)
        ws = _dot(jnp.concatenate([w[rs], qe[rs]], axis=0), s[i], _NN)
        v_new.append(u[rs] - ws[:c])
        o_s.append(ws[c:])
    o4 = jnp.concatenate(o_s, axis=0) + _dot(qk, jnp.concatenate(v_new, axis=0), _NN)
    s_new = tuple(s[i] * eg[i * c:i * c + 1, :] + _dot(kd[i * c:(i + 1) * c], v_new[i], _TN) for i in range(hb))
    o = jnp.concatenate([o4[i * c:(i + 1) * c] for i in range(hb)], axis=1)
    return s_new, o


def _gdn_maps(n_chunks):
    def chunk_of(d, step):
        return step + d * (n_chunks - 1 - 2 * step)
    return chunk_of


def _heads(ref, h0, n):
    return ref[:, h0 * HEAD_DIM:(h0 + n) * HEAD_DIM]


def gdn_fwd(qkv, proj, ba_blk, alog_row, dtb_row, *, n_heads, side=None, name):
    nrow = qkv.shape[0]
    nc = nrow // CHUNK
    h = n_heads
    wa = h * HEAD_DIM
    hb = math.gcd(h, GDN_HEAD_BLOCK)
    chunk_of = _gdn_maps(nc)

    def body(q_ref, k_ref, v_ref, ba_ref, al_ref, dt_ref, o_ref, sh_ref, s_scr):
        d = pl.program_id(0)
        n = pl.program_id(1)

        @pl.when(n == 0)
        def _():
            s_scr[...] = jnp.zeros_like(s_scr)

        ba, al, dt = ba_ref[...], al_ref[...], dt_ref[...]
        for h0 in range(0, h, hb):
            s = tuple(s_scr[h0 + i] for i in range(hb))
            for i in range(hb):
                sh_ref[h0 + i] = s[i]
            s2, o = gdn_group(s, _heads(q_ref, h0, hb), _heads(k_ref, h0, hb), _heads(v_ref, h0, hb), ba, al, dt,
                              d=d, head0=h0, n_heads=h)
            o_ref[:, h0 * HEAD_DIM:(h0 + hb) * HEAD_DIM] = o
            for i in range(hb):
                s_scr[h0 + i] = s2[i]

    blk = (CHUNK, wa)
    in_specs = [
        pl.BlockSpec(blk, lambda d, n: (chunk_of(d, n), 0)),
        pl.BlockSpec(blk, lambda d, n: (chunk_of(d, n), 1)),
        pl.BlockSpec(blk, lambda d, n: (chunk_of(d, n), 2)),
        pl.BlockSpec((CHUNK, LANES), lambda d, n: (chunk_of(d, n), ba_blk)),
        pl.BlockSpec((1, LANES), lambda d, n: (0, 0)),
        pl.BlockSpec((1, LANES), lambda d, n: (0, 0)),
    ]
    out_specs = [
        pl.BlockSpec((None, CHUNK, wa), lambda d, n: (d, chunk_of(d, n), 0)),
        pl.BlockSpec((h, None, HEAD_DIM, HEAD_DIM), lambda d, n: (d, n, 0, 0)),
    ]
    out_shape = [jax.ShapeDtypeStruct((2, nrow, wa), F32),
                 jax.ShapeDtypeStruct((2 * h, nc, HEAD_DIM, HEAD_DIM), F32)]
    return _call(body, grid=(2, nc), in_specs=in_specs, out_specs=out_specs, out_shape=out_shape,
                 scratch_shapes=[pltpu.VMEM((h, HEAD_DIM, HEAD_DIM), F32)], sem=("parallel", "arbitrary"),
                 name=name, args=(qkv, qkv, qkv, proj, alog_row, dtb_row), side=side)


def gdn_bwd(qkv, proj, ba_blk, alog_row, dtb_row, s_hist, do, *, n_heads, side=None, name):
    nrow = qkv.shape[0]
    nc = nrow // CHUNK
    h = n_heads
    wa = h * HEAD_DIM
    hb = math.gcd(h, GDN_HEAD_BLOCK)
    chunk_of = _gdn_maps(nc)

    def cb(d, n):
        return chunk_of(d, nc - 1 - n)

    def body(q_ref, k_ref, v_ref, ba_ref, al_ref, dt_ref, sh_ref, do_ref,
             dqkv_ref, dba_ref, dal_ref, ddt_ref, ds_scr):
        d = pl.program_id(0)
        n = pl.program_id(1)

        @pl.when(n == 0)
        def _():
            ds_scr[...] = jnp.zeros_like(ds_scr)
            dal_ref[...] = jnp.zeros_like(dal_ref)
            ddt_ref[...] = jnp.zeros_like(ddt_ref)

        ba, al, dt = ba_ref[...], al_ref[...], dt_ref[...]
        dba_sum = jnp.zeros_like(ba)
        dal_sum = jnp.zeros_like(al)
        ddt_sum = jnp.zeros_like(dt)
        for h0 in range(0, h, hb):
            f = functools.partial(gdn_group, d=d, head0=h0, n_heads=h)
            s = tuple(sh_ref[h0 + i] for i in range(hb))
            _, vjp = jax.vjp(f, s, _heads(q_ref, h0, hb), _heads(k_ref, h0, hb), _heads(v_ref, h0, hb), ba, al, dt)
            ds, dq, dk, dv, dba, dal, ddt = vjp((tuple(ds_scr[h0 + i] for i in range(hb)), _heads(do_ref, h0, hb)))
            for i in range(hb):
                ds_scr[h0 + i] = ds[i]
            cols = slice(h0 * HEAD_DIM, (h0 + hb) * HEAD_DIM)
            dqkv_ref[0, :, cols] = dq
            dqkv_ref[1, :, cols] = dk
            dqkv_ref[2, :, cols] = dv
            dba_sum = dba_sum + dba
            dal_sum = dal_sum + dal
            ddt_sum = ddt_sum + ddt
        dba_ref[...] = dba_sum
        dal_ref[...] += dal_sum
        ddt_ref[...] += ddt_sum

    blk = (CHUNK, wa)
    in_specs = [
        pl.BlockSpec(blk, lambda d, n: (cb(d, n), 0)),
        pl.BlockSpec(blk, lambda d, n: (cb(d, n), 1)),
        pl.BlockSpec(blk, lambda d, n: (cb(d, n), 2)),
        pl.BlockSpec((CHUNK, LANES), lambda d, n: (cb(d, n), ba_blk)),
        pl.BlockSpec((1, LANES), lambda d, n: (0, 0)),
        pl.BlockSpec((1, LANES), lambda d, n: (0, 0)),
        pl.BlockSpec((h, None, HEAD_DIM, HEAD_DIM), lambda d, n: (d, nc - 1 - n, 0, 0)),
        pl.BlockSpec(blk, lambda d, n: (cb(d, n), 0)),
    ]
    out_specs = [
        pl.BlockSpec((None, 3, CHUNK, wa), lambda d, n: (d, 0, cb(d, n), 0)),
        pl.BlockSpec((None, CHUNK, LANES), lambda d, n: (d, cb(d, n), 0)),
        pl.BlockSpec((None, 1, LANES), lambda d, n: (d, 0, 0)),
        pl.BlockSpec((None, 1, LANES), lambda d, n: (d, 0, 0)),
    ]
    out_shape = [jax.ShapeDtypeStruct((2, 3, nrow, wa), F32),
                 jax.ShapeDtypeStruct((2, nrow, LANES), F32),
                 jax.ShapeDtypeStruct((2, 1, LANES), F32),
                 jax.ShapeDtypeStruct((2, 1, LANES), F32)]
    return _call(body, grid=(2, nc), in_specs=in_specs, out_specs=out_specs, out_shape=out_shape,
                 scratch_shapes=[pltpu.VMEM((h, HEAD_DIM, HEAD_DIM), F32)], sem=("parallel", "arbitrary"),
                 name=name, args=(qkv, qkv, qkv, proj, alog_row, dtb_row, s_hist, do), side=side)


def s5_mats(lam_re, lam_im, log_dt, b_re, b_im, c_re, c_im):
    g = lam_re.shape[1]
    nb = g // S5_BLOCK_GROUPS
    dt = jnp.exp(log_dt)[..., None]
    mag = jnp.exp(lam_re * dt)
    ar = mag * jnp.cos(lam_im * dt)
    ai = mag * jnp.sin(lam_im * dt)
    den = lam_re * lam_re + lam_im * lam_im
    fr = ((ar - 1.0) * lam_re + ai * lam_im) / den
    fi = (ai * lam_re - (ar - 1.0) * lam_im) / den
    bbr = fr[..., None] * b_re - fi[..., None] * b_im
    bbi = fr[..., None] * b_im + fi[..., None] * b_re
    eye = jnp.eye(S5_BLOCK_GROUPS, dtype=F32)
    shp = (2, nb, S5_BLOCK_GROUPS, S5_STATE, S5_GROUP_CH)
    w_r = jnp.einsum('dsjpc,jk->dsjckp', bbr.reshape(shp), eye).reshape(2, nb, LANES, S5_BLOCK_STATE)
    w_i = jnp.einsum('dsjpc,jk->dsjckp', bbi.reshape(shp), eye).reshape(2, nb, LANES, S5_BLOCK_STATE)
    w = jnp.concatenate([w_r, w_i], axis=-1)
    shc = (2, nb, S5_BLOCK_GROUPS, S5_GROUP_CH, S5_STATE)
    c_r = jnp.einsum('dsjcp,jk->dsjpkc', c_re.reshape(shc), eye).reshape(2, nb, S5_BLOCK_STATE, LANES)
    c_i = jnp.einsum('dsjcp,jk->dsjpkc', c_im.reshape(shc), eye).reshape(2, nb, S5_BLOCK_STATE, LANES)
    cm = jnp.concatenate([c_r, -c_i], axis=-2)
    return (ar.reshape(2, nb, 1, S5_BLOCK_STATE), ai.reshape(2, nb, 1, S5_BLOCK_STATE), w, cm)


_S5_ROWS = 512


def s5_fwd(proj, u_blk, ar, ai, w, cm, *, name):
    nrow = proj.shape[0]
    nb = w.shape[1]
    nt = nrow // 8
    hs = S5_BLOCK_STATE
    rs = min(_S5_ROWS, nrow)

    def body(u_ref, ar_ref, ai_ref, w_ref, cm_ref, y_ref, x_ref):
        d = pl.program_id(0)
        wb = w_ref[...].astype(MXU_DT)
        for r0 in range(0, nrow, rs):
            x_ref[pl.ds(r0, rs), :] = _dot(u_ref[pl.ds(r0, rs), :].astype(MXU_DT), wb, _NN)
        a_r = ar_ref[...]
        a_i = ai_ref[...]
        rid = lax.broadcasted_iota(jnp.int32, (8, hs), 0)

        def run(reverse):
            def tile_step(tt, carry):
                s_r, s_i = carry
                base = pl.multiple_of((nt - 1 - tt if reverse else tt) * 8, 8)
                tile = x_ref[pl.ds(base, 8), :]
                o_r = jnp.zeros((8, hs), F32)
                o_i = jnp.zeros((8, hs), F32)
                for q in range(8):
                    rr = 7 - q if reverse else q
                    n_r = a_r * s_r - a_i * s_i + tile[rr:rr + 1, :hs]
                    n_i = a_r * s_i + a_i * s_r + tile[rr:rr + 1, hs:]
                    s_r, s_i = n_r, n_i
                    o_r = jnp.where(rid == rr, s_r, o_r)
                    o_i = jnp.where(rid == rr, s_i, o_i)
                x_ref[pl.ds(base, 8), pl.ds(0, hs)] = o_r
                x_ref[pl.ds(base, 8), pl.ds(hs, hs)] = o_i
                return s_r, s_i

            z = jnp.zeros((1, hs), F32)
            lax.fori_loop(0, nt, tile_step, (z, z))

        @pl.when(d == 0)
        def _():
            run(False)

        @pl.when(d == 1)
        def _():
            run(True)

        cb = cm_ref[...].astype(MXU_DT)
        for r0 in range(0, nrow, rs):
            y_ref[pl.ds(r0, rs), :] = _dot(x_ref[pl.ds(r0, rs), :].astype(MXU_DT), cb, _NN)

    in_specs = [
        pl.BlockSpec((nrow, LANES), lambda d, s: (0, u_blk + s)),
        pl.BlockSpec((None, None, 1, hs), lambda d, s: (d, s, 0, 0)),
        pl.BlockSpec((None, None, 1, hs), lambda d, s: (d, s, 0, 0)),
        pl.BlockSpec((None, None, LANES, 2 * hs), lambda d, s: (d, s, 0, 0)),
        pl.BlockSpec((None, None, 2 * hs, LANES), lambda d, s: (d, s, 0, 0)),
    ]
    out_specs = [
        pl.BlockSpec((None, nrow, LANES), lambda d, s: (d, 0, s)),
        pl.BlockSpec((None, nrow, 2 * hs), lambda d, s: (d, 0, s)),
    ]
    out_shape = [jax.ShapeDtypeStruct((2, nrow, nb * LANES), F32),
                 jax.ShapeDtypeStruct((2, nrow, nb * 2 * hs), F32)]
    return pl.pallas_call(
        body, grid=(2, nb), in_specs=in_specs, out_specs=out_specs, out_shape=out_shape,
        compiler_params=_params(("parallel", "parallel")), name=name)(proj, ar, ai, w, cm)


def s5_bwd(proj, u_blk, ar, ai, w, cm, xs, dy, du_in, *, side=None, name):
    nrow = proj.shape[0]
    nb = w.shape[1]
    nt = nrow // 8
    hs = S5_BLOCK_STATE
    rs = min(_S5_ROWS, nrow)

    def body(u_ref, dy_ref, dui_ref, x_ref, ar_ref, ai_ref, w_ref, cm_ref,
             du_ref, dw_ref, dcm_ref, dar_ref, dai_ref, g_ref):
        d = pl.program_id(1)
        cb = cm_ref[...].astype(MXU_DT)
        for r0 in range(0, nrow, rs):
            g_ref[pl.ds(r0, rs), :] = _dot(dy_ref[pl.ds(r0, rs), :].astype(MXU_DT), cb, _NT)
        a_r = ar_ref[...]
        a_i = ai_ref[...]
        rid = lax.broadcasted_iota(jnp.int32, (8, hs), 0)

        def run(descending):
            def tile_step(tt, carry):
                g_r, g_i, acc_r, acc_i = carry
                tidx = nt - 1 - tt if descending else tt
                base = pl.multiple_of(tidx * 8, 8)
                tile = g_ref[pl.ds(base, 8), :]
                o_r = jnp.zeros((8, hs), F32)
                o_i = jnp.zeros((8, hs), F32)
                for q in range(8):
                    rr = 7 - q if descending else q
                    n_r = tile[rr:rr + 1, :hs] + (a_r * g_r + a_i * g_i)
                    n_i = tile[rr:rr + 1, hs:] + (a_r * g_i - a_i * g_r)
                    g_r, g_i = n_r, n_i
                    o_r = jnp.where(rid == rr, g_r, o_r)
                    o_i = jnp.where(rid == rr, g_i, o_i)
                g_ref[pl.ds(base, 8), pl.ds(0, hs)] = o_r
                g_ref[pl.ds(base, 8), pl.ds(hs, hs)] = o_i
                xt = x_ref[pl.ds(base, 8), :]
                if descending:
                    nbase = pl.multiple_of(jnp.maximum(tidx - 1, 0) * 8, 8)
                    edge = x_ref[pl.ds(nbase, 8), :][7:8, :] * jnp.where(tidx > 0, 1.0, 0.0).astype(F32)
                    prev = jnp.where(lax.broadcasted_iota(jnp.int32, xt.shape, 0) == 0, edge, pltpu.roll(xt, 1, 0))
                else:
                    nbase = pl.multiple_of(jnp.minimum(tidx + 1, nt - 1) * 8, 8)
                    edge = x_ref[pl.ds(nbase, 8), :][0:1, :] * jnp.where(tidx < nt - 1, 1.0, 0.0).astype(F32)
                    prev = jnp.where(lax.broadcasted_iota(jnp.int32, xt.shape, 0) == 7, edge, pltpu.roll(xt, 7, 0))
                p_r = prev[:, :hs]
                p_i = prev[:, hs:]
                acc_r = acc_r + o_r * p_r + o_i * p_i
                acc_i = acc_i + o_i * p_r - o_r * p_i
                return g_r, g_i, acc_r, acc_i

            z = jnp.zeros((1, hs), F32)
            z8 = jnp.zeros((8, hs), F32)
            _, _, acc_r, acc_i = lax.fori_loop(0, nt, tile_step, (z, z, z8, z8))
            dar_ref[...] = jnp.sum(acc_r, axis=0, keepdims=True)
            dai_ref[...] = jnp.sum(acc_i, axis=0, keepdims=True)

        @pl.when(d == 0)
        def _():
            run(True)

        @pl.when(d == 1)
        def _():
            run(False)

        wb = w_ref[...].astype(MXU_DT)
        for r0 in range(0, nrow, rs):
            part = _dot(g_ref[pl.ds(r0, rs), :].astype(MXU_DT), wb, _NT)

            @pl.when(d == 0)
            def _(part=part, r0=r0):
                du_ref[pl.ds(r0, rs), :] = dui_ref[pl.ds(r0, rs), :] + part

            @pl.when(d == 1)
            def _(part=part, r0=r0):
                du_ref[pl.ds(r0, rs), :] += part

        dw_ref[...] = _dot(u_ref[...].astype(MXU_DT), g_ref[...].astype(MXU_DT), _TN)
        dcm_ref[...] = _dot(x_ref[...].astype(MXU_DT), dy_ref[...].astype(MXU_DT), _TN)

    in_specs = [
        pl.BlockSpec((nrow, LANES), lambda s, d: (0, u_blk + s)),
        pl.BlockSpec((nrow, LANES), lambda s, d: (0, s)),
        pl.BlockSpec((nrow, LANES), lambda s, d: (0, s)),
        pl.BlockSpec((None, nrow, 2 * hs), lambda s, d: (d, 0, s)),
        pl.BlockSpec((None, None, 1, hs), lambda s, d: (d, s, 0, 0)),
        pl.BlockSpec((None, None, 1, hs), lambda s, d: (d, s, 0, 0)),
        pl.BlockSpec((None, None, LANES, 2 * hs), lambda s, d: (d, s, 0, 0)),
        pl.BlockSpec((None, None, 2 * hs, LANES), lambda s, d: (d, s, 0, 0)),
    ]
    out_specs = [
        pl.BlockSpec((nrow, LANES), lambda s, d: (0, s)),
        pl.BlockSpec((None, None, LANES, 2 * hs), lambda s, d: (d, s, 0, 0)),
        pl.BlockSpec((None, None, 2 * hs, LANES), lambda s, d: (d, s, 0, 0)),
        pl.BlockSpec((None, None, 1, hs), lambda s, d: (d, s, 0, 0)),
        pl.BlockSpec((None, None, 1, hs), lambda s, d: (d, s, 0, 0)),
    ]
    out_shape = [jax.ShapeDtypeStruct((nrow, nb * LANES), F32),
                 jax.ShapeDtypeStruct(w.shape, F32), jax.ShapeDtypeStruct(cm.shape, F32),
                 jax.ShapeDtypeStruct(ar.shape, F32), jax.ShapeDtypeStruct(ai.shape, F32)]
    return _call(body, grid=(nb, 2), in_specs=in_specs, out_specs=out_specs, out_shape=out_shape,
                 scratch_shapes=[pltpu.VMEM((nrow, 2 * hs), F32)], sem=("parallel", "arbitrary"),
                 name=name, args=(proj, dy, du_in, xs, ar, ai, w, cm), side=side)


def _me():
    return lax.axis_index("x"), lax.axis_index("y"), lax.axis_index("c")


def all_gather(shards):
    na = len(shards)

    def plan(x_refs, out_refs, sems):
        send_sems, recv_sems, local_sems = sems
        x, y, c = _me()
        me, sibling = (x, y, c), (x, y, 1 - c)
        chips = [(1 - x, y), (x, 1 - y), (1 - x, 1 - y)]

        def slot(a, px, py, pc):
            return out_refs[a].at[4 * px + 2 * py + pc]

        def copy(a, k, block, to, src=None):
            return pltpu.make_async_remote_copy(
                src_ref=slot(a, *block) if src is None else src, dst_ref=slot(a, *block),
                send_sem=send_sems.at[a, k], recv_sem=recv_sems.at[a, k], device_id=to, device_id_type=MESH)

        mine = [pltpu.make_async_copy(x_refs[a], slot(a, *me), local_sems.at[a]) for a in range(na)]
        first = []
        for a in range(na):
            first.append(copy(a, 0, me, sibling, src=x_refs[a]))
            first += [copy(a, 1 + j, me, (*chip, c), src=x_refs[a]) for j, chip in enumerate(chips)]
        return me, sibling, chips, c, copy, mine, first

    def start(x_refs, out_refs, sems):
        _, _, _, _, _, mine, first = plan(x_refs, out_refs, sems)
        for cp in mine + first:
            cp.start()

    def finish(x_refs, out_refs, sems):
        me, sibling, chips, c, copy, mine, first = plan(x_refs, out_refs, sems)
        passed = []
        for j, chip in enumerate(chips):
            for a in range(na):
                copy(a, 1 + j, (*chip, c), me).wait_recv()
                fwd = copy(a, 4 + j, (*chip, c), sibling)
                fwd.start()
                passed.append(fwd)
        for a in range(na):
            copy(a, 0, sibling, me).wait_recv()
            for j, chip in enumerate(chips):
                copy(a, 4 + j, (*chip, 1 - c), me).wait_recv()
        for cp in first + passed:
            cp.wait_send()
        for cp in mine:
            cp.wait()

    return Side(shards, [jax.ShapeDtypeStruct((8,) + s.shape, s.dtype) for s in shards],
                [pltpu.SemaphoreType.DMA((na, 7)), pltpu.SemaphoreType.DMA((na, 7)), pltpu.SemaphoreType.DMA((na,))],
                start, finish)


_AXES = ("x", "y", "c")


def exchange(bufs, axes, *, half):
    na = len(bufs)
    if isinstance(half, bool):
        half = [half] * na

    def copies(in_refs, out_refs, sems):
        send_sems, recv_sems = sems
        me = _me()
        cps = []
        for a in range(na):
            bit = me[_AXES.index(axes[a])]
            peer = tuple(1 - v if ax == axes[a] else v for ax, v in zip(_AXES, me))
            cps.append(pltpu.make_async_remote_copy(
                src_ref=in_refs[a].at[1 - bit] if half[a] else in_refs[a], dst_ref=out_refs[a],
                send_sem=send_sems.at[a], recv_sem=recv_sems.at[a], device_id=peer, device_id_type=MESH))
        return cps

    def start(*refs):
        for cp in copies(*refs):
            cp.start()

    def finish(*refs):
        for cp in copies(*refs):
            cp.wait()

    return Side(bufs, [jax.ShapeDtypeStruct(b.shape[1:] if h else b.shape, b.dtype) for b, h in zip(bufs, half)],
                [pltpu.SemaphoreType.DMA((na,)), pltpu.SemaphoreType.DMA((na,))], start, finish)


def all_to_all(pieces, slot_fns):
    na = len(pieces)

    def copies(in_refs, out_refs, sems):
        send_sems, recv_sems, local_sems = sems
        x, y, c = _me()
        cps = []
        for a in range(na):
            mine = slot_fns[a](x, y, c)
            for k in range(8):
                tx, ty, tc = x ^ (k // 4), y ^ ((k // 2) % 2), c ^ (k % 2)
                src = in_refs[a].at[4 * tx + 2 * ty + tc]
                dst = out_refs[a].at[mine]
                if k == 0:
                    cps.append(pltpu.make_async_copy(src, dst, local_sems.at[a]))
                else:
                    cps.append(pltpu.make_async_remote_copy(
                        src_ref=src, dst_ref=dst, send_sem=send_sems.at[a, k - 1], recv_sem=recv_sems.at[a, k - 1],
                        device_id=(tx, ty, tc), device_id_type=MESH))
        return cps

    def start(*refs):
        for cp in copies(*refs):
            cp.start()

    def finish(*refs):
        for cp in copies(*refs):
            cp.wait()

    return Side(pieces, [jax.ShapeDtypeStruct(p.shape, p.dtype) for p in pieces],
                [pltpu.SemaphoreType.DMA((na, 7)), pltpu.SemaphoreType.DMA((na, 7)), pltpu.SemaphoreType.DMA((na,))],
                start, finish)


def add_half(buf, recv, bit, *, name):
    c = recv.shape[-1]
    r = math.prod(recv.shape[:-1])
    t = _tile(r, max(8, (ADD_BLOCK_BYTES // (4 * c)) // 8 * 8), 8)

    def body(bit_ref, a_ref, b_ref, o_ref):
        o_ref[...] = a_ref[...] + b_ref[...]

    grid_spec = pltpu.PrefetchScalarGridSpec(
        num_scalar_prefetch=1, grid=(r // t,),
        in_specs=[pl.BlockSpec((None, t, c), lambda i, b: (b[0], i, 0)), pl.BlockSpec((t, c), lambda i, b: (i, 0))],
        out_specs=pl.BlockSpec((t, c), lambda i, b: (i, 0)))
    out = pl.pallas_call(body, grid_spec=grid_spec, out_shape=jax.ShapeDtypeStruct((r, c), F32),
                         compiler_params=_params(("parallel",)), name=name)(
                             bit, buf.reshape(2, r, c), recv.reshape(r, c))
    return out.reshape(recv.shape)


def _pack(arrs, cols, mult):
    flat = jnp.concatenate([a.reshape(-1) for a in arrs])
    n = flat.shape[0]
    padded = -(-n // (cols * mult)) * (cols * mult)
    return jnp.pad(flat, (0, padded - n)).reshape(padded // cols, cols)


def _unpack(flat, shapes):
    out, off = [], 0
    for s in shapes:
        n = math.prod(s)
        out.append(flat[off:off + n].reshape(s))
        off += n
    return out


def _elementwise(fn, arrs, nout, name):
    r, c = arrs[0].shape
    t = _tile(r, ROW_TILE, 8)
    return rw_fwd(fn, [rows(a, t) for a in arrs], [(c, False)] * nout, nrow=r, t=t, name=name)


def kernel(x, ln_g, w_in, conv_w, a_log, dt_bias, head_norm_g, lam_re, lam_im, log_dt, b_re, b_im, c_re, c_im, d_skip, w_glu, b_glu, w_pa, w_pb, b_gate, w_out, final_g, loss_target, m_ln_g, m_w_in, m_conv_w, m_a_log, m_dt_bias, m_head_norm_g, m_lam_re, m_lam_im, m_log_dt, m_b_re, m_b_im, m_c_re, m_c_im, m_d_skip, m_w_glu, m_b_glu, m_w_pa, m_w_pb, m_b_gate, m_w_out, m_final_g, v_ln_g, v_w_in, v_conv_w, v_a_log, v_dt_bias, v_head_norm_g, v_lam_re, v_lam_im, v_log_dt, v_b_re, v_b_im, v_c_re, v_c_im, v_d_skip, v_w_glu, v_b_glu, v_w_pa, v_w_pb, v_b_gate, v_w_out, v_final_g):
    env = dict(locals())
    wts = {n: env[n] for n in WEIGHTS}
    mom_m = {n: env["m_" + n] for n in WEIGHTS}
    mom_v = {n: env["v_" + n] for n in WEIGHTS}

    xin = x[0]
    tgt = loss_target[0]
    nrow, dm = xin.shape
    depth = ln_g.shape[0]
    nh = dm // (2 * HEAD_DIM)
    wa = nh * HEAD_DIM
    wb = dm // 2
    ngrp = wb // S5_GROUP_CH
    pw = w_in.shape[-1]
    t = min(ROW_TILE, nrow)

    o_ba = 4 * wa
    o_u = o_ba + 4 * nh
    n_main = 4 * wa + 2 * wb + 2 * dm
    projp = -(-(n_main + LANES) // 512) * 512
    blk_za = 3 * wa // HEAD_DIM
    blk_u = 4 * wa // LANES
    ba_blk = n_main // LANES
    cx, cy, cc = _me()

    jb = o_ba // pw
    assert (o_u - 1) // pw == jb
    cut_lo, cut = o_ba - jb * pw, o_u - o_ba
    wide = -(-pw // LANES) * LANES + LANES
    col = lax.broadcasted_iota(jnp.int32, (pw, wide), 1)
    row = lax.broadcasted_iota(jnp.int32, (pw, wide), 0)
    holds_cut = (4 * cx + 2 * cy + cc) == jb
    src_plain = jnp.where(col < pw, col, -1)
    src_cut = jnp.where(col < cut_lo, col, jnp.where(col < pw - cut, col + cut, -1))
    src_cut = jnp.where((col >= wide - LANES) & (col < wide - LANES + cut), col - (wide - LANES) + cut_lo, src_cut)
    select = (row == jnp.where(holds_cut, src_cut, src_plain)).astype(WIRE_DT)
    w_in_tx = mm(w_in.reshape(depth * dm, pw), select, name="w_in_prepare").astype(WIRE_DT).reshape(depth, dm, wide)

    def gather_side(l):
        return all_gather([w_in_tx[l], w_glu[l].astype(WIRE_DT), w_pa[l].astype(WIRE_DT),
                           w_pb[l].astype(WIRE_DT), w_out[l].astype(WIRE_DT), conv_w[l]])

    def cat(g):
        return jnp.concatenate([g[j] for j in range(8)], axis=1)

    def assemble(gathered):
        g_in, g_glu, g_pa, g_pb, g_out, g_conv = gathered
        w_perm = jnp.concatenate(
            [g_in[j][:, :pw - cut if j == jb else pw] for j in range(8)]
            + [g_in[jb][:, wide - LANES:], jnp.zeros((dm, projp - n_main - LANES), WIRE_DT)], axis=1)
        return dict(w_in=w_perm, w_glu=g_glu.reshape(wb, wb), w_pa=cat(g_pa), w_pb=cat(g_pb),
                    w_out=g_out.reshape(dm, dm), conv_w=cat(g_conv))

    full = [assemble(run_side(gather_side(0), name="gather_weights"))]

    def small(l):
        z = jnp.zeros((1, LANES - 4 * nh), F32)
        alog_row = jnp.concatenate([jnp.zeros((1, 2 * nh), F32), a_log[l].reshape(1, 2 * nh), z], axis=1)
        dtb_row = jnp.concatenate([jnp.zeros((1, 2 * nh), F32), dt_bias[l].reshape(1, 2 * nh), z], axis=1)
        return alog_row, dtb_row

    saved = []
    cur = xin
    for l in range(depth):
        fw = full[l]
        (hh,) = rw_fwd(fn_rms, [rows(cur, t), bcast(ln_g[l][None])], [(dm, False)], nrow=nrow, t=t, name="rms_fwd")
        proj = mm(hh, fw["w_in"], name="proj_fwd")
        (qkv,) = rw_fwd(make_fn_prep(nh), [rows(proj, nrow, HEAD_DIM, 0, True), bcast(fw["conv_w"], HEAD_DIM, 0, True)],
                        [(HEAD_DIM, True)], nrow=nrow, t=nrow, ncol=3 * nh, name="prep_fwd")
        alog_row, dtb_row = small(l)
        nxt_gather = gather_side(l + 1) if l + 1 < depth else None
        o_dir, s_hist = gdn_fwd(qkv, proj, ba_blk, alog_row, dtb_row, n_heads=nh, side=nxt_gather, name="gdn_fwd")
        if nxt_gather is not None:
            full.append(assemble(nxt_gather.result))
        hn_ins = [rows3(o_dir, 0, t, HEAD_DIM), rows3(o_dir, 1, t, HEAD_DIM), rows(proj, t, HEAD_DIM, blk_za, True),
                  bcast(head_norm_g[l][None])]
        (ya_in,) = rw_fwd(fn_headnorm, hn_ins, [(HEAD_DIM, True)], nrow=nrow, t=t, ncol=nh, name="headnorm_fwd")
        mats, mats_vjp = jax.vjp(s5_mats, lam_re[l], lam_im[l], log_dt[l], b_re[l], b_im[l], c_re[l], c_im[l])
        yd, xs = s5_fwd(proj, blk_u, *mats, name="s5_fwd")
        ge_ins = [rows3(yd, 0, t, wb, False), rows3(yd, 1, t, wb, False), rows(proj, t, wb, 4 * wa // wb),
                  bcast(d_skip[l][None])]
        (ys,) = rw_fwd(fn_gelu, ge_ins, [(wb, False)], nrow=nrow, t=t, name="gelu_fwd")
        glu = mm(ys, fw["w_glu"], name="glu_fwd")
        gl_ins = [rows(ys, t), rows(glu, t), rows(proj, t, wb, (4 * wa + wb) // wb), bcast(b_glu[l][None])]
        (yb_in,) = rw_fwd(fn_glu, gl_ins, [(wb, False)], nrow=nrow, t=t, name="glugate_fwd")
        y_a = mm(ya_in, fw["w_pa"], name="pa_fwd")
        y_b = mm(yb_in, fw["w_pb"], name="pb_fwd")
        bg = b_gate[l][None]
        mg_ins = [rows(proj, t, dm, 3), rows(proj, t, dm, 4), rows(y_a, t), rows(y_b, t),
                  bcast(bg, dm, 0), bcast(bg, dm, 1)]
        (merged,) = rw_fwd(fn_merge, mg_ins, [(dm, False)], nrow=nrow, t=t, name="merge_fwd")
        nxt = mm(merged, fw["w_out"], add=cur, name="out_fwd")
        saved.append(dict(x=cur, h=hh, proj=proj, qkv=qkv, o_dir=o_dir, s_hist=s_hist, hn_ins=hn_ins,
                          mats=mats, mats_vjp=mats_vjp, xs=xs, ge_ins=ge_ins, ys=ys, gl_ins=gl_ins,
                          ya_in=ya_in, yb_in=yb_in, mg_ins=mg_ins, merged=merged,
                          alog_row=alog_row, dtb_row=dtb_row))
        cur = nxt

    loss_ins = [rows(cur, t), rows(tgt, t), bcast(final_g[None])]
    (row_loss,) = rw_fwd(fn_loss, loss_ins, [(1, False)], nrow=nrow, t=t, name="loss_fwd")
    ones = jnp.ones((nrow, 1), F32)
    dcur, dfinal = rw_bwd(fn_loss, loss_ins, [[rows(ones, t)]], nrow=nrow, t=t,
                          row_grads=[(0, False)], bc_grads=[2], name="loss_bwd")
    loss = lax.psum(jnp.sum(row_loss), ("x", "y", "c"))

    coord = {"x": cx, "y": cy, "c": cc}
    routes = {"cxy": ("c", "x", "y"), "cyx": ("c", "y", "x")}
    route = ["cxy", "cyx", "cxy", "cxy", "cyx", "cyx", "cxy"]
    conv_order = [4 * ((q // 2) % 2) + 2 * (q % 2) + q // 4 for q in range(8)]
    cw = 3 * wa // 8
    g_final = {n: [None] * depth for n in SHARDED}
    gsh = {n: [None] * depth for n in SHARDED}

    def own_columns(z):
        un = jnp.concatenate([z[:, :o_ba], z[:, n_main:n_main + 4 * nh], z[:, o_ba:n_main]], axis=1)
        return jnp.stack([un[:, d * pw:(d + 1) * pw] for d in range(8)])

    def rs_begin(l):
        in_a, in_b = gsh["w_in"][l]
        conv_buf = jnp.stack([gsh["conv_w"][l][:, d * cw:(d + 1) * cw] for d in conv_order])
        return dict(layer=l, stage=0, bufs=[
            in_a.reshape(8, dm // 16, projp), in_b.reshape(8, dm // 16, projp), gsh["w_out"][l],
            gsh["w_glu"][l], gsh["w_pa"][l], gsh["w_pb"][l], conv_buf])

    def rs_side(st):
        if st is None:
            return None
        if st["stage"] < 3:
            st["axes"] = [routes[r][st["stage"]] for r in route]
            st["bufs"] = [b.reshape((2, b.shape[0] // 2) + b.shape[1:]) for b in st["bufs"]]
            if st.get("whole") is not None:
                return exchange(st["bufs"] + [st["whole"]], st["axes"] + [routes["cxy"][st["stage"]]],
                                half=[True] * len(st["bufs"]) + [False])
            return exchange(st["bufs"], st["axes"], half=True)
        st["bufs"] = [b.reshape(b.shape[1:]) for b in st["bufs"]]
        return all_to_all([own_columns(st["bufs"][0]), own_columns(st["bufs"][1])],
                          [lambda x, y, c: 4 * c + 2 * x + y, lambda x, y, c: 4 * c + 2 * y + x])

    def rs_absorb(st, side):
        if st is None:
            return
        if st["stage"] < 3:
            if st.get("whole") is not None:
                (st["whole"],) = _elementwise(make_fn_sum(2), [st["whole"], side.result[-1]], 1, "ar_add")
            st["bufs"] = [add_half(b, r, coord[ax].astype(jnp.int32).reshape(1), name=f"rs_add_{st['stage']}")
                          for b, r, ax in zip(st["bufs"], side.result, st["axes"])]
        else:
            got_a, got_b = side.result
            g_final["w_in"][st["layer"]] = jnp.concatenate(
                [got_a.reshape(dm // 2, pw), got_b.reshape(dm // 2, pw)], axis=0)
            for n, b in zip(["w_out", "w_glu", "w_pa", "w_pb", "conv_w"], st["bufs"][2:]):
                g_final[n][st["layer"]] = b
        st["stage"] += 1

    grep = {n: [None] * depth for n in REPLICATED if n != "final_g"}
    pending = None
    for l in reversed(range(depth)):
        fw, sv = full[l], saved[l]
        dmerged = mm(dcur, fw["w_out"], tb=True, name="out_bwd_x")
        gsh["w_out"][l] = mm(sv["merged"], dcur, ta=True, scatter=("rows", slot_cxy), name="out_bwd_w")
        side = rs_side(pending)
        dla, dlb, dya, dyb, dbga, dbgb = rw_bwd(
            fn_merge, sv["mg_ins"], [[rows(dmerged, t)]], nrow=nrow, t=t,
            row_grads=[(0, False), (1, False), (2, False), (3, False)], bc_grads=[4, 5], side=side, name="merge_bwd")
        rs_absorb(pending, side)
        grep["b_gate"][l] = jnp.concatenate([dbga.reshape(dm), dbgb.reshape(dm)])
        dya_in = mm(dya, fw["w_pa"], tb=True, name="pa_bwd_x")
        gsh["w_pa"][l] = mm(sv["ya_in"], dya, ta=True, scatter=("cols", slot_cyx), name="pa_bwd_w")
        dyb_in = mm(dyb, fw["w_pb"], tb=True, name="pb_bwd_x")
        gsh["w_pb"][l] = mm(sv["yb_in"], dyb, ta=True, scatter=("cols", slot_cyx), name="pb_bwd_w")
        dys1, dglu, dzb, dbglu = rw_bwd(
            fn_glu, sv["gl_ins"], [[rows(dyb_in, t)]], nrow=nrow, t=t,
            row_grads=[(0, False), (1, False), (2, False)], bc_grads=[3], name="glugate_bwd")
        grep["b_glu"][l] = dbglu.reshape(wb)
        dys2 = mm(dglu, fw["w_glu"], tb=True, name="glu_bwd_x")
        gsh["w_glu"][l] = mm(sv["ys"], dglu, ta=True, scatter=("rows", slot_cxy), name="glu_bwd_w")
        dyd, du1, ddskip = rw_bwd(
            fn_gelu, sv["ge_ins"], [[rows(dys1, t), rows(dys2, t)]], nrow=nrow, t=t,
            row_grads=[(0, False), (2, False)], bc_grads=[3], name="gelu_bwd")
        grep["d_skip"][l] = ddskip.reshape(wb)
        side = rs_side(pending)
        du, dw_s5, dcm_s5, dar, dai = s5_bwd(sv["proj"], blk_u, *sv["mats"], sv["xs"], dyd, du1, side=side,
                                             name="s5_bwd")
        rs_absorb(pending, side)
        g_lr, g_li, g_ldt, g_br, g_bi, g_cr, g_ci = sv["mats_vjp"]((dar, dai, dw_s5, dcm_s5))
        for nme, val in zip(["lam_re", "lam_im", "log_dt", "b_re", "b_im", "c_re", "c_im"],
                            [g_lr, g_li, g_ldt, g_br, g_bi, g_cr, g_ci]):
            grep[nme][l] = val
        do, dza, dhn = rw_bwd(fn_headnorm, sv["hn_ins"], [[rows(dya_in, t, HEAD_DIM, 0, True)]], nrow=nrow, t=t,
                              ncol=nh, row_grads=[(0, True), (2, True)], bc_grads=[3], name="headnorm_bwd")
        grep["head_norm_g"][l] = jnp.sum(dhn, axis=0).reshape(HEAD_DIM)
        side = rs_side(pending)
        dqkv, dba_all, dal, ddt = gdn_bwd(sv["qkv"], sv["proj"], ba_blk, sv["alog_row"], sv["dtb_row"],
                                          sv["s_hist"], do, n_heads=nh, side=side, name="gdn_bwd")
        rs_absorb(pending, side)
        grep["a_log"][l] = jnp.sum(dal, axis=(0, 1))[2 * nh:4 * nh].reshape(2, nh)
        grep["dt_bias"][l] = jnp.sum(ddt, axis=(0, 1))[2 * nh:4 * nh].reshape(2, nh)
        (dba,) = rw_fwd(make_fn_sum(2), [rows3(dba_all, p, t, LANES, False) for p in range(2)],
                        [(LANES, False)], nrow=nrow, t=t, name="dba_sum")
        prep_ins = [rows(sv["proj"], nrow, HEAD_DIM, 0, True), bcast(fw["conv_w"], HEAD_DIM, 0, True)]
        dq_cots = [(dqkv.reshape(2, 3 * nrow, wa), (None, nrow, HEAD_DIM),
                    (lambda j, i, dd=dd: (dd, j // nh, j % nh))) for dd in range(2)]
        dqkv_raw, dconv = rw_bwd(make_fn_prep(nh), prep_ins, [dq_cots], nrow=nrow, t=nrow, ncol=3 * nh,
                                 row_grads=[(0, True)], bc_grads=[1], name="prep_bwd")
        gsh["conv_w"][l] = jnp.transpose(dconv, (1, 0, 2)).reshape(CONV_K, 3 * wa)
        dproj = jnp.concatenate([dqkv_raw, dza, du, dzb, dla, dlb, dba,
                                 jnp.zeros((nrow, projp - n_main - LANES), F32)], axis=1)
        side = rs_side(pending)
        dh = mm(dproj, fw["w_in"], tb=True, side=side, name="proj_bwd_x")
        rs_absorb(pending, side)
        gsh["w_in"][l] = [mm(sv["h"], dproj, ta=True, m_part=(part, 2), name="proj_bwd_w") for part in range(2)]
        dcur, dlng = rw_bwd(fn_rms, [rows(sv["x"], t), bcast(ln_g[l][None])], [[rows(dh, t)]], nrow=nrow, t=t,
                            row_grads=[(0, False)], bc_grads=[1], residual=rows(dcur, t), name="rms_bwd")
        grep["ln_g"][l] = dlng.reshape(dm)
        pending = rs_begin(l)
    grad_x = dcur[None]
    rep_list = [jnp.stack(grep[n]) for n in REPLICATED if n != "final_g"] + [dfinal.reshape(dm)]
    rep_shapes = [a.shape for a in rep_list]
    pending["whole"] = _pack(rep_list, COMM_COLS, ROW_TILE)
    for stage in range(RS_STAGES):
        side = rs_side(pending)
        run_side(side, name=f"rs_stage_{stage}")
        rs_absorb(pending, side)
    grads = {n: jnp.stack(g_final[n]) for n in SHARDED}
    for n, val in zip(REPLICATED, _unpack(pending["whole"].reshape(-1), rep_shapes)):
        grads[n] = val

    deltas, new_m, new_v = {}, {}, {}
    big = ["w_in", "w_glu", "w_pa", "w_pb", "w_out"]
    for n in big:
        shp = wts[n].shape
        two = [a.reshape(-1, shp[-1]) for a in (wts[n], grads[n], mom_m[n], mom_v[n])]
        d_, m_, v_ = _elementwise(fn_adamw, two, 3, "adamw_" + n)
        deltas[n], new_m[n], new_v[n] = d_.reshape(shp), m_.reshape(shp), v_.reshape(shp)
    rest = [n for n in WEIGHTS if n not in big]
    rest_shapes = [wts[n].shape for n in rest]
    packed = [_pack([src[n] for n in rest], COMM_COLS, ROW_TILE) for src in (wts, grads, mom_m, mom_v)]
    outs = _elementwise(fn_adamw, packed, 3, "adamw_small")
    for dst, arr in zip((deltas, new_m, new_v), outs):
        for n, val in zip(rest, _unpack(arr.reshape(-1), rest_shapes)):
            dst[n] = val

    return (loss, grad_x, *[grads[n] for n in WEIGHTS], *[deltas[n] for n in WEIGHTS],
            *[new_m[n] for n in WEIGHTS], *[new_v[n] for n in WEIGHTS])
```

```python
import functools
import math

import jax
import jax.numpy as jnp
from jax import lax
from jax.experimental import pallas as pl
from jax.experimental.pallas import tpu as pltpu

F32 = jnp.float32
MXU_DT = jnp.bfloat16
WIRE_DT = jnp.bfloat16

HEAD_DIM = 128
CHUNK = 64
CONV_K = 5
S5_GROUP_CH = 16
S5_STATE = 64
S5_BLOCK_GROUPS = 8
S5_BLOCK_STATE = S5_BLOCK_GROUPS * S5_STATE
RMS_EPS = 1e-6
LANES = 128
VMEM_LIMIT = 56 * 1024 * 1024
ROW_TILE = 256
COMM_COLS = 1024
ADD_BLOCK_BYTES = 2 * 1024 * 1024
RS_STAGES = 4

ADAM_LR = 0.001
ADAM_B1 = 0.9
ADAM_B2 = 0.999
ADAM_EPS = 1e-08
ADAM_WD = 0.01
ADAM_STEP = 10

WEIGHTS = ['ln_g', 'w_in', 'conv_w', 'a_log', 'dt_bias', 'head_norm_g', 'lam_re', 'lam_im', 'log_dt',
           'b_re', 'b_im', 'c_re', 'c_im', 'd_skip', 'w_glu', 'b_glu', 'w_pa', 'w_pb', 'b_gate',
           'w_out', 'final_g']
SHARDED = ['w_in', 'w_glu', 'w_pa', 'w_pb', 'w_out', 'conv_w']
REPLICATED = [n for n in WEIGHTS if n not in SHARDED]
MESH = pl.DeviceIdType.MESH


def _params(sem=None):
    return pltpu.CompilerParams(dimension_semantics=sem, vmem_limit_bytes=VMEM_LIMIT)


def _tile(n, cap, q=LANES):
    t = (min(n, cap) // q) * q
    while t > q and n % t:
        t -= q
    return t if t > 0 and n % t == 0 else n


class Side:
    def __init__(self, ins, out_sd, sems, start, finish):
        self.ins, self.out_sd, self.sems, self.start, self.finish = list(ins), list(out_sd), list(sems), start, finish
        self.result = None


def _call(body, *, grid, in_specs, out_specs, out_shape, scratch_shapes=(), sem, name, args, side=None):
    in_specs, out_specs, out_shape = list(in_specs), list(out_specs), list(out_shape)
    scratch_shapes = list(scratch_shapes)
    if side is None:
        return pl.pallas_call(body, grid=grid, in_specs=in_specs, out_specs=out_specs, out_shape=out_shape,
                              scratch_shapes=scratch_shapes, compiler_params=_params(sem), name=name)(*args)
    hbm = pl.BlockSpec(memory_space=pl.ANY)
    n_in, n_out, n_scr = len(in_specs), len(out_specs), len(scratch_shapes)
    s_in, s_out = len(side.ins), len(side.out_sd)

    def hosted(*refs):
        main_in, rest = refs[:n_in], refs[n_in:]
        side_in, rest = rest[:s_in], rest[s_in:]
        main_out, rest = rest[:n_out], rest[n_out:]
        side_out, rest = rest[:s_out], rest[s_out:]
        main_scr, sems = rest[:n_scr], rest[n_scr:]
        ids = [pl.program_id(k) for k in range(len(grid))]
        first = functools.reduce(jnp.logical_and, [i == 0 for i in ids])
        last = functools.reduce(jnp.logical_and, [i == g - 1 for i, g in zip(ids, grid)])

        @pl.when(first)
        def _():
            side.start(side_in, side_out, sems)

        body(*main_in, *main_out, *main_scr)

        @pl.when(last)
        def _():
            side.finish(side_in, side_out, sems)

    outs = pl.pallas_call(
        hosted, grid=grid, in_specs=in_specs + [hbm] * s_in, out_specs=out_specs + [hbm] * s_out,
        out_shape=out_shape + side.out_sd, scratch_shapes=scratch_shapes + side.sems,
        compiler_params=_params(("arbitrary",) * len(grid)), name=name)(*args, *side.ins)
    side.result = list(outs[n_out:])
    return list(outs[:n_out])


def run_side(side, *, name):
    hbm = pl.BlockSpec(memory_space=pl.ANY)
    s_in, s_out = len(side.ins), len(side.out_sd)

    def body(*refs):
        side_in, side_out, sems = refs[:s_in], refs[s_in:s_in + s_out], refs[s_in + s_out:]
        side.start(side_in, side_out, sems)
        side.finish(side_in, side_out, sems)

    side.result = list(pl.pallas_call(body, out_shape=side.out_sd, in_specs=[hbm] * s_in, out_specs=[hbm] * s_out,
                                      scratch_shapes=side.sems, name=name)(*side.ins))
    return side.result


def slot_cxy(d):
    return 4 * (d % 2) + 2 * (d // 4) + (d // 2) % 2


def slot_cyx(d):
    return 4 * (d % 2) + 2 * ((d // 2) % 2) + d // 4


def mm(a, b, *, ta=False, tb=False, add=None, m_part=None, scatter=None, side=None, name):
    m = a.shape[1] if ta else a.shape[0]
    k = a.shape[0] if ta else a.shape[1]
    n = b.shape[0] if tb else b.shape[1]
    m_off = 0
    if m_part is not None:
        m = m // m_part[1]
        m_off = m_part[0]
    tm, tn, tk = _tile(m, 1024), _tile(n, 512), _tile(k, 2048)
    if scatter is not None and scatter[0] == "rows":
        tm = m // 8
    if scatter is not None and scatter[0] == "cols":
        tn = n // 8
    if m_part is not None:
        assert tm == m
    nk = k // tk
    dn = (((0 if ta else 1,), (1 if tb else 0,)), ((), ()))

    def body(*refs):
        if add is None:
            a_ref, b_ref, o_ref, acc = refs
        else:
            a_ref, b_ref, add_ref, o_ref, acc = refs
        kk = pl.program_id(2)

        @pl.when(kk == 0)
        def _():
            acc[...] = jnp.zeros_like(acc)

        acc[...] += lax.dot_general(a_ref[...].astype(MXU_DT), b_ref[...].astype(MXU_DT), dn,
                                    preferred_element_type=F32)

        @pl.when(kk == nk - 1)
        def _():
            r = acc[...]
            if add is not None:
                r = r + add_ref[...]
            o_ref[...] = r

    if ta:
        a_spec = pl.BlockSpec((tk, tm), lambda i, j, kk: (kk, i + m_off))
    else:
        a_spec = pl.BlockSpec((tm, tk), lambda i, j, kk: (i + m_off, kk))
    b_spec = pl.BlockSpec((tn, tk), lambda i, j, kk: (j, kk)) if tb else pl.BlockSpec((tk, tn), lambda i, j, kk: (kk, j))
    out_sd = jax.ShapeDtypeStruct((m, n), F32)
    if scatter is None:
        o_spec = pl.BlockSpec((tm, tn), lambda i, j, kk: (i, j))
    elif scatter[0] == "rows":
        o_spec = pl.BlockSpec((None, tm, tn), lambda i, j, kk: (scatter[1](i), 0, j))
        out_sd = jax.ShapeDtypeStruct((8, tm, n), F32)
    else:
        o_spec = pl.BlockSpec((None, tm, tn), lambda i, j, kk: (scatter[1](j), i, 0))
        out_sd = jax.ShapeDtypeStruct((8, m, tn), F32)
    ins, specs = [a, b], [a_spec, b_spec]
    if add is not None:
        ins.append(add)
        specs.append(o_spec)
    return _call(body, grid=(m // tm, n // tn, nk), in_specs=specs, out_specs=[o_spec], out_shape=[out_sd],
                 scratch_shapes=[pltpu.VMEM((tm, tn), F32)], sem=("parallel", "parallel", "arbitrary"),
                 name=name, args=ins, side=side)[0]


def rows(arr, t, width=None, base=0, per_j=False):
    width = arr.shape[1] if width is None else width
    return (arr, (t, width), lambda j, i: (i, base + (j if per_j else 0)))


def rows3(arr, lead, t, width, per_j=True):
    return (arr, (None, t, width), lambda j, i: (lead, i, j if per_j else 0))


def bcast(arr, width=None, base=0, per_j=False):
    width = arr.shape[1] if width is None else width
    return (arr, (arr.shape[0], width), lambda j, i: (0, base + (j if per_j else 0)))


def _specs(items):
    return [pl.BlockSpec(bs, im) for (_, bs, im) in items]


def rw_fwd(fn, ins, outs, *, nrow, t, ncol=1, name):
    out_specs = [pl.BlockSpec((t, w), (lambda j, i: (i, j)) if pj else (lambda j, i: (i, 0))) for (w, pj) in outs]
    out_shape = [jax.ShapeDtypeStruct((nrow, w * (ncol if pj else 1)), F32) for (w, pj) in outs]
    nin = len(ins)

    def body(*refs):
        j = pl.program_id(0)
        res = fn(j, *[r[...] for r in refs[:nin]])
        for o_ref, r in zip(refs[nin:], res):
            o_ref[...] = r

    return pl.pallas_call(
        body, grid=(ncol, nrow // t), in_specs=_specs(ins), out_specs=out_specs, out_shape=out_shape,
        compiler_params=_params(("parallel", "parallel")), name=name)(*[x[0] for x in ins])


def rw_bwd(fn, ins, cots, *, nrow, t, ncol=1, row_grads, bc_grads, residual=None, side=None, name):
    nin = len(ins)
    flat_cots = [c for group in cots for c in group]
    extra = [residual] if residual is not None else []
    out_specs, out_shape = [], []
    for idx, pj in row_grads:
        w = ins[idx][1][-1]
        out_specs.append(pl.BlockSpec((t, w), (lambda j, i: (i, j)) if pj else (lambda j, i: (i, 0))))
        out_shape.append(jax.ShapeDtypeStruct((nrow, w * (ncol if pj else 1)), F32))
    for idx in bc_grads:
        r, w = ins[idx][1]
        out_specs.append(pl.BlockSpec((None, r, w), lambda j, i: (j, 0, 0)))
        out_shape.append(jax.ShapeDtypeStruct((ncol, r, w), F32))

    def body(*refs):
        j = pl.program_id(0)
        i = pl.program_id(1)
        vals = [r[...] for r in refs[:nin]]
        pos = nin
        cts = []
        for group in cots:
            c = refs[pos][...]
            for q in range(1, len(group)):
                c = c + refs[pos + q][...]
            pos += len(group)
            cts.append(c)
        res_ref = refs[pos] if residual is not None else None
        pos += len(extra)
        outs = refs[pos:]
        _, vjp = jax.vjp(lambda *a: tuple(fn(j, *a)), *vals)
        grads = vjp(tuple(cts))
        for q, (idx, _) in enumerate(row_grads):
            g = grads[idx]
            if q == 0 and res_ref is not None:
                g = g + res_ref[...]
            outs[q][...] = g
        for q, idx in enumerate(bc_grads):
            o_ref = outs[len(row_grads) + q]

            @pl.when(i == 0)
            def _(o_ref=o_ref):
                o_ref[...] = jnp.zeros_like(o_ref)

            o_ref[...] += grads[idx]

    all_in = list(ins) + flat_cots + extra
    return _call(body, grid=(ncol, nrow // t), in_specs=_specs(all_in), out_specs=out_specs, out_shape=out_shape,
                 sem=("parallel", "arbitrary"), name=name, args=[x[0] for x in all_in], side=side)


def _silu(x):
    return x * jax.nn.sigmoid(x)


@jax.custom_vjp
def _softplus(x):
    return jnp.maximum(x, 0.0) + jnp.log1p(jnp.exp(-jnp.abs(x)))


def _softplus_fwd(x):
    return _softplus(x), x


def _softplus_bwd(x, ct):
    return (ct * jax.nn.sigmoid(x),)


_softplus.defvjp(_softplus_fwd, _softplus_bwd)


def _gelu(x):
    return 0.5 * x * (1.0 + jnp.tanh(math.sqrt(2.0 / math.pi) * (x + 0.044715 * (x * x * x))))


def _row_shift_impl(x, s):
    n = x.shape[0]
    if s == 0:
        return x
    rolled = pltpu.roll(x, (-s) % n, 0)
    t = lax.broadcasted_iota(jnp.int32, x.shape, 0)
    ok = (t + s >= 0) & (t + s < n)
    return jnp.where(ok, rolled, 0.0)


@functools.partial(jax.custom_vjp, nondiff_argnums=(1,))
def _row_shift(x, s):
    return _row_shift_impl(x, s)


def _row_shift_fwd(x, s):
    return _row_shift_impl(x, s), None


def _row_shift_bwd(s, _, ct):
    return (_row_shift_impl(ct, -s),)


_row_shift.defvjp(_row_shift_fwd, _row_shift_bwd)


def fn_rms(j, x, g):
    return (x * lax.rsqrt(jnp.mean(x * x, axis=-1, keepdims=True) + RMS_EPS) * g,)


def make_fn_prep(n_heads):
    pad = (CONV_K - 1) // 2

    def fn_prep(j, x, w):
        y = _row_shift(x, -pad) * w[0:1, :]
        for i in range(1, CONV_K):
            y = y + _row_shift(x, i - pad) * w[i:i + 1, :]
        a = _silu(y)
        scale = jnp.where(j < n_heads, HEAD_DIM ** -0.5, 1.0).astype(F32)
        nrm = a * lax.rsqrt(jnp.sum(a * a, axis=-1, keepdims=True) + RMS_EPS) * scale
        return (jnp.where(j < 2 * n_heads, nrm, a),)

    return fn_prep


def fn_headnorm(j, o0, o1, z, g):
    o = o0 + o1
    n = o * lax.rsqrt(jnp.mean(o * o, axis=-1, keepdims=True) + RMS_EPS) * g
    return (n * _silu(z),)


def fn_gelu(j, y0, y1, u, dsk):
    return (_gelu(y0 + y1 + u * dsk),)


def fn_glu(j, ys, logit, z, b):
    return (ys * jax.nn.sigmoid(logit + b) * _silu(z),)


def fn_merge(j, la, lb, ya, yb, ba, bb):
    return (jax.nn.sigmoid(la + ba) * ya + jax.nn.sigmoid(lb + bb) * yb,)


def fn_loss(j, x, t, g):
    y = x * lax.rsqrt(jnp.mean(x * x, axis=-1, keepdims=True) + RMS_EPS) * g
    e = y - t
    return (0.5 * jnp.mean(e * e, axis=-1, keepdims=True),)


def make_fn_sum(n):
    def fn_sum(j, *xs):
        s = xs[0]
        for q in range(1, n):
            s = s + xs[q]
        return (s,)

    return fn_sum


def fn_adamw(j, w, g, m, v):
    m2 = ADAM_B1 * m + (1.0 - ADAM_B1) * g
    v2 = ADAM_B2 * v + (1.0 - ADAM_B2) * (g * g)
    m_hat = m2 / (1.0 - ADAM_B1 ** ADAM_STEP)
    v_hat = v2 / (1.0 - ADAM_B2 ** ADAM_STEP)
    delta = -ADAM_LR * (m_hat / (jnp.sqrt(v_hat) + ADAM_EPS) + ADAM_WD * w)
    return delta, m2, v2


def _dot(a, b, dims, precision=None):
    return lax.dot_general(a, b, (dims, ((), ())), precision=precision, preferred_element_type=F32)


_NN = ((1,), (0,))
_NT = ((1,), (1,))
_TN = ((0,), (0,))
GDN_HEAD_BLOCK = 4


def _split(a):
    hi = a.astype(MXU_DT)
    return hi, (a - hi.astype(F32)).astype(MXU_DT)


def _dot3(a, b, dims):
    ah, al = _split(a)
    bh, bl = _split(b)
    return _dot(ah, bh, dims) + (_dot(al, bh, dims) + _dot(ah, bl, dims))


def _unit_inverse(lmat):
    r = lmat.shape[0]
    eye = (lax.broadcasted_iota(jnp.int32, (r, r), 0) == lax.broadcasted_iota(jnp.int32, (r, r), 1)).astype(F32)
    pw = -lmat
    tinv = eye + pw
    for _ in range(int(math.ceil(math.log2(CHUNK))) - 1):
        pw = _dot3(pw, pw, _NN)
        tinv = tinv + _dot3(tinv, pw, _NN)
    return tinv


@jax.custom_vjp
def tri_solve(lmat, rhs):
    return _dot3(_unit_inverse(lmat), rhs, _NN)


def _tri_solve_fwd(lmat, rhs):
    tinv = _unit_inverse(lmat)
    x = _dot3(tinv, rhs, _NN)
    return x, (tinv, x)


def _tri_solve_bwd(res, dx):
    tinv, x = res
    drhs = _dot3(tinv, dx, _TN)
    return -_dot3(drhs, x, _NT), drhs


tri_solve.defvjp(_tri_solve_fwd, _tri_solve_bwd)


def gdn_group(s, q, k, v, ba, alog, dtb, *, d, head0, n_heads):
    hb = len(s)
    c = q.shape[0]
    r = hb * c

    def stack(x):
        return jnp.concatenate([x[:, i * HEAD_DIM:(i + 1) * HEAD_DIM] for i in range(hb)], axis=0)

    def pick(x, lane0):
        lane = lax.broadcasted_iota(jnp.int32, x.shape, 1)
        return jnp.concatenate(
            [jnp.sum(jnp.where(lane == lane0 + i, x, 0.0), axis=1, keepdims=True) for i in range(hb)], axis=0)

    q4, k4, v4 = stack(q), stack(k), stack(v)
    beta = pick(jax.nn.sigmoid(ba), d * n_heads + head0)
    g = pick(-jnp.exp(alog) * _softplus(ba + dtb), 2 * n_heads + d * n_heads + head0)
    ii = lax.broadcasted_iota(jnp.int32, (r, r), 0)
    jj = lax.broadcasted_iota(jnp.int32, (r, r), 1)
    same = (ii // c) == (jj // c)
    rel = (ii - jj) * (1 - 2 * d)
    incl = same & (rel >= 0)
    strict = same & (rel > 0)
    incl_t = same & (rel <= 0)
    g_row = jnp.sum(jnp.where(ii == jj, g, 0.0), axis=0, keepdims=True)
    gc_col = jnp.sum(jnp.where(incl, g_row, 0.0), axis=1, keepdims=True)
    gc_row = jnp.sum(jnp.where(incl_t, g, 0.0), axis=0, keepdims=True)
    g_tot = jnp.sum(jnp.where(same, g_row, 0.0), axis=1, keepdims=True)
    decay = jnp.exp(jnp.where(incl, gc_col - gc_row, -1e30))
    kb = k4 * beta
    vb = v4 * beta
    lmat = jnp.where(strict, _dot(kb, k4, _NT) * decay, 0.0)
    uw = tri_solve(lmat, jnp.concatenate([vb, kb * jnp.exp(gc_col)], axis=1))
    u, w = uw[:, :HEAD_DIM], uw[:, HEAD_DIM:]
    qk = _dot(q4, k4, _NT) * decay
    qe = q4 * jnp.exp(gc_col)
    kd = k4 * jnp.exp(g_tot - gc_col)
    eg = jnp.exp(g_tot)
    v_new, o_s = [], []
    for i in range(hb):
        rs = slice(i * c, (i + 1) * c)
        ws = _dot(jnp.concatenate([w[rs], qe[rs]], axis=0), s[i], _NN)
        v_new.append(u[rs] - ws[:c])
        o_s.append(ws[c:])
    o4 = jnp.concatenate(o_s, axis=0) + _dot(qk, jnp.concatenate(v_new, axis=0), _NN)
    s_new = tuple(s[i] * eg[i * c:i * c + 1, :] + _dot(kd[i * c:(i + 1) * c], v_new[i], _TN) for i in range(hb))
    o = jnp.concatenate([o4[i * c:(i + 1) * c] for i in range(hb)], axis=1)
    return s_new, o


def _gdn_maps(n_chunks):
    def chunk_of(d, step):
        return step + d * (n_chunks - 1 - 2 * step)
    return chunk_of


def _heads(ref, h0, n):
    return ref[:, h0 * HEAD_DIM:(h0 + n) * HEAD_DIM]


def gdn_fwd(qkv, proj, ba_blk, alog_row, dtb_row, *, n_heads, side=None, name):
    nrow = qkv.shape[0]
    nc = nrow // CHUNK
    h = n_heads
    wa = h * HEAD_DIM
    hb = math.gcd(h, GDN_HEAD_BLOCK)
    chunk_of = _gdn_maps(nc)

    def body(q_ref, k_ref, v_ref, ba_ref, al_ref, dt_ref, o_ref, sh_ref, s_scr):
        d = pl.program_id(0)
        n = pl.program_id(1)

        @pl.when(n == 0)
        def _():
            s_scr[...] = jnp.zeros_like(s_scr)

        ba, al, dt = ba_ref[...], al_ref[...], dt_ref[...]
        for h0 in range(0, h, hb):
            s = tuple(s_scr[h0 + i] for i in range(hb))
            for i in range(hb):
                sh_ref[h0 + i] = s[i]
            s2, o = gdn_group(s, _heads(q_ref, h0, hb), _heads(k_ref, h0, hb), _heads(v_ref, h0, hb), ba, al, dt,
                              d=d, head0=h0, n_heads=h)
            o_ref[:, h0 * HEAD_DIM:(h0 + hb) * HEAD_DIM] = o
            for i in range(hb):
                s_scr[h0 + i] = s2[i]

    blk = (CHUNK, wa)
    in_specs = [
        pl.BlockSpec(blk, lambda d, n: (chunk_of(d, n), 0)),
        pl.BlockSpec(blk, lambda d, n: (chunk_of(d, n), 1)),
        pl.BlockSpec(blk, lambda d, n: (chunk_of(d, n), 2)),
        pl.BlockSpec((CHUNK, LANES), lambda d, n: (chunk_of(d, n), ba_blk)),
        pl.BlockSpec((1, LANES), lambda d, n: (0, 0)),
        pl.BlockSpec((1, LANES), lambda d, n: (0, 0)),
    ]
    out_specs = [
        pl.BlockSpec((None, CHUNK, wa), lambda d, n: (d, chunk_of(d, n), 0)),
        pl.BlockSpec((h, None, HEAD_DIM, HEAD_DIM), lambda d, n: (d, n, 0, 0)),
    ]
    out_shape = [jax.ShapeDtypeStruct((2, nrow, wa), F32),
                 jax.ShapeDtypeStruct((2 * h, nc, HEAD_DIM, HEAD_DIM), F32)]
    return _call(body, grid=(2, nc), in_specs=in_specs, out_specs=out_specs, out_shape=out_shape,
                 scratch_shapes=[pltpu.VMEM((h, HEAD_DIM, HEAD_DIM), F32)], sem=("parallel", "arbitrary"),
                 name=name, args=(qkv, qkv, qkv, proj, alog_row, dtb_row), side=side)


def gdn_bwd(qkv, proj, ba_blk, alog_row, dtb_row, s_hist, do, *, n_heads, side=None, name):
    nrow = qkv.shape[0]
    nc = nrow // CHUNK
    h = n_heads
    wa = h * HEAD_DIM
    hb = math.gcd(h, GDN_HEAD_BLOCK)
    chunk_of = _gdn_maps(nc)

    def cb(d, n):
        return chunk_of(d, nc - 1 - n)

    def body(q_ref, k_ref, v_ref, ba_ref, al_ref, dt_ref, sh_ref, do_ref,
             dqkv_ref, dba_ref, dal_ref, ddt_ref, ds_scr):
        d = pl.program_id(0)
        n = pl.program_id(1)

        @pl.when(n == 0)
        def _():
            ds_scr[...] = jnp.zeros_like(ds_scr)
            dal_ref[...] = jnp.zeros_like(dal_ref)
            ddt_ref[...] = jnp.zeros_like(ddt_ref)

        ba, al, dt = ba_ref[...], al_ref[...], dt_ref[...]
        dba_sum = jnp.zeros_like(ba)
        dal_sum = jnp.zeros_like(al)
        ddt_sum = jnp.zeros_like(dt)
        for h0 in range(0, h, hb):
            f = functools.partial(gdn_group, d=d, head0=h0, n_heads=h)
            s = tuple(sh_ref[h0 + i] for i in range(hb))
            _, vjp = jax.vjp(f, s, _heads(q_ref, h0, hb), _heads(k_ref, h0, hb), _heads(v_ref, h0, hb), ba, al, dt)
            ds, dq, dk, dv, dba, dal, ddt = vjp((tuple(ds_scr[h0 + i] for i in range(hb)), _heads(do_ref, h0, hb)))
            for i in range(hb):
                ds_scr[h0 + i] = ds[i]
            cols = slice(h0 * HEAD_DIM, (h0 + hb) * HEAD_DIM)
            dqkv_ref[0, :, cols] = dq
            dqkv_ref[1, :, cols] = dk
            dqkv_ref[2, :, cols] = dv
            dba_sum = dba_sum + dba
            dal_sum = dal_sum + dal
            ddt_sum = ddt_sum + ddt
        dba_ref[...] = dba_sum
        dal_ref[...] += dal_sum
        ddt_ref[...] += ddt_sum

    blk = (CHUNK, wa)
    in_specs = [
        pl.BlockSpec(blk, lambda d, n: (cb(d, n), 0)),
        pl.BlockSpec(blk, lambda d, n: (cb(d, n), 1)),
        pl.BlockSpec(blk, lambda d, n: (cb(d, n), 2)),
        pl.BlockSpec((CHUNK, LANES), lambda d, n: (cb(d, n), ba_blk)),
        pl.BlockSpec((1, LANES), lambda d, n: (0, 0)),
        pl.BlockSpec((1, LANES), lambda d, n: (0, 0)),
        pl.BlockSpec((h, None, HEAD_DIM, HEAD_DIM), lambda d, n: (d, nc - 1 - n, 0, 0)),
        pl.BlockSpec(blk, lambda d, n: (cb(d, n), 0)),
    ]
    out_specs = [
        pl.BlockSpec((None, 3, CHUNK, wa), lambda d, n: (d, 0, cb(d, n), 0)),
        pl.BlockSpec((None, CHUNK, LANES), lambda d, n: (d, cb(d, n), 0)),
        pl.BlockSpec((None, 1, LANES), lambda d, n: (d, 0, 0)),
        pl.BlockSpec((None, 1, LANES), lambda d, n: (d, 0, 0)),
    ]
    out_shape = [jax.ShapeDtypeStruct((2, 3, nrow, wa), F32),
                 jax.ShapeDtypeStruct((2, nrow, LANES), F32),
                 jax.ShapeDtypeStruct((2, 1, LANES), F32),
                 jax.ShapeDtypeStruct((2, 1, LANES), F32)]
    return _call(body, grid=(2, nc), in_specs=in_specs, out_specs=out_specs, out_shape=out_shape,
                 scratch_shapes=[pltpu.VMEM((h, HEAD_DIM, HEAD_DIM), F32)], sem=("parallel", "arbitrary"),
                 name=name, args=(qkv, qkv, qkv, proj, alog_row, dtb_row, s_hist, do), side=side)


def s5_mats(lam_re, lam_im, log_dt, b_re, b_im, c_re, c_im):
    g = lam_re.shape[1]
    nb = g // S5_BLOCK_GROUPS
    dt = jnp.exp(log_dt)[..., None]
    mag = jnp.exp(lam_re * dt)
    ar = mag * jnp.cos(lam_im * dt)
    ai = mag * jnp.sin(lam_im * dt)
    den = lam_re * lam_re + lam_im * lam_im
    fr = ((ar - 1.0) * lam_re + ai * lam_im) / den
    fi = (ai * lam_re - (ar - 1.0) * lam_im) / den
    bbr = fr[..., None] * b_re - fi[..., None] * b_im
    bbi = fr[..., None] * b_im + fi[..., None] * b_re
    eye = jnp.eye(S5_BLOCK_GROUPS, dtype=F32)
    shp = (2, nb, S5_BLOCK_GROUPS, S5_STATE, S5_GROUP_CH)
    w_r = jnp.einsum('dsjpc,jk->dsjckp', bbr.reshape(shp), eye).reshape(2, nb, LANES, S5_BLOCK_STATE)
    w_i = jnp.einsum('dsjpc,jk->dsjckp', bbi.reshape(shp), eye).reshape(2, nb, LANES, S5_BLOCK_STATE)
    w = jnp.concatenate([w_r, w_i], axis=-1)
    shc = (2, nb, S5_BLOCK_GROUPS, S5_GROUP_CH, S5_STATE)
    c_r = jnp.einsum('dsjcp,jk->dsjpkc', c_re.reshape(shc), eye).reshape(2, nb, S5_BLOCK_STATE, LANES)
    c_i = jnp.einsum('dsjcp,jk->dsjpkc', c_im.reshape(shc), eye).reshape(2, nb, S5_BLOCK_STATE, LANES)
    cm = jnp.concatenate([c_r, -c_i], axis=-2)
    return (ar.reshape(2, nb, 1, S5_BLOCK_STATE), ai.reshape(2, nb, 1, S5_BLOCK_STATE), w, cm)


_S5_ROWS = 512


def s5_fwd(proj, u_blk, ar, ai, w, cm, *, name):
    nrow = proj.shape[0]
    nb = w.shape[1]
    nt = nrow // 8
    hs = S5_BLOCK_STATE
    rs = min(_S5_ROWS, nrow)

    def body(u_ref, ar_ref, ai_ref, w_ref, cm_ref, y_ref, x_ref):
        d = pl.program_id(0)
        wb = w_ref[...].astype(MXU_DT)
        for r0 in range(0, nrow, rs):
            x_ref[pl.ds(r0, rs), :] = _dot(u_ref[pl.ds(r0, rs), :].astype(MXU_DT), wb, _NN)
        a_r = ar_ref[...]
        a_i = ai_ref[...]
        rid = lax.broadcasted_iota(jnp.int32, (8, hs), 0)

        def run(reverse):
            def tile_step(tt, carry):
                s_r, s_i = carry
                base = pl.multiple_of((nt - 1 - tt if reverse else tt) * 8, 8)
                tile = x_ref[pl.ds(base, 8), :]
                o_r = jnp.zeros((8, hs), F32)
                o_i = jnp.zeros((8, hs), F32)
                for q in range(8):
                    rr = 7 - q if reverse else q
                    n_r = a_r * s_r - a_i * s_i + tile[rr:rr + 1, :hs]
                    n_i = a_r * s_i + a_i * s_r + tile[rr:rr + 1, hs:]
                    s_r, s_i = n_r, n_i
                    o_r = jnp.where(rid == rr, s_r, o_r)
                    o_i = jnp.where(rid == rr, s_i, o_i)
                x_ref[pl.ds(base, 8), pl.ds(0, hs)] = o_r
                x_ref[pl.ds(base, 8), pl.ds(hs, hs)] = o_i
                return s_r, s_i

            z = jnp.zeros((1, hs), F32)
            lax.fori_loop(0, nt, tile_step, (z, z))

        @pl.when(d == 0)
        def _():
            run(False)

        @pl.when(d == 1)
        def _():
            run(True)

        cb = cm_ref[...].astype(MXU_DT)
        for r0 in range(0, nrow, rs):
            y_ref[pl.ds(r0, rs), :] = _dot(x_ref[pl.ds(r0, rs), :].astype(MXU_DT), cb, _NN)

    in_specs = [
        pl.BlockSpec((nrow, LANES), lambda d, s: (0, u_blk + s)),
        pl.BlockSpec((None, None, 1, hs), lambda d, s: (d, s, 0, 0)),
        pl.BlockSpec((None, None, 1, hs), lambda d, s: (d, s, 0, 0)),
        pl.BlockSpec((None, None, LANES, 2 * hs), lambda d, s: (d, s, 0, 0)),
        pl.BlockSpec((None, None, 2 * hs, LANES), lambda d, s: (d, s, 0, 0)),
    ]
    out_specs = [
        pl.BlockSpec((None, nrow, LANES), lambda d, s: (d, 0, s)),
        pl.BlockSpec((None, nrow, 2 * hs), lambda d, s: (d, 0, s)),
    ]
    out_shape = [jax.ShapeDtypeStruct((2, nrow, nb * LANES), F32),
                 jax.ShapeDtypeStruct((2, nrow, nb * 2 * hs), F32)]
    return pl.pallas_call(
        body, grid=(2, nb), in_specs=in_specs, out_specs=out_specs, out_shape=out_shape,
        compiler_params=_params(("parallel", "parallel")), name=name)(proj, ar, ai, w, cm)


def s5_bwd(proj, u_blk, ar, ai, w, cm, xs, dy, du_in, *, side=None, name):
    nrow = proj.shape[0]
    nb = w.shape[1]
    nt = nrow // 8
    hs = S5_BLOCK_STATE
    rs = min(_S5_ROWS, nrow)

    def body(u_ref, dy_ref, dui_ref, x_ref, ar_ref, ai_ref, w_ref, cm_ref,
             du_ref, dw_ref, dcm_ref, dar_ref, dai_ref, g_ref):
        d = pl.program_id(1)
        cb = cm_ref[...].astype(MXU_DT)
        for r0 in range(0, nrow, rs):
            g_ref[pl.ds(r0, rs), :] = _dot(dy_ref[pl.ds(r0, rs), :].astype(MXU_DT), cb, _NT)
        a_r = ar_ref[...]
        a_i = ai_ref[...]
        rid = lax.broadcasted_iota(jnp.int32, (8, hs), 0)

        def run(descending):
            def tile_step(tt, carry):
                g_r, g_i, acc_r, acc_i = carry
                tidx = nt - 1 - tt if descending else tt
                base = pl.multiple_of(tidx * 8, 8)
                tile = g_ref[pl.ds(base, 8), :]
                o_r = jnp.zeros((8, hs), F32)
                o_i = jnp.zeros((8, hs), F32)
                for q in range(8):
                    rr = 7 - q if descending else q
                    n_r = tile[rr:rr + 1, :hs] + (a_r * g_r + a_i * g_i)
                    n_i = tile[rr:rr + 1, hs:] + (a_r * g_i - a_i * g_r)
                    g_r, g_i = n_r, n_i
                    o_r = jnp.where(rid == rr, g_r, o_r)
                    o_i = jnp.where(rid == rr, g_i, o_i)
                g_ref[pl.ds(base, 8), pl.ds(0, hs)] = o_r
                g_ref[pl.ds(base, 8), pl.ds(hs, hs)] = o_i
                xt = x_ref[pl.ds(base, 8), :]
                if descending:
                    nbase = pl.multiple_of(jnp.maximum(tidx - 1, 0) * 8, 8)
                    edge = x_ref[pl.ds(nbase, 8), :][7:8, :] * jnp.where(tidx > 0, 1.0, 0.0).astype(F32)
                    prev = jnp.where(lax.broadcasted_iota(jnp.int32, xt.shape, 0) == 0, edge, pltpu.roll(xt, 1, 0))
                else:
                    nbase = pl.multiple_of(jnp.minimum(tidx + 1, nt - 1) * 8, 8)
                    edge = x_ref[pl.ds(nbase, 8), :][0:1, :] * jnp.where(tidx < nt - 1, 1.0, 0.0).astype(F32)
                    prev = jnp.where(lax.broadcasted_iota(jnp.int32, xt.shape, 0) == 7, edge, pltpu.roll(xt, 7, 0))
                p_r = prev[:, :hs]
                p_i = prev[:, hs:]
                acc_r = acc_r + o_r * p_r + o_i * p_i
                acc_i = acc_i + o_i * p_r - o_r * p_i
                return g_r, g_i, acc_r, acc_i

            z = jnp.zeros((1, hs), F32)
            z8 = jnp.zeros((8, hs), F32)
            _, _, acc_r, acc_i = lax.fori_loop(0, nt, tile_step, (z, z, z8, z8))
            dar_ref[...] = jnp.sum(acc_r, axis=0, keepdims=True)
            dai_ref[...] = jnp.sum(acc_i, axis=0, keepdims=True)

        @pl.when(d == 0)
        def _():
            run(True)

        @pl.when(d == 1)
        def _():
            run(False)

        wb = w_ref[...].astype(MXU_DT)
        for r0 in range(0, nrow, rs):
            part = _dot(g_ref[pl.ds(r0, rs), :].astype(MXU_DT), wb, _NT)

            @pl.when(d == 0)
            def _(part=part, r0=r0):
                du_ref[pl.ds(r0, rs), :] = dui_ref[pl.ds(r0, rs), :] + part

            @pl.when(d == 1)
            def _(part=part, r0=r0):
                du_ref[pl.ds(r0, rs), :] += part

        dw_ref[...] = _dot(u_ref[...].astype(MXU_DT), g_ref[...].astype(MXU_DT), _TN)
        dcm_ref[...] = _dot(x_ref[...].astype(MXU_DT), dy_ref[...].astype(MXU_DT), _TN)

    in_specs = [
        pl.BlockSpec((nrow, LANES), lambda s, d: (0, u_blk + s)),
        pl.BlockSpec((nrow, LANES), lambda s, d: (0, s)),
        pl.BlockSpec((nrow, LANES), lambda s, d: (0, s)),
        pl.BlockSpec((None, nrow, 2 * hs), lambda s, d: (d, 0, s)),
        pl.BlockSpec((None, None, 1, hs), lambda s, d: (d, s, 0, 0)),
        pl.BlockSpec((None, None, 1, hs), lambda s, d: (d, s, 0, 0)),
        pl.BlockSpec((None, None, LANES, 2 * hs), lambda s, d: (d, s, 0, 0)),
        pl.BlockSpec((None, None, 2 * hs, LANES), lambda s, d: (d, s, 0, 0)),
    ]
    out_specs = [
        pl.BlockSpec((nrow, LANES), lambda s, d: (0, s)),
        pl.BlockSpec((None, None, LANES, 2 * hs), lambda s, d: (d, s, 0, 0)),
        pl.BlockSpec((None, None, 2 * hs, LANES), lambda s, d: (d, s, 0, 0)),
        pl.BlockSpec((None, None, 1, hs), lambda s, d: (d, s, 0, 0)),
        pl.BlockSpec((None, None, 1, hs), lambda s, d: (d, s, 0, 0)),
    ]
    out_shape = [jax.ShapeDtypeStruct((nrow, nb * LANES), F32),
                 jax.ShapeDtypeStruct(w.shape, F32), jax.ShapeDtypeStruct(cm.shape, F32),
                 jax.ShapeDtypeStruct(ar.shape, F32), jax.ShapeDtypeStruct(ai.shape, F32)]
    return _call(body, grid=(nb, 2), in_specs=in_specs, out_specs=out_specs, out_shape=out_shape,
                 scratch_shapes=[pltpu.VMEM((nrow, 2 * hs), F32)], sem=("parallel", "arbitrary"),
                 name=name, args=(proj, dy, du_in, xs, ar, ai, w, cm), side=side)


def _me():
    return lax.axis_index("x"), lax.axis_index("y"), lax.axis_index("c")


def all_gather(shards):
    na = len(shards)

    def plan(x_refs, out_refs, sems):
        send_sems, recv_sems, local_sems = sems
        x, y, c = _me()
        me, sibling = (x, y, c), (x, y, 1 - c)
        chips = [(1 - x, y), (x, 1 - y), (1 - x, 1 - y)]

        def slot(a, px, py, pc):
            return out_refs[a].at[4 * px + 2 * py + pc]

        def copy(a, k, block, to, src=None):
            return pltpu.make_async_remote_copy(
                src_ref=slot(a, *block) if src is None else src, dst_ref=slot(a, *block),
                send_sem=send_sems.at[a, k], recv_sem=recv_sems.at[a, k], device_id=to, device_id_type=MESH)

        mine = [pltpu.make_async_copy(x_refs[a], slot(a, *me), local_sems.at[a]) for a in range(na)]
        first = []
        for a in range(na):
            first.append(copy(a, 0, me, sibling, src=x_refs[a]))
            first += [copy(a, 1 + j, me, (*chip, c), src=x_refs[a]) for j, chip in enumerate(chips)]
        return me, sibling, chips, c, copy, mine, first

    def start(x_refs, out_refs, sems):
        _, _, _, _, _, mine, first = plan(x_refs, out_refs, sems)
        for cp in mine + first:
            cp.start()

    def finish(x_refs, out_refs, sems):
        me, sibling, chips, c, copy, mine, first = plan(x_refs, out_refs, sems)
        passed = []
        for j, chip in enumerate(chips):
            for a in range(na):
                copy(a, 1 + j, (*chip, c), me).wait_recv()
                fwd = copy(a, 4 + j, (*chip, c), sibling)
                fwd.start()
                passed.append(fwd)
        for a in range(na):
            copy(a, 0, sibling, me).wait_recv()
            for j, chip in enumerate(chips):
                copy(a, 4 + j, (*chip, 1 - c), me).wait_recv()
        for cp in first + passed:
            cp.wait_send()
        for cp in mine:
            cp.wait()

    return Side(shards, [jax.ShapeDtypeStruct((8,) + s.shape, s.dtype) for s in shards],
                [pltpu.SemaphoreType.DMA((na, 7)), pltpu.SemaphoreType.DMA((na, 7)), pltpu.SemaphoreType.DMA((na,))],
                start, finish)


_AXES = ("x", "y", "c")


def exchange(bufs, axes, *, half):
    na = len(bufs)
    if isinstance(half, bool):
        half = [half] * na

    def copies(in_refs, out_refs, sems):
        send_sems, recv_sems = sems
        me = _me()
        cps = []
        for a in range(na):
            bit = me[_AXES.index(axes[a])]
            peer = tuple(1 - v if ax == axes[a] else v for ax, v in zip(_AXES, me))
            cps.append(pltpu.make_async_remote_copy(
                src_ref=in_refs[a].at[1 - bit] if half[a] else in_refs[a], dst_ref=out_refs[a],
                send_sem=send_sems.at[a], recv_sem=recv_sems.at[a], device_id=peer, device_id_type=MESH))
        return cps

    def start(*refs):
        for cp in copies(*refs):
            cp.start()

    def finish(*refs):
        for cp in copies(*refs):
            cp.wait()

    return Side(bufs, [jax.ShapeDtypeStruct(b.shape[1:] if h else b.shape, b.dtype) for b, h in zip(bufs, half)],
                [pltpu.SemaphoreType.DMA((na,)), pltpu.SemaphoreType.DMA((na,))], start, finish)


def all_to_all(pieces, slot_fns):
    na = len(pieces)

    def copies(in_refs, out_refs, sems):
        send_sems, recv_sems, local_sems = sems
        x, y, c = _me()
        cps = []
        for a in range(na):
            mine = slot_fns[a](x, y, c)
            for k in range(8):
                tx, ty, tc = x ^ (k // 4), y ^ ((k // 2) % 2), c ^ (k % 2)
                src = in_refs[a].at[4 * tx + 2 * ty + tc]
                dst = out_refs[a].at[mine]
                if k == 0:
                    cps.append(pltpu.make_async_copy(src, dst, local_sems.at[a]))
                else:
                    cps.append(pltpu.make_async_remote_copy(
                        src_ref=src, dst_ref=dst, send_sem=send_sems.at[a, k - 1], recv_sem=recv_sems.at[a, k - 1],
                        device_id=(tx, ty, tc), device_id_type=MESH))
        return cps

    def start(*refs):
        for cp in copies(*refs):
            cp.start()

    def finish(*refs):
        for cp in copies(*refs):
            cp.wait()

    return Side(pieces, [jax.ShapeDtypeStruct(p.shape, p.dtype) for p in pieces],
                [pltpu.SemaphoreType.DMA((na, 7)), pltpu.SemaphoreType.DMA((na, 7)), pltpu.SemaphoreType.DMA((na,))],
                start, finish)


def add_half(buf, recv, bit, *, name):
    c = recv.shape[-1]
    r = math.prod(recv.shape[:-1])
    t = _tile(r, max(8, (ADD_BLOCK_BYTES // (4 * c)) // 8 * 8), 8)

    def body(bit_ref, a_ref, b_ref, o_ref):
        o_ref[...] = a_ref[...] + b_ref[...]

    grid_spec = pltpu.PrefetchScalarGridSpec(
        num_scalar_prefetch=1, grid=(r // t,),
        in_specs=[pl.BlockSpec((None, t, c), lambda i, b: (b[0], i, 0)), pl.BlockSpec((t, c), lambda i, b: (i, 0))],
        out_specs=pl.BlockSpec((t, c), lambda i, b: (i, 0)))
    out = pl.pallas_call(body, grid_spec=grid_spec, out_shape=jax.ShapeDtypeStruct((r, c), F32),
                         compiler_params=_params(("parallel",)), name=name)(
                             bit, buf.reshape(2, r, c), recv.reshape(r, c))
    return out.reshape(recv.shape)


def _pack_rows(n, cols):
    return -(-n // (8 * cols)) * 8


def _pack(arrs, cols, mult):
    parts = []
    for a in arrs:
        n = math.prod(a.shape)
        nr = _pack_rows(n, cols)
        parts.append(jnp.pad(a.reshape(-1), (0, nr * cols - n)).reshape(nr, cols))
    total = sum(p.shape[0] for p in parts)
    pad_rows = -(-total // mult) * mult - total
    if pad_rows:
        parts.append(jnp.zeros((pad_rows, cols), arrs[0].dtype))
    return jnp.concatenate(parts, axis=0)


def _unpack(buf, shapes):
    out, r = [], 0
    cols = buf.shape[1]
    for s in shapes:
        n = math.prod(s)
        nr = _pack_rows(n, cols)
        piece = lax.optimization_barrier(buf[r:r + nr])
        out.append(piece.reshape(-1)[:n].reshape(s))
        r += nr
    return out


def _elementwise(fn, arrs, nout, name):
    r, c = arrs[0].shape
    t = _tile(r, ROW_TILE, 8)
    return rw_fwd(fn, [rows(a, t) for a in arrs], [(c, False)] * nout, nrow=r, t=t, name=name)


def kernel(x, ln_g, w_in, conv_w, a_log, dt_bias, head_norm_g, lam_re, lam_im, log_dt, b_re, b_im, c_re, c_im, d_skip, w_glu, b_glu, w_pa, w_pb, b_gate, w_out, final_g, loss_target, m_ln_g, m_w_in, m_conv_w, m_a_log, m_dt_bias, m_head_norm_g, m_lam_re, m_lam_im, m_log_dt, m_b_re, m_b_im, m_c_re, m_c_im, m_d_skip, m_w_glu, m_b_glu, m_w_pa, m_w_pb, m_b_gate, m_w_out, m_final_g, v_ln_g, v_w_in, v_conv_w, v_a_log, v_dt_bias, v_head_norm_g, v_lam_re, v_lam_im, v_log_dt, v_b_re, v_b_im, v_c_re, v_c_im, v_d_skip, v_w_glu, v_b_glu, v_w_pa, v_w_pb, v_b_gate, v_w_out, v_final_g):
    env = dict(locals())
    wts = {n: env[n] for n in WEIGHTS}
    mom_m = {n: env["m_" + n] for n in WEIGHTS}
    mom_v = {n: env["v_" + n] for n in WEIGHTS}

    xin = x[0]
    tgt = loss_target[0]
    nrow, dm = xin.shape
    depth = ln_g.shape[0]
    nh = dm // (2 * HEAD_DIM)
    wa = nh * HEAD_DIM
    wb = dm // 2
    ngrp = wb // S5_GROUP_CH
    pw = w_in.shape[-1]
    t = min(ROW_TILE, nrow)

    o_ba = 4 * wa
    o_u = o_ba + 4 * nh
    n_main = 4 * wa + 2 * wb + 2 * dm
    projp = -(-(n_main + LANES) // 512) * 512
    blk_za = 3 * wa // HEAD_DIM
    blk_u = 4 * wa // LANES
    ba_blk = n_main // LANES
    cx, cy, cc = _me()

    jb = o_ba // pw
    assert (o_u - 1) // pw == jb
    cut_lo, cut = o_ba - jb * pw, o_u - o_ba
    wide = -(-pw // LANES) * LANES + LANES
    col = lax.broadcasted_iota(jnp.int32, (pw, wide), 1)
    row = lax.broadcasted_iota(jnp.int32, (pw, wide), 0)
    holds_cut = (4 * cx + 2 * cy + cc) == jb
    src_plain = jnp.where(col < pw, col, -1)
    src_cut = jnp.where(col < cut_lo, col, jnp.where(col < pw - cut, col + cut, -1))
    src_cut = jnp.where((col >= wide - LANES) & (col < wide - LANES + cut), col - (wide - LANES) + cut_lo, src_cut)
    select = (row == jnp.where(holds_cut, src_cut, src_plain)).astype(WIRE_DT)
    w_in_tx = mm(w_in.reshape(depth * dm, pw), select, name="w_in_prepare").astype(WIRE_DT).reshape(depth, dm, wide)

    def gather_side(l):
        return all_gather([w_in_tx[l], w_glu[l].astype(WIRE_DT), w_pa[l].astype(WIRE_DT),
                           w_pb[l].astype(WIRE_DT), w_out[l].astype(WIRE_DT), conv_w[l]])

    def cat(g):
        return jnp.concatenate([g[j] for j in range(8)], axis=1)

    def assemble(gathered):
        g_in, g_glu, g_pa, g_pb, g_out, g_conv = gathered
        w_perm = jnp.concatenate(
            [g_in[j][:, :pw - cut if j == jb else pw] for j in range(8)]
            + [g_in[jb][:, wide - LANES:], jnp.zeros((dm, projp - n_main - LANES), WIRE_DT)], axis=1)
        return dict(w_in=w_perm, w_glu=g_glu.reshape(wb, wb), w_pa=cat(g_pa), w_pb=cat(g_pb),
                    w_out=g_out.reshape(dm, dm), conv_w=cat(g_conv))

    full = [assemble(run_side(gather_side(0), name="gather_weights"))]

    def small(l):
        z = jnp.zeros((1, LANES - 4 * nh), F32)
        alog_row = jnp.concatenate([jnp.zeros((1, 2 * nh), F32), a_log[l].reshape(1, 2 * nh), z], axis=1)
        dtb_row = jnp.concatenate([jnp.zeros((1, 2 * nh), F32), dt_bias[l].reshape(1, 2 * nh), z], axis=1)
        return alog_row, dtb_row

    saved = []
    cur = xin
    for l in range(depth):
        fw = full[l]
        (hh,) = rw_fwd(fn_rms, [rows(cur, t), bcast(ln_g[l][None])], [(dm, False)], nrow=nrow, t=t, name="rms_fwd")
        proj = mm(hh, fw["w_in"], name="proj_fwd")
        (qkv,) = rw_fwd(make_fn_prep(nh), [rows(proj, nrow, HEAD_DIM, 0, True), bcast(fw["conv_w"], HEAD_DIM, 0, True)],
                        [(HEAD_DIM, True)], nrow=nrow, t=nrow, ncol=3 * nh, name="prep_fwd")
        alog_row, dtb_row = small(l)
        nxt_gather = gather_side(l + 1) if l + 1 < depth else None
        o_dir, s_hist = gdn_fwd(qkv, proj, ba_blk, alog_row, dtb_row, n_heads=nh, side=nxt_gather, name="gdn_fwd")
        if nxt_gather is not None:
            full.append(assemble(nxt_gather.result))
        hn_ins = [rows3(o_dir, 0, t, HEAD_DIM), rows3(o_dir, 1, t, HEAD_DIM), rows(proj, t, HEAD_DIM, blk_za, True),
                  bcast(head_norm_g[l][None])]
        (ya_in,) = rw_fwd(fn_headnorm, hn_ins, [(HEAD_DIM, True)], nrow=nrow, t=t, ncol=nh, name="headnorm_fwd")
        mats, mats_vjp = jax.vjp(s5_mats, lam_re[l], lam_im[l], log_dt[l], b_re[l], b_im[l], c_re[l], c_im[l])
        yd, xs = s5_fwd(proj, blk_u, *mats, name="s5_fwd")
        ge_ins = [rows3(yd, 0, t, wb, False), rows3(yd, 1, t, wb, False), rows(proj, t, wb, 4 * wa // wb),
                  bcast(d_skip[l][None])]
        (ys,) = rw_fwd(fn_gelu, ge_ins, [(wb, False)], nrow=nrow, t=t, name="gelu_fwd")
        glu = mm(ys, fw["w_glu"], name="glu_fwd")
        gl_ins = [rows(ys, t), rows(glu, t), rows(proj, t, wb, (4 * wa + wb) // wb), bcast(b_glu[l][None])]
        (yb_in,) = rw_fwd(fn_glu, gl_ins, [(wb, False)], nrow=nrow, t=t, name="glugate_fwd")
        y_a = mm(ya_in, fw["w_pa"], name="pa_fwd")
        y_b = mm(yb_in, fw["w_pb"], name="pb_fwd")
        bg = b_gate[l][None]
        mg_ins = [rows(proj, t, dm, 3), rows(proj, t, dm, 4), rows(y_a, t), rows(y_b, t),
                  bcast(bg, dm, 0), bcast(bg, dm, 1)]
        (merged,) = rw_fwd(fn_merge, mg_ins, [(dm, False)], nrow=nrow, t=t, name="merge_fwd")
        nxt = mm(merged, fw["w_out"], add=cur, name="out_fwd")
        saved.append(dict(x=cur, h=hh, proj=proj, qkv=qkv, o_dir=o_dir, s_hist=s_hist, hn_ins=hn_ins,
                          mats=mats, mats_vjp=mats_vjp, xs=xs, ge_ins=ge_ins, ys=ys, gl_ins=gl_ins,
                          ya_in=ya_in, yb_in=yb_in, mg_ins=mg_ins, merged=merged,
                          alog_row=alog_row, dtb_row=dtb_row))
        cur = nxt

    loss_ins = [rows(cur, t), rows(tgt, t), bcast(final_g[None])]
    (row_loss,) = rw_fwd(fn_loss, loss_ins, [(1, False)], nrow=nrow, t=t, name="loss_fwd")
    ones = jnp.ones((nrow, 1), F32)
    dcur, dfinal = rw_bwd(fn_loss, loss_ins, [[rows(ones, t)]], nrow=nrow, t=t,
                          row_grads=[(0, False)], bc_grads=[2], name="loss_bwd")
    loss = lax.psum(jnp.sum(row_loss), ("x", "y", "c"))

    coord = {"x": cx, "y": cy, "c": cc}
    routes = {"cxy": ("c", "x", "y"), "cyx": ("c", "y", "x")}
    route = ["cxy", "cyx", "cxy", "cxy", "cyx", "cyx", "cxy"]
    conv_order = [4 * ((q // 2) % 2) + 2 * (q % 2) + q // 4 for q in range(8)]
    cw = 3 * wa // 8
    g_final = {n: [None] * depth for n in SHARDED}
    gsh = {n: [None] * depth for n in SHARDED}

    def own_columns(z):
        un = jnp.concatenate([z[:, :o_ba], z[:, n_main:n_main + 4 * nh], z[:, o_ba:n_main]], axis=1)
        return jnp.stack([un[:, d * pw:(d + 1) * pw] for d in range(8)])

    def rs_begin(l):
        in_a, in_b = gsh["w_in"][l]
        conv_buf = jnp.stack([gsh["conv_w"][l][:, d * cw:(d + 1) * cw] for d in conv_order])
        return dict(layer=l, stage=0, bufs=[
            in_a.reshape(8, dm // 16, projp), in_b.reshape(8, dm // 16, projp), gsh["w_out"][l],
            gsh["w_glu"][l], gsh["w_pa"][l], gsh["w_pb"][l], conv_buf])

    def rs_side(st):
        if st is None:
            return None
        if st["stage"] < 3:
            st["axes"] = [routes[r][st["stage"]] for r in route]
            st["bufs"] = [b.reshape((2, b.shape[0] // 2) + b.shape[1:]) for b in st["bufs"]]
            if st.get("whole") is not None:
                return exchange(st["bufs"] + [st["whole"]], st["axes"] + [routes["cxy"][st["stage"]]],
                                half=[True] * len(st["bufs"]) + [False])
            return exchange(st["bufs"], st["axes"], half=True)
        st["bufs"] = [b.reshape(b.shape[1:]) for b in st["bufs"]]
        return all_to_all([own_columns(st["bufs"][0]), own_columns(st["bufs"][1])],
                          [lambda x, y, c: 4 * c + 2 * x + y, lambda x, y, c: 4 * c + 2 * y + x])

    def rs_absorb(st, side):
        if st is None:
            return
        if st["stage"] < 3:
            if st.get("whole") is not None:
                (st["whole"],) = _elementwise(make_fn_sum(2), [st["whole"], side.result[-1]], 1, "ar_add")
            st["bufs"] = [add_half(b, r, coord[ax].astype(jnp.int32).reshape(1), name=f"rs_add_{st['stage']}")
                          for b, r, ax in zip(st["bufs"], side.result, st["axes"])]
        else:
            got_a, got_b = side.result
            g_final["w_in"][st["layer"]] = jnp.concatenate(
                [got_a.reshape(dm // 2, pw), got_b.reshape(dm // 2, pw)], axis=0)
            for n, b in zip(["w_out", "w_glu", "w_pa", "w_pb", "conv_w"], st["bufs"][2:]):
                g_final[n][st["layer"]] = b
        st["stage"] += 1

    grep = {n: [None] * depth for n in REPLICATED if n != "final_g"}
    pending = None
    for l in reversed(range(depth)):
        fw, sv = full[l], saved[l]
        dmerged = mm(dcur, fw["w_out"], tb=True, name="out_bwd_x")
        gsh["w_out"][l] = mm(sv["merged"], dcur, ta=True, scatter=("rows", slot_cxy), name="out_bwd_w")
        side = rs_side(pending)
        dla, dlb, dya, dyb, dbga, dbgb = rw_bwd(
            fn_merge, sv["mg_ins"], [[rows(dmerged, t)]], nrow=nrow, t=t,
            row_grads=[(0, False), (1, False), (2, False), (3, False)], bc_grads=[4, 5], side=side, name="merge_bwd")
        rs_absorb(pending, side)
        grep["b_gate"][l] = jnp.concatenate([dbga.reshape(dm), dbgb.reshape(dm)])
        dya_in = mm(dya, fw["w_pa"], tb=True, name="pa_bwd_x")
        gsh["w_pa"][l] = mm(sv["ya_in"], dya, ta=True, scatter=("cols", slot_cyx), name="pa_bwd_w")
        dyb_in = mm(dyb, fw["w_pb"], tb=True, name="pb_bwd_x")
        gsh["w_pb"][l] = mm(sv["yb_in"], dyb, ta=True, scatter=("cols", slot_cyx), name="pb_bwd_w")
        dys1, dglu, dzb, dbglu = rw_bwd(
            fn_glu, sv["gl_ins"], [[rows(dyb_in, t)]], nrow=nrow, t=t,
            row_grads=[(0, False), (1, False), (2, False)], bc_grads=[3], name="glugate_bwd")
        grep["b_glu"][l] = dbglu.reshape(wb)
        dys2 = mm(dglu, fw["w_glu"], tb=True, name="glu_bwd_x")
        gsh["w_glu"][l] = mm(sv["ys"], dglu, ta=True, scatter=("rows", slot_cxy), name="glu_bwd_w")
        dyd, du1, ddskip = rw_bwd(
            fn_gelu, sv["ge_ins"], [[rows(dys1, t), rows(dys2, t)]], nrow=nrow, t=t,
            row_grads=[(0, False), (2, False)], bc_grads=[3], name="gelu_bwd")
        grep["d_skip"][l] = ddskip.reshape(wb)
        side = rs_side(pending)
        du, dw_s5, dcm_s5, dar, dai = s5_bwd(sv["proj"], blk_u, *sv["mats"], sv["xs"], dyd, du1, side=side,
                                             name="s5_bwd")
        rs_absorb(pending, side)
        g_lr, g_li, g_ldt, g_br, g_bi, g_cr, g_ci = sv["mats_vjp"]((dar, dai, dw_s5, dcm_s5))
        for nme, val in zip(["lam_re", "lam_im", "log_dt", "b_re", "b_im", "c_re", "c_im"],
                            [g_lr, g_li, g_ldt, g_br, g_bi, g_cr, g_ci]):
            grep[nme][l] = val
        do, dza, dhn = rw_bwd(fn_headnorm, sv["hn_ins"], [[rows(dya_in, t, HEAD_DIM, 0, True)]], nrow=nrow, t=t,
                              ncol=nh, row_grads=[(0, True), (2, True)], bc_grads=[3], name="headnorm_bwd")
        grep["head_norm_g"][l] = jnp.sum(dhn, axis=0).reshape(HEAD_DIM)
        side = rs_side(pending)
        dqkv, dba_all, dal, ddt = gdn_bwd(sv["qkv"], sv["proj"], ba_blk, sv["alog_row"], sv["dtb_row"],
                                          sv["s_hist"], do, n_heads=nh, side=side, name="gdn_bwd")
        rs_absorb(pending, side)
        grep["a_log"][l] = jnp.sum(dal, axis=(0, 1))[2 * nh:4 * nh].reshape(2, nh)
        grep["dt_bias"][l] = jnp.sum(ddt, axis=(0, 1))[2 * nh:4 * nh].reshape(2, nh)
        (dba,) = rw_fwd(make_fn_sum(2), [rows3(dba_all, p, t, LANES, False) for p in range(2)],
                        [(LANES, False)], nrow=nrow, t=t, name="dba_sum")
        prep_ins = [rows(sv["proj"], nrow, HEAD_DIM, 0, True), bcast(fw["conv_w"], HEAD_DIM, 0, True)]
        dq_cots = [(dqkv.reshape(2, 3 * nrow, wa), (None, nrow, HEAD_DIM),
                    (lambda j, i, dd=dd: (dd, j // nh, j % nh))) for dd in range(2)]
        dqkv_raw, dconv = rw_bwd(make_fn_prep(nh), prep_ins, [dq_cots], nrow=nrow, t=nrow, ncol=3 * nh,
                                 row_grads=[(0, True)], bc_grads=[1], name="prep_bwd")
        gsh["conv_w"][l] = jnp.transpose(dconv, (1, 0, 2)).reshape(CONV_K, 3 * wa)
        dproj = jnp.concatenate([dqkv_raw, dza, du, dzb, dla, dlb, dba,
                                 jnp.zeros((nrow, projp - n_main - LANES), F32)], axis=1)
        side = rs_side(pending)
        dh = mm(dproj, fw["w_in"], tb=True, side=side, name="proj_bwd_x")
        rs_absorb(pending, side)
        gsh["w_in"][l] = [mm(sv["h"], dproj, ta=True, m_part=(part, 2), name="proj_bwd_w") for part in range(2)]
        dcur, dlng = rw_bwd(fn_rms, [rows(sv["x"], t), bcast(ln_g[l][None])], [[rows(dh, t)]], nrow=nrow, t=t,
                            row_grads=[(0, False)], bc_grads=[1], residual=rows(dcur, t), name="rms_bwd")
        grep["ln_g"][l] = dlng.reshape(dm)
        pending = rs_begin(l)
    grad_x = dcur[None]
    rep_list = [jnp.stack(grep[n]) for n in REPLICATED if n != "final_g"] + [dfinal.reshape(dm)]
    rep_shapes = [a.shape for a in rep_list]
    pending["whole"] = _pack(rep_list, COMM_COLS, ROW_TILE)
    for stage in range(RS_STAGES):
        side = rs_side(pending)
        run_side(side, name=f"rs_stage_{stage}")
        rs_absorb(pending, side)
    grads = {n: jnp.stack(g_final[n]) for n in SHARDED}
    for n, val in zip(REPLICATED, _unpack(pending["whole"], rep_shapes)):
        grads[n] = val

    deltas, new_m, new_v = {}, {}, {}
    big = ["w_in", "w_glu", "w_pa", "w_pb", "w_out"]
    for n in big:
        shp = wts[n].shape
        two = [a.reshape(-1, shp[-1]) for a in (wts[n], grads[n], mom_m[n], mom_v[n])]
        d_, m_, v_ = _elementwise(fn_adamw, two, 3, "adamw_" + n)
        deltas[n], new_m[n], new_v[n] = d_.reshape(shp), m_.reshape(shp), v_.reshape(shp)
    rest = [n for n in WEIGHTS if n not in big]
    rest_shapes = [wts[n].shape for n in rest]
    packed = [_pack([src[n] for n in rest], COMM_COLS, ROW_TILE) for src in (wts, grads, mom_m, mom_v)]
    outs = _elementwise(fn_adamw, packed, 3, "adamw_small")
    for dst, arr in zip((deltas, new_m, new_v), outs):
        for n, val in zip(rest, _unpack(arr, rest_shapes)):
            dst[n] = val

    return (loss, grad_x, *[grads[n] for n in WEIGHTS], *[deltas[n] for n in WEIGHTS],
            *[new_m[n] for n in WEIGHTS], *[new_v[n] for n in WEIGHTS])
```

```python
import functools
import math

import jax
import jax.numpy as jnp
from jax import lax
from jax.experimental import pallas as pl
from jax.experimental.pallas import tpu as pltpu

F32 = jnp.float32
MXU_DT = jnp.bfloat16
WIRE_DT = jnp.bfloat16

HEAD_DIM = 128
CHUNK = 64
CONV_K = 5
S5_GROUP_CH = 16
S5_STATE = 64
S5_BLOCK_GROUPS = 8
S5_BLOCK_STATE = S5_BLOCK_GROUPS * S5_STATE
RMS_EPS = 1e-6
LANES = 128
VMEM_LIMIT = 56 * 1024 * 1024
ROW_TILE = 256
COMM_COLS = 1024
ADD_BLOCK_BYTES = 2 * 1024 * 1024
RS_STAGES = 4

ADAM_LR = 0.001
ADAM_B1 = 0.9
ADAM_B2 = 0.999
ADAM_EPS = 1e-08
ADAM_WD = 0.01
ADAM_STEP = 10

WEIGHTS = ['ln_g', 'w_in', 'conv_w', 'a_log', 'dt_bias', 'head_norm_g', 'lam_re', 'lam_im', 'log_dt',
           'b_re', 'b_im', 'c_re', 'c_im', 'd_skip', 'w_glu', 'b_glu', 'w_pa', 'w_pb', 'b_gate',
           'w_out', 'final_g']
SHARDED = ['w_in', 'w_glu', 'w_pa', 'w_pb', 'w_out', 'conv_w']
REPLICATED = [n for n in WEIGHTS if n not in SHARDED]
MESH = pl.DeviceIdType.MESH


def _params(sem=None):
    return pltpu.CompilerParams(dimension_semantics=sem, vmem_limit_bytes=VMEM_LIMIT)


def _tile(n, cap, q=LANES):
    t = (min(n, cap) // q) * q
    while t > q and n % t:
        t -= q
    return t if t > 0 and n % t == 0 else n


class Side:
    def __init__(self, ins, out_sd, sems, start, finish):
        self.ins, self.out_sd, self.sems, self.start, self.finish = list(ins), list(out_sd), list(sems), start, finish
        self.result = None


def _call(body, *, grid, in_specs, out_specs, out_shape, scratch_shapes=(), sem, name, args, side=None):
    in_specs, out_specs, out_shape = list(in_specs), list(out_specs), list(out_shape)
    scratch_shapes = list(scratch_shapes)
    if side is None:
        return pl.pallas_call(body, grid=grid, in_specs=in_specs, out_specs=out_specs, out_shape=out_shape,
                              scratch_shapes=scratch_shapes, compiler_params=_params(sem), name=name)(*args)
    hbm = pl.BlockSpec(memory_space=pl.ANY)
    n_in, n_out, n_scr = len(in_specs), len(out_specs), len(scratch_shapes)
    s_in, s_out = len(side.ins), len(side.out_sd)

    def hosted(*refs):
        main_in, rest = refs[:n_in], refs[n_in:]
        side_in, rest = rest[:s_in], rest[s_in:]
        main_out, rest = rest[:n_out], rest[n_out:]
        side_out, rest = rest[:s_out], rest[s_out:]
        main_scr, sems = rest[:n_scr], rest[n_scr:]
        ids = [pl.program_id(k) for k in range(len(grid))]
        first = functools.reduce(jnp.logical_and, [i == 0 for i in ids])
        last = functools.reduce(jnp.logical_and, [i == g - 1 for i, g in zip(ids, grid)])

        @pl.when(first)
        def _():
            side.start(side_in, side_out, sems)

        body(*main_in, *main_out, *main_scr)

        @pl.when(last)
        def _():
            side.finish(side_in, side_out, sems)

    outs = pl.pallas_call(
        hosted, grid=grid, in_specs=in_specs + [hbm] * s_in, out_specs=out_specs + [hbm] * s_out,
        out_shape=out_shape + side.out_sd, scratch_shapes=scratch_shapes + side.sems,
        compiler_params=_params(("arbitrary",) * len(grid)), name=name)(*args, *side.ins)
    side.result = list(outs[n_out:])
    return list(outs[:n_out])


def run_side(side, *, name):
    hbm = pl.BlockSpec(memory_space=pl.ANY)
    s_in, s_out = len(side.ins), len(side.out_sd)

    def body(*refs):
        side_in, side_out, sems = refs[:s_in], refs[s_in:s_in + s_out], refs[s_in + s_out:]
        side.start(side_in, side_out, sems)
        side.finish(side_in, side_out, sems)

    side.result = list(pl.pallas_call(body, out_shape=side.out_sd, in_specs=[hbm] * s_in, out_specs=[hbm] * s_out,
                                      scratch_shapes=side.sems, name=name)(*side.ins))
    return side.result


def slot_cxy(d):
    return 4 * (d % 2) + 2 * (d // 4) + (d // 2) % 2


def slot_cyx(d):
    return 4 * (d % 2) + 2 * ((d // 2) % 2) + d // 4


def mm(a, b, *, ta=False, tb=False, add=None, m_part=None, scatter=None, side=None, name):
    m = a.shape[1] if ta else a.shape[0]
    k = a.shape[0] if ta else a.shape[1]
    n = b.shape[0] if tb else b.shape[1]
    m_off = 0
    if m_part is not None:
        m = m // m_part[1]
        m_off = m_part[0]
    tm, tn, tk = _tile(m, 1024), _tile(n, 512), _tile(k, 2048)
    if scatter is not None and scatter[0] == "rows":
        tm = m // 8
    if scatter is not None and scatter[0] == "cols":
        tn = n // 8
    if m_part is not None:
        assert tm == m
    nk = k // tk
    dn = (((0 if ta else 1,), (1 if tb else 0,)), ((), ()))

    def body(*refs):
        if add is None:
            a_ref, b_ref, o_ref, acc = refs
        else:
            a_ref, b_ref, add_ref, o_ref, acc = refs
        kk = pl.program_id(2)

        @pl.when(kk == 0)
        def _():
            acc[...] = jnp.zeros_like(acc)

        acc[...] += lax.dot_general(a_ref[...].astype(MXU_DT), b_ref[...].astype(MXU_DT), dn,
                                    preferred_element_type=F32)

        @pl.when(kk == nk - 1)
        def _():
            r = acc[...]
            if add is not None:
                r = r + add_ref[...]
            o_ref[...] = r

    if ta:
        a_spec = pl.BlockSpec((tk, tm), lambda i, j, kk: (kk, i + m_off))
    else:
        a_spec = pl.BlockSpec((tm, tk), lambda i, j, kk: (i + m_off, kk))
    b_spec = pl.BlockSpec((tn, tk), lambda i, j, kk: (j, kk)) if tb else pl.BlockSpec((tk, tn), lambda i, j, kk: (kk, j))
    out_sd = jax.ShapeDtypeStruct((m, n), F32)
    if scatter is None:
        o_spec = pl.BlockSpec((tm, tn), lambda i, j, kk: (i, j))
    elif scatter[0] == "rows":
        o_spec = pl.BlockSpec((None, tm, tn), lambda i, j, kk: (scatter[1](i), 0, j))
        out_sd = jax.ShapeDtypeStruct((8, tm, n), F32)
    else:
        o_spec = pl.BlockSpec((None, tm, tn), lambda i, j, kk: (scatter[1](j), i, 0))
        out_sd = jax.ShapeDtypeStruct((8, m, tn), F32)
    ins, specs = [a, b], [a_spec, b_spec]
    if add is not None:
        ins.append(add)
        specs.append(o_spec)
    return _call(body, grid=(m // tm, n // tn, nk), in_specs=specs, out_specs=[o_spec], out_shape=[out_sd],
                 scratch_shapes=[pltpu.VMEM((tm, tn), F32)], sem=("parallel", "parallel", "arbitrary"),
                 name=name, args=ins, side=side)[0]


def rows(arr, t, width=None, base=0, per_j=False):
    width = arr.shape[1] if width is None else width
    return (arr, (t, width), lambda j, i: (i, base + (j if per_j else 0)))


def rows3(arr, lead, t, width, per_j=True):
    return (arr, (None, t, width), lambda j, i: (lead, i, j if per_j else 0))


def bcast(arr, width=None, base=0, per_j=False):
    width = arr.shape[1] if width is None else width
    return (arr, (arr.shape[0], width), lambda j, i: (0, base + (j if per_j else 0)))


def _specs(items):
    return [pl.BlockSpec(bs, im) for (_, bs, im) in items]


def rw_fwd(fn, ins, outs, *, nrow, t, ncol=1, name):
    out_specs = [pl.BlockSpec((t, w), (lambda j, i: (i, j)) if pj else (lambda j, i: (i, 0))) for (w, pj) in outs]
    out_shape = [jax.ShapeDtypeStruct((nrow, w * (ncol if pj else 1)), F32) for (w, pj) in outs]
    nin = len(ins)

    def body(*refs):
        j = pl.program_id(0)
        res = fn(j, *[r[...] for r in refs[:nin]])
        for o_ref, r in zip(refs[nin:], res):
            o_ref[...] = r

    return pl.pallas_call(
        body, grid=(ncol, nrow // t), in_specs=_specs(ins), out_specs=out_specs, out_shape=out_shape,
        compiler_params=_params(("parallel", "parallel")), name=name)(*[x[0] for x in ins])


def rw_bwd(fn, ins, cots, *, nrow, t, ncol=1, row_grads, bc_grads, residual=None, side=None, name):
    nin = len(ins)
    flat_cots = [c for group in cots for c in group]
    extra = [residual] if residual is not None else []
    out_specs, out_shape = [], []
    for idx, pj in row_grads:
        w = ins[idx][1][-1]
        out_specs.append(pl.BlockSpec((t, w), (lambda j, i: (i, j)) if pj else (lambda j, i: (i, 0))))
        out_shape.append(jax.ShapeDtypeStruct((nrow, w * (ncol if pj else 1)), F32))
    for idx in bc_grads:
        r, w = ins[idx][1]
        out_specs.append(pl.BlockSpec((None, r, w), lambda j, i: (j, 0, 0)))
        out_shape.append(jax.ShapeDtypeStruct((ncol, r, w), F32))

    def body(*refs):
        j = pl.program_id(0)
        i = pl.program_id(1)
        vals = [r[...] for r in refs[:nin]]
        pos = nin
        cts = []
        for group in cots:
            c = refs[pos][...]
            for q in range(1, len(group)):
                c = c + refs[pos + q][...]
            pos += len(group)
            cts.append(c)
        res_ref = refs[pos] if residual is not None else None
        pos += len(extra)
        outs = refs[pos:]
        _, vjp = jax.vjp(lambda *a: tuple(fn(j, *a)), *vals)
        grads = vjp(tuple(cts))
        for q, (idx, _) in enumerate(row_grads):
            g = grads[idx]
            if q == 0 and res_ref is not None:
                g = g + res_ref[...]
            outs[q][...] = g
        for q, idx in enumerate(bc_grads):
            o_ref = outs[len(row_grads) + q]

            @pl.when(i == 0)
            def _(o_ref=o_ref):
                o_ref[...] = jnp.zeros_like(o_ref)

            o_ref[...] += grads[idx]

    all_in = list(ins) + flat_cots + extra
    return _call(body, grid=(ncol, nrow // t), in_specs=_specs(all_in), out_specs=out_specs, out_shape=out_shape,
                 sem=("parallel", "arbitrary"), name=name, args=[x[0] for x in all_in], side=side)


def _silu(x):
    return x * jax.nn.sigmoid(x)


@jax.custom_vjp
def _softplus(x):
    return jnp.maximum(x, 0.0) + jnp.log1p(jnp.exp(-jnp.abs(x)))


def _softplus_fwd(x):
    return _softplus(x), x


def _softplus_bwd(x, ct):
    return (ct * jax.nn.sigmoid(x),)


_softplus.defvjp(_softplus_fwd, _softplus_bwd)


def _gelu(x):
    return 0.5 * x * (1.0 + jnp.tanh(math.sqrt(2.0 / math.pi) * (x + 0.044715 * (x * x * x))))


def _row_shift_impl(x, s):
    n = x.shape[0]
    if s == 0:
        return x
    rolled = pltpu.roll(x, (-s) % n, 0)
    t = lax.broadcasted_iota(jnp.int32, x.shape, 0)
    ok = (t + s >= 0) & (t + s < n)
    return jnp.where(ok, rolled, 0.0)


@functools.partial(jax.custom_vjp, nondiff_argnums=(1,))
def _row_shift(x, s):
    return _row_shift_impl(x, s)


def _row_shift_fwd(x, s):
    return _row_shift_impl(x, s), None


def _row_shift_bwd(s, _, ct):
    return (_row_shift_impl(ct, -s),)


_row_shift.defvjp(_row_shift_fwd, _row_shift_bwd)


def fn_rms(j, x, g):
    return (x * lax.rsqrt(jnp.mean(x * x, axis=-1, keepdims=True) + RMS_EPS) * g,)


def make_fn_prep(n_heads):
    pad = (CONV_K - 1) // 2

    def fn_prep(j, x, w):
        y = _row_shift(x, -pad) * w[0:1, :]
        for i in range(1, CONV_K):
            y = y + _row_shift(x, i - pad) * w[i:i + 1, :]
        a = _silu(y)
        scale = jnp.where(j < n_heads, HEAD_DIM ** -0.5, 1.0).astype(F32)
        nrm = a * lax.rsqrt(jnp.sum(a * a, axis=-1, keepdims=True) + RMS_EPS) * scale
        return (jnp.where(j < 2 * n_heads, nrm, a),)

    return fn_prep


def fn_headnorm(j, o0, o1, z, g):
    o = o0 + o1
    n = o * lax.rsqrt(jnp.mean(o * o, axis=-1, keepdims=True) + RMS_EPS) * g
    return (n * _silu(z),)


def fn_gelu(j, y0, y1, u, dsk):
    return (_gelu(y0 + y1 + u * dsk),)


def fn_glu(j, ys, logit, z, b):
    return (ys * jax.nn.sigmoid(logit + b) * _silu(z),)


def fn_merge(j, la, lb, ya, yb, ba, bb):
    return (jax.nn.sigmoid(la + ba) * ya + jax.nn.sigmoid(lb + bb) * yb,)


def fn_loss(j, x, t, g):
    y = x * lax.rsqrt(jnp.mean(x * x, axis=-1, keepdims=True) + RMS_EPS) * g
    e = y - t
    return (0.5 * jnp.mean(e * e, axis=-1, keepdims=True),)


def make_fn_sum(n):
    def fn_sum(j, *xs):
        s = xs[0]
        for q in range(1, n):
            s = s + xs[q]
        return (s,)

    return fn_sum


def fn_adamw(j, w, g, m, v):
    m2 = ADAM_B1 * m + (1.0 - ADAM_B1) * g
    v2 = ADAM_B2 * v + (1.0 - ADAM_B2) * (g * g)
    m_hat = m2 / (1.0 - ADAM_B1 ** ADAM_STEP)
    v_hat = v2 / (1.0 - ADAM_B2 ** ADAM_STEP)
    delta = -ADAM_LR * (m_hat / (jnp.sqrt(v_hat) + ADAM_EPS) + ADAM_WD * w)
    return delta, m2, v2


def _dot(a, b, dims, precision=None):
    return lax.dot_general(a, b, (dims, ((), ())), precision=precision, preferred_element_type=F32)


_NN = ((1,), (0,))
_NT = ((1,), (1,))
_TN = ((0,), (0,))
GDN_HEAD_BLOCK = 4


def _split(a):
    hi = a.astype(MXU_DT)
    return hi, (a - hi.astype(F32)).astype(MXU_DT)


def _dot3(a, b, dims):
    ah, al = _split(a)
    bh, bl = _split(b)
    return _dot(ah, bh, dims) + (_dot(al, bh, dims) + _dot(ah, bl, dims))


def _unit_inverse(lmat):
    r = lmat.shape[0]
    eye = (lax.broadcasted_iota(jnp.int32, (r, r), 0) == lax.broadcasted_iota(jnp.int32, (r, r), 1)).astype(F32)
    pw = -lmat
    tinv = eye + pw
    for _ in range(int(math.ceil(math.log2(CHUNK))) - 1):
        pw = _dot3(pw, pw, _NN)
        tinv = tinv + _dot3(tinv, pw, _NN)
    return tinv


@jax.custom_vjp
def tri_solve(lmat, rhs):
    return _dot3(_unit_inverse(lmat), rhs, _NN)


def _tri_solve_fwd(lmat, rhs):
    tinv = _unit_inverse(lmat)
    x = _dot3(tinv, rhs, _NN)
    return x, (tinv, x)


def _tri_solve_bwd(res, dx):
    tinv, x = res
    drhs = _dot3(tinv, dx, _TN)
    return -_dot3(drhs, x, _NT), drhs


tri_solve.defvjp(_tri_solve_fwd, _tri_solve_bwd)


def gdn_group(s, q, k, v, ba, alog, dtb, *, d, head0, n_heads):
    hb = len(s)
    c = q.shape[0]
    r = hb * c

    def stack(x):
        return jnp.concatenate([x[:, i * HEAD_DIM:(i + 1) * HEAD_DIM] for i in range(hb)], axis=0)

    def pick(x, lane0):
        lane = lax.broadcasted_iota(jnp.int32, x.shape, 1)
        return jnp.concatenate(
            [jnp.sum(jnp.where(lane == lane0 + i, x, 0.0), axis=1, keepdims=True) for i in range(hb)], axis=0)

    q4, k4, v4 = stack(q), stack(k), stack(v)
    beta = pick(jax.nn.sigmoid(ba), d * n_heads + head0)
    g = pick(-jnp.exp(alog) * _softplus(ba + dtb), 2 * n_heads + d * n_heads + head0)
    ii = lax.broadcasted_iota(jnp.int32, (r, r), 0)
    jj = lax.broadcasted_iota(jnp.int32, (r, r), 1)
    same = (ii // c) == (jj // c)
    rel = (ii - jj) * (1 - 2 * d)
    incl = same & (rel >= 0)
    strict = same & (rel > 0)
    incl_t = same & (rel <= 0)
    g_row = jnp.sum(jnp.where(ii == jj, g, 0.0), axis=0, keepdims=True)
    gc_col = jnp.sum(jnp.where(incl, g_row, 0.0), axis=1, keepdims=True)
    gc_row = jnp.sum(jnp.where(incl_t, g, 0.0), axis=0, keepdims=True)
    g_tot = jnp.sum(jnp.where(same, g_row, 0.0), axis=1, keepdims=True)
    decay = jnp.exp(jnp.where(incl, gc_col - gc_row, -1e30))
    kb = k4 * beta
    vb = v4 * beta
    lmat = jnp.where(strict, _dot(kb, k4, _NT) * decay, 0.0)
    uw = tri_solve(lmat, jnp.concatenate([vb, kb * jnp.exp(gc_col)], axis=1))
    u, w = uw[:, :HEAD_DIM], uw[:, HEAD_DIM:]
    qk = _dot(q4, k4, _NT) * decay
    qe = q4 * jnp.exp(gc_col)
    kd = k4 * jnp.exp(g_tot - gc_col)
    eg = jnp.exp(g_tot)
    v_new, o_s = [], []
    for i in range(hb):
        rs = slice(i * c, (i + 1) * c)
        ws = _dot(jnp.concatenate([w[rs], qe[rs]], axis=0), s[i], _NN)
        v_new.append(u[rs] - ws[:c])
        o_s.append(ws[c:])
    o4 = jnp.concatenate(o_s, axis=0) + _dot(qk, jnp.concatenate(v_new, axis=0), _NN)
    s_new = tuple(s[i] * eg[i * c:i * c + 1, :] + _dot(kd[i * c:(i + 1) * c], v_new[i], _TN) for i in range(hb))
    o = jnp.concatenate([o4[i * c:(i + 1) * c] for i in range(hb)], axis=1)
    return s_new, o


def _gdn_maps(n_chunks):
    def chunk_of(d, step):
        return step + d * (n_chunks - 1 - 2 * step)
    return chunk_of


def _heads(ref, h0, n):
    return ref[:, h0 * HEAD_DIM:(h0 + n) * HEAD_DIM]


def gdn_fwd(qkv, proj, ba_blk, alog_row, dtb_row, *, n_heads, side=None, name):
    nrow = qkv.shape[0]
    nc = nrow // CHUNK
    h = n_heads
    wa = h * HEAD_DIM
    hb = math.gcd(h, GDN_HEAD_BLOCK)
    chunk_of = _gdn_maps(nc)

    def body(q_ref, k_ref, v_ref, ba_ref, al_ref, dt_ref, o_ref, sh_ref, s_scr):
        d = pl.program_id(0)
        n = pl.program_id(1)

        @pl.when(n == 0)
        def _():
            s_scr[...] = jnp.zeros_like(s_scr)

        ba, al, dt = ba_ref[...], al_ref[...], dt_ref[...]
        for h0 in range(0, h, hb):
            s = tuple(s_scr[h0 + i] for i in range(hb))
            for i in range(hb):
                sh_ref[h0 + i] = s[i]
            s2, o = gdn_group(s, _heads(q_ref, h0, hb), _heads(k_ref, h0, hb), _heads(v_ref, h0, hb), ba, al, dt,
                              d=d, head0=h0, n_heads=h)
            o_ref[:, h0 * HEAD_DIM:(h0 + hb) * HEAD_DIM] = o
            for i in range(hb):
                s_scr[h0 + i] = s2[i]

    blk = (CHUNK, wa)
    in_specs = [
        pl.BlockSpec(blk, lambda d, n: (chunk_of(d, n), 0)),
        pl.BlockSpec(blk, lambda d, n: (chunk_of(d, n), 1)),
        pl.BlockSpec(blk, lambda d, n: (chunk_of(d, n), 2)),
        pl.BlockSpec((CHUNK, LANES), lambda d, n: (chunk_of(d, n), ba_blk)),
        pl.BlockSpec((1, LANES), lambda d, n: (0, 0)),
        pl.BlockSpec((1, LANES), lambda d, n: (0, 0)),
    ]
    out_specs = [
        pl.BlockSpec((None, CHUNK, wa), lambda d, n: (d, chunk_of(d, n), 0)),
        pl.BlockSpec((h, None, HEAD_DIM, HEAD_DIM), lambda d, n: (d, n, 0, 0)),
    ]
    out_shape = [jax.ShapeDtypeStruct((2, nrow, wa), F32),
                 jax.ShapeDtypeStruct((2 * h, nc, HEAD_DIM, HEAD_DIM), F32)]
    return _call(body, grid=(2, nc), in_specs=in_specs, out_specs=out_specs, out_shape=out_shape,
                 scratch_shapes=[pltpu.VMEM((h, HEAD_DIM, HEAD_DIM), F32)], sem=("parallel", "arbitrary"),
                 name=name, args=(qkv, qkv, qkv, proj, alog_row, dtb_row), side=side)


def gdn_bwd(qkv, proj, ba_blk, alog_row, dtb_row, s_hist, do, *, n_heads, side=None, name):
    nrow = qkv.shape[0]
    nc = nrow // CHUNK
    h = n_heads
    wa = h * HEAD_DIM
    hb = math.gcd(h, GDN_HEAD_BLOCK)
    chunk_of = _gdn_maps(nc)

    def cb(d, n):
        return chunk_of(d, nc - 1 - n)

    def body(q_ref, k_ref, v_ref, ba_ref, al_ref, dt_ref, sh_ref, do_ref,
             dqkv_ref, dba_ref, dal_ref, ddt_ref, ds_scr):
        d = pl.program_id(0)
        n = pl.program_id(1)

        @pl.when(n == 0)
        def _():
            ds_scr[...] = jnp.zeros_like(ds_scr)
            dal_ref[...] = jnp.zeros_like(dal_ref)
            ddt_ref[...] = jnp.zeros_like(ddt_ref)

        ba, al, dt = ba_ref[...], al_ref[...], dt_ref[...]
        dba_sum = jnp.zeros_like(ba)
        dal_sum = jnp.zeros_like(al)
        ddt_sum = jnp.zeros_like(dt)
        for h0 in range(0, h, hb):
            f = functools.partial(gdn_group, d=d, head0=h0, n_heads=h)
            s = tuple(sh_ref[h0 + i] for i in range(hb))
            _, vjp = jax.vjp(f, s, _heads(q_ref, h0, hb), _heads(k_ref, h0, hb), _heads(v_ref, h0, hb), ba, al, dt)
            ds, dq, dk, dv, dba, dal, ddt = vjp((tuple(ds_scr[h0 + i] for i in range(hb)), _heads(do_ref, h0, hb)))
            for i in range(hb):
                ds_scr[h0 + i] = ds[i]
            cols = slice(h0 * HEAD_DIM, (h0 + hb) * HEAD_DIM)
            dqkv_ref[0, :, cols] = dq
            dqkv_ref[1, :, cols] = dk
            dqkv_ref[2, :, cols] = dv
            dba_sum = dba_sum + dba
            dal_sum = dal_sum + dal
            ddt_sum = ddt_sum + ddt
        dba_ref[...] = dba_sum
        dal_ref[...] += dal_sum
        ddt_ref[...] += ddt_sum

    blk = (CHUNK, wa)
    in_specs = [
        pl.BlockSpec(blk, lambda d, n: (cb(d, n), 0)),
        pl.BlockSpec(blk, lambda d, n: (cb(d, n), 1)),
        pl.BlockSpec(blk, lambda d, n: (cb(d, n), 2)),
        pl.BlockSpec((CHUNK, LANES), lambda d, n: (cb(d, n), ba_blk)),
        pl.BlockSpec((1, LANES), lambda d, n: (0, 0)),
        pl.BlockSpec((1, LANES), lambda d, n: (0, 0)),
        pl.BlockSpec((h, None, HEAD_DIM, HEAD_DIM), lambda d, n: (d, nc - 1 - n, 0, 0)),
        pl.BlockSpec(blk, lambda d, n: (cb(d, n), 0)),
    ]
    out_specs = [
        pl.BlockSpec((None, 3, CHUNK, wa), lambda d, n: (d, 0, cb(d, n), 0)),
        pl.BlockSpec((None, CHUNK, LANES), lambda d, n: (d, cb(d, n), 0)),
        pl.BlockSpec((None, 1, LANES), lambda d, n: (d, 0, 0)),
        pl.BlockSpec((None, 1, LANES), lambda d, n: (d, 0, 0)),
    ]
    out_shape = [jax.ShapeDtypeStruct((2, 3, nrow, wa), F32),
                 jax.ShapeDtypeStruct((2, nrow, LANES), F32),
                 jax.ShapeDtypeStruct((2, 1, LANES), F32),
                 jax.ShapeDtypeStruct((2, 1, LANES), F32)]
    return _call(body, grid=(2, nc), in_specs=in_specs, out_specs=out_specs, out_shape=out_shape,
                 scratch_shapes=[pltpu.VMEM((h, HEAD_DIM, HEAD_DIM), F32)], sem=("parallel", "arbitrary"),
                 name=name, args=(qkv, qkv, qkv, proj, alog_row, dtb_row, s_hist, do), side=side)


def s5_mats(lam_re, lam_im, log_dt, b_re, b_im, c_re, c_im):
    g = lam_re.shape[1]
    nb = g // S5_BLOCK_GROUPS
    dt = jnp.exp(log_dt)[..., None]
    mag = jnp.exp(lam_re * dt)
    ar = mag * jnp.cos(lam_im * dt)
    ai = mag * jnp.sin(lam_im * dt)
    den = lam_re * lam_re + lam_im * lam_im
    fr = ((ar - 1.0) * lam_re + ai * lam_im) / den
    fi = (ai * lam_re - (ar - 1.0) * lam_im) / den
    bbr = fr[..., None] * b_re - fi[..., None] * b_im
    bbi = fr[..., None] * b_im + fi[..., None] * b_re
    eye = jnp.eye(S5_BLOCK_GROUPS, dtype=F32)
    shp = (2, nb, S5_BLOCK_GROUPS, S5_STATE, S5_GROUP_CH)
    w_r = jnp.einsum('dsjpc,jk->dsjckp', bbr.reshape(shp), eye).reshape(2, nb, LANES, S5_BLOCK_STATE)
    w_i = jnp.einsum('dsjpc,jk->dsjckp', bbi.reshape(shp), eye).reshape(2, nb, LANES, S5_BLOCK_STATE)
    w = jnp.concatenate([w_r, w_i], axis=-1)
    shc = (2, nb, S5_BLOCK_GROUPS, S5_GROUP_CH, S5_STATE)
    c_r = jnp.einsum('dsjcp,jk->dsjpkc', c_re.reshape(shc), eye).reshape(2, nb, S5_BLOCK_STATE, LANES)
    c_i = jnp.einsum('dsjcp,jk->dsjpkc', c_im.reshape(shc), eye).reshape(2, nb, S5_BLOCK_STATE, LANES)
    cm = jnp.concatenate([c_r, -c_i], axis=-2)
    return (ar.reshape(2, nb, 1, S5_BLOCK_STATE), ai.reshape(2, nb, 1, S5_BLOCK_STATE), w, cm)


_S5_ROWS = 512
S5_SEGMENTS = 8


def segment_major(a):
    n, c = a.shape
    return jnp.transpose(a.reshape(S5_SEGMENTS, n // S5_SEGMENTS, c), (1, 0, 2)).reshape(n, c)


def time_major(a):
    n, c = a.shape
    return jnp.transpose(a.reshape(n // S5_SEGMENTS, S5_SEGMENTS, c), (1, 0, 2)).reshape(n, c)


def _segmented_scan(x_ref, a_r, a_i, *, reverse, visit=None):
    nrow, two_hs = x_ref.shape
    hs = two_hs // 2
    steps = nrow // S5_SEGMENTS
    assert steps & (steps - 1) == 0
    b_r = jnp.broadcast_to(a_r, (S5_SEGMENTS, hs))
    b_i = jnp.broadcast_to(a_i, (S5_SEGMENTS, hs))
    rid = lax.broadcasted_iota(jnp.int32, (S5_SEGMENTS, hs), 0)

    def rows_of(i):
        return pl.ds(pl.multiple_of((steps - 1 - i if reverse else i) * S5_SEGMENTS, S5_SEGMENTS), S5_SEGMENTS)

    def local(i, carry):
        s_r, s_i = carry
        rows = rows_of(i)
        n_r = b_r * s_r - b_i * s_i + x_ref[rows, pl.ds(0, hs)]
        n_i = b_r * s_i + b_i * s_r + x_ref[rows, pl.ds(hs, hs)]
        x_ref[rows, pl.ds(0, hs)] = n_r
        x_ref[rows, pl.ds(hs, hs)] = n_i
        return n_r, n_i

    z = jnp.zeros((S5_SEGMENTS, hs), F32)
    e_r, e_i = lax.fori_loop(0, steps, local, (z, z))
    q_r, q_i = a_r, a_i
    for _ in range(steps.bit_length() - 1):
        q_r, q_i = q_r * q_r - q_i * q_i, 2.0 * q_r * q_i
    c_r = jnp.zeros((1, hs), F32)
    c_i = jnp.zeros((1, hs), F32)
    en_r, en_i = z, z
    for s in (reversed(range(S5_SEGMENTS)) if reverse else range(S5_SEGMENTS)):
        en_r = jnp.where(rid == s, c_r, en_r)
        en_i = jnp.where(rid == s, c_i, en_i)
        c_r, c_i = (e_r[s:s + 1] + q_r * c_r - q_i * c_i, e_i[s:s + 1] + q_r * c_i + q_i * c_r)

    def fix(i, carry):
        p_r, p_i = carry
        rows = rows_of(i)
        x_r = x_ref[rows, pl.ds(0, hs)] + (p_r * en_r - p_i * en_i)
        x_i = x_ref[rows, pl.ds(hs, hs)] + (p_r * en_i + p_i * en_r)
        x_ref[rows, pl.ds(0, hs)] = x_r
        x_ref[rows, pl.ds(hs, hs)] = x_i
        if visit is not None:
            visit(steps - 1 - i if reverse else i, rows, x_r, x_i)
        return p_r * b_r - p_i * b_i, p_r * b_i + p_i * b_r

    return lax.fori_loop(0, steps, fix, (b_r, b_i))


def s5_fwd(u_seg, ar, ai, w, cm, *, name):
    nrow = u_seg.shape[0]
    nb = w.shape[1]
    hs = S5_BLOCK_STATE
    rs = min(_S5_ROWS, nrow)

    def body(u_ref, ar_ref, ai_ref, w_ref, cm_ref, y_ref, x_ref):
        d = pl.program_id(0)
        wb = w_ref[...].astype(MXU_DT)
        for r0 in range(0, nrow, rs):
            x_ref[pl.ds(r0, rs), :] = _dot(u_ref[pl.ds(r0, rs), :].astype(MXU_DT), wb, _NN)
        @pl.when(d == 0)
        def _():
            _segmented_scan(x_ref, ar_ref[...], ai_ref[...], reverse=False)

        @pl.when(d == 1)
        def _():
            _segmented_scan(x_ref, ar_ref[...], ai_ref[...], reverse=True)

        cb = cm_ref[...].astype(MXU_DT)
        for r0 in range(0, nrow, rs):
            y_ref[pl.ds(r0, rs), :] = _dot(x_ref[pl.ds(r0, rs), :].astype(MXU_DT), cb, _NN)

    in_specs = [
        pl.BlockSpec((nrow, LANES), lambda d, s: (0, s)),
        pl.BlockSpec((None, None, 1, hs), lambda d, s: (d, s, 0, 0)),
        pl.BlockSpec((None, None, 1, hs), lambda d, s: (d, s, 0, 0)),
        pl.BlockSpec((None, None, LANES, 2 * hs), lambda d, s: (d, s, 0, 0)),
        pl.BlockSpec((None, None, 2 * hs, LANES), lambda d, s: (d, s, 0, 0)),
    ]
    out_specs = [
        pl.BlockSpec((None, nrow, LANES), lambda d, s: (d, 0, s)),
        pl.BlockSpec((None, nrow, 2 * hs), lambda d, s: (d, 0, s)),
    ]
    out_shape = [jax.ShapeDtypeStruct((2, nrow, nb * LANES), F32),
                 jax.ShapeDtypeStruct((2, nrow, nb * 2 * hs), F32)]
    return pl.pallas_call(
        body, grid=(2, nb), in_specs=in_specs, out_specs=out_specs, out_shape=out_shape,
        compiler_params=_params(("parallel", "parallel")), name=name)(u_seg, ar, ai, w, cm)


def s5_bwd(u_seg, ar, ai, w, cm, xs, dy, du_in, *, side=None, name):
    nrow = u_seg.shape[0]
    nb = w.shape[1]
    steps = nrow // S5_SEGMENTS
    hs = S5_BLOCK_STATE
    rs = min(_S5_ROWS, nrow)

    def body(u_ref, dy_ref, dui_ref, x_ref, ar_ref, ai_ref, w_ref, cm_ref,
             du_ref, dw_ref, dcm_ref, dar_ref, dai_ref, g_ref, acc_ref):
        d = pl.program_id(1)
        cb = cm_ref[...].astype(MXU_DT)
        for r0 in range(0, nrow, rs):
            g_ref[pl.ds(r0, rs), :] = _dot(dy_ref[pl.ds(r0, rs), :].astype(MXU_DT), cb, _NT)
        acc_ref[...] = jnp.zeros_like(acc_ref)
        rid = lax.broadcasted_iota(jnp.int32, (S5_SEGMENTS, 2 * hs), 0)

        def run(forward_recurrence):
            last = pl.ds((steps - 1) * S5_SEGMENTS, S5_SEGMENTS)
            first = pl.ds(0, S5_SEGMENTS)
            if forward_recurrence:
                wrap = jnp.where(rid == 0, 0.0, pltpu.roll(x_ref[last, :], 1, 0))
            else:
                wrap = jnp.where(rid == S5_SEGMENTS - 1, 0.0, pltpu.roll(x_ref[first, :], S5_SEGMENTS - 1, 0))

            def visit(step, rows, g_r, g_i):
                if forward_recurrence:
                    nbr = jnp.maximum(step - 1, 0)
                    edge = step == 0
                else:
                    nbr = jnp.minimum(step + 1, steps - 1)
                    edge = step == steps - 1
                prev = x_ref[pl.ds(pl.multiple_of(nbr * S5_SEGMENTS, S5_SEGMENTS), S5_SEGMENTS), :]
                prev = jnp.where(edge, wrap, prev)
                p_r, p_i = prev[:, :hs], prev[:, hs:]
                acc_ref[:, pl.ds(0, hs)] += g_r * p_r + g_i * p_i
                acc_ref[:, pl.ds(hs, hs)] += g_i * p_r - g_r * p_i

            _segmented_scan(g_ref, ar_ref[...], -ai_ref[...], reverse=forward_recurrence, visit=visit)

        @pl.when(d == 0)
        def _():
            run(True)

        @pl.when(d == 1)
        def _():
            run(False)

        dar_ref[...] = jnp.sum(acc_ref[:, pl.ds(0, hs)], axis=0, keepdims=True)
        dai_ref[...] = jnp.sum(acc_ref[:, pl.ds(hs, hs)], axis=0, keepdims=True)

        wb = w_ref[...].astype(MXU_DT)
        for r0 in range(0, nrow, rs):
            part = _dot(g_ref[pl.ds(r0, rs), :].astype(MXU_DT), wb, _NT)

            @pl.when(d == 0)
            def _(part=part, r0=r0):
                du_ref[pl.ds(r0, rs), :] = dui_ref[pl.ds(r0, rs), :] + part

            @pl.when(d == 1)
            def _(part=part, r0=r0):
                du_ref[pl.ds(r0, rs), :] += part

        dw_ref[...] = _dot(u_ref[...].astype(MXU_DT), g_ref[...].astype(MXU_DT), _TN)
        dcm_ref[...] = _dot(x_ref[...].astype(MXU_DT), dy_ref[...].astype(MXU_DT), _TN)

    in_specs = [
        pl.BlockSpec((nrow, LANES), lambda s, d: (0, s)),
        pl.BlockSpec((nrow, LANES), lambda s, d: (0, s)),
        pl.BlockSpec((nrow, LANES), lambda s, d: (0, s)),
        pl.BlockSpec((None, nrow, 2 * hs), lambda s, d: (d, 0, s)),
        pl.BlockSpec((None, None, 1, hs), lambda s, d: (d, s, 0, 0)),
        pl.BlockSpec((None, None, 1, hs), lambda s, d: (d, s, 0, 0)),
        pl.BlockSpec((None, None, LANES, 2 * hs), lambda s, d: (d, s, 0, 0)),
        pl.BlockSpec((None, None, 2 * hs, LANES), lambda s, d: (d, s, 0, 0)),
    ]
    out_specs = [
        pl.BlockSpec((nrow, LANES), lambda s, d: (0, s)),
        pl.BlockSpec((None, None, LANES, 2 * hs), lambda s, d: (d, s, 0, 0)),
        pl.BlockSpec((None, None, 2 * hs, LANES), lambda s, d: (d, s, 0, 0)),
        pl.BlockSpec((None, None, 1, hs), lambda s, d: (d, s, 0, 0)),
        pl.BlockSpec((None, None, 1, hs), lambda s, d: (d, s, 0, 0)),
    ]
    out_shape = [jax.ShapeDtypeStruct((nrow, nb * LANES), F32),
                 jax.ShapeDtypeStruct(w.shape, F32), jax.ShapeDtypeStruct(cm.shape, F32),
                 jax.ShapeDtypeStruct(ar.shape, F32), jax.ShapeDtypeStruct(ai.shape, F32)]
    return _call(body, grid=(nb, 2), in_specs=in_specs, out_specs=out_specs, out_shape=out_shape,
                 scratch_shapes=[pltpu.VMEM((nrow, 2 * hs), F32), pltpu.VMEM((S5_SEGMENTS, 2 * hs), F32)],
                 sem=("parallel", "arbitrary"), name=name, args=(u_seg, dy, du_in, xs, ar, ai, w, cm), side=side)


def _me():
    return lax.axis_index("x"), lax.axis_index("y"), lax.axis_index("c")


def all_gather(shards):
    na = len(shards)

    def plan(x_refs, out_refs, sems):
        send_sems, recv_sems, local_sems = sems
        x, y, c = _me()
        me, sibling = (x, y, c), (x, y, 1 - c)
        chips = [(1 - x, y), (x, 1 - y), (1 - x, 1 - y)]

        def slot(a, px, py, pc):
            return out_refs[a].at[4 * px + 2 * py + pc]

        def copy(a, k, block, to, src=None):
            return pltpu.make_async_remote_copy(
                src_ref=slot(a, *block) if src is None else src, dst_ref=slot(a, *block),
                send_sem=send_sems.at[a, k], recv_sem=recv_sems.at[a, k], device_id=to, device_id_type=MESH)

        mine = [pltpu.make_async_copy(x_refs[a], slot(a, *me), local_sems.at[a]) for a in range(na)]
        first = []
        for a in range(na):
            first.append(copy(a, 0, me, sibling, src=x_refs[a]))
            first += [copy(a, 1 + j, me, (*chip, c), src=x_refs[a]) for j, chip in enumerate(chips)]
        return me, sibling, chips, c, copy, mine, first

    def start(x_refs, out_refs, sems):
        _, _, _, _, _, mine, first = plan(x_refs, out_refs, sems)
        for cp in mine + first:
            cp.start()

    def finish(x_refs, out_refs, sems):
        me, sibling, chips, c, copy, mine, first = plan(x_refs, out_refs, sems)
        passed = []
        for j, chip in enumerate(chips):
            for a in range(na):
                copy(a, 1 + j, (*chip, c), me).wait_recv()
                fwd = copy(a, 4 + j, (*chip, c), sibling)
                fwd.start()
                passed.append(fwd)
        for a in range(na):
            copy(a, 0, sibling, me).wait_recv()
            for j, chip in enumerate(chips):
                copy(a, 4 + j, (*chip, 1 - c), me).wait_recv()
        for cp in first + passed:
            cp.wait_send()
        for cp in mine:
            cp.wait()

    return Side(shards, [jax.ShapeDtypeStruct((8,) + s.shape, s.dtype) for s in shards],
                [pltpu.SemaphoreType.DMA((na, 7)), pltpu.SemaphoreType.DMA((na, 7)), pltpu.SemaphoreType.DMA((na,))],
                start, finish)


_AXES = ("x", "y", "c")


def exchange(bufs, axes, *, half):
    na = len(bufs)
    if isinstance(half, bool):
        half = [half] * na

    def copies(in_refs, out_refs, sems):
        send_sems, recv_sems = sems
        me = _me()
        cps = []
        for a in range(na):
            bit = me[_AXES.index(axes[a])]
            peer = tuple(1 - v if ax == axes[a] else v for ax, v in zip(_AXES, me))
            cps.append(pltpu.make_async_remote_copy(
                src_ref=in_refs[a].at[1 - bit] if half[a] else in_refs[a], dst_ref=out_refs[a],
                send_sem=send_sems.at[a], recv_sem=recv_sems.at[a], device_id=peer, device_id_type=MESH))
        return cps

    def start(*refs):
        for cp in copies(*refs):
            cp.start()

    def finish(*refs):
        for cp in copies(*refs):
            cp.wait()

    return Side(bufs, [jax.ShapeDtypeStruct(b.shape[1:] if h else b.shape, b.dtype) for b, h in zip(bufs, half)],
                [pltpu.SemaphoreType.DMA((na,)), pltpu.SemaphoreType.DMA((na,))], start, finish)


def all_to_all(pieces, slot_fns):
    na = len(pieces)

    def copies(in_refs, out_refs, sems):
        send_sems, recv_sems, local_sems = sems
        x, y, c = _me()
        cps = []
        for a in range(na):
            mine = slot_fns[a](x, y, c)
            for k in range(8):
                tx, ty, tc = x ^ (k // 4), y ^ ((k // 2) % 2), c ^ (k % 2)
                src = in_refs[a].at[4 * tx + 2 * ty + tc]
                dst = out_refs[a].at[mine]
                if k == 0:
                    cps.append(pltpu.make_async_copy(src, dst, local_sems.at[a]))
                else:
                    cps.append(pltpu.make_async_remote_copy(
                        src_ref=src, dst_ref=dst, send_sem=send_sems.at[a, k - 1], recv_sem=recv_sems.at[a, k - 1],
                        device_id=(tx, ty, tc), device_id_type=MESH))
        return cps

    def start(*refs):
        for cp in copies(*refs):
            cp.start()

    def finish(*refs):
        for cp in copies(*refs):
            cp.wait()

    return Side(pieces, [jax.ShapeDtypeStruct(p.shape, p.dtype) for p in pieces],
                [pltpu.SemaphoreType.DMA((na, 7)), pltpu.SemaphoreType.DMA((na, 7)), pltpu.SemaphoreType.DMA((na,))],
                start, finish)


def add_half(buf, recv, bit, *, name):
    c = recv.shape[-1]
    r = math.prod(recv.shape[:-1])
    t = _tile(r, max(8, (ADD_BLOCK_BYTES // (4 * c)) // 8 * 8), 8)

    def body(bit_ref, a_ref, b_ref, o_ref):
        o_ref[...] = a_ref[...] + b_ref[...]

    grid_spec = pltpu.PrefetchScalarGridSpec(
        num_scalar_prefetch=1, grid=(r // t,),
        in_specs=[pl.BlockSpec((None, t, c), lambda i, b: (b[0], i, 0)), pl.BlockSpec((t, c), lambda i, b: (i, 0))],
        out_specs=pl.BlockSpec((t, c), lambda i, b: (i, 0)))
    out = pl.pallas_call(body, grid_spec=grid_spec, out_shape=jax.ShapeDtypeStruct((r, c), F32),
                         compiler_params=_params(("parallel",)), name=name)(
                             bit, buf.reshape(2, r, c), recv.reshape(r, c))
    return out.reshape(recv.shape)


def _pack_rows(n, cols):
    return -(-n // (8 * cols)) * 8


def _pack(arrs, cols, mult):
    parts = []
    for a in arrs:
        n = math.prod(a.shape)
        nr = _pack_rows(n, cols)
        parts.append(jnp.pad(a.reshape(-1), (0, nr * cols - n)).reshape(nr, cols))
    total = sum(p.shape[0] for p in parts)
    pad_rows = -(-total // mult) * mult - total
    if pad_rows:
        parts.append(jnp.zeros((pad_rows, cols), arrs[0].dtype))
    return jnp.concatenate(parts, axis=0)


def _unpack(buf, shapes):
    out, r = [], 0
    cols = buf.shape[1]
    for s in shapes:
        n = math.prod(s)
        nr = _pack_rows(n, cols)
        piece = lax.optimization_barrier(buf[r:r + nr])
        out.append(piece.reshape(-1)[:n].reshape(s))
        r += nr
    return out


def _elementwise(fn, arrs, nout, name):
    r, c = arrs[0].shape
    t = _tile(r, ROW_TILE, 8)
    return rw_fwd(fn, [rows(a, t) for a in arrs], [(c, False)] * nout, nrow=r, t=t, name=name)


def kernel(x, ln_g, w_in, conv_w, a_log, dt_bias, head_norm_g, lam_re, lam_im, log_dt, b_re, b_im, c_re, c_im, d_skip, w_glu, b_glu, w_pa, w_pb, b_gate, w_out, final_g, loss_target, m_ln_g, m_w_in, m_conv_w, m_a_log, m_dt_bias, m_head_norm_g, m_lam_re, m_lam_im, m_log_dt, m_b_re, m_b_im, m_c_re, m_c_im, m_d_skip, m_w_glu, m_b_glu, m_w_pa, m_w_pb, m_b_gate, m_w_out, m_final_g, v_ln_g, v_w_in, v_conv_w, v_a_log, v_dt_bias, v_head_norm_g, v_lam_re, v_lam_im, v_log_dt, v_b_re, v_b_im, v_c_re, v_c_im, v_d_skip, v_w_glu, v_b_glu, v_w_pa, v_w_pb, v_b_gate, v_w_out, v_final_g):
    env = dict(locals())
    wts = {n: env[n] for n in WEIGHTS}
    mom_m = {n: env["m_" + n] for n in WEIGHTS}
    mom_v = {n: env["v_" + n] for n in WEIGHTS}

    xin = x[0]
    tgt = loss_target[0]
    nrow, dm = xin.shape
    depth = ln_g.shape[0]
    nh = dm // (2 * HEAD_DIM)
    wa = nh * HEAD_DIM
    wb = dm // 2
    ngrp = wb // S5_GROUP_CH
    pw = w_in.shape[-1]
    t = min(ROW_TILE, nrow)

    o_ba = 4 * wa
    o_u = o_ba + 4 * nh
    n_main = 4 * wa + 2 * wb + 2 * dm
    projp = -(-(n_main + LANES) // 512) * 512
    blk_za = 3 * wa // HEAD_DIM
    blk_u = 4 * wa // LANES
    ba_blk = n_main // LANES
    cx, cy, cc = _me()

    jb = o_ba // pw
    assert (o_u - 1) // pw == jb
    cut_lo, cut = o_ba - jb * pw, o_u - o_ba
    wide = -(-pw // LANES) * LANES + LANES
    col = lax.broadcasted_iota(jnp.int32, (pw, wide), 1)
    row = lax.broadcasted_iota(jnp.int32, (pw, wide), 0)
    holds_cut = (4 * cx + 2 * cy + cc) == jb
    src_plain = jnp.where(col < pw, col, -1)
    src_cut = jnp.where(col < cut_lo, col, jnp.where(col < pw - cut, col + cut, -1))
    src_cut = jnp.where((col >= wide - LANES) & (col < wide - LANES + cut), col - (wide - LANES) + cut_lo, src_cut)
    select = (row == jnp.where(holds_cut, src_cut, src_plain)).astype(WIRE_DT)
    w_in_tx = mm(w_in.reshape(depth * dm, pw), select, name="w_in_prepare").astype(WIRE_DT).reshape(depth, dm, wide)

    def gather_side(l):
        return all_gather([w_in_tx[l], w_glu[l].astype(WIRE_DT), w_pa[l].astype(WIRE_DT),
                           w_pb[l].astype(WIRE_DT), w_out[l].astype(WIRE_DT), conv_w[l]])

    def cat(g):
        return jnp.concatenate([g[j] for j in range(8)], axis=1)

    def assemble(gathered):
        g_in, g_glu, g_pa, g_pb, g_out, g_conv = gathered
        w_perm = jnp.concatenate(
            [g_in[j][:, :pw - cut if j == jb else pw] for j in range(8)]
            + [g_in[jb][:, wide - LANES:], jnp.zeros((dm, projp - n_main - LANES), WIRE_DT)], axis=1)
        return dict(w_in=w_perm, w_glu=g_glu.reshape(wb, wb), w_pa=cat(g_pa), w_pb=cat(g_pb),
                    w_out=g_out.reshape(dm, dm), conv_w=cat(g_conv))

    full = [assemble(run_side(gather_side(0), name="gather_weights"))]

    def small(l):
        z = jnp.zeros((1, LANES - 4 * nh), F32)
        alog_row = jnp.concatenate([jnp.zeros((1, 2 * nh), F32), a_log[l].reshape(1, 2 * nh), z], axis=1)
        dtb_row = jnp.concatenate([jnp.zeros((1, 2 * nh), F32), dt_bias[l].reshape(1, 2 * nh), z], axis=1)
        return alog_row, dtb_row

    saved = []
    cur = xin
    for l in range(depth):
        fw = full[l]
        (hh,) = rw_fwd(fn_rms, [rows(cur, t), bcast(ln_g[l][None])], [(dm, False)], nrow=nrow, t=t, name="rms_fwd")
        proj = mm(hh, fw["w_in"], name="proj_fwd")
        (qkv,) = rw_fwd(make_fn_prep(nh), [rows(proj, nrow, HEAD_DIM, 0, True), bcast(fw["conv_w"], HEAD_DIM, 0, True)],
                        [(HEAD_DIM, True)], nrow=nrow, t=nrow, ncol=3 * nh, name="prep_fwd")
        alog_row, dtb_row = small(l)
        nxt_gather = gather_side(l + 1) if l + 1 < depth else None
        o_dir, s_hist = gdn_fwd(qkv, proj, ba_blk, alog_row, dtb_row, n_heads=nh, side=nxt_gather, name="gdn_fwd")
        if nxt_gather is not None:
            full.append(assemble(nxt_gather.result))
        hn_ins = [rows3(o_dir, 0, t, HEAD_DIM), rows3(o_dir, 1, t, HEAD_DIM), rows(proj, t, HEAD_DIM, blk_za, True),
                  bcast(head_norm_g[l][None])]
        (ya_in,) = rw_fwd(fn_headnorm, hn_ins, [(HEAD_DIM, True)], nrow=nrow, t=t, ncol=nh, name="headnorm_fwd")
        mats, mats_vjp = jax.vjp(s5_mats, lam_re[l], lam_im[l], log_dt[l], b_re[l], b_im[l], c_re[l], c_im[l])
        u_seg = segment_major(proj[:, 4 * wa:4 * wa + wb])
        yd, xs = s5_fwd(u_seg, *mats, name="s5_fwd")
        ge_ins = [rows3(yd, 0, t, wb, False), rows3(yd, 1, t, wb, False), rows(u_seg, t), bcast(d_skip[l][None])]
        (ys_seg,) = rw_fwd(fn_gelu, ge_ins, [(wb, False)], nrow=nrow, t=t, name="gelu_fwd")
        ys = time_major(ys_seg)
        glu = mm(ys, fw["w_glu"], name="glu_fwd")
        gl_ins = [rows(ys, t), rows(glu, t), rows(proj, t, wb, (4 * wa + wb) // wb), bcast(b_glu[l][None])]
        (yb_in,) = rw_fwd(fn_glu, gl_ins, [(wb, False)], nrow=nrow, t=t, name="glugate_fwd")
        y_a = mm(ya_in, fw["w_pa"], name="pa_fwd")
        y_b = mm(yb_in, fw["w_pb"], name="pb_fwd")
        bg = b_gate[l][None]
        mg_ins = [rows(proj, t, dm, 3), rows(proj, t, dm, 4), rows(y_a, t), rows(y_b, t),
                  bcast(bg, dm, 0), bcast(bg, dm, 1)]
        (merged,) = rw_fwd(fn_merge, mg_ins, [(dm, False)], nrow=nrow, t=t, name="merge_fwd")
        nxt = mm(merged, fw["w_out"], add=cur, name="out_fwd")
        saved.append(dict(x=cur, h=hh, proj=proj, qkv=qkv, o_dir=o_dir, s_hist=s_hist, hn_ins=hn_ins,
                          mats=mats, mats_vjp=mats_vjp, xs=xs, ge_ins=ge_ins, ys=ys, gl_ins=gl_ins, u_seg=u_seg,
                          ya_in=ya_in, yb_in=yb_in, mg_ins=mg_ins, merged=merged,
                          alog_row=alog_row, dtb_row=dtb_row))
        cur = nxt

    loss_ins = [rows(cur, t), rows(tgt, t), bcast(final_g[None])]
    (row_loss,) = rw_fwd(fn_loss, loss_ins, [(1, False)], nrow=nrow, t=t, name="loss_fwd")
    ones = jnp.ones((nrow, 1), F32)
    dcur, dfinal = rw_bwd(fn_loss, loss_ins, [[rows(ones, t)]], nrow=nrow, t=t,
                          row_grads=[(0, False)], bc_grads=[2], name="loss_bwd")
    loss = lax.psum(jnp.sum(row_loss), ("x", "y", "c"))

    coord = {"x": cx, "y": cy, "c": cc}
    routes = {"cxy": ("c", "x", "y"), "cyx": ("c", "y", "x")}
    route = ["cxy", "cyx", "cxy", "cxy", "cyx", "cyx", "cxy"]
    conv_order = [4 * ((q // 2) % 2) + 2 * (q % 2) + q // 4 for q in range(8)]
    cw = 3 * wa // 8
    g_final = {n: [None] * depth for n in SHARDED}
    gsh = {n: [None] * depth for n in SHARDED}

    def own_columns(z):
        un = jnp.concatenate([z[:, :o_ba], z[:, n_main:n_main + 4 * nh], z[:, o_ba:n_main]], axis=1)
        return jnp.stack([un[:, d * pw:(d + 1) * pw] for d in range(8)])

    def rs_begin(l):
        in_a, in_b = gsh["w_in"][l]
        conv_buf = jnp.stack([gsh["conv_w"][l][:, d * cw:(d + 1) * cw] for d in conv_order])
        return dict(layer=l, stage=0, bufs=[
            in_a.reshape(8, dm // 16, projp), in_b.reshape(8, dm // 16, projp), gsh["w_out"][l],
            gsh["w_glu"][l], gsh["w_pa"][l], gsh["w_pb"][l], conv_buf])

    def rs_side(st):
        if st is None:
            return None
        if st["stage"] < 3:
            st["axes"] = [routes[r][st["stage"]] for r in route]
            st["bufs"] = [b.reshape((2, b.shape[0] // 2) + b.shape[1:]) for b in st["bufs"]]
            if st.get("whole") is not None:
                return exchange(st["bufs"] + [st["whole"]], st["axes"] + [routes["cxy"][st["stage"]]],
                                half=[True] * len(st["bufs"]) + [False])
            return exchange(st["bufs"], st["axes"], half=True)
        st["bufs"] = [b.reshape(b.shape[1:]) for b in st["bufs"]]
        return all_to_all([own_columns(st["bufs"][0]), own_columns(st["bufs"][1])],
                          [lambda x, y, c: 4 * c + 2 * x + y, lambda x, y, c: 4 * c + 2 * y + x])

    def rs_absorb(st, side):
        if st is None:
            return
        if st["stage"] < 3:
            if st.get("whole") is not None:
                (st["whole"],) = _elementwise(make_fn_sum(2), [st["whole"], side.result[-1]], 1, "ar_add")
            st["bufs"] = [add_half(b, r, coord[ax].astype(jnp.int32).reshape(1), name=f"rs_add_{st['stage']}")
                          for b, r, ax in zip(st["bufs"], side.result, st["axes"])]
        else:
            got_a, got_b = side.result
            g_final["w_in"][st["layer"]] = jnp.concatenate(
                [got_a.reshape(dm // 2, pw), got_b.reshape(dm // 2, pw)], axis=0)
            for n, b in zip(["w_out", "w_glu", "w_pa", "w_pb", "conv_w"], st["bufs"][2:]):
                g_final[n][st["layer"]] = b
        st["stage"] += 1

    grep = {n: [None] * depth for n in REPLICATED if n != "final_g"}
    pending = None
    for l in reversed(range(depth)):
        fw, sv = full[l], saved[l]
        dmerged = mm(dcur, fw["w_out"], tb=True, name="out_bwd_x")
        gsh["w_out"][l] = mm(sv["merged"], dcur, ta=True, scatter=("rows", slot_cxy), name="out_bwd_w")
        side = rs_side(pending)
        dla, dlb, dya, dyb, dbga, dbgb = rw_bwd(
            fn_merge, sv["mg_ins"], [[rows(dmerged, t)]], nrow=nrow, t=t,
            row_grads=[(0, False), (1, False), (2, False), (3, False)], bc_grads=[4, 5], side=side, name="merge_bwd")
        rs_absorb(pending, side)
        grep["b_gate"][l] = jnp.concatenate([dbga.reshape(dm), dbgb.reshape(dm)])
        dya_in = mm(dya, fw["w_pa"], tb=True, name="pa_bwd_x")
        gsh["w_pa"][l] = mm(sv["ya_in"], dya, ta=True, scatter=("cols", slot_cyx), name="pa_bwd_w")
        dyb_in = mm(dyb, fw["w_pb"], tb=True, name="pb_bwd_x")
        gsh["w_pb"][l] = mm(sv["yb_in"], dyb, ta=True, scatter=("cols", slot_cyx), name="pb_bwd_w")
        dys1, dglu, dzb, dbglu = rw_bwd(
            fn_glu, sv["gl_ins"], [[rows(dyb_in, t)]], nrow=nrow, t=t,
            row_grads=[(0, False), (1, False), (2, False)], bc_grads=[3], name="glugate_bwd")
        grep["b_glu"][l] = dbglu.reshape(wb)
        dys2 = mm(dglu, fw["w_glu"], tb=True, name="glu_bwd_x")
        gsh["w_glu"][l] = mm(sv["ys"], dglu, ta=True, scatter=("rows", slot_cxy), name="glu_bwd_w")
        dyd, du1, ddskip = rw_bwd(
            fn_gelu, sv["ge_ins"], [[rows(segment_major(dys1), t), rows(segment_major(dys2), t)]], nrow=nrow, t=t,
            row_grads=[(0, False), (2, False)], bc_grads=[3], name="gelu_bwd")
        grep["d_skip"][l] = ddskip.reshape(wb)
        side = rs_side(pending)
        du_seg, dw_s5, dcm_s5, dar, dai = s5_bwd(sv["u_seg"], *sv["mats"], sv["xs"], dyd, du1, side=side,
                                                 name="s5_bwd")
        du = time_major(du_seg)
        rs_absorb(pending, side)
        g_lr, g_li, g_ldt, g_br, g_bi, g_cr, g_ci = sv["mats_vjp"]((dar, dai, dw_s5, dcm_s5))
        for nme, val in zip(["lam_re", "lam_im", "log_dt", "b_re", "b_im", "c_re", "c_im"],
                            [g_lr, g_li, g_ldt, g_br, g_bi, g_cr, g_ci]):
            grep[nme][l] = val
        do, dza, dhn = rw_bwd(fn_headnorm, sv["hn_ins"], [[rows(dya_in, t, HEAD_DIM, 0, True)]], nrow=nrow, t=t,
                              ncol=nh, row_grads=[(0, True), (2, True)], bc_grads=[3], name="headnorm_bwd")
        grep["head_norm_g"][l] = jnp.sum(dhn, axis=0).reshape(HEAD_DIM)
        side = rs_side(pending)
        dqkv, dba_all, dal, ddt = gdn_bwd(sv["qkv"], sv["proj"], ba_blk, sv["alog_row"], sv["dtb_row"],
                                          sv["s_hist"], do, n_heads=nh, side=side, name="gdn_bwd")
        rs_absorb(pending, side)
        grep["a_log"][l] = jnp.sum(dal, axis=(0, 1))[2 * nh:4 * nh].reshape(2, nh)
        grep["dt_bias"][l] = jnp.sum(ddt, axis=(0, 1))[2 * nh:4 * nh].reshape(2, nh)
        (dba,) = rw_fwd(make_fn_sum(2), [rows3(dba_all, p, t, LANES, False) for p in range(2)],
                        [(LANES, False)], nrow=nrow, t=t, name="dba_sum")
        prep_ins = [rows(sv["proj"], nrow, HEAD_DIM, 0, True), bcast(fw["conv_w"], HEAD_DIM, 0, True)]
        dq_cots = [(dqkv.reshape(2, 3 * nrow, wa), (None, nrow, HEAD_DIM),
                    (lambda j, i, dd=dd: (dd, j // nh, j % nh))) for dd in range(2)]
        dqkv_raw, dconv = rw_bwd(make_fn_prep(nh), prep_ins, [dq_cots], nrow=nrow, t=nrow, ncol=3 * nh,
                                 row_grads=[(0, True)], bc_grads=[1], name="prep_bwd")
        gsh["conv_w"][l] = jnp.transpose(dconv, (1, 0, 2)).reshape(CONV_K, 3 * wa)
        dproj = jnp.concatenate([dqkv_raw, dza, du, dzb, dla, dlb, dba,
                                 jnp.zeros((nrow, projp - n_main - LANES), F32)], axis=1).astype(MXU_DT)
        side = rs_side(pending)
        dh = mm(dproj, fw["w_in"], tb=True, side=side, name="proj_bwd_x")
        rs_absorb(pending, side)
        gsh["w_in"][l] = [mm(sv["h"], dproj, ta=True, m_part=(part, 2), name="proj_bwd_w") for part in range(2)]
        dcur, dlng = rw_bwd(fn_rms, [rows(sv["x"], t), bcast(ln_g[l][None])], [[rows(dh, t)]], nrow=nrow, t=t,
                            row_grads=[(0, False)], bc_grads=[1], residual=rows(dcur, t), name="rms_bwd")
        grep["ln_g"][l] = dlng.reshape(dm)
        pending = rs_begin(l)
    grad_x = dcur[None]
    rep_list = [jnp.stack(grep[n]) for n in REPLICATED if n != "final_g"] + [dfinal.reshape(dm)]
    rep_shapes = [a.shape for a in rep_list]
    pending["whole"] = _pack(rep_list, COMM_COLS, ROW_TILE)
    for stage in range(RS_STAGES):
        side = rs_side(pending)
        run_side(side, name=f"rs_stage_{stage}")
        rs_absorb(pending, side)
    grads = {n: jnp.stack(g_final[n]) for n in SHARDED}
    for n, val in zip(REPLICATED, _unpack(pending["whole"], rep_shapes)):
        grads[n] = val

    deltas, new_m, new_v = {}, {}, {}
    big = ["w_in", "w_glu", "w_pa", "w_pb", "w_out"]
    for n in big:
        shp = wts[n].shape
        two = [a.reshape(-1, shp[-1]) for a in (wts[n], grads[n], mom_m[n], mom_v[n])]
        d_, m_, v_ = _elementwise(fn_adamw, two, 3, "adamw_" + n)
        deltas[n], new_m[n], new_v[n] = d_.reshape(shp), m_.reshape(shp), v_.reshape(shp)
    rest = [n for n in WEIGHTS if n not in big]
    rest_shapes = [wts[n].shape for n in rest]
    packed = [_pack([src[n] for n in rest], COMM_COLS, ROW_TILE) for src in (wts, grads, mom_m, mom_v)]
    outs = _elementwise(fn_adamw, packed, 3, "adamw_small")
    for dst, arr in zip((deltas, new_m, new_v), outs):
        for n, val in zip(rest, _unpack(arr, rest_shapes)):
            dst[n] = val

    return (loss, grad_x, *[grads[n] for n in WEIGHTS], *[deltas[n] for n in WEIGHTS],
            *[new_m[n] for n in WEIGHTS], *[new_v[n] for n in WEIGHTS])
```

```python
import functools
import math

import jax
import jax.numpy as jnp
from jax import lax
from jax.experimental import pallas as pl
from jax.experimental.pallas import tpu as pltpu

F32 = jnp.float32
MXU_DT = jnp.bfloat16
WIRE_DT = jnp.bfloat16

HEAD_DIM = 128
CHUNK = 64
CONV_K = 5
S5_GROUP_CH = 16
S5_STATE = 64
S5_BLOCK_GROUPS = 8
S5_BLOCK_STATE = S5_BLOCK_GROUPS * S5_STATE
RMS_EPS = 1e-6
LANES = 128
VMEM_LIMIT = 56 * 1024 * 1024
ROW_TILE = 256
COMM_COLS = 1024
ADD_BLOCK_BYTES = 2 * 1024 * 1024
RS_STAGES = 4

ADAM_LR = 0.001
ADAM_B1 = 0.9
ADAM_B2 = 0.999
ADAM_EPS = 1e-08
ADAM_WD = 0.01
ADAM_STEP = 10

WEIGHTS = ['ln_g', 'w_in', 'conv_w', 'a_log', 'dt_bias', 'head_norm_g', 'lam_re', 'lam_im', 'log_dt',
           'b_re', 'b_im', 'c_re', 'c_im', 'd_skip', 'w_glu', 'b_glu', 'w_pa', 'w_pb', 'b_gate',
           'w_out', 'final_g']
SHARDED = ['w_in', 'w_glu', 'w_pa', 'w_pb', 'w_out', 'conv_w']
REPLICATED = [n for n in WEIGHTS if n not in SHARDED]
MESH = pl.DeviceIdType.MESH


def _params(sem=None):
    return pltpu.CompilerParams(dimension_semantics=sem, vmem_limit_bytes=VMEM_LIMIT)


def _tile(n, cap, q=LANES):
    t = (min(n, cap) // q) * q
    while t > q and n % t:
        t -= q
    return t if t > 0 and n % t == 0 else n


class Side:
    def __init__(self, ins, out_sd, sems, start, finish):
        self.ins, self.out_sd, self.sems, self.start, self.finish = list(ins), list(out_sd), list(sems), start, finish
        self.result = None


def _call(body, *, grid, in_specs, out_specs, out_shape, scratch_shapes=(), sem, name, args, side=None):
    in_specs, out_specs, out_shape = list(in_specs), list(out_specs), list(out_shape)
    scratch_shapes = list(scratch_shapes)
    if side is None:
        return pl.pallas_call(body, grid=grid, in_specs=in_specs, out_specs=out_specs, out_shape=out_shape,
                              scratch_shapes=scratch_shapes, compiler_params=_params(sem), name=name)(*args)
    hbm = pl.BlockSpec(memory_space=pl.ANY)
    n_in, n_out, n_scr = len(in_specs), len(out_specs), len(scratch_shapes)
    s_in, s_out = len(side.ins), len(side.out_sd)

    def hosted(*refs):
        main_in, rest = refs[:n_in], refs[n_in:]
        side_in, rest = rest[:s_in], rest[s_in:]
        main_out, rest = rest[:n_out], rest[n_out:]
        side_out, rest = rest[:s_out], rest[s_out:]
        main_scr, sems = rest[:n_scr], rest[n_scr:]
        ids = [pl.program_id(k) for k in range(len(grid))]
        first = functools.reduce(jnp.logical_and, [i == 0 for i in ids])
        last = functools.reduce(jnp.logical_and, [i == g - 1 for i, g in zip(ids, grid)])

        @pl.when(first)
        def _():
            side.start(side_in, side_out, sems)

        body(*main_in, *main_out, *main_scr)

        @pl.when(last)
        def _():
            side.finish(side_in, side_out, sems)

    outs = pl.pallas_call(
        hosted, grid=grid, in_specs=in_specs + [hbm] * s_in, out_specs=out_specs + [hbm] * s_out,
        out_shape=out_shape + side.out_sd, scratch_shapes=scratch_shapes + side.sems,
        compiler_params=_params(("arbitrary",) * len(grid)), name=name)(*args, *side.ins)
    side.result = list(outs[n_out:])
    return list(outs[:n_out])


def run_side(side, *, name):
    hbm = pl.BlockSpec(memory_space=pl.ANY)
    s_in, s_out = len(side.ins), len(side.out_sd)

    def body(*refs):
        side_in, side_out, sems = refs[:s_in], refs[s_in:s_in + s_out], refs[s_in + s_out:]
        side.start(side_in, side_out, sems)
        side.finish(side_in, side_out, sems)

    side.result = list(pl.pallas_call(body, out_shape=side.out_sd, in_specs=[hbm] * s_in, out_specs=[hbm] * s_out,
                                      scratch_shapes=side.sems, name=name)(*side.ins))
    return side.result


def slot_cxy(d):
    return 4 * (d % 2) + 2 * (d // 4) + (d // 2) % 2


def slot_cyx(d):
    return 4 * (d % 2) + 2 * ((d // 2) % 2) + d // 4


def mm(a, b, *, ta=False, tb=False, add=None, m_part=None, scatter=None, side=None, name):
    m = a.shape[1] if ta else a.shape[0]
    k = a.shape[0] if ta else a.shape[1]
    n = b.shape[0] if tb else b.shape[1]
    m_off = 0
    if m_part is not None:
        m = m // m_part[1]
        m_off = m_part[0]
    tm, tn, tk = _tile(m, 1024), _tile(n, 512), _tile(k, 2048)
    if scatter is not None and scatter[0] == "rows":
        tm = m // 8
    if scatter is not None and scatter[0] == "cols":
        tn = n // 8
    if m_part is not None:
        assert tm == m
    nk = k // tk
    dn = (((0 if ta else 1,), (1 if tb else 0,)), ((), ()))

    def body(*refs):
        if add is None:
            a_ref, b_ref, o_ref, acc = refs
        else:
            a_ref, b_ref, add_ref, o_ref, acc = refs
        kk = pl.program_id(2)

        @pl.when(kk == 0)
        def _():
            acc[...] = jnp.zeros_like(acc)

        acc[...] += lax.dot_general(a_ref[...].astype(MXU_DT), b_ref[...].astype(MXU_DT), dn,
                                    preferred_element_type=F32)

        @pl.when(kk == nk - 1)
        def _():
            r = acc[...]
            if add is not None:
                r = r + add_ref[...]
            o_ref[...] = r

    if ta:
        a_spec = pl.BlockSpec((tk, tm), lambda i, j, kk: (kk, i + m_off))
    else:
        a_spec = pl.BlockSpec((tm, tk), lambda i, j, kk: (i + m_off, kk))
    b_spec = pl.BlockSpec((tn, tk), lambda i, j, kk: (j, kk)) if tb else pl.BlockSpec((tk, tn), lambda i, j, kk: (kk, j))
    out_sd = jax.ShapeDtypeStruct((m, n), F32)
    if scatter is None:
        o_spec = pl.BlockSpec((tm, tn), lambda i, j, kk: (i, j))
    elif scatter[0] == "rows":
        o_spec = pl.BlockSpec((None, tm, tn), lambda i, j, kk: (scatter[1](i), 0, j))
        out_sd = jax.ShapeDtypeStruct((8, tm, n), F32)
    else:
        o_spec = pl.BlockSpec((None, tm, tn), lambda i, j, kk: (scatter[1](j), i, 0))
        out_sd = jax.ShapeDtypeStruct((8, m, tn), F32)
    ins, specs = [a, b], [a_spec, b_spec]
    if add is not None:
        ins.append(add)
        specs.append(o_spec)
    return _call(body, grid=(m // tm, n // tn, nk), in_specs=specs, out_specs=[o_spec], out_shape=[out_sd],
                 scratch_shapes=[pltpu.VMEM((tm, tn), F32)], sem=("parallel", "parallel", "arbitrary"),
                 name=name, args=ins, side=side)[0]


def rows(arr, t, width=None, base=0, per_j=False):
    width = arr.shape[1] if width is None else width
    return (arr, (t, width), lambda j, i: (i, base + (j if per_j else 0)))


def rows3(arr, lead, t, width, per_j=True):
    return (arr, (None, t, width), lambda j, i: (lead, i, j if per_j else 0))


def bcast(arr, width=None, base=0, per_j=False):
    width = arr.shape[1] if width is None else width
    return (arr, (arr.shape[0], width), lambda j, i: (0, base + (j if per_j else 0)))


def _specs(items):
    return [pl.BlockSpec(bs, im) for (_, bs, im) in items]


def rw_fwd(fn, ins, outs, *, nrow, t, ncol=1, name):
    out_specs = [pl.BlockSpec((t, w), (lambda j, i: (i, j)) if pj else (lambda j, i: (i, 0))) for (w, pj) in outs]
    out_shape = [jax.ShapeDtypeStruct((nrow, w * (ncol if pj else 1)), F32) for (w, pj) in outs]
    nin = len(ins)

    def body(*refs):
        j = pl.program_id(0)
        res = fn(j, *[r[...] for r in refs[:nin]])
        for o_ref, r in zip(refs[nin:], res):
            o_ref[...] = r

    return pl.pallas_call(
        body, grid=(ncol, nrow // t), in_specs=_specs(ins), out_specs=out_specs, out_shape=out_shape,
        compiler_params=_params(("parallel", "parallel")), name=name)(*[x[0] for x in ins])


def rw_bwd(fn, ins, cots, *, nrow, t, ncol=1, row_grads, bc_grads, residual=None, side=None, name):
    nin = len(ins)
    flat_cots = [c for group in cots for c in group]
    extra = [residual] if residual is not None else []
    out_specs, out_shape = [], []
    for idx, pj in row_grads:
        w = ins[idx][1][-1]
        out_specs.append(pl.BlockSpec((t, w), (lambda j, i: (i, j)) if pj else (lambda j, i: (i, 0))))
        out_shape.append(jax.ShapeDtypeStruct((nrow, w * (ncol if pj else 1)), F32))
    for idx in bc_grads:
        r, w = ins[idx][1]
        out_specs.append(pl.BlockSpec((None, r, w), lambda j, i: (j, 0, 0)))
        out_shape.append(jax.ShapeDtypeStruct((ncol, r, w), F32))

    def body(*refs):
        j = pl.program_id(0)
        i = pl.program_id(1)
        vals = [r[...] for r in refs[:nin]]
        pos = nin
        cts = []
        for group in cots:
            c = refs[pos][...]
            for q in range(1, len(group)):
                c = c + refs[pos + q][...]
            pos += len(group)
            cts.append(c)
        res_ref = refs[pos] if residual is not None else None
        pos += len(extra)
        outs = refs[pos:]
        _, vjp = jax.vjp(lambda *a: tuple(fn(j, *a)), *vals)
        grads = vjp(tuple(cts))
        for q, (idx, _) in enumerate(row_grads):
            g = grads[idx]
            if q == 0 and res_ref is not None:
                g = g + res_ref[...]
            outs[q][...] = g
        for q, idx in enumerate(bc_grads):
            o_ref = outs[len(row_grads) + q]

            @pl.when(i == 0)
            def _(o_ref=o_ref):
                o_ref[...] = jnp.zeros_like(o_ref)

            o_ref[...] += grads[idx]

    all_in = list(ins) + flat_cots + extra
    return _call(body, grid=(ncol, nrow // t), in_specs=_specs(all_in), out_specs=out_specs, out_shape=out_shape,
                 sem=("parallel", "arbitrary"), name=name, args=[x[0] for x in all_in], side=side)


def _silu(x):
    return x * jax.nn.sigmoid(x)


@jax.custom_vjp
def _softplus(x):
    return jnp.maximum(x, 0.0) + jnp.log1p(jnp.exp(-jnp.abs(x)))


def _softplus_fwd(x):
    return _softplus(x), x


def _softplus_bwd(x, ct):
    return (ct * jax.nn.sigmoid(x),)


_softplus.defvjp(_softplus_fwd, _softplus_bwd)


def _gelu(x):
    return 0.5 * x * (1.0 + jnp.tanh(math.sqrt(2.0 / math.pi) * (x + 0.044715 * (x * x * x))))


def _row_shift_impl(x, s):
    n = x.shape[0]
    if s == 0:
        return x
    rolled = pltpu.roll(x, (-s) % n, 0)
    t = lax.broadcasted_iota(jnp.int32, x.shape, 0)
    ok = (t + s >= 0) & (t + s < n)
    return jnp.where(ok, rolled, 0.0)


@functools.partial(jax.custom_vjp, nondiff_argnums=(1,))
def _row_shift(x, s):
    return _row_shift_impl(x, s)


def _row_shift_fwd(x, s):
    return _row_shift_impl(x, s), None


def _row_shift_bwd(s, _, ct):
    return (_row_shift_impl(ct, -s),)


_row_shift.defvjp(_row_shift_fwd, _row_shift_bwd)


def fn_rms(j, x, g):
    return (x * lax.rsqrt(jnp.mean(x * x, axis=-1, keepdims=True) + RMS_EPS) * g,)


def make_fn_prep(n_heads):
    pad = (CONV_K - 1) // 2

    def fn_prep(j, x, w):
        y = _row_shift(x, -pad) * w[0:1, :]
        for i in range(1, CONV_K):
            y = y + _row_shift(x, i - pad) * w[i:i + 1, :]
        a = _silu(y)
        scale = jnp.where(j < n_heads, HEAD_DIM ** -0.5, 1.0).astype(F32)
        nrm = a * lax.rsqrt(jnp.sum(a * a, axis=-1, keepdims=True) + RMS_EPS) * scale
        return (jnp.where(j < 2 * n_heads, nrm, a),)

    return fn_prep


def fn_headnorm(j, o0, o1, z, g):
    o = o0 + o1
    n = o * lax.rsqrt(jnp.mean(o * o, axis=-1, keepdims=True) + RMS_EPS) * g
    return (n * _silu(z),)


def fn_gelu(j, y0, y1, u, dsk):
    return (_gelu(y0 + y1 + u * dsk),)


def fn_glu(j, ys, logit, z, b):
    return (ys * jax.nn.sigmoid(logit + b) * _silu(z),)


def fn_merge(j, la, lb, ya, yb, ba, bb):
    return (jax.nn.sigmoid(la + ba) * ya + jax.nn.sigmoid(lb + bb) * yb,)


def fn_loss(j, x, t, g):
    y = x * lax.rsqrt(jnp.mean(x * x, axis=-1, keepdims=True) + RMS_EPS) * g
    e = y - t
    return (0.5 * jnp.mean(e * e, axis=-1, keepdims=True),)


def make_fn_sum(n):
    def fn_sum(j, *xs):
        s = xs[0]
        for q in range(1, n):
            s = s + xs[q]
        return (s,)

    return fn_sum


def fn_adamw(j, w, g, m, v):
    m2 = ADAM_B1 * m + (1.0 - ADAM_B1) * g
    v2 = ADAM_B2 * v + (1.0 - ADAM_B2) * (g * g)
    m_hat = m2 / (1.0 - ADAM_B1 ** ADAM_STEP)
    v_hat = v2 / (1.0 - ADAM_B2 ** ADAM_STEP)
    delta = -ADAM_LR * (m_hat / (jnp.sqrt(v_hat) + ADAM_EPS) + ADAM_WD * w)
    return delta, m2, v2


def _dot(a, b, dims, precision=None):
    return lax.dot_general(a, b, (dims, ((), ())), precision=precision, preferred_element_type=F32)


_NN = ((1,), (0,))
_NT = ((1,), (1,))
_TN = ((0,), (0,))
GDN_HEAD_BLOCK = 4


def _split(a):
    hi = a.astype(MXU_DT)
    return hi, (a - hi.astype(F32)).astype(MXU_DT)


def _dot3s(a, b, dims):
    (ah, al), (bh, bl) = a, b
    return _dot(ah, bh, dims) + (_dot(al, bh, dims) + _dot(ah, bl, dims))


def _dot3(a, b, dims):
    return _dot3s(_split(a), _split(b), dims)


def _unit_inverse(lmat):
    r = lmat.shape[0]
    eye = (lax.broadcasted_iota(jnp.int32, (r, r), 0) == lax.broadcasted_iota(jnp.int32, (r, r), 1)).astype(F32)
    pw = -lmat
    tinv = eye + pw
    pws = _split(pw)
    for _ in range(int(math.ceil(math.log2(CHUNK))) - 1):
        pws = _split(_dot3s(pws, pws, _NN))
        tinv = tinv + _dot3s(_split(tinv), pws, _NN)
    return tinv


@jax.custom_vjp
def tri_solve(lmat, rhs):
    return _dot3(_unit_inverse(lmat), rhs, _NN)


def _tri_solve_fwd(lmat, rhs):
    tinv = _unit_inverse(lmat)
    x = _dot3(tinv, rhs, _NN)
    return x, (tinv, x)


def _tri_solve_bwd(res, dx):
    tinv, x = res
    drhs = _dot3(tinv, dx, _TN)
    return -_dot3(drhs, x, _NT), drhs


tri_solve.defvjp(_tri_solve_fwd, _tri_solve_bwd)


def gdn_group(s, q, k, v, ba, alog, dtb, *, d, head0, n_heads):
    hb = len(s)
    c = q.shape[0]
    r = hb * c

    def stack(x):
        return jnp.concatenate([x[:, i * HEAD_DIM:(i + 1) * HEAD_DIM] for i in range(hb)], axis=0)

    def pick(x, lane0):
        lane = lax.broadcasted_iota(jnp.int32, x.shape, 1)
        return jnp.concatenate(
            [jnp.sum(jnp.where(lane == lane0 + i, x, 0.0), axis=1, keepdims=True) for i in range(hb)], axis=0)

    q4, k4, v4 = stack(q), stack(k), stack(v)
    beta = pick(jax.nn.sigmoid(ba), d * n_heads + head0)
    g = pick(-jnp.exp(alog) * _softplus(ba + dtb), 2 * n_heads + d * n_heads + head0)
    ii = lax.broadcasted_iota(jnp.int32, (r, r), 0)
    jj = lax.broadcasted_iota(jnp.int32, (r, r), 1)
    same = (ii // c) == (jj // c)
    rel = (ii - jj) * (1 - 2 * d)
    incl = same & (rel >= 0)
    strict = same & (rel > 0)
    incl_t = same & (rel <= 0)
    g_row = jnp.sum(jnp.where(ii == jj, g, 0.0), axis=0, keepdims=True)
    gc_col = jnp.sum(jnp.where(incl, g_row, 0.0), axis=1, keepdims=True)
    gc_row = jnp.sum(jnp.where(incl_t, g, 0.0), axis=0, keepdims=True)
    g_tot = jnp.sum(jnp.where(same, g_row, 0.0), axis=1, keepdims=True)
    decay = jnp.exp(jnp.where(incl, gc_col - gc_row, -1e30))
    kb = k4 * beta
    vb = v4 * beta
    lmat = jnp.where(strict, _dot(kb, k4, _NT) * decay, 0.0)
    uw = tri_solve(lmat, jnp.concatenate([vb, kb * jnp.exp(gc_col)], axis=1))
    u, w = uw[:, :HEAD_DIM], uw[:, HEAD_DIM:]
    qk = _dot(q4, k4, _NT) * decay
    qe = q4 * jnp.exp(gc_col)
    kd = k4 * jnp.exp(g_tot - gc_col)
    eg = jnp.exp(g_tot)
    v_new, o_s = [], []
    for i in range(hb):
        rs = slice(i * c, (i + 1) * c)
        ws = _dot(jnp.concatenate([w[rs], qe[rs]], axis=0), s[i], _NN)
        v_new.append(u[rs] - ws[:c])
        o_s.append(ws[c:])
    o4 = jnp.concatenate(o_s, axis=0) + _dot(qk, jnp.concatenate(v_new, axis=0), _NN)
    s_new = tuple(s[i] * eg[i * c:i * c + 1, :] + _dot(kd[i * c:(i + 1) * c], v_new[i], _TN) for i in range(hb))
    o = jnp.concatenate([o4[i * c:(i + 1) * c] for i in range(hb)], axis=1)
    return s_new, o


def _gdn_maps(n_chunks):
    def chunk_of(d, step):
        return step + d * (n_chunks - 1 - 2 * step)
    return chunk_of


def _heads(ref, h0, n):
    return ref[:, h0 * HEAD_DIM:(h0 + n) * HEAD_DIM]


def gdn_fwd(qkv, proj, ba_blk, alog_row, dtb_row, *, n_heads, side=None, name):
    nrow = qkv.shape[0]
    nc = nrow // CHUNK
    h = n_heads
    wa = h * HEAD_DIM
    hb = math.gcd(h, GDN_HEAD_BLOCK)
    chunk_of = _gdn_maps(nc)

    def body(q_ref, k_ref, v_ref, ba_ref, al_ref, dt_ref, o_ref, sh_ref, s_scr):
        d = pl.program_id(0)
        n = pl.program_id(1)

        @pl.when(n == 0)
        def _():
            s_scr[...] = jnp.zeros_like(s_scr)

        ba, al, dt = ba_ref[...], al_ref[...], dt_ref[...]
        for h0 in range(0, h, hb):
            s = tuple(s_scr[h0 + i] for i in range(hb))
            for i in range(hb):
                sh_ref[h0 + i] = s[i]
            s2, o = gdn_group(s, _heads(q_ref, h0, hb), _heads(k_ref, h0, hb), _heads(v_ref, h0, hb), ba, al, dt,
                              d=d, head0=h0, n_heads=h)
            o_ref[:, h0 * HEAD_DIM:(h0 + hb) * HEAD_DIM] = o
            for i in range(hb):
                s_scr[h0 + i] = s2[i]

    blk = (CHUNK, wa)
    in_specs = [
        pl.BlockSpec(blk, lambda d, n: (chunk_of(d, n), 0)),
        pl.BlockSpec(blk, lambda d, n: (chunk_of(d, n), 1)),
        pl.BlockSpec(blk, lambda d, n: (chunk_of(d, n), 2)),
        pl.BlockSpec((CHUNK, LANES), lambda d, n: (chunk_of(d, n), ba_blk)),
        pl.BlockSpec((1, LANES), lambda d, n: (0, 0)),
        pl.BlockSpec((1, LANES), lambda d, n: (0, 0)),
    ]
    out_specs = [
        pl.BlockSpec((None, CHUNK, wa), lambda d, n: (d, chunk_of(d, n), 0)),
        pl.BlockSpec((h, None, HEAD_DIM, HEAD_DIM), lambda d, n: (d, n, 0, 0)),
    ]
    out_shape = [jax.ShapeDtypeStruct((2, nrow, wa), F32),
                 jax.ShapeDtypeStruct((2 * h, nc, HEAD_DIM, HEAD_DIM), F32)]
    return _call(body, grid=(2, nc), in_specs=in_specs, out_specs=out_specs, out_shape=out_shape,
                 scratch_shapes=[pltpu.VMEM((h, HEAD_DIM, HEAD_DIM), F32)], sem=("parallel", "arbitrary"),
                 name=name, args=(qkv, qkv, qkv, proj, alog_row, dtb_row), side=side)


def gdn_bwd(qkv, proj, ba_blk, alog_row, dtb_row, s_hist, do, *, n_heads, side=None, name):
    nrow = qkv.shape[0]
    nc = nrow // CHUNK
    h = n_heads
    wa = h * HEAD_DIM
    hb = math.gcd(h, GDN_HEAD_BLOCK)
    chunk_of = _gdn_maps(nc)

    def cb(d, n):
        return chunk_of(d, nc - 1 - n)

    def body(q_ref, k_ref, v_ref, ba_ref, al_ref, dt_ref, sh_ref, do_ref,
             dqkv_ref, dba_ref, dal_ref, ddt_ref, ds_scr):
        d = pl.program_id(0)
        n = pl.program_id(1)

        @pl.when(n == 0)
        def _():
            ds_scr[...] = jnp.zeros_like(ds_scr)
            dal_ref[...] = jnp.zeros_like(dal_ref)
            ddt_ref[...] = jnp.zeros_like(ddt_ref)

        ba, al, dt = ba_ref[...], al_ref[...], dt_ref[...]
        dba_sum = jnp.zeros_like(ba)
        dal_sum = jnp.zeros_like(al)
        ddt_sum = jnp.zeros_like(dt)
        for h0 in range(0, h, hb):
            f = functools.partial(gdn_group, d=d, head0=h0, n_heads=h)
            s = tuple(sh_ref[h0 + i] for i in range(hb))
            _, vjp = jax.vjp(f, s, _heads(q_ref, h0, hb), _heads(k_ref, h0, hb), _heads(v_ref, h0, hb), ba, al, dt)
            ds, dq, dk, dv, dba, dal, ddt = vjp((tuple(ds_scr[h0 + i] for i in range(hb)), _heads(do_ref, h0, hb)))
            for i in range(hb):
                ds_scr[h0 + i] = ds[i]
            cols = slice(h0 * HEAD_DIM, (h0 + hb) * HEAD_DIM)
            dqkv_ref[0, :, cols] = dq
            dqkv_ref[1, :, cols] = dk
            dqkv_ref[2, :, cols] = dv
            dba_sum = dba_sum + dba
            dal_sum = dal_sum + dal
            ddt_sum = ddt_sum + ddt
        dba_ref[...] = dba_sum
        dal_ref[...] += dal_sum
        ddt_ref[...] += ddt_sum

    blk = (CHUNK, wa)
    in_specs = [
        pl.BlockSpec(blk, lambda d, n: (cb(d, n), 0)),
        pl.BlockSpec(blk, lambda d, n: (cb(d, n), 1)),
        pl.BlockSpec(blk, lambda d, n: (cb(d, n), 2)),
        pl.BlockSpec((CHUNK, LANES), lambda d, n: (cb(d, n), ba_blk)),
        pl.BlockSpec((1, LANES), lambda d, n: (0, 0)),
        pl.BlockSpec((1, LANES), lambda d, n: (0, 0)),
        pl.BlockSpec((h, None, HEAD_DIM, HEAD_DIM), lambda d, n: (d, nc - 1 - n, 0, 0)),
        pl.BlockSpec(blk, lambda d, n: (cb(d, n), 0)),
    ]
    out_specs = [
        pl.BlockSpec((None, 3, CHUNK, wa), lambda d, n: (d, 0, cb(d, n), 0)),
        pl.BlockSpec((None, CHUNK, LANES), lambda d, n: (d, cb(d, n), 0)),
        pl.BlockSpec((None, 1, LANES), lambda d, n: (d, 0, 0)),
        pl.BlockSpec((None, 1, LANES), lambda d, n: (d, 0, 0)),
    ]
    out_shape = [jax.ShapeDtypeStruct((2, 3, nrow, wa), F32),
                 jax.ShapeDtypeStruct((2, nrow, LANES), F32),
                 jax.ShapeDtypeStruct((2, 1, LANES), F32),
                 jax.ShapeDtypeStruct((2, 1, LANES), F32)]
    return _call(body, grid=(2, nc), in_specs=in_specs, out_specs=out_specs, out_shape=out_shape,
                 scratch_shapes=[pltpu.VMEM((h, HEAD_DIM, HEAD_DIM), F32)], sem=("parallel", "arbitrary"),
                 name=name, args=(qkv, qkv, qkv, proj, alog_row, dtb_row, s_hist, do), side=side)


def s5_mats(lam_re, lam_im, log_dt, b_re, b_im, c_re, c_im):
    g = lam_re.shape[1]
    nb = g // S5_BLOCK_GROUPS
    dt = jnp.exp(log_dt)[..., None]
    mag = jnp.exp(lam_re * dt)
    ar = mag * jnp.cos(lam_im * dt)
    ai = mag * jnp.sin(lam_im * dt)
    den = lam_re * lam_re + lam_im * lam_im
    fr = ((ar - 1.0) * lam_re + ai * lam_im) / den
    fi = (ai * lam_re - (ar - 1.0) * lam_im) / den
    bbr = fr[..., None] * b_re - fi[..., None] * b_im
    bbi = fr[..., None] * b_im + fi[..., None] * b_re
    eye = jnp.eye(S5_BLOCK_GROUPS, dtype=F32)
    shp = (2, nb, S5_BLOCK_GROUPS, S5_STATE, S5_GROUP_CH)
    w_r = jnp.einsum('dsjpc,jk->dsjckp', bbr.reshape(shp), eye).reshape(2, nb, LANES, S5_BLOCK_STATE)
    w_i = jnp.einsum('dsjpc,jk->dsjckp', bbi.reshape(shp), eye).reshape(2, nb, LANES, S5_BLOCK_STATE)
    w = jnp.concatenate([w_r, w_i], axis=-1)
    shc = (2, nb, S5_BLOCK_GROUPS, S5_GROUP_CH, S5_STATE)
    c_r = jnp.einsum('dsjcp,jk->dsjpkc', c_re.reshape(shc), eye).reshape(2, nb, S5_BLOCK_STATE, LANES)
    c_i = jnp.einsum('dsjcp,jk->dsjpkc', c_im.reshape(shc), eye).reshape(2, nb, S5_BLOCK_STATE, LANES)
    cm = jnp.concatenate([c_r, -c_i], axis=-2)
    return (ar.reshape(2, nb, 1, S5_BLOCK_STATE), ai.reshape(2, nb, 1, S5_BLOCK_STATE), w, cm)


_S5_ROWS = 512
S5_SEGMENTS = 8


def segment_major(a):
    n, c = a.shape
    return jnp.transpose(a.reshape(S5_SEGMENTS, n // S5_SEGMENTS, c), (1, 0, 2)).reshape(n, c)


def time_major(a):
    n, c = a.shape
    return jnp.transpose(a.reshape(n // S5_SEGMENTS, S5_SEGMENTS, c), (1, 0, 2)).reshape(n, c)


def _segmented_scan(x_ref, a_r, a_i, *, reverse, visit=None):
    nrow, two_hs = x_ref.shape
    hs = two_hs // 2
    steps = nrow // S5_SEGMENTS
    assert steps & (steps - 1) == 0
    b_r = jnp.broadcast_to(a_r, (S5_SEGMENTS, hs))
    b_i = jnp.broadcast_to(a_i, (S5_SEGMENTS, hs))
    rid = lax.broadcasted_iota(jnp.int32, (S5_SEGMENTS, hs), 0)

    def rows_of(i):
        return pl.ds(pl.multiple_of((steps - 1 - i if reverse else i) * S5_SEGMENTS, S5_SEGMENTS), S5_SEGMENTS)

    def local(i, carry):
        s_r, s_i = carry
        rows = rows_of(i)
        n_r = b_r * s_r - b_i * s_i + x_ref[rows, pl.ds(0, hs)]
        n_i = b_r * s_i + b_i * s_r + x_ref[rows, pl.ds(hs, hs)]
        x_ref[rows, pl.ds(0, hs)] = n_r
        x_ref[rows, pl.ds(hs, hs)] = n_i
        return n_r, n_i

    z = jnp.zeros((S5_SEGMENTS, hs), F32)
    e_r, e_i = lax.fori_loop(0, steps, local, (z, z))
    q_r, q_i = a_r, a_i
    for _ in range(steps.bit_length() - 1):
        q_r, q_i = q_r * q_r - q_i * q_i, 2.0 * q_r * q_i
    c_r = jnp.zeros((1, hs), F32)
    c_i = jnp.zeros((1, hs), F32)
    en_r, en_i = z, z
    for s in (reversed(range(S5_SEGMENTS)) if reverse else range(S5_SEGMENTS)):
        en_r = jnp.where(rid == s, c_r, en_r)
        en_i = jnp.where(rid == s, c_i, en_i)
        c_r, c_i = (e_r[s:s + 1] + q_r * c_r - q_i * c_i, e_i[s:s + 1] + q_r * c_i + q_i * c_r)

    def fix(i, carry):
        p_r, p_i = carry
        rows = rows_of(i)
        x_r = x_ref[rows, pl.ds(0, hs)] + (p_r * en_r - p_i * en_i)
        x_i = x_ref[rows, pl.ds(hs, hs)] + (p_r * en_i + p_i * en_r)
        x_ref[rows, pl.ds(0, hs)] = x_r
        x_ref[rows, pl.ds(hs, hs)] = x_i
        if visit is not None:
            visit(steps - 1 - i if reverse else i, rows, x_r, x_i)
        return p_r * b_r - p_i * b_i, p_r * b_i + p_i * b_r

    return lax.fori_loop(0, steps, fix, (b_r, b_i))


def s5_fwd(u_seg, ar, ai, w, cm, *, name):
    nrow = u_seg.shape[0]
    nb = w.shape[1]
    hs = S5_BLOCK_STATE
    rs = min(_S5_ROWS, nrow)

    def body(u_ref, ar_ref, ai_ref, w_ref, cm_ref, y_ref, x_ref):
        d = pl.program_id(0)
        wb = w_ref[...].astype(MXU_DT)
        for r0 in range(0, nrow, rs):
            x_ref[pl.ds(r0, rs), :] = _dot(u_ref[pl.ds(r0, rs), :].astype(MXU_DT), wb, _NN)
        @pl.when(d == 0)
        def _():
            _segmented_scan(x_ref, ar_ref[...], ai_ref[...], reverse=False)

        @pl.when(d == 1)
        def _():
            _segmented_scan(x_ref, ar_ref[...], ai_ref[...], reverse=True)

        cb = cm_ref[...].astype(MXU_DT)
        for r0 in range(0, nrow, rs):
            y_ref[pl.ds(r0, rs), :] = _dot(x_ref[pl.ds(r0, rs), :].astype(MXU_DT), cb, _NN)

    in_specs = [
        pl.BlockSpec((nrow, LANES), lambda d, s: (0, s)),
        pl.BlockSpec((None, None, 1, hs), lambda d, s: (d, s, 0, 0)),
        pl.BlockSpec((None, None, 1, hs), lambda d, s: (d, s, 0, 0)),
        pl.BlockSpec((None, None, LANES, 2 * hs), lambda d, s: (d, s, 0, 0)),
        pl.BlockSpec((None, None, 2 * hs, LANES), lambda d, s: (d, s, 0, 0)),
    ]
    out_specs = [
        pl.BlockSpec((None, nrow, LANES), lambda d, s: (d, 0, s)),
        pl.BlockSpec((None, nrow, 2 * hs), lambda d, s: (d, 0, s)),
    ]
    out_shape = [jax.ShapeDtypeStruct((2, nrow, nb * LANES), F32),
                 jax.ShapeDtypeStruct((2, nrow, nb * 2 * hs), F32)]
    return pl.pallas_call(
        body, grid=(2, nb), in_specs=in_specs, out_specs=out_specs, out_shape=out_shape,
        compiler_params=_params(("parallel", "parallel")), name=name)(u_seg, ar, ai, w, cm)


def s5_bwd(u_seg, ar, ai, w, cm, xs, dy, du_in, *, side=None, name):
    nrow = u_seg.shape[0]
    nb = w.shape[1]
    steps = nrow // S5_SEGMENTS
    hs = S5_BLOCK_STATE
    rs = min(_S5_ROWS, nrow)

    def body(u_ref, dy_ref, dui_ref, x_ref, ar_ref, ai_ref, w_ref, cm_ref,
             du_ref, dw_ref, dcm_ref, dar_ref, dai_ref, g_ref, acc_ref):
        d = pl.program_id(1)
        cb = cm_ref[...].astype(MXU_DT)
        for r0 in range(0, nrow, rs):
            g_ref[pl.ds(r0, rs), :] = _dot(dy_ref[pl.ds(r0, rs), :].astype(MXU_DT), cb, _NT)
        acc_ref[...] = jnp.zeros_like(acc_ref)
        rid = lax.broadcasted_iota(jnp.int32, (S5_SEGMENTS, 2 * hs), 0)

        def run(forward_recurrence):
            last = pl.ds((steps - 1) * S5_SEGMENTS, S5_SEGMENTS)
            first = pl.ds(0, S5_SEGMENTS)
            if forward_recurrence:
                wrap = jnp.where(rid == 0, 0.0, pltpu.roll(x_ref[last, :], 1, 0))
            else:
                wrap = jnp.where(rid == S5_SEGMENTS - 1, 0.0, pltpu.roll(x_ref[first, :], S5_SEGMENTS - 1, 0))

            def visit(step, rows, g_r, g_i):
                if forward_recurrence:
                    nbr = jnp.maximum(step - 1, 0)
                    edge = step == 0
                else:
                    nbr = jnp.minimum(step + 1, steps - 1)
                    edge = step == steps - 1
                prev = x_ref[pl.ds(pl.multiple_of(nbr * S5_SEGMENTS, S5_SEGMENTS), S5_SEGMENTS), :]
                prev = jnp.where(edge, wrap, prev)
                p_r, p_i = prev[:, :hs], prev[:, hs:]
                acc_ref[:, pl.ds(0, hs)] += g_r * p_r + g_i * p_i
                acc_ref[:, pl.ds(hs, hs)] += g_i * p_r - g_r * p_i

            _segmented_scan(g_ref, ar_ref[...], -ai_ref[...], reverse=forward_recurrence, visit=visit)

        @pl.when(d == 0)
        def _():
            run(True)

        @pl.when(d == 1)
        def _():
            run(False)

        dar_ref[...] = jnp.sum(acc_ref[:, pl.ds(0, hs)], axis=0, keepdims=True)
        dai_ref[...] = jnp.sum(acc_ref[:, pl.ds(hs, hs)], axis=0, keepdims=True)

        wb = w_ref[...].astype(MXU_DT)
        for r0 in range(0, nrow, rs):
            part = _dot(g_ref[pl.ds(r0, rs), :].astype(MXU_DT), wb, _NT)

            @pl.when(d == 0)
            def _(part=part, r0=r0):
                du_ref[pl.ds(r0, rs), :] = dui_ref[pl.ds(r0, rs), :] + part

            @pl.when(d == 1)
            def _(part=part, r0=r0):
                du_ref[pl.ds(r0, rs), :] += part

        dw_ref[...] = _dot(u_ref[...].astype(MXU_DT), g_ref[...].astype(MXU_DT), _TN)
        dcm_ref[...] = _dot(x_ref[...].astype(MXU_DT), dy_ref[...].astype(MXU_DT), _TN)

    in_specs = [
        pl.BlockSpec((nrow, LANES), lambda s, d: (0, s)),
        pl.BlockSpec((nrow, LANES), lambda s, d: (0, s)),
        pl.BlockSpec((nrow, LANES), lambda s, d: (0, s)),
        pl.BlockSpec((None, nrow, 2 * hs), lambda s, d: (d, 0, s)),
        pl.BlockSpec((None, None, 1, hs), lambda s, d: (d, s, 0, 0)),
        pl.BlockSpec((None, None, 1, hs), lambda s, d: (d, s, 0, 0)),
        pl.BlockSpec((None, None, LANES, 2 * hs), lambda s, d: (d, s, 0, 0)),
        pl.BlockSpec((None, None, 2 * hs, LANES), lambda s, d: (d, s, 0, 0)),
    ]
    out_specs = [
        pl.BlockSpec((nrow, LANES), lambda s, d: (0, s)),
        pl.BlockSpec((None, None, LANES, 2 * hs), lambda s, d: (d, s, 0, 0)),
        pl.BlockSpec((None, None, 2 * hs, LANES), lambda s, d: (d, s, 0, 0)),
        pl.BlockSpec((None, None, 1, hs), lambda s, d: (d, s, 0, 0)),
        pl.BlockSpec((None, None, 1, hs), lambda s, d: (d, s, 0, 0)),
    ]
    out_shape = [jax.ShapeDtypeStruct((nrow, nb * LANES), F32),
                 jax.ShapeDtypeStruct(w.shape, F32), jax.ShapeDtypeStruct(cm.shape, F32),
                 jax.ShapeDtypeStruct(ar.shape, F32), jax.ShapeDtypeStruct(ai.shape, F32)]
    return _call(body, grid=(nb, 2), in_specs=in_specs, out_specs=out_specs, out_shape=out_shape,
                 scratch_shapes=[pltpu.VMEM((nrow, 2 * hs), F32), pltpu.VMEM((S5_SEGMENTS, 2 * hs), F32)],
                 sem=("parallel", "arbitrary"), name=name, args=(u_seg, dy, du_in, xs, ar, ai, w, cm), side=side)


def _me():
    return lax.axis_index("x"), lax.axis_index("y"), lax.axis_index("c")


def all_gather(shards):
    na = len(shards)

    def plan(x_refs, out_refs, sems):
        send_sems, recv_sems, local_sems = sems
        x, y, c = _me()
        me, sibling = (x, y, c), (x, y, 1 - c)
        chips = [(1 - x, y), (x, 1 - y), (1 - x, 1 - y)]

        def slot(a, px, py, pc):
            return out_refs[a].at[4 * px + 2 * py + pc]

        def copy(a, k, block, to, src=None):
            return pltpu.make_async_remote_copy(
                src_ref=slot(a, *block) if src is None else src, dst_ref=slot(a, *block),
                send_sem=send_sems.at[a, k], recv_sem=recv_sems.at[a, k], device_id=to, device_id_type=MESH)

        mine = [pltpu.make_async_copy(x_refs[a], slot(a, *me), local_sems.at[a]) for a in range(na)]
        first = []
        for a in range(na):
            first.append(copy(a, 0, me, sibling, src=x_refs[a]))
            first += [copy(a, 1 + j, me, (*chip, c), src=x_refs[a]) for j, chip in enumerate(chips)]
        return me, sibling, chips, c, copy, mine, first

    def start(x_refs, out_refs, sems):
        _, _, _, _, _, mine, first = plan(x_refs, out_refs, sems)
        for cp in mine + first:
            cp.start()

    def finish(x_refs, out_refs, sems):
        me, sibling, chips, c, copy, mine, first = plan(x_refs, out_refs, sems)
        passed = []
        for j, chip in enumerate(chips):
            for a in range(na):
                copy(a, 1 + j, (*chip, c), me).wait_recv()
                fwd = copy(a, 4 + j, (*chip, c), sibling)
                fwd.start()
                passed.append(fwd)
        for a in range(na):
            copy(a, 0, sibling, me).wait_recv()
            for j, chip in enumerate(chips):
                copy(a, 4 + j, (*chip, 1 - c), me).wait_recv()
        for cp in first + passed:
            cp.wait_send()
        for cp in mine:
            cp.wait()

    return Side(shards, [jax.ShapeDtypeStruct((8,) + s.shape, s.dtype) for s in shards],
                [pltpu.SemaphoreType.DMA((na, 7)), pltpu.SemaphoreType.DMA((na, 7)), pltpu.SemaphoreType.DMA((na,))],
                start, finish)


_AXES = ("x", "y", "c")


def exchange(bufs, axes, *, half):
    na = len(bufs)
    if isinstance(half, bool):
        half = [half] * na

    def copies(in_refs, out_refs, sems):
        send_sems, recv_sems = sems
        me = _me()
        cps = []
        for a in range(na):
            bit = me[_AXES.index(axes[a])]
            peer = tuple(1 - v if ax == axes[a] else v for ax, v in zip(_AXES, me))
            cps.append(pltpu.make_async_remote_copy(
                src_ref=in_refs[a].at[1 - bit] if half[a] else in_refs[a], dst_ref=out_refs[a],
                send_sem=send_sems.at[a], recv_sem=recv_sems.at[a], device_id=peer, device_id_type=MESH))
        return cps

    def start(*refs):
        for cp in copies(*refs):
            cp.start()

    def finish(*refs):
        for cp in copies(*refs):
            cp.wait()

    return Side(bufs, [jax.ShapeDtypeStruct(b.shape[1:] if h else b.shape, b.dtype) for b, h in zip(bufs, half)],
                [pltpu.SemaphoreType.DMA((na,)), pltpu.SemaphoreType.DMA((na,))], start, finish)


def all_to_all(pieces, slot_fns):
    na = len(pieces)

    def copies(in_refs, out_refs, sems):
        send_sems, recv_sems, local_sems = sems
        x, y, c = _me()
        cps = []
        for a in range(na):
            mine = slot_fns[a](x, y, c)
            for k in range(8):
                tx, ty, tc = x ^ (k // 4), y ^ ((k // 2) % 2), c ^ (k % 2)
                src = in_refs[a].at[4 * tx + 2 * ty + tc]
                dst = out_refs[a].at[mine]
                if k == 0:
                    cps.append(pltpu.make_async_copy(src, dst, local_sems.at[a]))
                else:
                    cps.append(pltpu.make_async_remote_copy(
                        src_ref=src, dst_ref=dst, send_sem=send_sems.at[a, k - 1], recv_sem=recv_sems.at[a, k - 1],
                        device_id=(tx, ty, tc), device_id_type=MESH))
        return cps

    def start(*refs):
        for cp in copies(*refs):
            cp.start()

    def finish(*refs):
        for cp in copies(*refs):
            cp.wait()

    return Side(pieces, [jax.ShapeDtypeStruct(p.shape, p.dtype) for p in pieces],
                [pltpu.SemaphoreType.DMA((na, 7)), pltpu.SemaphoreType.DMA((na, 7)), pltpu.SemaphoreType.DMA((na,))],
                start, finish)


def add_half(buf, recv, bit, *, narrow, name):
    c = recv.shape[-1]
    r = math.prod(recv.shape[:-1])
    t = _tile(r, max(16, (ADD_BLOCK_BYTES // (4 * c)) // 16 * 16), 16)

    def body(bit_ref, a_ref, b_ref, o_ref, *tx_ref):
        s = a_ref[...] + b_ref[...].astype(F32)
        o_ref[...] = s
        if narrow:
            tx_ref[0][...] = s.astype(WIRE_DT)

    o_spec = pl.BlockSpec((t, c), lambda i, b: (i, 0))
    grid_spec = pltpu.PrefetchScalarGridSpec(
        num_scalar_prefetch=1, grid=(r // t,),
        in_specs=[pl.BlockSpec((None, t, c), lambda i, b: (b[0], i, 0)), o_spec],
        out_specs=[o_spec, o_spec] if narrow else [o_spec])
    out_shape = [jax.ShapeDtypeStruct((r, c), F32)] + ([jax.ShapeDtypeStruct((r, c), WIRE_DT)] if narrow else [])
    outs = pl.pallas_call(body, grid_spec=grid_spec, out_shape=out_shape,
                          compiler_params=_params(("parallel",)), name=name)(
                              bit, buf.reshape(2, r, c), recv.reshape(r, c))
    return [o.reshape(recv.shape) for o in outs]


def _pack_rows(n, cols):
    return -(-n // (8 * cols)) * 8


def _pack(arrs, cols, mult):
    parts = []
    for a in arrs:
        n = math.prod(a.shape)
        nr = _pack_rows(n, cols)
        parts.append(jnp.pad(a.reshape(-1), (0, nr * cols - n)).reshape(nr, cols))
    total = sum(p.shape[0] for p in parts)
    pad_rows = -(-total // mult) * mult - total
    if pad_rows:
        parts.append(jnp.zeros((pad_rows, cols), arrs[0].dtype))
    return jnp.concatenate(parts, axis=0)


def _unpack(buf, shapes):
    out, r = [], 0
    cols = buf.shape[1]
    for s in shapes:
        n = math.prod(s)
        nr = _pack_rows(n, cols)
        piece = lax.optimization_barrier(buf[r:r + nr])
        out.append(piece.reshape(-1)[:n].reshape(s))
        r += nr
    return out


def _elementwise(fn, arrs, nout, name):
    r, c = arrs[0].shape
    t = _tile(r, ROW_TILE, 8)
    return rw_fwd(fn, [rows(a, t) for a in arrs], [(c, False)] * nout, nrow=r, t=t, name=name)


def kernel(x, ln_g, w_in, conv_w, a_log, dt_bias, head_norm_g, lam_re, lam_im, log_dt, b_re, b_im, c_re, c_im, d_skip, w_glu, b_glu, w_pa, w_pb, b_gate, w_out, final_g, loss_target, m_ln_g, m_w_in, m_conv_w, m_a_log, m_dt_bias, m_head_norm_g, m_lam_re, m_lam_im, m_log_dt, m_b_re, m_b_im, m_c_re, m_c_im, m_d_skip, m_w_glu, m_b_glu, m_w_pa, m_w_pb, m_b_gate, m_w_out, m_final_g, v_ln_g, v_w_in, v_conv_w, v_a_log, v_dt_bias, v_head_norm_g, v_lam_re, v_lam_im, v_log_dt, v_b_re, v_b_im, v_c_re, v_c_im, v_d_skip, v_w_glu, v_b_glu, v_w_pa, v_w_pb, v_b_gate, v_w_out, v_final_g):
    env = dict(locals())
    wts = {n: env[n] for n in WEIGHTS}
    mom_m = {n: env["m_" + n] for n in WEIGHTS}
    mom_v = {n: env["v_" + n] for n in WEIGHTS}

    xin = x[0]
    tgt = loss_target[0]
    nrow, dm = xin.shape
    depth = ln_g.shape[0]
    nh = dm // (2 * HEAD_DIM)
    wa = nh * HEAD_DIM
    wb = dm // 2
    ngrp = wb // S5_GROUP_CH
    pw = w_in.shape[-1]
    t = min(ROW_TILE, nrow)

    o_ba = 4 * wa
    o_u = o_ba + 4 * nh
    n_main = 4 * wa + 2 * wb + 2 * dm
    projp = -(-(n_main + LANES) // 512) * 512
    blk_za = 3 * wa // HEAD_DIM
    blk_u = 4 * wa // LANES
    ba_blk = n_main // LANES
    cx, cy, cc = _me()

    jb = o_ba // pw
    assert (o_u - 1) // pw == jb
    cut_lo, cut = o_ba - jb * pw, o_u - o_ba
    wide = -(-pw // LANES) * LANES + LANES
    col = lax.broadcasted_iota(jnp.int32, (pw, wide), 1)
    row = lax.broadcasted_iota(jnp.int32, (pw, wide), 0)
    holds_cut = (4 * cx + 2 * cy + cc) == jb
    src_plain = jnp.where(col < pw, col, -1)
    src_cut = jnp.where(col < cut_lo, col, jnp.where(col < pw - cut, col + cut, -1))
    src_cut = jnp.where((col >= wide - LANES) & (col < wide - LANES + cut), col - (wide - LANES) + cut_lo, src_cut)
    select = (row == jnp.where(holds_cut, src_cut, src_plain)).astype(WIRE_DT)
    w_in_tx = mm(w_in.reshape(depth * dm, pw), select, name="w_in_prepare").astype(WIRE_DT).reshape(depth, dm, wide)

    def gather_side(l):
        return all_gather([w_in_tx[l], w_glu[l].astype(WIRE_DT), w_pa[l].astype(WIRE_DT),
                           w_pb[l].astype(WIRE_DT), w_out[l].astype(WIRE_DT), conv_w[l]])

    def cat(g):
        return jnp.concatenate([g[j] for j in range(8)], axis=1)

    def assemble(gathered):
        g_in, g_glu, g_pa, g_pb, g_out, g_conv = gathered
        w_perm = jnp.concatenate(
            [g_in[j][:, :pw - cut if j == jb else pw] for j in range(8)]
            + [g_in[jb][:, wide - LANES:], jnp.zeros((dm, projp - n_main - LANES), WIRE_DT)], axis=1)
        return dict(w_in=w_perm, w_glu=g_glu.reshape(wb, wb), w_pa=cat(g_pa), w_pb=cat(g_pb),
                    w_out=g_out.reshape(dm, dm), conv_w=cat(g_conv))

    full = [assemble(run_side(gather_side(0), name="gather_weights"))]

    def small(l):
        z = jnp.zeros((1, LANES - 4 * nh), F32)
        alog_row = jnp.concatenate([jnp.zeros((1, 2 * nh), F32), a_log[l].reshape(1, 2 * nh), z], axis=1)
        dtb_row = jnp.concatenate([jnp.zeros((1, 2 * nh), F32), dt_bias[l].reshape(1, 2 * nh), z], axis=1)
        return alog_row, dtb_row

    saved = []
    cur = xin
    for l in range(depth):
        fw = full[l]
        (hh,) = rw_fwd(fn_rms, [rows(cur, t), bcast(ln_g[l][None])], [(dm, False)], nrow=nrow, t=t, name="rms_fwd")
        proj = mm(hh, fw["w_in"], name="proj_fwd")
        (qkv,) = rw_fwd(make_fn_prep(nh), [rows(proj, nrow, HEAD_DIM, 0, True), bcast(fw["conv_w"], HEAD_DIM, 0, True)],
                        [(HEAD_DIM, True)], nrow=nrow, t=nrow, ncol=3 * nh, name="prep_fwd")
        alog_row, dtb_row = small(l)
        nxt_gather = gather_side(l + 1) if l + 1 < depth else None
        o_dir, s_hist = gdn_fwd(qkv, proj, ba_blk, alog_row, dtb_row, n_heads=nh, side=nxt_gather, name="gdn_fwd")
        if nxt_gather is not None:
            full.append(assemble(nxt_gather.result))
        hn_ins = [rows3(o_dir, 0, t, HEAD_DIM), rows3(o_dir, 1, t, HEAD_DIM), rows(proj, t, HEAD_DIM, blk_za, True),
                  bcast(head_norm_g[l][None])]
        (ya_in,) = rw_fwd(fn_headnorm, hn_ins, [(HEAD_DIM, True)], nrow=nrow, t=t, ncol=nh, name="headnorm_fwd")
        mats, mats_vjp = jax.vjp(s5_mats, lam_re[l], lam_im[l], log_dt[l], b_re[l], b_im[l], c_re[l], c_im[l])
        u_seg = segment_major(proj[:, 4 * wa:4 * wa + wb])
        yd, xs = s5_fwd(u_seg, *mats, name="s5_fwd")
        ge_ins = [rows3(yd, 0, t, wb, False), rows3(yd, 1, t, wb, False), rows(u_seg, t), bcast(d_skip[l][None])]
        (ys_seg,) = rw_fwd(fn_gelu, ge_ins, [(wb, False)], nrow=nrow, t=t, name="gelu_fwd")
        ys = time_major(ys_seg)
        glu = mm(ys, fw["w_glu"], name="glu_fwd")
        gl_ins = [rows(ys, t), rows(glu, t), rows(proj, t, wb, (4 * wa + wb) // wb), bcast(b_glu[l][None])]
        (yb_in,) = rw_fwd(fn_glu, gl_ins, [(wb, False)], nrow=nrow, t=t, name="glugate_fwd")
        y_a = mm(ya_in, fw["w_pa"], name="pa_fwd")
        y_b = mm(yb_in, fw["w_pb"], name="pb_fwd")
        bg = b_gate[l][None]
        mg_ins = [rows(proj, t, dm, 3), rows(proj, t, dm, 4), rows(y_a, t), rows(y_b, t),
                  bcast(bg, dm, 0), bcast(bg, dm, 1)]
        (merged,) = rw_fwd(fn_merge, mg_ins, [(dm, False)], nrow=nrow, t=t, name="merge_fwd")
        nxt = mm(merged, fw["w_out"], add=cur, name="out_fwd")
        saved.append(dict(x=cur, h=hh, proj=proj, qkv=qkv, o_dir=o_dir, s_hist=s_hist, hn_ins=hn_ins,
                          mats=mats, mats_vjp=mats_vjp, xs=xs, ge_ins=ge_ins, ys=ys, gl_ins=gl_ins, u_seg=u_seg,
                          ya_in=ya_in, yb_in=yb_in, mg_ins=mg_ins, merged=merged,
                          alog_row=alog_row, dtb_row=dtb_row))
        cur = nxt

    loss_ins = [rows(cur, t), rows(tgt, t), bcast(final_g[None])]
    (row_loss,) = rw_fwd(fn_loss, loss_ins, [(1, False)], nrow=nrow, t=t, name="loss_fwd")
    ones = jnp.ones((nrow, 1), F32)
    dcur, dfinal = rw_bwd(fn_loss, loss_ins, [[rows(ones, t)]], nrow=nrow, t=t,
                          row_grads=[(0, False)], bc_grads=[2], name="loss_bwd")
    loss = lax.psum(jnp.sum(row_loss), ("x", "y", "c"))

    coord = {"x": cx, "y": cy, "c": cc}
    routes = {"cxy": ("c", "x", "y"), "cyx": ("c", "y", "x")}
    route = ["cxy", "cyx", "cxy", "cxy", "cyx", "cyx", "cxy"]
    conv_order = [4 * ((q // 2) % 2) + 2 * (q % 2) + q // 4 for q in range(8)]
    cw = 3 * wa // 8
    g_final = {n: [None] * depth for n in SHARDED}
    gsh = {n: [None] * depth for n in SHARDED}

    def own_columns(z):
        un = jnp.concatenate([z[:, :o_ba], z[:, n_main:n_main + 4 * nh], z[:, o_ba:n_main]], axis=1)
        return jnp.stack([un[:, d * pw:(d + 1) * pw] for d in range(8)])

    def rs_begin(l):
        in_a, in_b = gsh["w_in"][l]
        conv_buf = jnp.stack([gsh["conv_w"][l][:, d * cw:(d + 1) * cw] for d in conv_order])
        return dict(layer=l, stage=0, bufs=[
            in_a.reshape(8, dm // 16, projp), in_b.reshape(8, dm // 16, projp), gsh["w_out"][l],
            gsh["w_glu"][l], gsh["w_pa"][l], gsh["w_pb"][l], conv_buf])

    def rs_side(st):
        if st is None:
            return None
        if st["stage"] < 3:
            st["axes"] = [routes[r][st["stage"]] for r in route]

            def halves(b):
                return b.reshape((2, b.shape[0] // 2) + b.shape[1:])

            st["bufs"] = [halves(b) for b in st["bufs"]]
            send = [halves(b) for b in st["tx"]] if st.get("tx") else st["bufs"]
            if st.get("whole") is not None:
                return exchange(send + [st["whole"]], st["axes"] + [routes["cxy"][st["stage"]]],
                                half=[True] * len(send) + [False])
            return exchange(send, st["axes"], half=True)
        st["bufs"] = [b.reshape(b.shape[1:]) for b in st["bufs"]]
        return all_to_all([own_columns(st["bufs"][0]), own_columns(st["bufs"][1])],
                          [lambda x, y, c: 4 * c + 2 * x + y, lambda x, y, c: 4 * c + 2 * y + x])

    def rs_absorb(st, side):
        if st is None:
            return
        if st["stage"] < 3:
            if st.get("whole") is not None:
                (st["whole"],) = _elementwise(make_fn_sum(2), [st["whole"], side.result[-1]], 1, "ar_add")
            narrow = st["stage"] < 2
            sums = [add_half(b, r, coord[ax].astype(jnp.int32).reshape(1), narrow=narrow, name=f"rs_add_{st['stage']}")
                    for b, r, ax in zip(st["bufs"], side.result, st["axes"])]
            st["bufs"] = [s[0] for s in sums]
            st["tx"] = [s[1] for s in sums] if narrow else None
        else:
            got_a, got_b = side.result
            g_final["w_in"][st["layer"]] = jnp.concatenate(
                [got_a.reshape(dm // 2, pw), got_b.reshape(dm // 2, pw)], axis=0)
            for n, b in zip(["w_out", "w_glu", "w_pa", "w_pb", "conv_w"], st["bufs"][2:]):
                g_final[n][st["layer"]] = b
        st["stage"] += 1

    grep = {n: [None] * depth for n in REPLICATED if n != "final_g"}
    pending = None
    for l in reversed(range(depth)):
        fw, sv = full[l], saved[l]
        dmerged = mm(dcur, fw["w_out"], tb=True, name="out_bwd_x")
        gsh["w_out"][l] = mm(sv["merged"], dcur, ta=True, scatter=("rows", slot_cxy), name="out_bwd_w")
        side = rs_side(pending)
        dla, dlb, dya, dyb, dbga, dbgb = rw_bwd(
            fn_merge, sv["mg_ins"], [[rows(dmerged, t)]], nrow=nrow, t=t,
            row_grads=[(0, False), (1, False), (2, False), (3, False)], bc_grads=[4, 5], side=side, name="merge_bwd")
        rs_absorb(pending, side)
        grep["b_gate"][l] = jnp.concatenate([dbga.reshape(dm), dbgb.reshape(dm)])
        dya_in = mm(dya, fw["w_pa"], tb=True, name="pa_bwd_x")
        gsh["w_pa"][l] = mm(sv["ya_in"], dya, ta=True, scatter=("cols", slot_cyx), name="pa_bwd_w")
        dyb_in = mm(dyb, fw["w_pb"], tb=True, name="pb_bwd_x")
        gsh["w_pb"][l] = mm(sv["yb_in"], dyb, ta=True, scatter=("cols", slot_cyx), name="pb_bwd_w")
        dys1, dglu, dzb, dbglu = rw_bwd(
            fn_glu, sv["gl_ins"], [[rows(dyb_in, t)]], nrow=nrow, t=t,
            row_grads=[(0, False), (1, False), (2, False)], bc_grads=[3], name="glugate_bwd")
        grep["b_glu"][l] = dbglu.reshape(wb)
        dys2 = mm(dglu, fw["w_glu"], tb=True, name="glu_bwd_x")
        gsh["w_glu"][l] = mm(sv["ys"], dglu, ta=True, scatter=("rows", slot_cxy), name="glu_bwd_w")
        dyd, du1, ddskip = rw_bwd(
            fn_gelu, sv["ge_ins"], [[rows(segment_major(dys1), t), rows(segment_major(dys2), t)]], nrow=nrow, t=t,
            row_grads=[(0, False), (2, False)], bc_grads=[3], name="gelu_bwd")
        grep["d_skip"][l] = ddskip.reshape(wb)
        side = rs_side(pending)
        du_seg, dw_s5, dcm_s5, dar, dai = s5_bwd(sv["u_seg"], *sv["mats"], sv["xs"], dyd, du1, side=side,
                                                 name="s5_bwd")
        du = time_major(du_seg)
        rs_absorb(pending, side)
        g_lr, g_li, g_ldt, g_br, g_bi, g_cr, g_ci = sv["mats_vjp"]((dar, dai, dw_s5, dcm_s5))
        for nme, val in zip(["lam_re", "lam_im", "log_dt", "b_re", "b_im", "c_re", "c_im"],
                            [g_lr, g_li, g_ldt, g_br, g_bi, g_cr, g_ci]):
            grep[nme][l] = val
        do, dza, dhn = rw_bwd(fn_headnorm, sv["hn_ins"], [[rows(dya_in, t, HEAD_DIM, 0, True)]], nrow=nrow, t=t,
                              ncol=nh, row_grads=[(0, True), (2, True)], bc_grads=[3], name="headnorm_bwd")
        grep["head_norm_g"][l] = jnp.sum(dhn, axis=0).reshape(HEAD_DIM)
        side = rs_side(pending)
        dqkv, dba_all, dal, ddt = gdn_bwd(sv["qkv"], sv["proj"], ba_blk, sv["alog_row"], sv["dtb_row"],
                                          sv["s_hist"], do, n_heads=nh, side=side, name="gdn_bwd")
        rs_absorb(pending, side)
        grep["a_log"][l] = jnp.sum(dal, axis=(0, 1))[2 * nh:4 * nh].reshape(2, nh)
        grep["dt_bias"][l] = jnp.sum(ddt, axis=(0, 1))[2 * nh:4 * nh].reshape(2, nh)
        (dba,) = rw_fwd(make_fn_sum(2), [rows3(dba_all, p, t, LANES, False) for p in range(2)],
                        [(LANES, False)], nrow=nrow, t=t, name="dba_sum")
        prep_ins = [rows(sv["proj"], nrow, HEAD_DIM, 0, True), bcast(fw["conv_w"], HEAD_DIM, 0, True)]
        dq_cots = [(dqkv.reshape(2, 3 * nrow, wa), (None, nrow, HEAD_DIM),
                    (lambda j, i, dd=dd: (dd, j // nh, j % nh))) for dd in range(2)]
        dqkv_raw, dconv = rw_bwd(make_fn_prep(nh), prep_ins, [dq_cots], nrow=nrow, t=nrow, ncol=3 * nh,
                                 row_grads=[(0, True)], bc_grads=[1], name="prep_bwd")
        gsh["conv_w"][l] = jnp.transpose(dconv, (1, 0, 2)).reshape(CONV_K, 3 * wa)
        dproj = jnp.concatenate([dqkv_raw, dza, du, dzb, dla, dlb, dba,
                                 jnp.zeros((nrow, projp - n_main - LANES), F32)], axis=1).astype(MXU_DT)
        side = rs_side(pending)
        dh = mm(dproj, fw["w_in"], tb=True, side=side, name="proj_bwd_x")
        rs_absorb(pending, side)
        gsh["w_in"][l] = [mm(sv["h"], dproj, ta=True, m_part=(part, 2), name="proj_bwd_w") for part in range(2)]
        dcur, dlng = rw_bwd(fn_rms, [rows(sv["x"], t), bcast(ln_g[l][None])], [[rows(dh, t)]], nrow=nrow, t=t,
                            row_grads=[(0, False)], bc_grads=[1], residual=rows(dcur, t), name="rms_bwd")
        grep["ln_g"][l] = dlng.reshape(dm)
        pending = rs_begin(l)
    grad_x = dcur[None]
    rep_list = [jnp.stack(grep[n]) for n in REPLICATED if n != "final_g"] + [dfinal.reshape(dm)]
    rep_shapes = [a.shape for a in rep_list]
    pending["whole"] = _pack(rep_list, COMM_COLS, ROW_TILE)
    for stage in range(RS_STAGES):
        side = rs_side(pending)
        run_side(side, name=f"rs_stage_{stage}")
        rs_absorb(pending, side)
    grads = {n: jnp.stack(g_final[n]) for n in SHARDED}
    for n, val in zip(REPLICATED, _unpack(pending["whole"], rep_shapes)):
        grads[n] = val

    deltas, new_m, new_v = {}, {}, {}
    big = ["w_in", "w_glu", "w_pa", "w_pb", "w_out"]
    for n in big:
        shp = wts[n].shape
        two = [a.reshape(-1, shp[-1]) for a in (wts[n], grads[n], mom_m[n], mom_v[n])]
        d_, m_, v_ = _elementwise(fn_adamw, two, 3, "adamw_" + n)
        deltas[n], new_m[n], new_v[n] = d_.reshape(shp), m_.reshape(shp), v_.reshape(shp)
    rest = [n for n in WEIGHTS if n not in big]
    rest_shapes = [wts[n].shape for n in rest]
    packed = [_pack([src[n] for n in rest], COMM_COLS, ROW_TILE) for src in (wts, grads, mom_m, mom_v)]
    outs = _elementwise(fn_adamw, packed, 3, "adamw_small")
    for dst, arr in zip((deltas, new_m, new_v), outs):
        for n, val in zip(rest, _unpack(arr, rest_shapes)):
            dst[n] = val

    return (loss, grad_x, *[grads[n] for n in WEIGHTS], *[deltas[n] for n in WEIGHTS],
            *[new_m[n] for n in WEIGHTS], *[new_v[n] for n in WEIGHTS])
```

```python
import functools
import math

import jax
import jax.numpy as jnp
from jax import lax
from jax.experimental import pallas as pl
from jax.experimental.pallas import tpu as pltpu

F32 = jnp.float32
MXU_DT = jnp.bfloat16
WIRE_DT = jnp.bfloat16

HEAD_DIM = 128
CHUNK = 64
CONV_K = 5
S5_GROUP_CH = 16
S5_STATE = 64
S5_BLOCK_GROUPS = 8
S5_BLOCK_STATE = S5_BLOCK_GROUPS * S5_STATE
RMS_EPS = 1e-6
LANES = 128
VMEM_LIMIT = 56 * 1024 * 1024
ROW_TILE = 256
COMM_COLS = 1024
ADD_BLOCK_BYTES = 2 * 1024 * 1024
RS_STAGES = 4
ELEMENTWISE_BLOCK_BYTES = 1024 * 1024

ADAM_LR = 0.001
ADAM_B1 = 0.9
ADAM_B2 = 0.999
ADAM_EPS = 1e-08
ADAM_WD = 0.01
ADAM_STEP = 10

WEIGHTS = ['ln_g', 'w_in', 'conv_w', 'a_log', 'dt_bias', 'head_norm_g', 'lam_re', 'lam_im', 'log_dt',
           'b_re', 'b_im', 'c_re', 'c_im', 'd_skip', 'w_glu', 'b_glu', 'w_pa', 'w_pb', 'b_gate',
           'w_out', 'final_g']
SHARDED = ['w_in', 'w_glu', 'w_pa', 'w_pb', 'w_out', 'conv_w']
REPLICATED = [n for n in WEIGHTS if n not in SHARDED]
MESH = pl.DeviceIdType.MESH


def _params(sem=None):
    return pltpu.CompilerParams(dimension_semantics=sem, vmem_limit_bytes=VMEM_LIMIT)


def _tile(n, cap, q=LANES):
    t = (min(n, cap) // q) * q
    while t > q and n % t:
        t -= q
    return t if t > 0 and n % t == 0 else n


class Side:
    def __init__(self, ins, out_sd, sems, start, finish):
        self.ins, self.out_sd, self.sems, self.start, self.finish = list(ins), list(out_sd), list(sems), start, finish
        self.result = None


def _call(body, *, grid, in_specs, out_specs, out_shape, scratch_shapes=(), sem, name, args, side=None):
    in_specs, out_specs, out_shape = list(in_specs), list(out_specs), list(out_shape)
    scratch_shapes = list(scratch_shapes)
    if side is None:
        return pl.pallas_call(body, grid=grid, in_specs=in_specs, out_specs=out_specs, out_shape=out_shape,
                              scratch_shapes=scratch_shapes, compiler_params=_params(sem), name=name)(*args)
    hbm = pl.BlockSpec(memory_space=pl.ANY)
    n_in, n_out, n_scr = len(in_specs), len(out_specs), len(scratch_shapes)
    s_in, s_out = len(side.ins), len(side.out_sd)

    def hosted(*refs):
        main_in, rest = refs[:n_in], refs[n_in:]
        side_in, rest = rest[:s_in], rest[s_in:]
        main_out, rest = rest[:n_out], rest[n_out:]
        side_out, rest = rest[:s_out], rest[s_out:]
        main_scr, sems = rest[:n_scr], rest[n_scr:]
        ids = [pl.program_id(k) for k in range(len(grid))]
        first = functools.reduce(jnp.logical_and, [i == 0 for i in ids])
        last = functools.reduce(jnp.logical_and, [i == g - 1 for i, g in zip(ids, grid)])

        @pl.when(first)
        def _():
            side.start(side_in, side_out, sems)

        body(*main_in, *main_out, *main_scr)

        @pl.when(last)
        def _():
            side.finish(side_in, side_out, sems)

    outs = pl.pallas_call(
        hosted, grid=grid, in_specs=in_specs + [hbm] * s_in, out_specs=out_specs + [hbm] * s_out,
        out_shape=out_shape + side.out_sd, scratch_shapes=scratch_shapes + side.sems,
        compiler_params=_params(("arbitrary",) * len(grid)), name=name)(*args, *side.ins)
    side.result = list(outs[n_out:])
    return list(outs[:n_out])


def run_side(side, *, name):
    hbm = pl.BlockSpec(memory_space=pl.ANY)
    s_in, s_out = len(side.ins), len(side.out_sd)

    def body(*refs):
        side_in, side_out, sems = refs[:s_in], refs[s_in:s_in + s_out], refs[s_in + s_out:]
        side.start(side_in, side_out, sems)
        side.finish(side_in, side_out, sems)

    side.result = list(pl.pallas_call(body, out_shape=side.out_sd, in_specs=[hbm] * s_in, out_specs=[hbm] * s_out,
                                      scratch_shapes=side.sems, name=name)(*side.ins))
    return side.result


def slot_cxy(d):
    return 4 * (d % 2) + 2 * (d // 4) + (d // 2) % 2


def slot_cyx(d):
    return 4 * (d % 2) + 2 * ((d // 2) % 2) + d // 4


def mm(a, b, *, ta=False, tb=False, add=None, m_part=None, scatter=None, side=None, name):
    m = a.shape[1] if ta else a.shape[0]
    k = a.shape[0] if ta else a.shape[1]
    n = b.shape[0] if tb else b.shape[1]
    m_off = 0
    if m_part is not None:
        m = m // m_part[1]
        m_off = m_part[0]
    tm, tn, tk = _tile(m, 1024), _tile(n, 512), _tile(k, 2048)
    if scatter is not None and scatter[0] == "rows":
        tm = m // 8
    if scatter is not None and scatter[0] == "cols":
        tn = n // 8
    if m_part is not None:
        assert tm == m
    nk = k // tk
    dn = (((0 if ta else 1,), (1 if tb else 0,)), ((), ()))

    def body(*refs):
        if add is None:
            a_ref, b_ref, o_ref, acc = refs
        else:
            a_ref, b_ref, add_ref, o_ref, acc = refs
        kk = pl.program_id(2)

        @pl.when(kk == 0)
        def _():
            acc[...] = jnp.zeros_like(acc)

        acc[...] += lax.dot_general(a_ref[...].astype(MXU_DT), b_ref[...].astype(MXU_DT), dn,
                                    preferred_element_type=F32)

        @pl.when(kk == nk - 1)
        def _():
            r = acc[...]
            if add is not None:
                r = r + add_ref[...]
            o_ref[...] = r

    if ta:
        a_spec = pl.BlockSpec((tk, tm), lambda i, j, kk: (kk, i + m_off))
    else:
        a_spec = pl.BlockSpec((tm, tk), lambda i, j, kk: (i + m_off, kk))
    b_spec = pl.BlockSpec((tn, tk), lambda i, j, kk: (j, kk)) if tb else pl.BlockSpec((tk, tn), lambda i, j, kk: (kk, j))
    out_sd = jax.ShapeDtypeStruct((m, n), F32)
    if scatter is None:
        o_spec = pl.BlockSpec((tm, tn), lambda i, j, kk: (i, j))
    elif scatter[0] == "rows":
        o_spec = pl.BlockSpec((None, tm, tn), lambda i, j, kk: (scatter[1](i), 0, j))
        out_sd = jax.ShapeDtypeStruct((8, tm, n), F32)
    else:
        o_spec = pl.BlockSpec((None, tm, tn), lambda i, j, kk: (scatter[1](j), i, 0))
        out_sd = jax.ShapeDtypeStruct((8, m, tn), F32)
    ins, specs = [a, b], [a_spec, b_spec]
    if add is not None:
        ins.append(add)
        specs.append(o_spec)
    return _call(body, grid=(m // tm, n // tn, nk), in_specs=specs, out_specs=[o_spec], out_shape=[out_sd],
                 scratch_shapes=[pltpu.VMEM((tm, tn), F32)], sem=("parallel", "parallel", "arbitrary"),
                 name=name, args=ins, side=side)[0]


def rows(arr, t, width=None, base=0, per_j=False):
    width = arr.shape[1] if width is None else width
    return (arr, (t, width), lambda j, i: (i, base + (j if per_j else 0)))


def rows3(arr, lead, t, width, per_j=True):
    return (arr, (None, t, width), lambda j, i: (lead, i, j if per_j else 0))


def bcast(arr, width=None, base=0, per_j=False):
    width = arr.shape[1] if width is None else width
    return (arr, (arr.shape[0], width), lambda j, i: (0, base + (j if per_j else 0)))


def _specs(items):
    return [pl.BlockSpec(bs, im) for (_, bs, im) in items]


def rw_fwd(fn, ins, outs, *, nrow, t, ncol=1, name):
    out_specs = [pl.BlockSpec((t, w), (lambda j, i: (i, j)) if pj else (lambda j, i: (i, 0))) for (w, pj) in outs]
    out_shape = [jax.ShapeDtypeStruct((nrow, w * (ncol if pj else 1)), F32) for (w, pj) in outs]
    nin = len(ins)

    def body(*refs):
        j = pl.program_id(0)
        res = fn(j, *[r[...] for r in refs[:nin]])
        for o_ref, r in zip(refs[nin:], res):
            o_ref[...] = r

    return pl.pallas_call(
        body, grid=(ncol, nrow // t), in_specs=_specs(ins), out_specs=out_specs, out_shape=out_shape,
        compiler_params=_params(("parallel", "parallel")), name=name)(*[x[0] for x in ins])


def rw_bwd(fn, ins, cots, *, nrow, t, ncol=1, row_grads, bc_grads, residual=None, side=None, name):
    nin = len(ins)
    flat_cots = [c for group in cots for c in group]
    extra = [residual] if residual is not None else []
    out_specs, out_shape = [], []
    for idx, pj in row_grads:
        w = ins[idx][1][-1]
        out_specs.append(pl.BlockSpec((t, w), (lambda j, i: (i, j)) if pj else (lambda j, i: (i, 0))))
        out_shape.append(jax.ShapeDtypeStruct((nrow, w * (ncol if pj else 1)), F32))
    for idx in bc_grads:
        r, w = ins[idx][1]
        out_specs.append(pl.BlockSpec((None, r, w), lambda j, i: (j, 0, 0)))
        out_shape.append(jax.ShapeDtypeStruct((ncol, r, w), F32))

    def body(*refs):
        j = pl.program_id(0)
        i = pl.program_id(1)
        vals = [r[...] for r in refs[:nin]]
        pos = nin
        cts = []
        for group in cots:
            c = refs[pos][...]
            for q in range(1, len(group)):
                c = c + refs[pos + q][...]
            pos += len(group)
            cts.append(c)
        res_ref = refs[pos] if residual is not None else None
        pos += len(extra)
        outs = refs[pos:]
        _, vjp = jax.vjp(lambda *a: tuple(fn(j, *a)), *vals)
        grads = vjp(tuple(cts))
        for q, (idx, _) in enumerate(row_grads):
            g = grads[idx]
            if q == 0 and res_ref is not None:
                g = g + res_ref[...]
            outs[q][...] = g
        for q, idx in enumerate(bc_grads):
            o_ref = outs[len(row_grads) + q]

            @pl.when(i == 0)
            def _(o_ref=o_ref):
                o_ref[...] = jnp.zeros_like(o_ref)

            o_ref[...] += grads[idx]

    all_in = list(ins) + flat_cots + extra
    return _call(body, grid=(ncol, nrow // t), in_specs=_specs(all_in), out_specs=out_specs, out_shape=out_shape,
                 sem=("parallel", "arbitrary"), name=name, args=[x[0] for x in all_in], side=side)


def _silu(x):
    return x * jax.nn.sigmoid(x)


@jax.custom_vjp
def _softplus(x):
    return jnp.maximum(x, 0.0) + jnp.log1p(jnp.exp(-jnp.abs(x)))


def _softplus_fwd(x):
    return _softplus(x), x


def _softplus_bwd(x, ct):
    return (ct * jax.nn.sigmoid(x),)


_softplus.defvjp(_softplus_fwd, _softplus_bwd)


def _gelu(x):
    return 0.5 * x * (1.0 + jnp.tanh(math.sqrt(2.0 / math.pi) * (x + 0.044715 * (x * x * x))))


def _row_shift_impl(x, s):
    n = x.shape[0]
    if s == 0:
        return x
    rolled = pltpu.roll(x, (-s) % n, 0)
    t = lax.broadcasted_iota(jnp.int32, x.shape, 0)
    ok = (t + s >= 0) & (t + s < n)
    return jnp.where(ok, rolled, 0.0)


@functools.partial(jax.custom_vjp, nondiff_argnums=(1,))
def _row_shift(x, s):
    return _row_shift_impl(x, s)


def _row_shift_fwd(x, s):
    return _row_shift_impl(x, s), None


def _row_shift_bwd(s, _, ct):
    return (_row_shift_impl(ct, -s),)


_row_shift.defvjp(_row_shift_fwd, _row_shift_bwd)


def fn_rms(j, x, g):
    return (x * lax.rsqrt(jnp.mean(x * x, axis=-1, keepdims=True) + RMS_EPS) * g,)


def make_fn_prep(n_heads):
    pad = (CONV_K - 1) // 2

    def fn_prep(j, x, w):
        y = _row_shift(x, -pad) * w[0:1, :]
        for i in range(1, CONV_K):
            y = y + _row_shift(x, i - pad) * w[i:i + 1, :]
        a = _silu(y)
        scale = jnp.where(j < n_heads, HEAD_DIM ** -0.5, 1.0).astype(F32)
        nrm = a * lax.rsqrt(jnp.sum(a * a, axis=-1, keepdims=True) + RMS_EPS) * scale
        return (jnp.where(j < 2 * n_heads, nrm, a),)

    return fn_prep


def fn_headnorm(j, o0, o1, z, g):
    o = o0 + o1
    n = o * lax.rsqrt(jnp.mean(o * o, axis=-1, keepdims=True) + RMS_EPS) * g
    return (n * _silu(z),)


def fn_gelu(j, y0, y1, u, dsk):
    return (_gelu(y0 + y1 + u * dsk),)


def fn_glu(j, ys, logit, z, b):
    return (ys * jax.nn.sigmoid(logit + b) * _silu(z),)


def fn_merge(j, la, lb, ya, yb, ba, bb):
    return (jax.nn.sigmoid(la + ba) * ya + jax.nn.sigmoid(lb + bb) * yb,)


def fn_loss(j, x, t, g):
    y = x * lax.rsqrt(jnp.mean(x * x, axis=-1, keepdims=True) + RMS_EPS) * g
    e = y - t
    return (0.5 * jnp.mean(e * e, axis=-1, keepdims=True),)


def make_fn_sum(n):
    def fn_sum(j, *xs):
        s = xs[0]
        for q in range(1, n):
            s = s + xs[q]
        return (s,)

    return fn_sum


def fn_adamw(j, w, g, m, v):
    m2 = ADAM_B1 * m + (1.0 - ADAM_B1) * g
    v2 = ADAM_B2 * v + (1.0 - ADAM_B2) * (g * g)
    m_hat = m2 / (1.0 - ADAM_B1 ** ADAM_STEP)
    v_hat = v2 / (1.0 - ADAM_B2 ** ADAM_STEP)
    delta = -ADAM_LR * (m_hat / (jnp.sqrt(v_hat) + ADAM_EPS) + ADAM_WD * w)
    return delta, m2, v2


def _dot(a, b, dims, precision=None):
    return lax.dot_general(a, b, (dims, ((), ())), precision=precision, preferred_element_type=F32)


_NN = ((1,), (0,))
_NT = ((1,), (1,))
_TN = ((0,), (0,))
GDN_HEAD_BLOCK = 4
INVERSE_WIDE_FACTORS = 3


def _split(a):
    hi = a.astype(MXU_DT)
    return hi, (a - hi.astype(F32)).astype(MXU_DT)


def _dot3s(a, b, dims):
    (ah, al), (bh, bl) = a, b
    return _dot(ah, bh, dims) + (_dot(al, bh, dims) + _dot(ah, bl, dims))


def _dot3(a, b, dims):
    return _dot3s(_split(a), _split(b), dims)


def _unit_inverse(lmat):
    r = lmat.shape[0]
    eye = (lax.broadcasted_iota(jnp.int32, (r, r), 0) == lax.broadcasted_iota(jnp.int32, (r, r), 1)).astype(F32)
    pw = -lmat
    tinv = eye + pw
    pws = _split(pw)
    n_fact = int(math.ceil(math.log2(CHUNK))) - 1
    for k in range(n_fact):
        if k < INVERSE_WIDE_FACTORS:
            pws = _split(_dot3s(pws, pws, _NN))
            tinv = tinv + _dot3s(_split(tinv), pws, _NN)
        else:
            pw = _dot(pws[0], pws[0], _NN)
            pws = (pw.astype(MXU_DT), None)
            tinv = tinv + _dot(tinv.astype(MXU_DT), pws[0], _NN)
    return tinv


@jax.custom_vjp
def tri_solve(lmat, rhs):
    return _dot3(_unit_inverse(lmat), rhs, _NN)


def _tri_solve_fwd(lmat, rhs):
    tinv = _unit_inverse(lmat)
    x = _dot3(tinv, rhs, _NN)
    return x, (tinv, x)


def _tri_solve_bwd(res, dx):
    tinv, x = res
    drhs = _dot3(tinv, dx, _TN)
    return -_dot3(drhs, x, _NT), drhs


tri_solve.defvjp(_tri_solve_fwd, _tri_solve_bwd)


def gdn_group(s, q, k, v, ba, alog, dtb, *, d, head0, n_heads):
    hb = len(s)
    c = q.shape[0]
    r = hb * c

    def stack(x):
        return jnp.concatenate([x[:, i * HEAD_DIM:(i + 1) * HEAD_DIM] for i in range(hb)], axis=0)

    def pick(x, lane0):
        lane = lax.broadcasted_iota(jnp.int32, x.shape, 1)
        return jnp.concatenate(
            [jnp.sum(jnp.where(lane == lane0 + i, x, 0.0), axis=1, keepdims=True) for i in range(hb)], axis=0)

    q4, k4, v4 = stack(q), stack(k), stack(v)
    beta = pick(jax.nn.sigmoid(ba), d * n_heads + head0)
    g = pick(-jnp.exp(alog) * _softplus(ba + dtb), 2 * n_heads + d * n_heads + head0)
    ii = lax.broadcasted_iota(jnp.int32, (r, r), 0)
    jj = lax.broadcasted_iota(jnp.int32, (r, r), 1)
    same = (ii // c) == (jj // c)
    rel = (ii - jj) * (1 - 2 * d)
    incl = same & (rel >= 0)
    strict = same & (rel > 0)
    incl_t = same & (rel <= 0)
    g_row = jnp.sum(jnp.where(ii == jj, g, 0.0), axis=0, keepdims=True)
    gc_col = jnp.sum(jnp.where(incl, g_row, 0.0), axis=1, keepdims=True)
    gc_row = jnp.sum(jnp.where(incl_t, g, 0.0), axis=0, keepdims=True)
    g_tot = jnp.sum(jnp.where(same, g_row, 0.0), axis=1, keepdims=True)
    decay = jnp.exp(jnp.where(incl, gc_col - gc_row, -1e30))
    kb = k4 * beta
    vb = v4 * beta
    lmat = jnp.where(strict, _dot(kb, k4, _NT) * decay, 0.0)
    uw = tri_solve(lmat, jnp.concatenate([vb, kb * jnp.exp(gc_col)], axis=1))
    u, w = uw[:, :HEAD_DIM], uw[:, HEAD_DIM:]
    qk = _dot(q4, k4, _NT) * decay
    qe = q4 * jnp.exp(gc_col)
    kd = k4 * jnp.exp(g_tot - gc_col)
    eg = jnp.exp(g_tot)
    v_new, o_s = [], []
    for i in range(hb):
        rs = slice(i * c, (i + 1) * c)
        ws = _dot(jnp.concatenate([w[rs], qe[rs]], axis=0), s[i], _NN)
        v_new.append(u[rs] - ws[:c])
        o_s.append(ws[c:])
    o4 = jnp.concatenate(o_s, axis=0) + _dot(qk, jnp.concatenate(v_new, axis=0), _NN)
    s_new = tuple(s[i] * eg[i * c:i * c + 1, :] + _dot(kd[i * c:(i + 1) * c], v_new[i], _TN) for i in range(hb))
    o = jnp.concatenate([o4[i * c:(i + 1) * c] for i in range(hb)], axis=1)
    return s_new, o


def _gdn_maps(n_chunks):
    def chunk_of(d, step):
        return step + d * (n_chunks - 1 - 2 * step)
    return chunk_of


def _heads(ref, h0, n):
    return ref[:, h0 * HEAD_DIM:(h0 + n) * HEAD_DIM]


def gdn_fwd(qkv, proj, ba_blk, alog_row, dtb_row, *, n_heads, side=None, name):
    nrow = qkv.shape[0]
    nc = nrow // CHUNK
    h = n_heads
    wa = h * HEAD_DIM
    hb = math.gcd(h, GDN_HEAD_BLOCK)
    chunk_of = _gdn_maps(nc)

    def body(q_ref, k_ref, v_ref, ba_ref, al_ref, dt_ref, o_ref, sh_ref, s_scr):
        d = pl.program_id(0)
        n = pl.program_id(1)

        @pl.when(n == 0)
        def _():
            s_scr[...] = jnp.zeros_like(s_scr)

        ba, al, dt = ba_ref[...], al_ref[...], dt_ref[...]
        for h0 in range(0, h, hb):
            s = tuple(s_scr[h0 + i] for i in range(hb))
            for i in range(hb):
                sh_ref[h0 + i] = s[i]
            s2, o = gdn_group(s, _heads(q_ref, h0, hb), _heads(k_ref, h0, hb), _heads(v_ref, h0, hb), ba, al, dt,
                              d=d, head0=h0, n_heads=h)
            o_ref[:, h0 * HEAD_DIM:(h0 + hb) * HEAD_DIM] = o
            for i in range(hb):
                s_scr[h0 + i] = s2[i]

    blk = (CHUNK, wa)
    in_specs = [
        pl.BlockSpec(blk, lambda d, n: (chunk_of(d, n), 0)),
        pl.BlockSpec(blk, lambda d, n: (chunk_of(d, n), 1)),
        pl.BlockSpec(blk, lambda d, n: (chunk_of(d, n), 2)),
        pl.BlockSpec((CHUNK, LANES), lambda d, n: (chunk_of(d, n), ba_blk)),
        pl.BlockSpec((1, LANES), lambda d, n: (0, 0)),
        pl.BlockSpec((1, LANES), lambda d, n: (0, 0)),
    ]
    out_specs = [
        pl.BlockSpec((None, CHUNK, wa), lambda d, n: (d, chunk_of(d, n), 0)),
        pl.BlockSpec((h, None, HEAD_DIM, HEAD_DIM), lambda d, n: (d, n, 0, 0)),
    ]
    out_shape = [jax.ShapeDtypeStruct((2, nrow, wa), F32),
                 jax.ShapeDtypeStruct((2 * h, nc, HEAD_DIM, HEAD_DIM), F32)]
    return _call(body, grid=(2, nc), in_specs=in_specs, out_specs=out_specs, out_shape=out_shape,
                 scratch_shapes=[pltpu.VMEM((h, HEAD_DIM, HEAD_DIM), F32)], sem=("parallel", "arbitrary"),
                 name=name, args=(qkv, qkv, qkv, proj, alog_row, dtb_row), side=side)


def gdn_bwd(qkv, proj, ba_blk, alog_row, dtb_row, s_hist, do, *, n_heads, side=None, name):
    nrow = qkv.shape[0]
    nc = nrow // CHUNK
    h = n_heads
    wa = h * HEAD_DIM
    hb = math.gcd(h, GDN_HEAD_BLOCK)
    chunk_of = _gdn_maps(nc)

    def cb(d, n):
        return chunk_of(d, nc - 1 - n)

    def body(q_ref, k_ref, v_ref, ba_ref, al_ref, dt_ref, sh_ref, do_ref,
             dqkv_ref, dba_ref, dal_ref, ddt_ref, ds_scr):
        d = pl.program_id(0)
        n = pl.program_id(1)

        @pl.when(n == 0)
        def _():
            ds_scr[...] = jnp.zeros_like(ds_scr)
            dal_ref[...] = jnp.zeros_like(dal_ref)
            ddt_ref[...] = jnp.zeros_like(ddt_ref)

        ba, al, dt = ba_ref[...], al_ref[...], dt_ref[...]
        dba_sum = jnp.zeros_like(ba)
        dal_sum = jnp.zeros_like(al)
        ddt_sum = jnp.zeros_like(dt)
        for h0 in range(0, h, hb):
            f = functools.partial(gdn_group, d=d, head0=h0, n_heads=h)
            s = tuple(sh_ref[h0 + i] for i in range(hb))
            _, vjp = jax.vjp(f, s, _heads(q_ref, h0, hb), _heads(k_ref, h0, hb), _heads(v_ref, h0, hb), ba, al, dt)
            ds, dq, dk, dv, dba, dal, ddt = vjp((tuple(ds_scr[h0 + i] for i in range(hb)), _heads(do_ref, h0, hb)))
            for i in range(hb):
                ds_scr[h0 + i] = ds[i]
            cols = slice(h0 * HEAD_DIM, (h0 + hb) * HEAD_DIM)
            dqkv_ref[0, :, cols] = dq
            dqkv_ref[1, :, cols] = dk
            dqkv_ref[2, :, cols] = dv
            dba_sum = dba_sum + dba
            dal_sum = dal_sum + dal
            ddt_sum = ddt_sum + ddt
        dba_ref[...] = dba_sum
        dal_ref[...] += dal_sum
        ddt_ref[...] += ddt_sum

    blk = (CHUNK, wa)
    in_specs = [
        pl.BlockSpec(blk, lambda d, n: (cb(d, n), 0)),
        pl.BlockSpec(blk, lambda d, n: (cb(d, n), 1)),
        pl.BlockSpec(blk, lambda d, n: (cb(d, n), 2)),
        pl.BlockSpec((CHUNK, LANES), lambda d, n: (cb(d, n), ba_blk)),
        pl.BlockSpec((1, LANES), lambda d, n: (0, 0)),
        pl.BlockSpec((1, LANES), lambda d, n: (0, 0)),
        pl.BlockSpec((h, None, HEAD_DIM, HEAD_DIM), lambda d, n: (d, nc - 1 - n, 0, 0)),
        pl.BlockSpec(blk, lambda d, n: (cb(d, n), 0)),
    ]
    out_specs = [
        pl.BlockSpec((None, 3, CHUNK, wa), lambda d, n: (d, 0, cb(d, n), 0)),
        pl.BlockSpec((None, CHUNK, LANES), lambda d, n: (d, cb(d, n), 0)),
        pl.BlockSpec((None, 1, LANES), lambda d, n: (d, 0, 0)),
        pl.BlockSpec((None, 1, LANES), lambda d, n: (d, 0, 0)),
    ]
    out_shape = [jax.ShapeDtypeStruct((2, 3, nrow, wa), F32),
                 jax.ShapeDtypeStruct((2, nrow, LANES), F32),
                 jax.ShapeDtypeStruct((2, 1, LANES), F32),
                 jax.ShapeDtypeStruct((2, 1, LANES), F32)]
    return _call(body, grid=(2, nc), in_specs=in_specs, out_specs=out_specs, out_shape=out_shape,
                 scratch_shapes=[pltpu.VMEM((h, HEAD_DIM, HEAD_DIM), F32)], sem=("parallel", "arbitrary"),
                 name=name, args=(qkv, qkv, qkv, proj, alog_row, dtb_row, s_hist, do), side=side)


def s5_mats(lam_re, lam_im, log_dt, b_re, b_im, c_re, c_im):
    g = lam_re.shape[1]
    nb = g // S5_BLOCK_GROUPS
    dt = jnp.exp(log_dt)[..., None]
    mag = jnp.exp(lam_re * dt)
    ar = mag * jnp.cos(lam_im * dt)
    ai = mag * jnp.sin(lam_im * dt)
    den = lam_re * lam_re + lam_im * lam_im
    fr = ((ar - 1.0) * lam_re + ai * lam_im) / den
    fi = (ai * lam_re - (ar - 1.0) * lam_im) / den
    bbr = fr[..., None] * b_re - fi[..., None] * b_im
    bbi = fr[..., None] * b_im + fi[..., None] * b_re
    eye = jnp.eye(S5_BLOCK_GROUPS, dtype=F32)
    shp = (2, nb, S5_BLOCK_GROUPS, S5_STATE, S5_GROUP_CH)
    w_r = jnp.einsum('dsjpc,jk->dsjckp', bbr.reshape(shp), eye).reshape(2, nb, LANES, S5_BLOCK_STATE)
    w_i = jnp.einsum('dsjpc,jk->dsjckp', bbi.reshape(shp), eye).reshape(2, nb, LANES, S5_BLOCK_STATE)
    w = jnp.concatenate([w_r, w_i], axis=-1)
    shc = (2, nb, S5_BLOCK_GROUPS, S5_GROUP_CH, S5_STATE)
    c_r = jnp.einsum('dsjcp,jk->dsjpkc', c_re.reshape(shc), eye).reshape(2, nb, S5_BLOCK_STATE, LANES)
    c_i = jnp.einsum('dsjcp,jk->dsjpkc', c_im.reshape(shc), eye).reshape(2, nb, S5_BLOCK_STATE, LANES)
    cm = jnp.concatenate([c_r, -c_i], axis=-2)
    return (ar.reshape(2, nb, 1, S5_BLOCK_STATE), ai.reshape(2, nb, 1, S5_BLOCK_STATE), w, cm)


_S5_ROWS = 512
S5_SEGMENTS = 8


def segment_major(a):
    n, c = a.shape
    return jnp.transpose(a.reshape(S5_SEGMENTS, n // S5_SEGMENTS, c), (1, 0, 2)).reshape(n, c)


def time_major(a):
    n, c = a.shape
    return jnp.transpose(a.reshape(n // S5_SEGMENTS, S5_SEGMENTS, c), (1, 0, 2)).reshape(n, c)


def _segmented_scan(x_ref, a_r, a_i, *, reverse, visit=None):
    nrow, two_hs = x_ref.shape
    hs = two_hs // 2
    steps = nrow // S5_SEGMENTS
    assert steps & (steps - 1) == 0
    b_r = jnp.broadcast_to(a_r, (S5_SEGMENTS, hs))
    b_i = jnp.broadcast_to(a_i, (S5_SEGMENTS, hs))
    rid = lax.broadcasted_iota(jnp.int32, (S5_SEGMENTS, hs), 0)

    def rows_of(i):
        return pl.ds(pl.multiple_of((steps - 1 - i if reverse else i) * S5_SEGMENTS, S5_SEGMENTS), S5_SEGMENTS)

    def local(i, carry):
        s_r, s_i = carry
        rows = rows_of(i)
        n_r = b_r * s_r - b_i * s_i + x_ref[rows, pl.ds(0, hs)]
        n_i = b_r * s_i + b_i * s_r + x_ref[rows, pl.ds(hs, hs)]
        x_ref[rows, pl.ds(0, hs)] = n_r
        x_ref[rows, pl.ds(hs, hs)] = n_i
        return n_r, n_i

    z = jnp.zeros((S5_SEGMENTS, hs), F32)
    e_r, e_i = lax.fori_loop(0, steps, local, (z, z))
    q_r, q_i = a_r, a_i
    for _ in range(steps.bit_length() - 1):
        q_r, q_i = q_r * q_r - q_i * q_i, 2.0 * q_r * q_i
    c_r = jnp.zeros((1, hs), F32)
    c_i = jnp.zeros((1, hs), F32)
    en_r, en_i = z, z
    for s in (reversed(range(S5_SEGMENTS)) if reverse else range(S5_SEGMENTS)):
        en_r = jnp.where(rid == s, c_r, en_r)
        en_i = jnp.where(rid == s, c_i, en_i)
        c_r, c_i = (e_r[s:s + 1] + q_r * c_r - q_i * c_i, e_i[s:s + 1] + q_r * c_i + q_i * c_r)

    def fix(i, carry):
        p_r, p_i = carry
        rows = rows_of(i)
        x_r = x_ref[rows, pl.ds(0, hs)] + (p_r * en_r - p_i * en_i)
        x_i = x_ref[rows, pl.ds(hs, hs)] + (p_r * en_i + p_i * en_r)
        x_ref[rows, pl.ds(0, hs)] = x_r
        x_ref[rows, pl.ds(hs, hs)] = x_i
        if visit is not None:
            visit(steps - 1 - i if reverse else i, rows, x_r, x_i)
        return p_r * b_r - p_i * b_i, p_r * b_i + p_i * b_r

    return lax.fori_loop(0, steps, fix, (b_r, b_i))


def s5_fwd(u_seg, ar, ai, w, cm, *, name):
    nrow = u_seg.shape[0]
    nb = w.shape[1]
    hs = S5_BLOCK_STATE
    rs = min(_S5_ROWS, nrow)

    def body(u_ref, ar_ref, ai_ref, w_ref, cm_ref, y_ref, x_ref):
        d = pl.program_id(0)
        wb = w_ref[...].astype(MXU_DT)
        for r0 in range(0, nrow, rs):
            x_ref[pl.ds(r0, rs), :] = _dot(u_ref[pl.ds(r0, rs), :].astype(MXU_DT), wb, _NN)
        @pl.when(d == 0)
        def _():
            _segmented_scan(x_ref, ar_ref[...], ai_ref[...], reverse=False)

        @pl.when(d == 1)
        def _():
            _segmented_scan(x_ref, ar_ref[...], ai_ref[...], reverse=True)

        cb = cm_ref[...].astype(MXU_DT)
        for r0 in range(0, nrow, rs):
            y_ref[pl.ds(r0, rs), :] = _dot(x_ref[pl.ds(r0, rs), :].astype(MXU_DT), cb, _NN)

    in_specs = [
        pl.BlockSpec((nrow, LANES), lambda d, s: (0, s)),
        pl.BlockSpec((None, None, 1, hs), lambda d, s: (d, s, 0, 0)),
        pl.BlockSpec((None, None, 1, hs), lambda d, s: (d, s, 0, 0)),
        pl.BlockSpec((None, None, LANES, 2 * hs), lambda d, s: (d, s, 0, 0)),
        pl.BlockSpec((None, None, 2 * hs, LANES), lambda d, s: (d, s, 0, 0)),
    ]
    out_specs = [
        pl.BlockSpec((None, nrow, LANES), lambda d, s: (d, 0, s)),
        pl.BlockSpec((None, nrow, 2 * hs), lambda d, s: (d, 0, s)),
    ]
    out_shape = [jax.ShapeDtypeStruct((2, nrow, nb * LANES), F32),
                 jax.ShapeDtypeStruct((2, nrow, nb * 2 * hs), F32)]
    return pl.pallas_call(
        body, grid=(2, nb), in_specs=in_specs, out_specs=out_specs, out_shape=out_shape,
        compiler_params=_params(("parallel", "parallel")), name=name)(u_seg, ar, ai, w, cm)


def s5_bwd(u_seg, ar, ai, w, cm, xs, dy, du_in, *, side=None, name):
    nrow = u_seg.shape[0]
    nb = w.shape[1]
    steps = nrow // S5_SEGMENTS
    hs = S5_BLOCK_STATE
    rs = min(_S5_ROWS, nrow)

    def body(u_ref, dy_ref, dui_ref, x_ref, ar_ref, ai_ref, w_ref, cm_ref,
             du_ref, dw_ref, dcm_ref, dar_ref, dai_ref, g_ref, acc_ref):
        d = pl.program_id(1)
        cb = cm_ref[...].astype(MXU_DT)
        for r0 in range(0, nrow, rs):
            g_ref[pl.ds(r0, rs), :] = _dot(dy_ref[pl.ds(r0, rs), :].astype(MXU_DT), cb, _NT)
        acc_ref[...] = jnp.zeros_like(acc_ref)
        rid = lax.broadcasted_iota(jnp.int32, (S5_SEGMENTS, 2 * hs), 0)

        def run(forward_recurrence):
            last = pl.ds((steps - 1) * S5_SEGMENTS, S5_SEGMENTS)
            first = pl.ds(0, S5_SEGMENTS)
            if forward_recurrence:
                wrap = jnp.where(rid == 0, 0.0, pltpu.roll(x_ref[last, :], 1, 0))
            else:
                wrap = jnp.where(rid == S5_SEGMENTS - 1, 0.0, pltpu.roll(x_ref[first, :], S5_SEGMENTS - 1, 0))

            def visit(step, rows, g_r, g_i):
                if forward_recurrence:
                    nbr = jnp.maximum(step - 1, 0)
                    edge = step == 0
                else:
                    nbr = jnp.minimum(step + 1, steps - 1)
                    edge = step == steps - 1
                prev = x_ref[pl.ds(pl.multiple_of(nbr * S5_SEGMENTS, S5_SEGMENTS), S5_SEGMENTS), :]
                prev = jnp.where(edge, wrap, prev)
                p_r, p_i = prev[:, :hs], prev[:, hs:]
                acc_ref[:, pl.ds(0, hs)] += g_r * p_r + g_i * p_i
                acc_ref[:, pl.ds(hs, hs)] += g_i * p_r - g_r * p_i

            _segmented_scan(g_ref, ar_ref[...], -ai_ref[...], reverse=forward_recurrence, visit=visit)

        @pl.when(d == 0)
        def _():
            run(True)

        @pl.when(d == 1)
        def _():
            run(False)

        dar_ref[...] = jnp.sum(acc_ref[:, pl.ds(0, hs)], axis=0, keepdims=True)
        dai_ref[...] = jnp.sum(acc_ref[:, pl.ds(hs, hs)], axis=0, keepdims=True)

        wb = w_ref[...].astype(MXU_DT)
        for r0 in range(0, nrow, rs):
            part = _dot(g_ref[pl.ds(r0, rs), :].astype(MXU_DT), wb, _NT)

            @pl.when(d == 0)
            def _(part=part, r0=r0):
                du_ref[pl.ds(r0, rs), :] = dui_ref[pl.ds(r0, rs), :] + part

            @pl.when(d == 1)
            def _(part=part, r0=r0):
                du_ref[pl.ds(r0, rs), :] += part

        dw_ref[...] = _dot(u_ref[...].astype(MXU_DT), g_ref[...].astype(MXU_DT), _TN)
        dcm_ref[...] = _dot(x_ref[...].astype(MXU_DT), dy_ref[...].astype(MXU_DT), _TN)

    in_specs = [
        pl.BlockSpec((nrow, LANES), lambda s, d: (0, s)),
        pl.BlockSpec((nrow, LANES), lambda s, d: (0, s)),
        pl.BlockSpec((nrow, LANES), lambda s, d: (0, s)),
        pl.BlockSpec((None, nrow, 2 * hs), lambda s, d: (d, 0, s)),
        pl.BlockSpec((None, None, 1, hs), lambda s, d: (d, s, 0, 0)),
        pl.BlockSpec((None, None, 1, hs), lambda s, d: (d, s, 0, 0)),
        pl.BlockSpec((None, None, LANES, 2 * hs), lambda s, d: (d, s, 0, 0)),
        pl.BlockSpec((None, None, 2 * hs, LANES), lambda s, d: (d, s, 0, 0)),
    ]
    out_specs = [
        pl.BlockSpec((nrow, LANES), lambda s, d: (0, s)),
        pl.BlockSpec((None, None, LANES, 2 * hs), lambda s, d: (d, s, 0, 0)),
        pl.BlockSpec((None, None, 2 * hs, LANES), lambda s, d: (d, s, 0, 0)),
        pl.BlockSpec((None, None, 1, hs), lambda s, d: (d, s, 0, 0)),
        pl.BlockSpec((None, None, 1, hs), lambda s, d: (d, s, 0, 0)),
    ]
    out_shape = [jax.ShapeDtypeStruct((nrow, nb * LANES), F32),
                 jax.ShapeDtypeStruct(w.shape, F32), jax.ShapeDtypeStruct(cm.shape, F32),
                 jax.ShapeDtypeStruct(ar.shape, F32), jax.ShapeDtypeStruct(ai.shape, F32)]
    return _call(body, grid=(nb, 2), in_specs=in_specs, out_specs=out_specs, out_shape=out_shape,
                 scratch_shapes=[pltpu.VMEM((nrow, 2 * hs), F32), pltpu.VMEM((S5_SEGMENTS, 2 * hs), F32)],
                 sem=("parallel", "arbitrary"), name=name, args=(u_seg, dy, du_in, xs, ar, ai, w, cm), side=side)


def _me():
    return lax.axis_index("x"), lax.axis_index("y"), lax.axis_index("c")


def all_gather(shards):
    na = len(shards)

    def plan(x_refs, out_refs, sems):
        send_sems, recv_sems, local_sems = sems
        x, y, c = _me()
        me, sibling = (x, y, c), (x, y, 1 - c)
        chips = [(1 - x, y), (x, 1 - y), (1 - x, 1 - y)]

        def slot(a, px, py, pc):
            return out_refs[a].at[4 * px + 2 * py + pc]

        def copy(a, k, block, to, src=None):
            return pltpu.make_async_remote_copy(
                src_ref=slot(a, *block) if src is None else src, dst_ref=slot(a, *block),
                send_sem=send_sems.at[a, k], recv_sem=recv_sems.at[a, k], device_id=to, device_id_type=MESH)

        mine = [pltpu.make_async_copy(x_refs[a], slot(a, *me), local_sems.at[a]) for a in range(na)]
        first = []
        for a in range(na):
            first.append(copy(a, 0, me, sibling, src=x_refs[a]))
            first += [copy(a, 1 + j, me, (*chip, c), src=x_refs[a]) for j, chip in enumerate(chips)]
        return me, sibling, chips, c, copy, mine, first

    def start(x_refs, out_refs, sems):
        _, _, _, _, _, mine, first = plan(x_refs, out_refs, sems)
        for cp in mine + first:
            cp.start()

    def finish(x_refs, out_refs, sems):
        me, sibling, chips, c, copy, mine, first = plan(x_refs, out_refs, sems)
        passed = []
        for j, chip in enumerate(chips):
            for a in range(na):
                copy(a, 1 + j, (*chip, c), me).wait_recv()
                fwd = copy(a, 4 + j, (*chip, c), sibling)
                fwd.start()
                passed.append(fwd)
        for a in range(na):
            copy(a, 0, sibling, me).wait_recv()
            for j, chip in enumerate(chips):
                copy(a, 4 + j, (*chip, 1 - c), me).wait_recv()
        for cp in first + passed:
            cp.wait_send()
        for cp in mine:
            cp.wait()

    return Side(shards, [jax.ShapeDtypeStruct((8,) + s.shape, s.dtype) for s in shards],
                [pltpu.SemaphoreType.DMA((na, 7)), pltpu.SemaphoreType.DMA((na, 7)), pltpu.SemaphoreType.DMA((na,))],
                start, finish)


_AXES = ("x", "y", "c")


def exchange(bufs, axes, *, half):
    na = len(bufs)
    if isinstance(half, bool):
        half = [half] * na

    def copies(in_refs, out_refs, sems):
        send_sems, recv_sems = sems
        me = _me()
        cps = []
        for a in range(na):
            bit = me[_AXES.index(axes[a])]
            peer = tuple(1 - v if ax == axes[a] else v for ax, v in zip(_AXES, me))
            cps.append(pltpu.make_async_remote_copy(
                src_ref=in_refs[a].at[1 - bit] if half[a] else in_refs[a], dst_ref=out_refs[a],
                send_sem=send_sems.at[a], recv_sem=recv_sems.at[a], device_id=peer, device_id_type=MESH))
        return cps

    def start(*refs):
        for cp in copies(*refs):
            cp.start()

    def finish(*refs):
        for cp in copies(*refs):
            cp.wait()

    return Side(bufs, [jax.ShapeDtypeStruct(b.shape[1:] if h else b.shape, b.dtype) for b, h in zip(bufs, half)],
                [pltpu.SemaphoreType.DMA((na,)), pltpu.SemaphoreType.DMA((na,))], start, finish)


def all_to_all(pieces, slot_fns):
    na = len(pieces)

    def copies(in_refs, out_refs, sems):
        send_sems, recv_sems, local_sems = sems
        x, y, c = _me()
        cps = []
        for a in range(na):
            mine = slot_fns[a](x, y, c)
            for k in range(8):
                tx, ty, tc = x ^ (k // 4), y ^ ((k // 2) % 2), c ^ (k % 2)
                src = in_refs[a].at[4 * tx + 2 * ty + tc]
                dst = out_refs[a].at[mine]
                if k == 0:
                    cps.append(pltpu.make_async_copy(src, dst, local_sems.at[a]))
                else:
                    cps.append(pltpu.make_async_remote_copy(
                        src_ref=src, dst_ref=dst, send_sem=send_sems.at[a, k - 1], recv_sem=recv_sems.at[a, k - 1],
                        device_id=(tx, ty, tc), device_id_type=MESH))
        return cps

    def start(*refs):
        for cp in copies(*refs):
            cp.start()

    def finish(*refs):
        for cp in copies(*refs):
            cp.wait()

    return Side(pieces, [jax.ShapeDtypeStruct(p.shape, p.dtype) for p in pieces],
                [pltpu.SemaphoreType.DMA((na, 7)), pltpu.SemaphoreType.DMA((na, 7)), pltpu.SemaphoreType.DMA((na,))],
                start, finish)


def add_half(buf, recv, bit, *, narrow, name):
    c = recv.shape[-1]
    r = math.prod(recv.shape[:-1])
    t = _tile(r, max(16, (ADD_BLOCK_BYTES // (4 * c)) // 16 * 16), 16)

    def body(bit_ref, a_ref, b_ref, o_ref, *tx_ref):
        s = a_ref[...] + b_ref[...].astype(F32)
        o_ref[...] = s
        if narrow:
            tx_ref[0][...] = s.astype(WIRE_DT)

    o_spec = pl.BlockSpec((t, c), lambda i, b: (i, 0))
    grid_spec = pltpu.PrefetchScalarGridSpec(
        num_scalar_prefetch=1, grid=(r // t,),
        in_specs=[pl.BlockSpec((None, t, c), lambda i, b: (b[0], i, 0)), o_spec],
        out_specs=[o_spec, o_spec] if narrow else [o_spec])
    out_shape = [jax.ShapeDtypeStruct((r, c), F32)] + ([jax.ShapeDtypeStruct((r, c), WIRE_DT)] if narrow else [])
    outs = pl.pallas_call(body, grid_spec=grid_spec, out_shape=out_shape,
                          compiler_params=_params(("parallel",)), name=name)(
                              bit, buf.reshape(2, r, c), recv.reshape(r, c))
    return [o.reshape(recv.shape) for o in outs]


def _pack_rows(n, cols):
    return -(-n // (8 * cols)) * 8


def _pack(arrs, cols, mult):
    parts = []
    for a in arrs:
        n = math.prod(a.shape)
        nr = _pack_rows(n, cols)
        parts.append(jnp.pad(a.reshape(-1), (0, nr * cols - n)).reshape(nr, cols))
    total = sum(p.shape[0] for p in parts)
    pad_rows = -(-total // mult) * mult - total
    if pad_rows:
        parts.append(jnp.zeros((pad_rows, cols), arrs[0].dtype))
    return jnp.concatenate(parts, axis=0)


def _unpack(buf, shapes):
    out, r = [], 0
    cols = buf.shape[1]
    for s in shapes:
        n = math.prod(s)
        nr = _pack_rows(n, cols)
        piece = lax.optimization_barrier(buf[r:r + nr])
        out.append(piece.reshape(-1)[:n].reshape(s))
        r += nr
    return out


def _elementwise(fn, arrs, nout, name):
    r, c = arrs[0].shape
    lanes = -(-c // LANES) * LANES
    t = _tile(r, max(ROW_TILE, (ELEMENTWISE_BLOCK_BYTES // (4 * lanes)) // 8 * 8), 8)
    return rw_fwd(fn, [rows(a, t) for a in arrs], [(c, False)] * nout, nrow=r, t=t, name=name)


def kernel(x, ln_g, w_in, conv_w, a_log, dt_bias, head_norm_g, lam_re, lam_im, log_dt, b_re, b_im, c_re, c_im, d_skip, w_glu, b_glu, w_pa, w_pb, b_gate, w_out, final_g, loss_target, m_ln_g, m_w_in, m_conv_w, m_a_log, m_dt_bias, m_head_norm_g, m_lam_re, m_lam_im, m_log_dt, m_b_re, m_b_im, m_c_re, m_c_im, m_d_skip, m_w_glu, m_b_glu, m_w_pa, m_w_pb, m_b_gate, m_w_out, m_final_g, v_ln_g, v_w_in, v_conv_w, v_a_log, v_dt_bias, v_head_norm_g, v_lam_re, v_lam_im, v_log_dt, v_b_re, v_b_im, v_c_re, v_c_im, v_d_skip, v_w_glu, v_b_glu, v_w_pa, v_w_pb, v_b_gate, v_w_out, v_final_g):
    env = dict(locals())
    wts = {n: env[n] for n in WEIGHTS}
    mom_m = {n: env["m_" + n] for n in WEIGHTS}
    mom_v = {n: env["v_" + n] for n in WEIGHTS}

    xin = x[0]
    tgt = loss_target[0]
    nrow, dm = xin.shape
    depth = ln_g.shape[0]
    nh = dm // (2 * HEAD_DIM)
    wa = nh * HEAD_DIM
    wb = dm // 2
    ngrp = wb // S5_GROUP_CH
    pw = w_in.shape[-1]
    t = min(ROW_TILE, nrow)

    o_ba = 4 * wa
    o_u = o_ba + 4 * nh
    n_main = 4 * wa + 2 * wb + 2 * dm
    projp = -(-(n_main + LANES) // 512) * 512
    blk_za = 3 * wa // HEAD_DIM
    blk_u = 4 * wa // LANES
    ba_blk = n_main // LANES
    cx, cy, cc = _me()

    jb = o_ba // pw
    assert (o_u - 1) // pw == jb
    cut_lo, cut = o_ba - jb * pw, o_u - o_ba
    wide = -(-pw // LANES) * LANES + LANES
    col = lax.broadcasted_iota(jnp.int32, (pw, wide), 1)
    row = lax.broadcasted_iota(jnp.int32, (pw, wide), 0)
    holds_cut = (4 * cx + 2 * cy + cc) == jb
    src_plain = jnp.where(col < pw, col, -1)
    src_cut = jnp.where(col < cut_lo, col, jnp.where(col < pw - cut, col + cut, -1))
    src_cut = jnp.where((col >= wide - LANES) & (col < wide - LANES + cut), col - (wide - LANES) + cut_lo, src_cut)
    select = (row == jnp.where(holds_cut, src_cut, src_plain)).astype(WIRE_DT)
    w_in_tx = mm(w_in.reshape(depth * dm, pw), select, name="w_in_prepare").astype(WIRE_DT).reshape(depth, dm, wide)

    def gather_side(l):
        return all_gather([w_in_tx[l], w_glu[l].astype(WIRE_DT), w_pa[l].astype(WIRE_DT),
                           w_pb[l].astype(WIRE_DT), w_out[l].astype(WIRE_DT), conv_w[l]])

    def cat(g):
        return jnp.concatenate([g[j] for j in range(8)], axis=1)

    def assemble(gathered):
        g_in, g_glu, g_pa, g_pb, g_out, g_conv = gathered
        w_perm = jnp.concatenate(
            [g_in[j][:, :pw - cut if j == jb else pw] for j in range(8)]
            + [g_in[jb][:, wide - LANES:], jnp.zeros((dm, projp - n_main - LANES), WIRE_DT)], axis=1)
        return dict(w_in=w_perm, w_glu=g_glu.reshape(wb, wb), w_pa=cat(g_pa), w_pb=cat(g_pb),
                    w_out=g_out.reshape(dm, dm), conv_w=cat(g_conv))

    full = [assemble(run_side(gather_side(0), name="gather_weights"))]

    def small(l):
        z = jnp.zeros((1, LANES - 4 * nh), F32)
        alog_row = jnp.concatenate([jnp.zeros((1, 2 * nh), F32), a_log[l].reshape(1, 2 * nh), z], axis=1)
        dtb_row = jnp.concatenate([jnp.zeros((1, 2 * nh), F32), dt_bias[l].reshape(1, 2 * nh), z], axis=1)
        return alog_row, dtb_row

    saved = []
    cur = xin
    for l in range(depth):
        fw = full[l]
        (hh,) = rw_fwd(fn_rms, [rows(cur, t), bcast(ln_g[l][None])], [(dm, False)], nrow=nrow, t=t, name="rms_fwd")
        proj = mm(hh, fw["w_in"], name="proj_fwd")
        (qkv,) = rw_fwd(make_fn_prep(nh), [rows(proj, nrow, HEAD_DIM, 0, True), bcast(fw["conv_w"], HEAD_DIM, 0, True)],
                        [(HEAD_DIM, True)], nrow=nrow, t=nrow, ncol=3 * nh, name="prep_fwd")
        alog_row, dtb_row = small(l)
        nxt_gather = gather_side(l + 1) if l + 1 < depth else None
        o_dir, s_hist = gdn_fwd(qkv, proj, ba_blk, alog_row, dtb_row, n_heads=nh, side=nxt_gather, name="gdn_fwd")
        if nxt_gather is not None:
            full.append(assemble(nxt_gather.result))
        hn_ins = [rows3(o_dir, 0, t, HEAD_DIM), rows3(o_dir, 1, t, HEAD_DIM), rows(proj, t, HEAD_DIM, blk_za, True),
                  bcast(head_norm_g[l][None])]
        (ya_in,) = rw_fwd(fn_headnorm, hn_ins, [(HEAD_DIM, True)], nrow=nrow, t=t, ncol=nh, name="headnorm_fwd")
        mats, mats_vjp = jax.vjp(s5_mats, lam_re[l], lam_im[l], log_dt[l], b_re[l], b_im[l], c_re[l], c_im[l])
        u_seg = segment_major(proj[:, 4 * wa:4 * wa + wb])
        yd, xs = s5_fwd(u_seg, *mats, name="s5_fwd")
        ge_ins = [rows3(yd, 0, t, wb, False), rows3(yd, 1, t, wb, False), rows(u_seg, t), bcast(d_skip[l][None])]
        (ys_seg,) = rw_fwd(fn_gelu, ge_ins, [(wb, False)], nrow=nrow, t=t, name="gelu_fwd")
        ys = time_major(ys_seg)
        glu = mm(ys, fw["w_glu"], name="glu_fwd")
        gl_ins = [rows(ys, t), rows(glu, t), rows(proj, t, wb, (4 * wa + wb) // wb), bcast(b_glu[l][None])]
        (yb_in,) = rw_fwd(fn_glu, gl_ins, [(wb, False)], nrow=nrow, t=t, name="glugate_fwd")
        y_a = mm(ya_in, fw["w_pa"], name="pa_fwd")
        y_b = mm(yb_in, fw["w_pb"], name="pb_fwd")
        bg = b_gate[l][None]
        mg_ins = [rows(proj, t, dm, 3), rows(proj, t, dm, 4), rows(y_a, t), rows(y_b, t),
                  bcast(bg, dm, 0), bcast(bg, dm, 1)]
        (merged,) = rw_fwd(fn_merge, mg_ins, [(dm, False)], nrow=nrow, t=t, name="merge_fwd")
        nxt = mm(merged, fw["w_out"], add=cur, name="out_fwd")
        saved.append(dict(x=cur, h=hh, proj=proj, qkv=qkv, o_dir=o_dir, s_hist=s_hist, hn_ins=hn_ins,
                          mats=mats, mats_vjp=mats_vjp, xs=xs, ge_ins=ge_ins, ys=ys, gl_ins=gl_ins, u_seg=u_seg,
                          ya_in=ya_in, yb_in=yb_in, mg_ins=mg_ins, merged=merged,
                          alog_row=alog_row, dtb_row=dtb_row))
        cur = nxt

    loss_ins = [rows(cur, t), rows(tgt, t), bcast(final_g[None])]
    (row_loss,) = rw_fwd(fn_loss, loss_ins, [(1, False)], nrow=nrow, t=t, name="loss_fwd")
    ones = jnp.ones((nrow, 1), F32)
    dcur, dfinal = rw_bwd(fn_loss, loss_ins, [[rows(ones, t)]], nrow=nrow, t=t,
                          row_grads=[(0, False)], bc_grads=[2], name="loss_bwd")
    loss = lax.psum(jnp.sum(row_loss), ("x", "y", "c"))

    coord = {"x": cx, "y": cy, "c": cc}
    routes = {"cxy": ("c", "x", "y"), "cyx": ("c", "y", "x")}
    route = ["cxy", "cyx", "cxy", "cxy", "cyx", "cyx", "cxy"]
    conv_order = [4 * ((q // 2) % 2) + 2 * (q % 2) + q // 4 for q in range(8)]
    cw = 3 * wa // 8
    g_final = {n: [None] * depth for n in SHARDED}
    gsh = {n: [None] * depth for n in SHARDED}

    def own_columns(z):
        un = jnp.concatenate([z[:, :o_ba], z[:, n_main:n_main + 4 * nh], z[:, o_ba:n_main]], axis=1)
        return jnp.stack([un[:, d * pw:(d + 1) * pw] for d in range(8)])

    def rs_begin(l):
        in_a, in_b = gsh["w_in"][l]
        conv_buf = jnp.stack([gsh["conv_w"][l][:, d * cw:(d + 1) * cw] for d in conv_order])
        return dict(layer=l, stage=0, bufs=[
            in_a.reshape(8, dm // 16, projp), in_b.reshape(8, dm // 16, projp), gsh["w_out"][l],
            gsh["w_glu"][l], gsh["w_pa"][l], gsh["w_pb"][l], conv_buf])

    def rs_side(st):
        if st is None:
            return None
        if st["stage"] < 3:
            st["axes"] = [routes[r][st["stage"]] for r in route]

            def halves(b):
                return b.reshape((2, b.shape[0] // 2) + b.shape[1:])

            st["bufs"] = [halves(b) for b in st["bufs"]]
            send = [halves(b) for b in st["tx"]] if st.get("tx") else st["bufs"]
            if st.get("whole") is not None:
                return exchange(send + [st["whole"]], st["axes"] + [routes["cxy"][st["stage"]]],
                                half=[True] * len(send) + [False])
            return exchange(send, st["axes"], half=True)
        st["bufs"] = [b.reshape(b.shape[1:]) for b in st["bufs"]]
        return all_to_all([own_columns(st["bufs"][0]), own_columns(st["bufs"][1])],
                          [lambda x, y, c: 4 * c + 2 * x + y, lambda x, y, c: 4 * c + 2 * y + x])

    def rs_absorb(st, side):
        if st is None:
            return
        if st["stage"] < 3:
            if st.get("whole") is not None:
                (st["whole"],) = _elementwise(make_fn_sum(2), [st["whole"], side.result[-1]], 1, "ar_add")
            narrow = st["stage"] < 2
            sums = [add_half(b, r, coord[ax].astype(jnp.int32).reshape(1), narrow=narrow, name=f"rs_add_{st['stage']}")
                    for b, r, ax in zip(st["bufs"], side.result, st["axes"])]
            st["bufs"] = [s[0] for s in sums]
            st["tx"] = [s[1] for s in sums] if narrow else None
        else:
            got_a, got_b = side.result
            g_final["w_in"][st["layer"]] = jnp.concatenate(
                [got_a.reshape(dm // 2, pw), got_b.reshape(dm // 2, pw)], axis=0)
            for n, b in zip(["w_out", "w_glu", "w_pa", "w_pb", "conv_w"], st["bufs"][2:]):
                g_final[n][st["layer"]] = b
        st["stage"] += 1

    grep = {n: [None] * depth for n in REPLICATED if n != "final_g"}
    pending = None
    for l in reversed(range(depth)):
        fw, sv = full[l], saved[l]
        dmerged = mm(dcur, fw["w_out"], tb=True, name="out_bwd_x")
        gsh["w_out"][l] = mm(sv["merged"], dcur, ta=True, scatter=("rows", slot_cxy), name="out_bwd_w")
        side = rs_side(pending)
        dla, dlb, dya, dyb, dbga, dbgb = rw_bwd(
            fn_merge, sv["mg_ins"], [[rows(dmerged, t)]], nrow=nrow, t=t,
            row_grads=[(0, False), (1, False), (2, False), (3, False)], bc_grads=[4, 5], side=side, name="merge_bwd")
        rs_absorb(pending, side)
        grep["b_gate"][l] = jnp.concatenate([dbga.reshape(dm), dbgb.reshape(dm)])
        dya_in = mm(dya, fw["w_pa"], tb=True, name="pa_bwd_x")
        gsh["w_pa"][l] = mm(sv["ya_in"], dya, ta=True, scatter=("cols", slot_cyx), name="pa_bwd_w")
        dyb_in = mm(dyb, fw["w_pb"], tb=True, name="pb_bwd_x")
        gsh["w_pb"][l] = mm(sv["yb_in"], dyb, ta=True, scatter=("cols", slot_cyx), name="pb_bwd_w")
        dys1, dglu, dzb, dbglu = rw_bwd(
            fn_glu, sv["gl_ins"], [[rows(dyb_in, t)]], nrow=nrow, t=t,
            row_grads=[(0, False), (1, False), (2, False)], bc_grads=[3], name="glugate_bwd")
        grep["b_glu"][l] = dbglu.reshape(wb)
        dys2 = mm(dglu, fw["w_glu"], tb=True, name="glu_bwd_x")
        gsh["w_glu"][l] = mm(sv["ys"], dglu, ta=True, scatter=("rows", slot_cxy), name="glu_bwd_w")
        dyd, du1, ddskip = rw_bwd(
            fn_gelu, sv["ge_ins"], [[rows(segment_major(dys1), t), rows(segment_major(dys2), t)]], nrow=nrow, t=t,
            row_grads=[(0, False), (2, False)], bc_grads=[3], name="gelu_bwd")
        grep["d_skip"][l] = ddskip.reshape(wb)
        side = rs_side(pending)
        du_seg, dw_s5, dcm_s5, dar, dai = s5_bwd(sv["u_seg"], *sv["mats"], sv["xs"], dyd, du1, side=side,
                                                 name="s5_bwd")
        du = time_major(du_seg)
        rs_absorb(pending, side)
        g_lr, g_li, g_ldt, g_br, g_bi, g_cr, g_ci = sv["mats_vjp"]((dar, dai, dw_s5, dcm_s5))
        for nme, val in zip(["lam_re", "lam_im", "log_dt", "b_re", "b_im", "c_re", "c_im"],
                            [g_lr, g_li, g_ldt, g_br, g_bi, g_cr, g_ci]):
            grep[nme][l] = val
        do, dza, dhn = rw_bwd(fn_headnorm, sv["hn_ins"], [[rows(dya_in, t, HEAD_DIM, 0, True)]], nrow=nrow, t=t,
                              ncol=nh, row_grads=[(0, True), (2, True)], bc_grads=[3], name="headnorm_bwd")
        grep["head_norm_g"][l] = jnp.sum(dhn, axis=0).reshape(HEAD_DIM)
        side = rs_side(pending)
        dqkv, dba_all, dal, ddt = gdn_bwd(sv["qkv"], sv["proj"], ba_blk, sv["alog_row"], sv["dtb_row"],
                                          sv["s_hist"], do, n_heads=nh, side=side, name="gdn_bwd")
        rs_absorb(pending, side)
        grep["a_log"][l] = jnp.sum(dal, axis=(0, 1))[2 * nh:4 * nh].reshape(2, nh)
        grep["dt_bias"][l] = jnp.sum(ddt, axis=(0, 1))[2 * nh:4 * nh].reshape(2, nh)
        (dba,) = rw_fwd(make_fn_sum(2), [rows3(dba_all, p, t, LANES, False) for p in range(2)],
                        [(LANES, False)], nrow=nrow, t=t, name="dba_sum")
        prep_ins = [rows(sv["proj"], nrow, HEAD_DIM, 0, True), bcast(fw["conv_w"], HEAD_DIM, 0, True)]
        dq_cots = [(dqkv.reshape(2, 3 * nrow, wa), (None, nrow, HEAD_DIM),
                    (lambda j, i, dd=dd: (dd, j // nh, j % nh))) for dd in range(2)]
        dqkv_raw, dconv = rw_bwd(make_fn_prep(nh), prep_ins, [dq_cots], nrow=nrow, t=nrow, ncol=3 * nh,
                                 row_grads=[(0, True)], bc_grads=[1], name="prep_bwd")
        gsh["conv_w"][l] = jnp.transpose(dconv, (1, 0, 2)).reshape(CONV_K, 3 * wa)
        dproj = jnp.concatenate([dqkv_raw, dza, du, dzb, dla, dlb, dba,
                                 jnp.zeros((nrow, projp - n_main - LANES), F32)], axis=1).astype(MXU_DT)
        side = rs_side(pending)
        dh = mm(dproj, fw["w_in"], tb=True, side=side, name="proj_bwd_x")
        rs_absorb(pending, side)
        gsh["w_in"][l] = [mm(sv["h"], dproj, ta=True, m_part=(part, 2), name="proj_bwd_w") for part in range(2)]
        dcur, dlng = rw_bwd(fn_rms, [rows(sv["x"], t), bcast(ln_g[l][None])], [[rows(dh, t)]], nrow=nrow, t=t,
                            row_grads=[(0, False)], bc_grads=[1], residual=rows(dcur, t), name="rms_bwd")
        grep["ln_g"][l] = dlng.reshape(dm)
        pending = rs_begin(l)
    grad_x = dcur[None]
    rep_list = [jnp.stack(grep[n]) for n in REPLICATED if n != "final_g"] + [dfinal.reshape(dm)]
    rep_shapes = [a.shape for a in rep_list]
    pending["whole"] = _pack(rep_list, COMM_COLS, ROW_TILE)
    for stage in range(RS_STAGES):
        side = rs_side(pending)
        run_side(side, name=f"rs_stage_{stage}")
        rs_absorb(pending, side)
    grads = {n: jnp.stack(g_final[n]) for n in SHARDED}
    for n, val in zip(REPLICATED, _unpack(pending["whole"], rep_shapes)):
        grads[n] = val

    deltas, new_m, new_v = {}, {}, {}
    for n in WEIGHTS:
        shp = wts[n].shape
        two = [a.reshape(-1, shp[-1]) for a in (wts[n], grads[n], mom_m[n], mom_v[n])]
        d_, m_, v_ = _elementwise(fn_adamw, two, 3, "adamw_" + n)
        deltas[n], new_m[n], new_v[n] = d_.reshape(shp), m_.reshape(shp), v_.reshape(shp)

    return (loss, grad_x, *[grads[n] for n in WEIGHTS], *[deltas[n] for n in WEIGHTS],
            *[new_m[n] for n in WEIGHTS], *[new_v[n] for n in WEIGHTS])
```

```python
import functools
import math

import jax
import jax.numpy as jnp
from jax import lax
from jax.experimental import pallas as pl
from jax.experimental.pallas import tpu as pltpu

F32 = jnp.float32
MXU_DT = jnp.bfloat16
WIRE_DT = jnp.bfloat16

HEAD_DIM = 128
CHUNK = 64
CONV_K = 5
S5_GROUP_CH = 16
S5_STATE = 64
S5_BLOCK_GROUPS = 8
S5_BLOCK_STATE = S5_BLOCK_GROUPS * S5_STATE
RMS_EPS = 1e-6
LANES = 128
VMEM_LIMIT = 56 * 1024 * 1024
ROW_TILE = 256
COMM_COLS = 1024
ADD_BLOCK_BYTES = 2 * 1024 * 1024
RS_STAGES = 4
ELEMENTWISE_BLOCK_BYTES = 1024 * 1024

ADAM_LR = 0.001
ADAM_B1 = 0.9
ADAM_B2 = 0.999
ADAM_EPS = 1e-08
ADAM_WD = 0.01
ADAM_STEP = 10

WEIGHTS = ['ln_g', 'w_in', 'conv_w', 'a_log', 'dt_bias', 'head_norm_g', 'lam_re', 'lam_im', 'log_dt',
           'b_re', 'b_im', 'c_re', 'c_im', 'd_skip', 'w_glu', 'b_glu', 'w_pa', 'w_pb', 'b_gate',
           'w_out', 'final_g']
SHARDED = ['w_in', 'w_glu', 'w_pa', 'w_pb', 'w_out', 'conv_w']
REPLICATED = [n for n in WEIGHTS if n not in SHARDED]
MESH = pl.DeviceIdType.MESH


def _params(sem=None):
    return pltpu.CompilerParams(dimension_semantics=sem, vmem_limit_bytes=VMEM_LIMIT)


def _tile(n, cap, q=LANES):
    t = (min(n, cap) // q) * q
    while t > q and n % t:
        t -= q
    return t if t > 0 and n % t == 0 else n


class Side:
    def __init__(self, ins, out_sd, sems, start, finish):
        self.ins, self.out_sd, self.sems, self.start, self.finish = list(ins), list(out_sd), list(sems), start, finish
        self.result = None


def _call(body, *, grid, in_specs, out_specs, out_shape, scratch_shapes=(), sem, name, args, side=None):
    in_specs, out_specs, out_shape = list(in_specs), list(out_specs), list(out_shape)
    scratch_shapes = list(scratch_shapes)
    if side is None:
        return pl.pallas_call(body, grid=grid, in_specs=in_specs, out_specs=out_specs, out_shape=out_shape,
                              scratch_shapes=scratch_shapes, compiler_params=_params(sem), name=name)(*args)
    hbm = pl.BlockSpec(memory_space=pl.ANY)
    n_in, n_out, n_scr = len(in_specs), len(out_specs), len(scratch_shapes)
    s_in, s_out = len(side.ins), len(side.out_sd)

    def hosted(*refs):
        main_in, rest = refs[:n_in], refs[n_in:]
        side_in, rest = rest[:s_in], rest[s_in:]
        main_out, rest = rest[:n_out], rest[n_out:]
        side_out, rest = rest[:s_out], rest[s_out:]
        main_scr, sems = rest[:n_scr], rest[n_scr:]
        ids = [pl.program_id(k) for k in range(len(grid))]
        first = functools.reduce(jnp.logical_and, [i == 0 for i in ids])
        last = functools.reduce(jnp.logical_and, [i == g - 1 for i, g in zip(ids, grid)])

        @pl.when(first)
        def _():
            side.start(side_in, side_out, sems)

        body(*main_in, *main_out, *main_scr)

        @pl.when(last)
        def _():
            side.finish(side_in, side_out, sems)

    outs = pl.pallas_call(
        hosted, grid=grid, in_specs=in_specs + [hbm] * s_in, out_specs=out_specs + [hbm] * s_out,
        out_shape=out_shape + side.out_sd, scratch_shapes=scratch_shapes + side.sems,
        compiler_params=_params(("arbitrary",) * len(grid)), name=name)(*args, *side.ins)
    side.result = list(outs[n_out:])
    return list(outs[:n_out])


def run_side(side, *, name):
    hbm = pl.BlockSpec(memory_space=pl.ANY)
    s_in, s_out = len(side.ins), len(side.out_sd)

    def body(*refs):
        side_in, side_out, sems = refs[:s_in], refs[s_in:s_in + s_out], refs[s_in + s_out:]
        side.start(side_in, side_out, sems)
        side.finish(side_in, side_out, sems)

    side.result = list(pl.pallas_call(body, out_shape=side.out_sd, in_specs=[hbm] * s_in, out_specs=[hbm] * s_out,
                                      scratch_shapes=side.sems, name=name)(*side.ins))
    return side.result


def slot_cxy(d):
    return 4 * (d % 2) + 2 * (d // 4) + (d // 2) % 2


def slot_cyx(d):
    return 4 * (d % 2) + 2 * ((d // 2) % 2) + d // 4


def mm(a, b, *, ta=False, tb=False, add=None, m_part=None, scatter=None, side=None, name):
    m = a.shape[1] if ta else a.shape[0]
    k = a.shape[0] if ta else a.shape[1]
    n = b.shape[0] if tb else b.shape[1]
    m_off = 0
    if m_part is not None:
        m = m // m_part[1]
        m_off = m_part[0]
    tm, tn, tk = _tile(m, 1024), _tile(n, 512), _tile(k, 2048)
    if scatter is not None and scatter[0] == "rows":
        tm = m // 8
    if scatter is not None and scatter[0] == "cols":
        tn = n // 8
    if m_part is not None:
        assert tm == m
    nk = k // tk
    dn = (((0 if ta else 1,), (1 if tb else 0,)), ((), ()))

    def body(*refs):
        if add is None:
            a_ref, b_ref, o_ref, acc = refs
        else:
            a_ref, b_ref, add_ref, o_ref, acc = refs
        kk = pl.program_id(2)

        @pl.when(kk == 0)
        def _():
            acc[...] = jnp.zeros_like(acc)

        acc[...] += lax.dot_general(a_ref[...].astype(MXU_DT), b_ref[...].astype(MXU_DT), dn,
                                    preferred_element_type=F32)

        @pl.when(kk == nk - 1)
        def _():
            r = acc[...]
            if add is not None:
                r = r + add_ref[...]
            o_ref[...] = r

    if ta:
        a_spec = pl.BlockSpec((tk, tm), lambda i, j, kk: (kk, i + m_off))
    else:
        a_spec = pl.BlockSpec((tm, tk), lambda i, j, kk: (i + m_off, kk))
    b_spec = pl.BlockSpec((tn, tk), lambda i, j, kk: (j, kk)) if tb else pl.BlockSpec((tk, tn), lambda i, j, kk: (kk, j))
    out_sd = jax.ShapeDtypeStruct((m, n), F32)
    if scatter is None:
        o_spec = pl.BlockSpec((tm, tn), lambda i, j, kk: (i, j))
    elif scatter[0] == "rows":
        o_spec = pl.BlockSpec((None, tm, tn), lambda i, j, kk: (scatter[1](i), 0, j))
        out_sd = jax.ShapeDtypeStruct((8, tm, n), F32)
    else:
        o_spec = pl.BlockSpec((None, tm, tn), lambda i, j, kk: (scatter[1](j), i, 0))
        out_sd = jax.ShapeDtypeStruct((8, m, tn), F32)
    ins, specs = [a, b], [a_spec, b_spec]
    if add is not None:
        ins.append(add)
        specs.append(o_spec)
    return _call(body, grid=(m // tm, n // tn, nk), in_specs=specs, out_specs=[o_spec], out_shape=[out_sd],
                 scratch_shapes=[pltpu.VMEM((tm, tn), F32)], sem=("parallel", "parallel", "arbitrary"),
                 name=name, args=ins, side=side)[0]


def rows(arr, t, width=None, base=0, per_j=False):
    width = arr.shape[1] if width is None else width
    return (arr, (t, width), lambda j, i: (i, base + (j if per_j else 0)))


def rows3(arr, lead, t, width, per_j=True):
    return (arr, (None, t, width), lambda j, i: (lead, i, j if per_j else 0))


def bcast(arr, width=None, base=0, per_j=False):
    width = arr.shape[1] if width is None else width
    return (arr, (arr.shape[0], width), lambda j, i: (0, base + (j if per_j else 0)))


def _specs(items):
    return [pl.BlockSpec(bs, im) for (_, bs, im) in items]


def rw_fwd(fn, ins, outs, *, nrow, t, ncol=1, name):
    out_specs = [pl.BlockSpec((t, w), (lambda j, i: (i, j)) if pj else (lambda j, i: (i, 0))) for (w, pj) in outs]
    out_shape = [jax.ShapeDtypeStruct((nrow, w * (ncol if pj else 1)), F32) for (w, pj) in outs]
    nin = len(ins)

    def body(*refs):
        j = pl.program_id(0)
        res = fn(j, *[r[...] for r in refs[:nin]])
        for o_ref, r in zip(refs[nin:], res):
            o_ref[...] = r

    return pl.pallas_call(
        body, grid=(ncol, nrow // t), in_specs=_specs(ins), out_specs=out_specs, out_shape=out_shape,
        compiler_params=_params(("parallel", "parallel")), name=name)(*[x[0] for x in ins])


def rw_bwd(fn, ins, cots, *, nrow, t, ncol=1, row_grads, bc_grads, residual=None, side=None, name):
    nin = len(ins)
    flat_cots = [c for group in cots for c in group]
    extra = [residual] if residual is not None else []
    out_specs, out_shape = [], []
    for idx, pj in row_grads:
        w = ins[idx][1][-1]
        out_specs.append(pl.BlockSpec((t, w), (lambda j, i: (i, j)) if pj else (lambda j, i: (i, 0))))
        out_shape.append(jax.ShapeDtypeStruct((nrow, w * (ncol if pj else 1)), F32))
    for idx in bc_grads:
        r, w = ins[idx][1]
        out_specs.append(pl.BlockSpec((None, r, w), lambda j, i: (j, 0, 0)))
        out_shape.append(jax.ShapeDtypeStruct((ncol, r, w), F32))

    def body(*refs):
        j = pl.program_id(0)
        i = pl.program_id(1)
        vals = [r[...] for r in refs[:nin]]
        pos = nin
        cts = []
        for group in cots:
            c = refs[pos][...]
            for q in range(1, len(group)):
                c = c + refs[pos + q][...]
            pos += len(group)
            cts.append(c)
        res_ref = refs[pos] if residual is not None else None
        pos += len(extra)
        outs = refs[pos:]
        _, vjp = jax.vjp(lambda *a: tuple(fn(j, *a)), *vals)
        grads = vjp(tuple(cts))
        for q, (idx, _) in enumerate(row_grads):
            g = grads[idx]
            if q == 0 and res_ref is not None:
                g = g + res_ref[...]
            outs[q][...] = g
        for q, idx in enumerate(bc_grads):
            o_ref = outs[len(row_grads) + q]

            @pl.when(i == 0)
            def _(o_ref=o_ref):
                o_ref[...] = jnp.zeros_like(o_ref)

            o_ref[...] += grads[idx]

    all_in = list(ins) + flat_cots + extra
    return _call(body, grid=(ncol, nrow // t), in_specs=_specs(all_in), out_specs=out_specs, out_shape=out_shape,
                 sem=("parallel", "arbitrary"), name=name, args=[x[0] for x in all_in], side=side)


def _silu(x):
    return x * jax.nn.sigmoid(x)


@jax.custom_vjp
def _softplus(x):
    return jnp.maximum(x, 0.0) + jnp.log1p(jnp.exp(-jnp.abs(x)))


def _softplus_fwd(x):
    return _softplus(x), x


def _softplus_bwd(x, ct):
    return (ct * jax.nn.sigmoid(x),)


_softplus.defvjp(_softplus_fwd, _softplus_bwd)


def _gelu(x):
    return 0.5 * x * (1.0 + jnp.tanh(math.sqrt(2.0 / math.pi) * (x + 0.044715 * (x * x * x))))


def _row_shift_impl(x, s):
    n = x.shape[0]
    if s == 0:
        return x
    rolled = pltpu.roll(x, (-s) % n, 0)
    t = lax.broadcasted_iota(jnp.int32, x.shape, 0)
    ok = (t + s >= 0) & (t + s < n)
    return jnp.where(ok, rolled, 0.0)


@functools.partial(jax.custom_vjp, nondiff_argnums=(1,))
def _row_shift(x, s):
    return _row_shift_impl(x, s)


def _row_shift_fwd(x, s):
    return _row_shift_impl(x, s), None


def _row_shift_bwd(s, _, ct):
    return (_row_shift_impl(ct, -s),)


_row_shift.defvjp(_row_shift_fwd, _row_shift_bwd)


def fn_rms(j, x, g):
    return (x * lax.rsqrt(jnp.mean(x * x, axis=-1, keepdims=True) + RMS_EPS) * g,)


def make_fn_prep(n_heads):
    pad = (CONV_K - 1) // 2

    def fn_prep(j, x, w):
        y = _row_shift(x, -pad) * w[0:1, :]
        for i in range(1, CONV_K):
            y = y + _row_shift(x, i - pad) * w[i:i + 1, :]
        a = _silu(y)
        scale = jnp.where(j < n_heads, HEAD_DIM ** -0.5, 1.0).astype(F32)
        nrm = a * lax.rsqrt(jnp.sum(a * a, axis=-1, keepdims=True) + RMS_EPS) * scale
        return (jnp.where(j < 2 * n_heads, nrm, a),)

    return fn_prep


def fn_headnorm(j, o0, o1, z, g):
    o = o0 + o1
    n = o * lax.rsqrt(jnp.mean(o * o, axis=-1, keepdims=True) + RMS_EPS) * g
    return (n * _silu(z),)


def fn_gelu(j, y0, y1, u, dsk):
    return (_gelu(y0 + y1 + u * dsk),)


def fn_glu(j, ys, logit, z, b):
    return (ys * jax.nn.sigmoid(logit + b) * _silu(z),)


def fn_merge(j, la, lb, ya, yb, ba, bb):
    return (jax.nn.sigmoid(la + ba) * ya + jax.nn.sigmoid(lb + bb) * yb,)


def fn_loss(j, x, t, g):
    y = x * lax.rsqrt(jnp.mean(x * x, axis=-1, keepdims=True) + RMS_EPS) * g
    e = y - t
    return (0.5 * jnp.mean(e * e, axis=-1, keepdims=True),)


def make_fn_sum(n):
    def fn_sum(j, *xs):
        s = xs[0]
        for q in range(1, n):
            s = s + xs[q]
        return (s,)

    return fn_sum


def fn_adamw(j, w, g, m, v):
    m2 = ADAM_B1 * m + (1.0 - ADAM_B1) * g
    v2 = ADAM_B2 * v + (1.0 - ADAM_B2) * (g * g)
    m_hat = m2 / (1.0 - ADAM_B1 ** ADAM_STEP)
    v_hat = v2 / (1.0 - ADAM_B2 ** ADAM_STEP)
    delta = -ADAM_LR * (m_hat / (jnp.sqrt(v_hat) + ADAM_EPS) + ADAM_WD * w)
    return delta, m2, v2


def _dot(a, b, dims, precision=None):
    return lax.dot_general(a, b, (dims, ((), ())), precision=precision, preferred_element_type=F32)


_NN = ((1,), (0,))
_NT = ((1,), (1,))
_TN = ((0,), (0,))
GDN_HEAD_BLOCK = 4
INVERSE_WIDE_FACTORS = 3


def _split(a):
    hi = a.astype(MXU_DT)
    return hi, (a - hi.astype(F32)).astype(MXU_DT)


def _dot3s(a, b, dims):
    (ah, al), (bh, bl) = a, b
    return _dot(ah, bh, dims) + (_dot(al, bh, dims) + _dot(ah, bl, dims))


def _dot3(a, b, dims):
    return _dot3s(_split(a), _split(b), dims)


def _unit_inverse(lmat):
    r = lmat.shape[0]
    eye = (lax.broadcasted_iota(jnp.int32, (r, r), 0) == lax.broadcasted_iota(jnp.int32, (r, r), 1)).astype(F32)
    pw = -lmat
    tinv = eye + pw
    pws = _split(pw)
    n_fact = int(math.ceil(math.log2(CHUNK))) - 1
    for k in range(n_fact):
        if k < INVERSE_WIDE_FACTORS:
            pws = _split(_dot3s(pws, pws, _NN))
            tinv = tinv + _dot3s(_split(tinv), pws, _NN)
        else:
            pw = _dot(pws[0], pws[0], _NN)
            pws = (pw.astype(MXU_DT), None)
            tinv = tinv + _dot(tinv.astype(MXU_DT), pws[0], _NN)
    return tinv


@jax.custom_vjp
def tri_apply(lmat, rhs, tinv):
    return _dot3(tinv, rhs, _NN)


def _tri_apply_fwd(lmat, rhs, tinv):
    x = _dot3(tinv, rhs, _NN)
    return x, (tinv, x)


def _tri_apply_bwd(res, dx):
    tinv, x = res
    drhs = _dot3(tinv, dx, _TN)
    return -_dot3(drhs, x, _NT), drhs, jnp.zeros_like(tinv)


tri_apply.defvjp(_tri_apply_fwd, _tri_apply_bwd)


def gdn_group(s, q, k, v, ba, alog, dtb, *, d, head0, n_heads, tinv=None):
    hb = len(s)
    c = q.shape[0]
    r = hb * c

    def stack(x):
        return jnp.concatenate([x[:, i * HEAD_DIM:(i + 1) * HEAD_DIM] for i in range(hb)], axis=0)

    def pick(x, lane0):
        lane = lax.broadcasted_iota(jnp.int32, x.shape, 1)
        return jnp.concatenate(
            [jnp.sum(jnp.where(lane == lane0 + i, x, 0.0), axis=1, keepdims=True) for i in range(hb)], axis=0)

    q4, k4, v4 = stack(q), stack(k), stack(v)
    beta = pick(jax.nn.sigmoid(ba), d * n_heads + head0)
    g = pick(-jnp.exp(alog) * _softplus(ba + dtb), 2 * n_heads + d * n_heads + head0)
    ii = lax.broadcasted_iota(jnp.int32, (r, r), 0)
    jj = lax.broadcasted_iota(jnp.int32, (r, r), 1)
    same = (ii // c) == (jj // c)
    rel = (ii - jj) * (1 - 2 * d)
    incl = same & (rel >= 0)
    strict = same & (rel > 0)
    incl_t = same & (rel <= 0)
    g_row = jnp.sum(jnp.where(ii == jj, g, 0.0), axis=0, keepdims=True)
    gc_col = jnp.sum(jnp.where(incl, g_row, 0.0), axis=1, keepdims=True)
    gc_row = jnp.sum(jnp.where(incl_t, g, 0.0), axis=0, keepdims=True)
    g_tot = jnp.sum(jnp.where(same, g_row, 0.0), axis=1, keepdims=True)
    decay = jnp.exp(jnp.where(incl, gc_col - gc_row, -1e30))
    kb = k4 * beta
    vb = v4 * beta
    lmat = jnp.where(strict, _dot(kb, k4, _NT) * decay, 0.0)
    fresh = tinv is None
    if fresh:
        tinv = _unit_inverse(lmat)
    uw = tri_apply(lmat, jnp.concatenate([vb, kb * jnp.exp(gc_col)], axis=1), tinv)
    u, w = uw[:, :HEAD_DIM], uw[:, HEAD_DIM:]
    qk = _dot(q4, k4, _NT) * decay
    qe = q4 * jnp.exp(gc_col)
    kd = k4 * jnp.exp(g_tot - gc_col)
    eg = jnp.exp(g_tot)
    v_new, o_s = [], []
    for i in range(hb):
        rs = slice(i * c, (i + 1) * c)
        ws = _dot(jnp.concatenate([w[rs], qe[rs]], axis=0), s[i], _NN)
        v_new.append(u[rs] - ws[:c])
        o_s.append(ws[c:])
    o4 = jnp.concatenate(o_s, axis=0) + _dot(qk, jnp.concatenate(v_new, axis=0), _NN)
    s_new = tuple(s[i] * eg[i * c:i * c + 1, :] + _dot(kd[i * c:(i + 1) * c], v_new[i], _TN) for i in range(hb))
    o = jnp.concatenate([o4[i * c:(i + 1) * c] for i in range(hb)], axis=1)
    return (s_new, o, tinv) if fresh else (s_new, o)


def _gdn_maps(n_chunks):
    def chunk_of(d, step):
        return step + d * (n_chunks - 1 - 2 * step)
    return chunk_of


def _heads(ref, h0, n):
    return ref[:, h0 * HEAD_DIM:(h0 + n) * HEAD_DIM]


def gdn_fwd(qkv, proj, ba_blk, alog_row, dtb_row, *, n_heads, side=None, name):
    nrow = qkv.shape[0]
    nc = nrow // CHUNK
    h = n_heads
    wa = h * HEAD_DIM
    hb = math.gcd(h, GDN_HEAD_BLOCK)
    chunk_of = _gdn_maps(nc)

    def body(q_ref, k_ref, v_ref, ba_ref, al_ref, dt_ref, o_ref, sh_ref, ti_ref, s_scr):
        d = pl.program_id(0)
        n = pl.program_id(1)

        @pl.when(n == 0)
        def _():
            s_scr[...] = jnp.zeros_like(s_scr)

        ba, al, dt = ba_ref[...], al_ref[...], dt_ref[...]
        for h0 in range(0, h, hb):
            s = tuple(s_scr[h0 + i] for i in range(hb))
            for i in range(hb):
                sh_ref[h0 + i] = s[i]
            s2, o, tinv = gdn_group(s, _heads(q_ref, h0, hb), _heads(k_ref, h0, hb), _heads(v_ref, h0, hb),
                                    ba, al, dt, d=d, head0=h0, n_heads=h)
            o_ref[:, h0 * HEAD_DIM:(h0 + hb) * HEAD_DIM] = o
            ti_ref[h0 // hb] = tinv
            for i in range(hb):
                s_scr[h0 + i] = s2[i]

    blk = (CHUNK, wa)
    in_specs = [
        pl.BlockSpec(blk, lambda d, n: (chunk_of(d, n), 0)),
        pl.BlockSpec(blk, lambda d, n: (chunk_of(d, n), 1)),
        pl.BlockSpec(blk, lambda d, n: (chunk_of(d, n), 2)),
        pl.BlockSpec((CHUNK, LANES), lambda d, n: (chunk_of(d, n), ba_blk)),
        pl.BlockSpec((1, LANES), lambda d, n: (0, 0)),
        pl.BlockSpec((1, LANES), lambda d, n: (0, 0)),
    ]
    ng, r = h // hb, hb * CHUNK
    out_specs = [
        pl.BlockSpec((None, CHUNK, wa), lambda d, n: (d, chunk_of(d, n), 0)),
        pl.BlockSpec((h, None, HEAD_DIM, HEAD_DIM), lambda d, n: (d, n, 0, 0)),
        pl.BlockSpec((None, None, ng, r, r), lambda d, n: (d, n, 0, 0, 0)),
    ]
    out_shape = [jax.ShapeDtypeStruct((2, nrow, wa), F32),
                 jax.ShapeDtypeStruct((2 * h, nc, HEAD_DIM, HEAD_DIM), F32),
                 jax.ShapeDtypeStruct((2, nc, ng, r, r), F32)]
    return _call(body, grid=(2, nc), in_specs=in_specs, out_specs=out_specs, out_shape=out_shape,
                 scratch_shapes=[pltpu.VMEM((h, HEAD_DIM, HEAD_DIM), F32)], sem=("parallel", "arbitrary"),
                 name=name, args=(qkv, qkv, qkv, proj, alog_row, dtb_row), side=side)


def gdn_bwd(qkv, proj, ba_blk, alog_row, dtb_row, s_hist, t_hist, do, *, n_heads, side=None, name):
    nrow = qkv.shape[0]
    nc = nrow // CHUNK
    h = n_heads
    wa = h * HEAD_DIM
    hb = math.gcd(h, GDN_HEAD_BLOCK)
    chunk_of = _gdn_maps(nc)

    def cb(d, n):
        return chunk_of(d, nc - 1 - n)

    def body(q_ref, k_ref, v_ref, ba_ref, al_ref, dt_ref, sh_ref, do_ref, ti_ref,
             dqkv_ref, dba_ref, dal_ref, ddt_ref, ds_scr):
        d = pl.program_id(0)
        n = pl.program_id(1)

        @pl.when(n == 0)
        def _():
            ds_scr[...] = jnp.zeros_like(ds_scr)
            dal_ref[...] = jnp.zeros_like(dal_ref)
            ddt_ref[...] = jnp.zeros_like(ddt_ref)

        ba, al, dt = ba_ref[...], al_ref[...], dt_ref[...]
        dba_sum = jnp.zeros_like(ba)
        dal_sum = jnp.zeros_like(al)
        ddt_sum = jnp.zeros_like(dt)
        for h0 in range(0, h, hb):
            f = functools.partial(gdn_group, d=d, head0=h0, n_heads=h, tinv=ti_ref[h0 // hb])
            s = tuple(sh_ref[h0 + i] for i in range(hb))
            _, vjp = jax.vjp(f, s, _heads(q_ref, h0, hb), _heads(k_ref, h0, hb), _heads(v_ref, h0, hb), ba, al, dt)
            ds, dq, dk, dv, dba, dal, ddt = vjp((tuple(ds_scr[h0 + i] for i in range(hb)), _heads(do_ref, h0, hb)))
            for i in range(hb):
                ds_scr[h0 + i] = ds[i]
            cols = slice(h0 * HEAD_DIM, (h0 + hb) * HEAD_DIM)
            dqkv_ref[0, :, cols] = dq
            dqkv_ref[1, :, cols] = dk
            dqkv_ref[2, :, cols] = dv
            dba_sum = dba_sum + dba
            dal_sum = dal_sum + dal
            ddt_sum = ddt_sum + ddt
        dba_ref[...] = dba_sum
        dal_ref[...] += dal_sum
        ddt_ref[...] += ddt_sum

    blk = (CHUNK, wa)
    in_specs = [
        pl.BlockSpec(blk, lambda d, n: (cb(d, n), 0)),
        pl.BlockSpec(blk, lambda d, n: (cb(d, n), 1)),
        pl.BlockSpec(blk, lambda d, n: (cb(d, n), 2)),
        pl.BlockSpec((CHUNK, LANES), lambda d, n: (cb(d, n), ba_blk)),
        pl.BlockSpec((1, LANES), lambda d, n: (0, 0)),
        pl.BlockSpec((1, LANES), lambda d, n: (0, 0)),
        pl.BlockSpec((h, None, HEAD_DIM, HEAD_DIM), lambda d, n: (d, nc - 1 - n, 0, 0)),
        pl.BlockSpec(blk, lambda d, n: (cb(d, n), 0)),
        pl.BlockSpec((None, None, h // hb, hb * CHUNK, hb * CHUNK), lambda d, n: (d, nc - 1 - n, 0, 0, 0)),
    ]
    out_specs = [
        pl.BlockSpec((None, 3, CHUNK, wa), lambda d, n: (d, 0, cb(d, n), 0)),
        pl.BlockSpec((None, CHUNK, LANES), lambda d, n: (d, cb(d, n), 0)),
        pl.BlockSpec((None, 1, LANES), lambda d, n: (d, 0, 0)),
        pl.BlockSpec((None, 1, LANES), lambda d, n: (d, 0, 0)),
    ]
    out_shape = [jax.ShapeDtypeStruct((2, 3, nrow, wa), F32),
                 jax.ShapeDtypeStruct((2, nrow, LANES), F32),
                 jax.ShapeDtypeStruct((2, 1, LANES), F32),
                 jax.ShapeDtypeStruct((2, 1, LANES), F32)]
    return _call(body, grid=(2, nc), in_specs=in_specs, out_specs=out_specs, out_shape=out_shape,
                 scratch_shapes=[pltpu.VMEM((h, HEAD_DIM, HEAD_DIM), F32)], sem=("parallel", "arbitrary"),
                 name=name, args=(qkv, qkv, qkv, proj, alog_row, dtb_row, s_hist, do, t_hist), side=side)


def s5_mats(lam_re, lam_im, log_dt, b_re, b_im, c_re, c_im):
    g = lam_re.shape[1]
    nb = g // S5_BLOCK_GROUPS
    dt = jnp.exp(log_dt)[..., None]
    mag = jnp.exp(lam_re * dt)
    ar = mag * jnp.cos(lam_im * dt)
    ai = mag * jnp.sin(lam_im * dt)
    den = lam_re * lam_re + lam_im * lam_im
    fr = ((ar - 1.0) * lam_re + ai * lam_im) / den
    fi = (ai * lam_re - (ar - 1.0) * lam_im) / den
    bbr = fr[..., None] * b_re - fi[..., None] * b_im
    bbi = fr[..., None] * b_im + fi[..., None] * b_re
    eye = jnp.eye(S5_BLOCK_GROUPS, dtype=F32)
    shp = (2, nb, S5_BLOCK_GROUPS, S5_STATE, S5_GROUP_CH)
    w_r = jnp.einsum('dsjpc,jk->dsjckp', bbr.reshape(shp), eye).reshape(2, nb, LANES, S5_BLOCK_STATE)
    w_i = jnp.einsum('dsjpc,jk->dsjckp', bbi.reshape(shp), eye).reshape(2, nb, LANES, S5_BLOCK_STATE)
    w = jnp.concatenate([w_r, w_i], axis=-1)
    shc = (2, nb, S5_BLOCK_GROUPS, S5_GROUP_CH, S5_STATE)
    c_r = jnp.einsum('dsjcp,jk->dsjpkc', c_re.reshape(shc), eye).reshape(2, nb, S5_BLOCK_STATE, LANES)
    c_i = jnp.einsum('dsjcp,jk->dsjpkc', c_im.reshape(shc), eye).reshape(2, nb, S5_BLOCK_STATE, LANES)
    cm = jnp.concatenate([c_r, -c_i], axis=-2)
    return (ar.reshape(2, nb, 1, S5_BLOCK_STATE), ai.reshape(2, nb, 1, S5_BLOCK_STATE), w, cm)


_S5_ROWS = 512
S5_SEGMENTS = 8


def segment_major(a):
    n, c = a.shape
    return jnp.transpose(a.reshape(S5_SEGMENTS, n // S5_SEGMENTS, c), (1, 0, 2)).reshape(n, c)


def time_major(a):
    n, c = a.shape
    return jnp.transpose(a.reshape(n // S5_SEGMENTS, S5_SEGMENTS, c), (1, 0, 2)).reshape(n, c)


def _segmented_scan(x_ref, a_r, a_i, *, reverse, visit=None):
    nrow, two_hs = x_ref.shape
    hs = two_hs // 2
    steps = nrow // S5_SEGMENTS
    assert steps & (steps - 1) == 0
    b_r = jnp.broadcast_to(a_r, (S5_SEGMENTS, hs))
    b_i = jnp.broadcast_to(a_i, (S5_SEGMENTS, hs))
    rid = lax.broadcasted_iota(jnp.int32, (S5_SEGMENTS, hs), 0)

    def rows_of(i):
        return pl.ds(pl.multiple_of((steps - 1 - i if reverse else i) * S5_SEGMENTS, S5_SEGMENTS), S5_SEGMENTS)

    def local(i, carry):
        s_r, s_i = carry
        rows = rows_of(i)
        n_r = b_r * s_r - b_i * s_i + x_ref[rows, pl.ds(0, hs)]
        n_i = b_r * s_i + b_i * s_r + x_ref[rows, pl.ds(hs, hs)]
        x_ref[rows, pl.ds(0, hs)] = n_r
        x_ref[rows, pl.ds(hs, hs)] = n_i
        return n_r, n_i

    z = jnp.zeros((S5_SEGMENTS, hs), F32)
    e_r, e_i = lax.fori_loop(0, steps, local, (z, z))
    q_r, q_i = a_r, a_i
    for _ in range(steps.bit_length() - 1):
        q_r, q_i = q_r * q_r - q_i * q_i, 2.0 * q_r * q_i
    c_r = jnp.zeros((1, hs), F32)
    c_i = jnp.zeros((1, hs), F32)
    en_r, en_i = z, z
    for s in (reversed(range(S5_SEGMENTS)) if reverse else range(S5_SEGMENTS)):
        en_r = jnp.where(rid == s, c_r, en_r)
        en_i = jnp.where(rid == s, c_i, en_i)
        c_r, c_i = (e_r[s:s + 1] + q_r * c_r - q_i * c_i, e_i[s:s + 1] + q_r * c_i + q_i * c_r)

    def fix(i, carry):
        p_r, p_i = carry
        rows = rows_of(i)
        x_r = x_ref[rows, pl.ds(0, hs)] + (p_r * en_r - p_i * en_i)
        x_i = x_ref[rows, pl.ds(hs, hs)] + (p_r * en_i + p_i * en_r)
        x_ref[rows, pl.ds(0, hs)] = x_r
        x_ref[rows, pl.ds(hs, hs)] = x_i
        if visit is not None:
            visit(steps - 1 - i if reverse else i, rows, x_r, x_i)
        return p_r * b_r - p_i * b_i, p_r * b_i + p_i * b_r

    return lax.fori_loop(0, steps, fix, (b_r, b_i))


def s5_fwd(u_seg, ar, ai, w, cm, *, name):
    nrow = u_seg.shape[0]
    nb = w.shape[1]
    hs = S5_BLOCK_STATE
    rs = min(_S5_ROWS, nrow)

    def body(u_ref, ar_ref, ai_ref, w_ref, cm_ref, y_ref, x_ref):
        d = pl.program_id(0)
        wb = w_ref[...].astype(MXU_DT)
        for r0 in range(0, nrow, rs):
            x_ref[pl.ds(r0, rs), :] = _dot(u_ref[pl.ds(r0, rs), :].astype(MXU_DT), wb, _NN)
        @pl.when(d == 0)
        def _():
            _segmented_scan(x_ref, ar_ref[...], ai_ref[...], reverse=False)

        @pl.when(d == 1)
        def _():
            _segmented_scan(x_ref, ar_ref[...], ai_ref[...], reverse=True)

        cb = cm_ref[...].astype(MXU_DT)
        for r0 in range(0, nrow, rs):
            y_ref[pl.ds(r0, rs), :] = _dot(x_ref[pl.ds(r0, rs), :].astype(MXU_DT), cb, _NN)

    in_specs = [
        pl.BlockSpec((nrow, LANES), lambda d, s: (0, s)),
        pl.BlockSpec((None, None, 1, hs), lambda d, s: (d, s, 0, 0)),
        pl.BlockSpec((None, None, 1, hs), lambda d, s: (d, s, 0, 0)),
        pl.BlockSpec((None, None, LANES, 2 * hs), lambda d, s: (d, s, 0, 0)),
        pl.BlockSpec((None, None, 2 * hs, LANES), lambda d, s: (d, s, 0, 0)),
    ]
    out_specs = [
        pl.BlockSpec((None, nrow, LANES), lambda d, s: (d, 0, s)),
        pl.BlockSpec((None, nrow, 2 * hs), lambda d, s: (d, 0, s)),
    ]
    out_shape = [jax.ShapeDtypeStruct((2, nrow, nb * LANES), F32),
                 jax.ShapeDtypeStruct((2, nrow, nb * 2 * hs), F32)]
    return pl.pallas_call(
        body, grid=(2, nb), in_specs=in_specs, out_specs=out_specs, out_shape=out_shape,
        compiler_params=_params(("parallel", "parallel")), name=name)(u_seg, ar, ai, w, cm)


def s5_bwd(u_seg, ar, ai, w, cm, xs, dy, du_in, *, side=None, name):
    nrow = u_seg.shape[0]
    nb = w.shape[1]
    steps = nrow // S5_SEGMENTS
    hs = S5_BLOCK_STATE
    rs = min(_S5_ROWS, nrow)

    def body(u_ref, dy_ref, dui_ref, x_ref, ar_ref, ai_ref, w_ref, cm_ref,
             du_ref, dw_ref, dcm_ref, dar_ref, dai_ref, g_ref, acc_ref):
        d = pl.program_id(1)
        cb = cm_ref[...].astype(MXU_DT)
        for r0 in range(0, nrow, rs):
            g_ref[pl.ds(r0, rs), :] = _dot(dy_ref[pl.ds(r0, rs), :].astype(MXU_DT), cb, _NT)
        acc_ref[...] = jnp.zeros_like(acc_ref)
        rid = lax.broadcasted_iota(jnp.int32, (S5_SEGMENTS, 2 * hs), 0)

        def run(forward_recurrence):
            last = pl.ds((steps - 1) * S5_SEGMENTS, S5_SEGMENTS)
            first = pl.ds(0, S5_SEGMENTS)
            if forward_recurrence:
                wrap = jnp.where(rid == 0, 0.0, pltpu.roll(x_ref[last, :], 1, 0))
            else:
                wrap = jnp.where(rid == S5_SEGMENTS - 1, 0.0, pltpu.roll(x_ref[first, :], S5_SEGMENTS - 1, 0))

            def visit(step, rows, g_r, g_i):
                if forward_recurrence:
                    nbr = jnp.maximum(step - 1, 0)
                    edge = step == 0
                else:
                    nbr = jnp.minimum(step + 1, steps - 1)
                    edge = step == steps - 1
                prev = x_ref[pl.ds(pl.multiple_of(nbr * S5_SEGMENTS, S5_SEGMENTS), S5_SEGMENTS), :]
                prev = jnp.where(edge, wrap, prev)
                p_r, p_i = prev[:, :hs], prev[:, hs:]
                acc_ref[:, pl.ds(0, hs)] += g_r * p_r + g_i * p_i
                acc_ref[:, pl.ds(hs, hs)] += g_i * p_r - g_r * p_i

            _segmented_scan(g_ref, ar_ref[...], -ai_ref[...], reverse=forward_recurrence, visit=visit)

        @pl.when(d == 0)
        def _():
            run(True)

        @pl.when(d == 1)
        def _():
            run(False)

        dar_ref[...] = jnp.sum(acc_ref[:, pl.ds(0, hs)], axis=0, keepdims=True)
        dai_ref[...] = jnp.sum(acc_ref[:, pl.ds(hs, hs)], axis=0, keepdims=True)

        wb = w_ref[...].astype(MXU_DT)
        for r0 in range(0, nrow, rs):
            part = _dot(g_ref[pl.ds(r0, rs), :].astype(MXU_DT), wb, _NT)

            @pl.when(d == 0)
            def _(part=part, r0=r0):
                du_ref[pl.ds(r0, rs), :] = dui_ref[pl.ds(r0, rs), :] + part

            @pl.when(d == 1)
            def _(part=part, r0=r0):
                du_ref[pl.ds(r0, rs), :] += part

        dw_ref[...] = _dot(u_ref[...].astype(MXU_DT), g_ref[...].astype(MXU_DT), _TN)
        dcm_ref[...] = _dot(x_ref[...].astype(MXU_DT), dy_ref[...].astype(MXU_DT), _TN)

    in_specs = [
        pl.BlockSpec((nrow, LANES), lambda s, d: (0, s)),
        pl.BlockSpec((nrow, LANES), lambda s, d: (0, s)),
        pl.BlockSpec((nrow, LANES), lambda s, d: (0, s)),
        pl.BlockSpec((None, nrow, 2 * hs), lambda s, d: (d, 0, s)),
        pl.BlockSpec((None, None, 1, hs), lambda s, d: (d, s, 0, 0)),
        pl.BlockSpec((None, None, 1, hs), lambda s, d: (d, s, 0, 0)),
        pl.BlockSpec((None, None, LANES, 2 * hs), lambda s, d: (d, s, 0, 0)),
        pl.BlockSpec((None, None, 2 * hs, LANES), lambda s, d: (d, s, 0, 0)),
    ]
    out_specs = [
        pl.BlockSpec((nrow, LANES), lambda s, d: (0, s)),
        pl.BlockSpec((None, None, LANES, 2 * hs), lambda s, d: (d, s, 0, 0)),
        pl.BlockSpec((None, None, 2 * hs, LANES), lambda s, d: (d, s, 0, 0)),
        pl.BlockSpec((None, None, 1, hs), lambda s, d: (d, s, 0, 0)),
        pl.BlockSpec((None, None, 1, hs), lambda s, d: (d, s, 0, 0)),
    ]
    out_shape = [jax.ShapeDtypeStruct((nrow, nb * LANES), F32),
                 jax.ShapeDtypeStruct(w.shape, F32), jax.ShapeDtypeStruct(cm.shape, F32),
                 jax.ShapeDtypeStruct(ar.shape, F32), jax.ShapeDtypeStruct(ai.shape, F32)]
    return _call(body, grid=(nb, 2), in_specs=in_specs, out_specs=out_specs, out_shape=out_shape,
                 scratch_shapes=[pltpu.VMEM((nrow, 2 * hs), F32), pltpu.VMEM((S5_SEGMENTS, 2 * hs), F32)],
                 sem=("parallel", "arbitrary"), name=name, args=(u_seg, dy, du_in, xs, ar, ai, w, cm), side=side)


def _me():
    return lax.axis_index("x"), lax.axis_index("y"), lax.axis_index("c")


def all_gather(shards):
    na = len(shards)

    def plan(x_refs, out_refs, sems):
        send_sems, recv_sems, local_sems = sems
        x, y, c = _me()
        me, sibling = (x, y, c), (x, y, 1 - c)
        chips = [(1 - x, y), (x, 1 - y), (1 - x, 1 - y)]

        def slot(a, px, py, pc):
            return out_refs[a].at[4 * px + 2 * py + pc]

        def copy(a, k, block, to, src=None):
            return pltpu.make_async_remote_copy(
                src_ref=slot(a, *block) if src is None else src, dst_ref=slot(a, *block),
                send_sem=send_sems.at[a, k], recv_sem=recv_sems.at[a, k], device_id=to, device_id_type=MESH)

        mine = [pltpu.make_async_copy(x_refs[a], slot(a, *me), local_sems.at[a]) for a in range(na)]
        first = []
        for a in range(na):
            first.append(copy(a, 0, me, sibling, src=x_refs[a]))
            first += [copy(a, 1 + j, me, (*chip, c), src=x_refs[a]) for j, chip in enumerate(chips)]
        return me, sibling, chips, c, copy, mine, first

    def start(x_refs, out_refs, sems):
        _, _, _, _, _, mine, first = plan(x_refs, out_refs, sems)
        for cp in mine + first:
            cp.start()

    def finish(x_refs, out_refs, sems):
        me, sibling, chips, c, copy, mine, first = plan(x_refs, out_refs, sems)
        passed = []
        for j, chip in enumerate(chips):
            for a in range(na):
                copy(a, 1 + j, (*chip, c), me).wait_recv()
                fwd = copy(a, 4 + j, (*chip, c), sibling)
                fwd.start()
                passed.append(fwd)
        for a in range(na):
            copy(a, 0, sibling, me).wait_recv()
            for j, chip in enumerate(chips):
                copy(a, 4 + j, (*chip, 1 - c), me).wait_recv()
        for cp in first + passed:
            cp.wait_send()
        for cp in mine:
            cp.wait()

    return Side(shards, [jax.ShapeDtypeStruct((8,) + s.shape, s.dtype) for s in shards],
                [pltpu.SemaphoreType.DMA((na, 7)), pltpu.SemaphoreType.DMA((na, 7)), pltpu.SemaphoreType.DMA((na,))],
                start, finish)


_AXES = ("x", "y", "c")


def exchange(bufs, axes, *, half):
    na = len(bufs)
    if isinstance(half, bool):
        half = [half] * na

    def copies(in_refs, out_refs, sems):
        send_sems, recv_sems = sems
        me = _me()
        cps = []
        for a in range(na):
            bit = me[_AXES.index(axes[a])]
            peer = tuple(1 - v if ax == axes[a] else v for ax, v in zip(_AXES, me))
            cps.append(pltpu.make_async_remote_copy(
                src_ref=in_refs[a].at[1 - bit] if half[a] else in_refs[a], dst_ref=out_refs[a],
                send_sem=send_sems.at[a], recv_sem=recv_sems.at[a], device_id=peer, device_id_type=MESH))
        return cps

    def start(*refs):
        for cp in copies(*refs):
            cp.start()

    def finish(*refs):
        for cp in copies(*refs):
            cp.wait()

    return Side(bufs, [jax.ShapeDtypeStruct(b.shape[1:] if h else b.shape, b.dtype) for b, h in zip(bufs, half)],
                [pltpu.SemaphoreType.DMA((na,)), pltpu.SemaphoreType.DMA((na,))], start, finish)


def all_to_all(pieces, slot_fns):
    na = len(pieces)

    def copies(in_refs, out_refs, sems):
        send_sems, recv_sems, local_sems = sems
        x, y, c = _me()
        cps = []
        for a in range(na):
            mine = slot_fns[a](x, y, c)
            for k in range(8):
                tx, ty, tc = x ^ (k // 4), y ^ ((k // 2) % 2), c ^ (k % 2)
                src = in_refs[a].at[4 * tx + 2 * ty + tc]
                dst = out_refs[a].at[mine]
                if k == 0:
                    cps.append(pltpu.make_async_copy(src, dst, local_sems.at[a]))
                else:
                    cps.append(pltpu.make_async_remote_copy(
                        src_ref=src, dst_ref=dst, send_sem=send_sems.at[a, k - 1], recv_sem=recv_sems.at[a, k - 1],
                        device_id=(tx, ty, tc), device_id_type=MESH))
        return cps

    def start(*refs):
        for cp in copies(*refs):
            cp.start()

    def finish(*refs):
        for cp in copies(*refs):
            cp.wait()

    return Side(pieces, [jax.ShapeDtypeStruct(p.shape, p.dtype) for p in pieces],
                [pltpu.SemaphoreType.DMA((na, 7)), pltpu.SemaphoreType.DMA((na, 7)), pltpu.SemaphoreType.DMA((na,))],
                start, finish)


def add_half(buf, recv, bit, *, narrow, name):
    c = recv.shape[-1]
    r = math.prod(recv.shape[:-1])
    t = _tile(r, max(16, (ADD_BLOCK_BYTES // (4 * c)) // 16 * 16), 16)

    def body(bit_ref, a_ref, b_ref, o_ref, *tx_ref):
        s = a_ref[...] + b_ref[...].astype(F32)
        o_ref[...] = s
        if narrow:
            tx_ref[0][...] = s.astype(WIRE_DT)

    o_spec = pl.BlockSpec((t, c), lambda i, b: (i, 0))
    grid_spec = pltpu.PrefetchScalarGridSpec(
        num_scalar_prefetch=1, grid=(r // t,),
        in_specs=[pl.BlockSpec((None, t, c), lambda i, b: (b[0], i, 0)), o_spec],
        out_specs=[o_spec, o_spec] if narrow else [o_spec])
    out_shape = [jax.ShapeDtypeStruct((r, c), F32)] + ([jax.ShapeDtypeStruct((r, c), WIRE_DT)] if narrow else [])
    outs = pl.pallas_call(body, grid_spec=grid_spec, out_shape=out_shape,
                          compiler_params=_params(("parallel",)), name=name)(
                              bit, buf.reshape(2, r, c), recv.reshape(r, c))
    return [o.reshape(recv.shape) for o in outs]


def _pack_rows(n, cols):
    return -(-n // (8 * cols)) * 8


def _pack(arrs, cols, mult):
    parts = []
    for a in arrs:
        n = math.prod(a.shape)
        nr = _pack_rows(n, cols)
        parts.append(jnp.pad(a.reshape(-1), (0, nr * cols - n)).reshape(nr, cols))
    total = sum(p.shape[0] for p in parts)
    pad_rows = -(-total // mult) * mult - total
    if pad_rows:
        parts.append(jnp.zeros((pad_rows, cols), arrs[0].dtype))
    return jnp.concatenate(parts, axis=0)


def _unpack(buf, shapes):
    out, r = [], 0
    cols = buf.shape[1]
    for s in shapes:
        n = math.prod(s)
        nr = _pack_rows(n, cols)
        piece = lax.optimization_barrier(buf[r:r + nr])
        out.append(piece.reshape(-1)[:n].reshape(s))
        r += nr
    return out


def _elementwise(fn, arrs, nout, name):
    r, c = arrs[0].shape
    lanes = -(-c // LANES) * LANES
    t = _tile(r, max(ROW_TILE, (ELEMENTWISE_BLOCK_BYTES // (4 * lanes)) // 8 * 8), 8)
    return rw_fwd(fn, [rows(a, t) for a in arrs], [(c, False)] * nout, nrow=r, t=t, name=name)


def kernel(x, ln_g, w_in, conv_w, a_log, dt_bias, head_norm_g, lam_re, lam_im, log_dt, b_re, b_im, c_re, c_im, d_skip, w_glu, b_glu, w_pa, w_pb, b_gate, w_out, final_g, loss_target, m_ln_g, m_w_in, m_conv_w, m_a_log, m_dt_bias, m_head_norm_g, m_lam_re, m_lam_im, m_log_dt, m_b_re, m_b_im, m_c_re, m_c_im, m_d_skip, m_w_glu, m_b_glu, m_w_pa, m_w_pb, m_b_gate, m_w_out, m_final_g, v_ln_g, v_w_in, v_conv_w, v_a_log, v_dt_bias, v_head_norm_g, v_lam_re, v_lam_im, v_log_dt, v_b_re, v_b_im, v_c_re, v_c_im, v_d_skip, v_w_glu, v_b_glu, v_w_pa, v_w_pb, v_b_gate, v_w_out, v_final_g):
    env = dict(locals())
    wts = {n: env[n] for n in WEIGHTS}
    mom_m = {n: env["m_" + n] for n in WEIGHTS}
    mom_v = {n: env["v_" + n] for n in WEIGHTS}

    xin = x[0]
    tgt = loss_target[0]
    nrow, dm = xin.shape
    depth = ln_g.shape[0]
    nh = dm // (2 * HEAD_DIM)
    wa = nh * HEAD_DIM
    wb = dm // 2
    ngrp = wb // S5_GROUP_CH
    pw = w_in.shape[-1]
    t = min(ROW_TILE, nrow)

    o_ba = 4 * wa
    o_u = o_ba + 4 * nh
    n_main = 4 * wa + 2 * wb + 2 * dm
    projp = -(-(n_main + LANES) // 512) * 512
    blk_za = 3 * wa // HEAD_DIM
    blk_u = 4 * wa // LANES
    ba_blk = n_main // LANES
    cx, cy, cc = _me()

    jb = o_ba // pw
    assert (o_u - 1) // pw == jb
    cut_lo, cut = o_ba - jb * pw, o_u - o_ba
    wide = -(-pw // LANES) * LANES + LANES
    col = lax.broadcasted_iota(jnp.int32, (pw, wide), 1)
    row = lax.broadcasted_iota(jnp.int32, (pw, wide), 0)
    holds_cut = (4 * cx + 2 * cy + cc) == jb
    src_plain = jnp.where(col < pw, col, -1)
    src_cut = jnp.where(col < cut_lo, col, jnp.where(col < pw - cut, col + cut, -1))
    src_cut = jnp.where((col >= wide - LANES) & (col < wide - LANES + cut), col - (wide - LANES) + cut_lo, src_cut)
    select = (row == jnp.where(holds_cut, src_cut, src_plain)).astype(WIRE_DT)
    w_in_tx = mm(w_in.reshape(depth * dm, pw), select, name="w_in_prepare").astype(WIRE_DT).reshape(depth, dm, wide)

    def gather_side(l):
        return all_gather([w_in_tx[l], w_glu[l].astype(WIRE_DT), w_pa[l].astype(WIRE_DT),
                           w_pb[l].astype(WIRE_DT), w_out[l].astype(WIRE_DT), conv_w[l]])

    def cat(g):
        return jnp.concatenate([g[j] for j in range(8)], axis=1)

    def assemble(gathered):
        g_in, g_glu, g_pa, g_pb, g_out, g_conv = gathered
        w_perm = jnp.concatenate(
            [g_in[j][:, :pw - cut if j == jb else pw] for j in range(8)]
            + [g_in[jb][:, wide - LANES:], jnp.zeros((dm, projp - n_main - LANES), WIRE_DT)], axis=1)
        return dict(w_in=w_perm, w_glu=g_glu.reshape(wb, wb), w_pa=cat(g_pa), w_pb=cat(g_pb),
                    w_out=g_out.reshape(dm, dm), conv_w=cat(g_conv))

    full = [assemble(run_side(gather_side(0), name="gather_weights"))]

    def small(l):
        z = jnp.zeros((1, LANES - 4 * nh), F32)
        alog_row = jnp.concatenate([jnp.zeros((1, 2 * nh), F32), a_log[l].reshape(1, 2 * nh), z], axis=1)
        dtb_row = jnp.concatenate([jnp.zeros((1, 2 * nh), F32), dt_bias[l].reshape(1, 2 * nh), z], axis=1)
        return alog_row, dtb_row

    saved = []
    cur = xin
    for l in range(depth):
        fw = full[l]
        (hh,) = rw_fwd(fn_rms, [rows(cur, t), bcast(ln_g[l][None])], [(dm, False)], nrow=nrow, t=t, name="rms_fwd")
        proj = mm(hh, fw["w_in"], name="proj_fwd")
        (qkv,) = rw_fwd(make_fn_prep(nh), [rows(proj, nrow, HEAD_DIM, 0, True), bcast(fw["conv_w"], HEAD_DIM, 0, True)],
                        [(HEAD_DIM, True)], nrow=nrow, t=nrow, ncol=3 * nh, name="prep_fwd")
        alog_row, dtb_row = small(l)
        nxt_gather = gather_side(l + 1) if l + 1 < depth else None
        o_dir, s_hist, t_hist = gdn_fwd(qkv, proj, ba_blk, alog_row, dtb_row, n_heads=nh, side=nxt_gather,
                                        name="gdn_fwd")
        if nxt_gather is not None:
            full.append(assemble(nxt_gather.result))
        hn_ins = [rows3(o_dir, 0, t, HEAD_DIM), rows3(o_dir, 1, t, HEAD_DIM), rows(proj, t, HEAD_DIM, blk_za, True),
                  bcast(head_norm_g[l][None])]
        (ya_in,) = rw_fwd(fn_headnorm, hn_ins, [(HEAD_DIM, True)], nrow=nrow, t=t, ncol=nh, name="headnorm_fwd")
        mats, mats_vjp = jax.vjp(s5_mats, lam_re[l], lam_im[l], log_dt[l], b_re[l], b_im[l], c_re[l], c_im[l])
        u_seg = segment_major(proj[:, 4 * wa:4 * wa + wb])
        yd, xs = s5_fwd(u_seg, *mats, name="s5_fwd")
        ge_ins = [rows3(yd, 0, t, wb, False), rows3(yd, 1, t, wb, False), rows(u_seg, t), bcast(d_skip[l][None])]
        (ys_seg,) = rw_fwd(fn_gelu, ge_ins, [(wb, False)], nrow=nrow, t=t, name="gelu_fwd")
        ys = time_major(ys_seg)
        glu = mm(ys, fw["w_glu"], name="glu_fwd")
        gl_ins = [rows(ys, t), rows(glu, t), rows(proj, t, wb, (4 * wa + wb) // wb), bcast(b_glu[l][None])]
        (yb_in,) = rw_fwd(fn_glu, gl_ins, [(wb, False)], nrow=nrow, t=t, name="glugate_fwd")
        y_a = mm(ya_in, fw["w_pa"], name="pa_fwd")
        y_b = mm(yb_in, fw["w_pb"], name="pb_fwd")
        bg = b_gate[l][None]
        mg_ins = [rows(proj, t, dm, 3), rows(proj, t, dm, 4), rows(y_a, t), rows(y_b, t),
                  bcast(bg, dm, 0), bcast(bg, dm, 1)]
        (merged,) = rw_fwd(fn_merge, mg_ins, [(dm, False)], nrow=nrow, t=t, name="merge_fwd")
        nxt = mm(merged, fw["w_out"], add=cur, name="out_fwd")
        saved.append(dict(x=cur, h=hh, proj=proj, qkv=qkv, o_dir=o_dir, s_hist=s_hist, t_hist=t_hist, hn_ins=hn_ins,
                          mats=mats, mats_vjp=mats_vjp, xs=xs, ge_ins=ge_ins, ys=ys, gl_ins=gl_ins, u_seg=u_seg,
                          ya_in=ya_in, yb_in=yb_in, mg_ins=mg_ins, merged=merged,
                          alog_row=alog_row, dtb_row=dtb_row))
        cur = nxt

    loss_ins = [rows(cur, t), rows(tgt, t), bcast(final_g[None])]
    (row_loss,) = rw_fwd(fn_loss, loss_ins, [(1, False)], nrow=nrow, t=t, name="loss_fwd")
    ones = jnp.ones((nrow, 1), F32)
    dcur, dfinal = rw_bwd(fn_loss, loss_ins, [[rows(ones, t)]], nrow=nrow, t=t,
                          row_grads=[(0, False)], bc_grads=[2], name="loss_bwd")
    loss = lax.psum(jnp.sum(row_loss), ("x", "y", "c"))

    coord = {"x": cx, "y": cy, "c": cc}
    routes = {"cxy": ("c", "x", "y"), "cyx": ("c", "y", "x")}
    route = ["cxy", "cyx", "cxy", "cxy", "cyx", "cyx", "cxy"]
    conv_order = [4 * ((q // 2) % 2) + 2 * (q % 2) + q // 4 for q in range(8)]
    cw = 3 * wa // 8
    g_final = {n: [None] * depth for n in SHARDED}
    gsh = {n: [None] * depth for n in SHARDED}

    def own_columns(z):
        un = jnp.concatenate([z[:, :o_ba], z[:, n_main:n_main + 4 * nh], z[:, o_ba:n_main]], axis=1)
        return jnp.stack([un[:, d * pw:(d + 1) * pw] for d in range(8)])

    def rs_begin(l):
        in_a, in_b = gsh["w_in"][l]
        conv_buf = jnp.stack([gsh["conv_w"][l][:, d * cw:(d + 1) * cw] for d in conv_order])
        return dict(layer=l, stage=0, bufs=[
            in_a.reshape(8, dm // 16, projp), in_b.reshape(8, dm // 16, projp), gsh["w_out"][l],
            gsh["w_glu"][l], gsh["w_pa"][l], gsh["w_pb"][l], conv_buf])

    def rs_side(st):
        if st is None:
            return None
        if st["stage"] < 3:
            st["axes"] = [routes[r][st["stage"]] for r in route]

            def halves(b):
                return b.reshape((2, b.shape[0] // 2) + b.shape[1:])

            st["bufs"] = [halves(b) for b in st["bufs"]]
            send = [halves(b) for b in st["tx"]] if st.get("tx") else st["bufs"]
            if st.get("whole") is not None:
                return exchange(send + [st["whole"]], st["axes"] + [routes["cxy"][st["stage"]]],
                                half=[True] * len(send) + [False])
            return exchange(send, st["axes"], half=True)
        st["bufs"] = [b.reshape(b.shape[1:]) for b in st["bufs"]]
        return all_to_all([own_columns(st["bufs"][0]), own_columns(st["bufs"][1])],
                          [lambda x, y, c: 4 * c + 2 * x + y, lambda x, y, c: 4 * c + 2 * y + x])

    def rs_absorb(st, side):
        if st is None:
            return
        if st["stage"] < 3:
            if st.get("whole") is not None:
                (st["whole"],) = _elementwise(make_fn_sum(2), [st["whole"], side.result[-1]], 1, "ar_add")
            narrow = st["stage"] < 2
            sums = [add_half(b, r, coord[ax].astype(jnp.int32).reshape(1), narrow=narrow, name=f"rs_add_{st['stage']}")
                    for b, r, ax in zip(st["bufs"], side.result, st["axes"])]
            st["bufs"] = [s[0] for s in sums]
            st["tx"] = [s[1] for s in sums] if narrow else None
        else:
            got_a, got_b = side.result
            g_final["w_in"][st["layer"]] = jnp.concatenate(
                [got_a.reshape(dm // 2, pw), got_b.reshape(dm // 2, pw)], axis=0)
            for n, b in zip(["w_out", "w_glu", "w_pa", "w_pb", "conv_w"], st["bufs"][2:]):
                g_final[n][st["layer"]] = b
        st["stage"] += 1

    grep = {n: [None] * depth for n in REPLICATED if n != "final_g"}
    pending = None
    for l in reversed(range(depth)):
        fw, sv = full[l], saved[l]
        dmerged = mm(dcur, fw["w_out"], tb=True, name="out_bwd_x")
        gsh["w_out"][l] = mm(sv["merged"], dcur, ta=True, scatter=("rows", slot_cxy), name="out_bwd_w")
        side = rs_side(pending)
        dla, dlb, dya, dyb, dbga, dbgb = rw_bwd(
            fn_merge, sv["mg_ins"], [[rows(dmerged, t)]], nrow=nrow, t=t,
            row_grads=[(0, False), (1, False), (2, False), (3, False)], bc_grads=[4, 5], side=side, name="merge_bwd")
        rs_absorb(pending, side)
        grep["b_gate"][l] = jnp.concatenate([dbga.reshape(dm), dbgb.reshape(dm)])
        dya_in = mm(dya, fw["w_pa"], tb=True, name="pa_bwd_x")
        gsh["w_pa"][l] = mm(sv["ya_in"], dya, ta=True, scatter=("cols", slot_cyx), name="pa_bwd_w")
        dyb_in = mm(dyb, fw["w_pb"], tb=True, name="pb_bwd_x")
        gsh["w_pb"][l] = mm(sv["yb_in"], dyb, ta=True, scatter=("cols", slot_cyx), name="pb_bwd_w")
        dys1, dglu, dzb, dbglu = rw_bwd(
            fn_glu, sv["gl_ins"], [[rows(dyb_in, t)]], nrow=nrow, t=t,
            row_grads=[(0, False), (1, False), (2, False)], bc_grads=[3], name="glugate_bwd")
        grep["b_glu"][l] = dbglu.reshape(wb)
        dys2 = mm(dglu, fw["w_glu"], tb=True, name="glu_bwd_x")
        gsh["w_glu"][l] = mm(sv["ys"], dglu, ta=True, scatter=("rows", slot_cxy), name="glu_bwd_w")
        dyd, du1, ddskip = rw_bwd(
            fn_gelu, sv["ge_ins"], [[rows(segment_major(dys1), t), rows(segment_major(dys2), t)]], nrow=nrow, t=t,
            row_grads=[(0, False), (2, False)], bc_grads=[3], name="gelu_bwd")
        grep["d_skip"][l] = ddskip.reshape(wb)
        side = rs_side(pending)
        du_seg, dw_s5, dcm_s5, dar, dai = s5_bwd(sv["u_seg"], *sv["mats"], sv["xs"], dyd, du1, side=side,
                                                 name="s5_bwd")
        du = time_major(du_seg)
        rs_absorb(pending, side)
        g_lr, g_li, g_ldt, g_br, g_bi, g_cr, g_ci = sv["mats_vjp"]((dar, dai, dw_s5, dcm_s5))
        for nme, val in zip(["lam_re", "lam_im", "log_dt", "b_re", "b_im", "c_re", "c_im"],
                            [g_lr, g_li, g_ldt, g_br, g_bi, g_cr, g_ci]):
            grep[nme][l] = val
        do, dza, dhn = rw_bwd(fn_headnorm, sv["hn_ins"], [[rows(dya_in, t, HEAD_DIM, 0, True)]], nrow=nrow, t=t,
                              ncol=nh, row_grads=[(0, True), (2, True)], bc_grads=[3], name="headnorm_bwd")
        grep["head_norm_g"][l] = jnp.sum(dhn, axis=0).reshape(HEAD_DIM)
        side = rs_side(pending)
        dqkv, dba_all, dal, ddt = gdn_bwd(sv["qkv"], sv["proj"], ba_blk, sv["alog_row"], sv["dtb_row"],
                                          sv["s_hist"], sv["t_hist"], do, n_heads=nh, side=side, name="gdn_bwd")
        rs_absorb(pending, side)
        grep["a_log"][l] = jnp.sum(dal, axis=(0, 1))[2 * nh:4 * nh].reshape(2, nh)
        grep["dt_bias"][l] = jnp.sum(ddt, axis=(0, 1))[2 * nh:4 * nh].reshape(2, nh)
        (dba,) = rw_fwd(make_fn_sum(2), [rows3(dba_all, p, t, LANES, False) for p in range(2)],
                        [(LANES, False)], nrow=nrow, t=t, name="dba_sum")
        prep_ins = [rows(sv["proj"], nrow, HEAD_DIM, 0, True), bcast(fw["conv_w"], HEAD_DIM, 0, True)]
        dq_cots = [(dqkv.reshape(2, 3 * nrow, wa), (None, nrow, HEAD_DIM),
                    (lambda j, i, dd=dd: (dd, j // nh, j % nh))) for dd in range(2)]
        dqkv_raw, dconv = rw_bwd(make_fn_prep(nh), prep_ins, [dq_cots], nrow=nrow, t=nrow, ncol=3 * nh,
                                 row_grads=[(0, True)], bc_grads=[1], name="prep_bwd")
        gsh["conv_w"][l] = jnp.transpose(dconv, (1, 0, 2)).reshape(CONV_K, 3 * wa)
        dproj = jnp.concatenate([dqkv_raw, dza, du, dzb, dla, dlb, dba,
                                 jnp.zeros((nrow, projp - n_main - LANES), F32)], axis=1).astype(MXU_DT)
        side = rs_side(pending)
        dh = mm(dproj, fw["w_in"], tb=True, side=side, name="proj_bwd_x")
        rs_absorb(pending, side)
        gsh["w_in"][l] = [mm(sv["h"], dproj, ta=True, m_part=(part, 2), name="proj_bwd_w") for part in range(2)]
        dcur, dlng = rw_bwd(fn_rms, [rows(sv["x"], t), bcast(ln_g[l][None])], [[rows(dh, t)]], nrow=nrow, t=t,
                            row_grads=[(0, False)], bc_grads=[1], residual=rows(dcur, t), name="rms_bwd")
        grep["ln_g"][l] = dlng.reshape(dm)
        pending = rs_begin(l)
    grad_x = dcur[None]
    rep_list = [jnp.stack(grep[n]) for n in REPLICATED if n != "final_g"] + [dfinal.reshape(dm)]
    rep_shapes = [a.shape for a in rep_list]
    pending["whole"] = _pack(rep_list, COMM_COLS, ROW_TILE)
    for stage in range(RS_STAGES):
        side = rs_side(pending)
        run_side(side, name=f"rs_stage_{stage}")
        rs_absorb(pending, side)
    grads = {n: jnp.stack(g_final[n]) for n in SHARDED}
    for n, val in zip(REPLICATED, _unpack(pending["whole"], rep_shapes)):
        grads[n] = val

    deltas, new_m, new_v = {}, {}, {}
    for n in WEIGHTS:
        shp = wts[n].shape
        two = [a.reshape(-1, shp[-1]) for a in (wts[n], grads[n], mom_m[n], mom_v[n])]
        d_, m_, v_ = _elementwise(fn_adamw, two, 3, "adamw_" + n)
        deltas[n], new_m[n], new_v[n] = d_.reshape(shp), m_.reshape(shp), v_.reshape(shp)

    return (loss, grad_x, *[grads[n] for n in WEIGHTS], *[deltas[n] for n in WEIGHTS],
            *[new_m[n] for n in WEIGHTS], *[new_v[n] for n in WEIGHTS])
```

```python
import functools
import math

import jax
import jax.numpy as jnp
from jax import lax
from jax.experimental import pallas as pl
from jax.experimental.pallas import tpu as pltpu

F32 = jnp.float32
MXU_DT = jnp.bfloat16
WIRE_DT = jnp.bfloat16

HEAD_DIM = 128
CHUNK = 64
CONV_K = 5
S5_GROUP_CH = 16
S5_STATE = 64
S5_BLOCK_GROUPS = 8
S5_BLOCK_STATE = S5_BLOCK_GROUPS * S5_STATE
RMS_EPS = 1e-6
LANES = 128
VMEM_LIMIT = 56 * 1024 * 1024
ROW_TILE = 256
COMM_COLS = 1024
ADD_BLOCK_BYTES = 2 * 1024 * 1024
RS_STAGES = 4
ELEMENTWISE_BLOCK_BYTES = 1024 * 1024

ADAM_LR = 0.001
ADAM_B1 = 0.9
ADAM_B2 = 0.999
ADAM_EPS = 1e-08
ADAM_WD = 0.01
ADAM_STEP = 10

WEIGHTS = ['ln_g', 'w_in', 'conv_w', 'a_log', 'dt_bias', 'head_norm_g', 'lam_re', 'lam_im', 'log_dt',
           'b_re', 'b_im', 'c_re', 'c_im', 'd_skip', 'w_glu', 'b_glu', 'w_pa', 'w_pb', 'b_gate',
           'w_out', 'final_g']
SHARDED = ['w_in', 'w_glu', 'w_pa', 'w_pb', 'w_out', 'conv_w']
REPLICATED = [n for n in WEIGHTS if n not in SHARDED]
MESH = pl.DeviceIdType.MESH


def _params(sem=None):
    return pltpu.CompilerParams(dimension_semantics=sem, vmem_limit_bytes=VMEM_LIMIT)


def _tile(n, cap, q=LANES):
    t = (min(n, cap) // q) * q
    while t > q and n % t:
        t -= q
    return t if t > 0 and n % t == 0 else n


class Side:
    def __init__(self, ins, out_sd, sems, start, finish):
        self.ins, self.out_sd, self.sems, self.start, self.finish = list(ins), list(out_sd), list(sems), start, finish
        self.result = None


def _call(body, *, grid, in_specs, out_specs, out_shape, scratch_shapes=(), sem, name, args, side=None):
    in_specs, out_specs, out_shape = list(in_specs), list(out_specs), list(out_shape)
    scratch_shapes = list(scratch_shapes)
    if side is None:
        return pl.pallas_call(body, grid=grid, in_specs=in_specs, out_specs=out_specs, out_shape=out_shape,
                              scratch_shapes=scratch_shapes, compiler_params=_params(sem), name=name)(*args)
    hbm = pl.BlockSpec(memory_space=pl.ANY)
    n_in, n_out, n_scr = len(in_specs), len(out_specs), len(scratch_shapes)
    s_in, s_out = len(side.ins), len(side.out_sd)

    def hosted(*refs):
        main_in, rest = refs[:n_in], refs[n_in:]
        side_in, rest = rest[:s_in], rest[s_in:]
        main_out, rest = rest[:n_out], rest[n_out:]
        side_out, rest = rest[:s_out], rest[s_out:]
        main_scr, sems = rest[:n_scr], rest[n_scr:]
        ids = [pl.program_id(k) for k in range(len(grid))]
        first = functools.reduce(jnp.logical_and, [i == 0 for i in ids])
        last = functools.reduce(jnp.logical_and, [i == g - 1 for i, g in zip(ids, grid)])

        @pl.when(first)
        def _():
            side.start(side_in, side_out, sems)

        body(*main_in, *main_out, *main_scr)

        @pl.when(last)
        def _():
            side.finish(side_in, side_out, sems)

    outs = pl.pallas_call(
        hosted, grid=grid, in_specs=in_specs + [hbm] * s_in, out_specs=out_specs + [hbm] * s_out,
        out_shape=out_shape + side.out_sd, scratch_shapes=scratch_shapes + side.sems,
        compiler_params=_params(("arbitrary",) * len(grid)), name=name)(*args, *side.ins)
    side.result = list(outs[n_out:])
    return list(outs[:n_out])


def run_side(side, *, name):
    hbm = pl.BlockSpec(memory_space=pl.ANY)
    s_in, s_out = len(side.ins), len(side.out_sd)

    def body(*refs):
        side_in, side_out, sems = refs[:s_in], refs[s_in:s_in + s_out], refs[s_in + s_out:]
        side.start(side_in, side_out, sems)
        side.finish(side_in, side_out, sems)

    side.result = list(pl.pallas_call(body, out_shape=side.out_sd, in_specs=[hbm] * s_in, out_specs=[hbm] * s_out,
                                      scratch_shapes=side.sems, name=name)(*side.ins))
    return side.result


def slot_cxy(d):
    return 4 * (d % 2) + 2 * (d // 4) + (d // 2) % 2


def slot_cyx(d):
    return 4 * (d % 2) + 2 * ((d // 2) % 2) + d // 4


def mm(a, b, *, ta=False, tb=False, add=None, m_part=None, scatter=None, side=None, name):
    m = a.shape[1] if ta else a.shape[0]
    k = a.shape[0] if ta else a.shape[1]
    n = b.shape[0] if tb else b.shape[1]
    m_off = 0
    if m_part is not None:
        m = m // m_part[1]
        m_off = m_part[0]
    tm, tn, tk = _tile(m, 1024), _tile(n, 512), _tile(k, 2048)
    if scatter is not None and scatter[0] == "rows":
        tm = m // 8
    if scatter is not None and scatter[0] == "cols":
        tn = n // 8
    if m_part is not None:
        assert tm == m
    nk = k // tk
    dn = (((0 if ta else 1,), (1 if tb else 0,)), ((), ()))

    def body(*refs):
        if add is None:
            a_ref, b_ref, o_ref, acc = refs
        else:
            a_ref, b_ref, add_ref, o_ref, acc = refs
        kk = pl.program_id(2)

        @pl.when(kk == 0)
        def _():
            acc[...] = jnp.zeros_like(acc)

        acc[...] += lax.dot_general(a_ref[...].astype(MXU_DT), b_ref[...].astype(MXU_DT), dn,
                                    preferred_element_type=F32)

        @pl.when(kk == nk - 1)
        def _():
            r = acc[...]
            if add is not None:
                r = r + add_ref[...]
            o_ref[...] = r

    if ta:
        a_spec = pl.BlockSpec((tk, tm), lambda i, j, kk: (kk, i + m_off))
    else:
        a_spec = pl.BlockSpec((tm, tk), lambda i, j, kk: (i + m_off, kk))
    b_spec = pl.BlockSpec((tn, tk), lambda i, j, kk: (j, kk)) if tb else pl.BlockSpec((tk, tn), lambda i, j, kk: (kk, j))
    out_sd = jax.ShapeDtypeStruct((m, n), F32)
    if scatter is None:
        o_spec = pl.BlockSpec((tm, tn), lambda i, j, kk: (i, j))
    elif scatter[0] == "rows":
        o_spec = pl.BlockSpec((None, tm, tn), lambda i, j, kk: (scatter[1](i), 0, j))
        out_sd = jax.ShapeDtypeStruct((8, tm, n), F32)
    else:
        o_spec = pl.BlockSpec((None, tm, tn), lambda i, j, kk: (scatter[1](j), i, 0))
        out_sd = jax.ShapeDtypeStruct((8, m, tn), F32)
    ins, specs = [a, b], [a_spec, b_spec]
    if add is not None:
        ins.append(add)
        specs.append(o_spec)
    return _call(body, grid=(m // tm, n // tn, nk), in_specs=specs, out_specs=[o_spec], out_shape=[out_sd],
                 scratch_shapes=[pltpu.VMEM((tm, tn), F32)], sem=("parallel", "parallel", "arbitrary"),
                 name=name, args=ins, side=side)[0]


def rows(arr, t, width=None, base=0, per_j=False):
    width = arr.shape[1] if width is None else width
    return (arr, (t, width), lambda j, i: (i, base + (j if per_j else 0)))


def rows3(arr, lead, t, width, per_j=True):
    return (arr, (None, t, width), lambda j, i: (lead, i, j if per_j else 0))


def bcast(arr, width=None, base=0, per_j=False):
    width = arr.shape[1] if width is None else width
    return (arr, (arr.shape[0], width), lambda j, i: (0, base + (j if per_j else 0)))


def _specs(items):
    return [pl.BlockSpec(bs, im) for (_, bs, im) in items]


def rw_fwd(fn, ins, outs, *, nrow, t, ncol=1, name):
    out_specs = [pl.BlockSpec((t, w), (lambda j, i: (i, j)) if pj else (lambda j, i: (i, 0))) for (w, pj) in outs]
    out_shape = [jax.ShapeDtypeStruct((nrow, w * (ncol if pj else 1)), F32) for (w, pj) in outs]
    nin = len(ins)

    def body(*refs):
        j = pl.program_id(0)
        res = fn(j, *[r[...] for r in refs[:nin]])
        for o_ref, r in zip(refs[nin:], res):
            o_ref[...] = r

    return pl.pallas_call(
        body, grid=(ncol, nrow // t), in_specs=_specs(ins), out_specs=out_specs, out_shape=out_shape,
        compiler_params=_params(("parallel", "parallel")), name=name)(*[x[0] for x in ins])


def rw_bwd(fn, ins, cots, *, nrow, t, ncol=1, row_grads, bc_grads, residual=None, side=None, name):
    nin = len(ins)
    flat_cots = [c for group in cots for c in group]
    extra = [residual] if residual is not None else []
    out_specs, out_shape = [], []
    for idx, pj in row_grads:
        w = ins[idx][1][-1]
        out_specs.append(pl.BlockSpec((t, w), (lambda j, i: (i, j)) if pj else (lambda j, i: (i, 0))))
        out_shape.append(jax.ShapeDtypeStruct((nrow, w * (ncol if pj else 1)), F32))
    for idx in bc_grads:
        r, w = ins[idx][1]
        out_specs.append(pl.BlockSpec((None, r, w), lambda j, i: (j, 0, 0)))
        out_shape.append(jax.ShapeDtypeStruct((ncol, r, w), F32))

    def body(*refs):
        j = pl.program_id(0)
        i = pl.program_id(1)
        vals = [r[...] for r in refs[:nin]]
        pos = nin
        cts = []
        for group in cots:
            c = refs[pos][...]
            for q in range(1, len(group)):
                c = c + refs[pos + q][...]
            pos += len(group)
            cts.append(c)
        res_ref = refs[pos] if residual is not None else None
        pos += len(extra)
        outs = refs[pos:]
        _, vjp = jax.vjp(lambda *a: tuple(fn(j, *a)), *vals)
        grads = vjp(tuple(cts))
        for q, (idx, _) in enumerate(row_grads):
            g = grads[idx]
            if q == 0 and res_ref is not None:
                g = g + res_ref[...]
            outs[q][...] = g
        for q, idx in enumerate(bc_grads):
            o_ref = outs[len(row_grads) + q]

            @pl.when(i == 0)
            def _(o_ref=o_ref):
                o_ref[...] = jnp.zeros_like(o_ref)

            o_ref[...] += grads[idx]

    all_in = list(ins) + flat_cots + extra
    return _call(body, grid=(ncol, nrow // t), in_specs=_specs(all_in), out_specs=out_specs, out_shape=out_shape,
                 sem=("parallel", "arbitrary"), name=name, args=[x[0] for x in all_in], side=side)


def _silu(x):
    return x * jax.nn.sigmoid(x)


@jax.custom_vjp
def _softplus(x):
    return jnp.maximum(x, 0.0) + jnp.log1p(jnp.exp(-jnp.abs(x)))


def _softplus_fwd(x):
    return _softplus(x), x


def _softplus_bwd(x, ct):
    return (ct * jax.nn.sigmoid(x),)


_softplus.defvjp(_softplus_fwd, _softplus_bwd)


def _gelu(x):
    return 0.5 * x * (1.0 + jnp.tanh(math.sqrt(2.0 / math.pi) * (x + 0.044715 * (x * x * x))))


def _row_shift_impl(x, s):
    n = x.shape[0]
    if s == 0:
        return x
    rolled = pltpu.roll(x, (-s) % n, 0)
    t = lax.broadcasted_iota(jnp.int32, x.shape, 0)
    ok = (t + s >= 0) & (t + s < n)
    return jnp.where(ok, rolled, 0.0)


@functools.partial(jax.custom_vjp, nondiff_argnums=(1,))
def _row_shift(x, s):
    return _row_shift_impl(x, s)


def _row_shift_fwd(x, s):
    return _row_shift_impl(x, s), None


def _row_shift_bwd(s, _, ct):
    return (_row_shift_impl(ct, -s),)


_row_shift.defvjp(_row_shift_fwd, _row_shift_bwd)


def fn_rms(j, x, g):
    return (x * lax.rsqrt(jnp.mean(x * x, axis=-1, keepdims=True) + RMS_EPS) * g,)


def make_fn_prep(n_heads):
    pad = (CONV_K - 1) // 2

    def fn_prep(j, x, w):
        y = _row_shift(x, -pad) * w[0:1, :]
        for i in range(1, CONV_K):
            y = y + _row_shift(x, i - pad) * w[i:i + 1, :]
        a = _silu(y)
        scale = jnp.where(j < n_heads, HEAD_DIM ** -0.5, 1.0).astype(F32)
        nrm = a * lax.rsqrt(jnp.sum(a * a, axis=-1, keepdims=True) + RMS_EPS) * scale
        return (jnp.where(j < 2 * n_heads, nrm, a),)

    return fn_prep


def fn_headnorm(j, o0, o1, z, g):
    o = o0 + o1
    n = o * lax.rsqrt(jnp.mean(o * o, axis=-1, keepdims=True) + RMS_EPS) * g
    return (n * _silu(z),)


def fn_gelu(j, y0, y1, u, dsk):
    return (_gelu(y0 + y1 + u * dsk),)


def fn_glu(j, ys, logit, z, b):
    return (ys * jax.nn.sigmoid(logit + b) * _silu(z),)


def fn_merge(j, la, lb, ya, yb, ba, bb):
    return (jax.nn.sigmoid(la + ba) * ya + jax.nn.sigmoid(lb + bb) * yb,)


def fn_loss(j, x, t, g):
    y = x * lax.rsqrt(jnp.mean(x * x, axis=-1, keepdims=True) + RMS_EPS) * g
    e = y - t
    return (0.5 * jnp.mean(e * e, axis=-1, keepdims=True),)


def make_fn_sum(n):
    def fn_sum(j, *xs):
        s = xs[0]
        for q in range(1, n):
            s = s + xs[q]
        return (s,)

    return fn_sum


def fn_adamw(j, w, g, m, v):
    m2 = ADAM_B1 * m + (1.0 - ADAM_B1) * g
    v2 = ADAM_B2 * v + (1.0 - ADAM_B2) * (g * g)
    m_hat = m2 / (1.0 - ADAM_B1 ** ADAM_STEP)
    v_hat = v2 / (1.0 - ADAM_B2 ** ADAM_STEP)
    delta = -ADAM_LR * (m_hat / (jnp.sqrt(v_hat) + ADAM_EPS) + ADAM_WD * w)
    return delta, m2, v2


def _dot(a, b, dims, precision=None):
    return lax.dot_general(a, b, (dims, ((), ())), precision=precision, preferred_element_type=F32)


_NN = ((1,), (0,))
_NT = ((1,), (1,))
_TN = ((0,), (0,))
GDN_HEAD_BLOCK = 4
INVERSE_WIDE_FACTORS = 0


def _split(a):
    hi = a.astype(MXU_DT)
    return hi, (a - hi.astype(F32)).astype(MXU_DT)


def _dot3s(a, b, dims):
    (ah, al), (bh, bl) = a, b
    return _dot(ah, bh, dims) + (_dot(al, bh, dims) + _dot(ah, bl, dims))


def _dot3(a, b, dims):
    return _dot3s(_split(a), _split(b), dims)


def _unit_inverse(lmat):
    r = lmat.shape[0]
    eye = (lax.broadcasted_iota(jnp.int32, (r, r), 0) == lax.broadcasted_iota(jnp.int32, (r, r), 1)).astype(F32)
    pw = -lmat
    tinv = eye + pw
    pws = _split(pw)
    n_fact = int(math.ceil(math.log2(CHUNK))) - 1
    for k in range(n_fact):
        if k < INVERSE_WIDE_FACTORS:
            pws = _split(_dot3s(pws, pws, _NN))
            tinv = tinv + _dot3s(_split(tinv), pws, _NN)
        else:
            pw = _dot(pws[0], pws[0], _NN)
            pws = (pw.astype(MXU_DT), None)
            tinv = tinv + _dot(tinv.astype(MXU_DT), pws[0], _NN)
    return tinv


@jax.custom_vjp
def tri_apply(lmat, rhs, tinv):
    return _dot3(tinv, rhs, _NN)


def _tri_apply_fwd(lmat, rhs, tinv):
    x = _dot3(tinv, rhs, _NN)
    return x, (tinv, x)


def _tri_apply_bwd(res, dx):
    tinv, x = res
    drhs = _dot3(tinv, dx, _TN)
    return -_dot3(drhs, x, _NT), drhs, jnp.zeros_like(tinv)


tri_apply.defvjp(_tri_apply_fwd, _tri_apply_bwd)


def gdn_group(s, q, k, v, ba, alog, dtb, *, d, head0, n_heads, tinv=None):
    hb = len(s)
    c = q.shape[0]
    r = hb * c

    def stack(x):
        return jnp.concatenate([x[:, i * HEAD_DIM:(i + 1) * HEAD_DIM] for i in range(hb)], axis=0)

    def pick(x, lane0):
        lane = lax.broadcasted_iota(jnp.int32, x.shape, 1)
        return jnp.concatenate(
            [jnp.sum(jnp.where(lane == lane0 + i, x, 0.0), axis=1, keepdims=True) for i in range(hb)], axis=0)

    q4, k4, v4 = stack(q), stack(k), stack(v)
    beta = pick(jax.nn.sigmoid(ba), d * n_heads + head0)
    g = pick(-jnp.exp(alog) * _softplus(ba + dtb), 2 * n_heads + d * n_heads + head0)
    ii = lax.broadcasted_iota(jnp.int32, (r, r), 0)
    jj = lax.broadcasted_iota(jnp.int32, (r, r), 1)
    same = (ii // c) == (jj // c)
    rel = (ii - jj) * (1 - 2 * d)
    incl = same & (rel >= 0)
    strict = same & (rel > 0)
    incl_t = same & (rel <= 0)
    g_row = jnp.sum(jnp.where(ii == jj, g, 0.0), axis=0, keepdims=True)
    gc_col = jnp.sum(jnp.where(incl, g_row, 0.0), axis=1, keepdims=True)
    gc_row = jnp.sum(jnp.where(incl_t, g, 0.0), axis=0, keepdims=True)
    g_tot = jnp.sum(jnp.where(same, g_row, 0.0), axis=1, keepdims=True)
    decay = jnp.exp(jnp.where(incl, gc_col - gc_row, -1e30))
    kb = k4 * beta
    vb = v4 * beta
    lmat = jnp.where(strict, _dot(kb, k4, _NT) * decay, 0.0)
    fresh = tinv is None
    if fresh:
        tinv = _unit_inverse(lmat)
    uw = tri_apply(lmat, jnp.concatenate([vb, kb * jnp.exp(gc_col)], axis=1), tinv)
    u, w = uw[:, :HEAD_DIM], uw[:, HEAD_DIM:]
    qk = _dot(q4, k4, _NT) * decay
    qe = q4 * jnp.exp(gc_col)
    kd = k4 * jnp.exp(g_tot - gc_col)
    eg = jnp.exp(g_tot)
    v_new, o_s = [], []
    for i in range(hb):
        rs = slice(i * c, (i + 1) * c)
        ws = _dot(jnp.concatenate([w[rs], qe[rs]], axis=0), s[i], _NN)
        v_new.append(u[rs] - ws[:c])
        o_s.append(ws[c:])
    o4 = jnp.concatenate(o_s, axis=0) + _dot(qk, jnp.concatenate(v_new, axis=0), _NN)
    s_new = tuple(s[i] * eg[i * c:i * c + 1, :] + _dot(kd[i * c:(i + 1) * c], v_new[i], _TN) for i in range(hb))
    o = jnp.concatenate([o4[i * c:(i + 1) * c] for i in range(hb)], axis=1)
    return (s_new, o, tinv) if fresh else (s_new, o)


def _gdn_maps(n_chunks):
    def chunk_of(d, step):
        return step + d * (n_chunks - 1 - 2 * step)
    return chunk_of


def _heads(ref, h0, n):
    return ref[:, h0 * HEAD_DIM:(h0 + n) * HEAD_DIM]


def gdn_fwd(qkv, proj, ba_blk, alog_row, dtb_row, *, n_heads, side=None, name):
    nrow = qkv.shape[0]
    nc = nrow // CHUNK
    h = n_heads
    wa = h * HEAD_DIM
    hb = math.gcd(h, GDN_HEAD_BLOCK)
    chunk_of = _gdn_maps(nc)

    def body(q_ref, k_ref, v_ref, ba_ref, al_ref, dt_ref, o_ref, sh_ref, ti_ref, s_scr):
        d = pl.program_id(0)
        n = pl.program_id(1)

        @pl.when(n == 0)
        def _():
            s_scr[...] = jnp.zeros_like(s_scr)

        ba, al, dt = ba_ref[...], al_ref[...], dt_ref[...]
        for h0 in range(0, h, hb):
            s = tuple(s_scr[h0 + i] for i in range(hb))
            for i in range(hb):
                sh_ref[h0 + i] = s[i]
            s2, o, tinv = gdn_group(s, _heads(q_ref, h0, hb), _heads(k_ref, h0, hb), _heads(v_ref, h0, hb),
                                    ba, al, dt, d=d, head0=h0, n_heads=h)
            o_ref[:, h0 * HEAD_DIM:(h0 + hb) * HEAD_DIM] = o
            ti_ref[h0 // hb] = tinv
            for i in range(hb):
                s_scr[h0 + i] = s2[i]

    blk = (CHUNK, wa)
    in_specs = [
        pl.BlockSpec(blk, lambda d, n: (chunk_of(d, n), 0)),
        pl.BlockSpec(blk, lambda d, n: (chunk_of(d, n), 1)),
        pl.BlockSpec(blk, lambda d, n: (chunk_of(d, n), 2)),
        pl.BlockSpec((CHUNK, LANES), lambda d, n: (chunk_of(d, n), ba_blk)),
        pl.BlockSpec((1, LANES), lambda d, n: (0, 0)),
        pl.BlockSpec((1, LANES), lambda d, n: (0, 0)),
    ]
    ng, r = h // hb, hb * CHUNK
    out_specs = [
        pl.BlockSpec((None, CHUNK, wa), lambda d, n: (d, chunk_of(d, n), 0)),
        pl.BlockSpec((h, None, HEAD_DIM, HEAD_DIM), lambda d, n: (d, n, 0, 0)),
        pl.BlockSpec((None, None, ng, r, r), lambda d, n: (d, n, 0, 0, 0)),
    ]
    out_shape = [jax.ShapeDtypeStruct((2, nrow, wa), F32),
                 jax.ShapeDtypeStruct((2 * h, nc, HEAD_DIM, HEAD_DIM), F32),
                 jax.ShapeDtypeStruct((2, nc, ng, r, r), F32)]
    return _call(body, grid=(2, nc), in_specs=in_specs, out_specs=out_specs, out_shape=out_shape,
                 scratch_shapes=[pltpu.VMEM((h, HEAD_DIM, HEAD_DIM), F32)], sem=("parallel", "arbitrary"),
                 name=name, args=(qkv, qkv, qkv, proj, alog_row, dtb_row), side=side)


def gdn_bwd(qkv, proj, ba_blk, alog_row, dtb_row, s_hist, t_hist, do, *, n_heads, side=None, name):
    nrow = qkv.shape[0]
    nc = nrow // CHUNK
    h = n_heads
    wa = h * HEAD_DIM
    hb = math.gcd(h, GDN_HEAD_BLOCK)
    chunk_of = _gdn_maps(nc)

    def cb(d, n):
        return chunk_of(d, nc - 1 - n)

    def body(q_ref, k_ref, v_ref, ba_ref, al_ref, dt_ref, sh_ref, do_ref, ti_ref,
             dqkv_ref, dba_ref, dal_ref, ddt_ref, ds_scr):
        d = pl.program_id(0)
        n = pl.program_id(1)

        @pl.when(n == 0)
        def _():
            ds_scr[...] = jnp.zeros_like(ds_scr)
            dal_ref[...] = jnp.zeros_like(dal_ref)
            ddt_ref[...] = jnp.zeros_like(ddt_ref)

        ba, al, dt = ba_ref[...], al_ref[...], dt_ref[...]
        dba_sum = jnp.zeros_like(ba)
        dal_sum = jnp.zeros_like(al)
        ddt_sum = jnp.zeros_like(dt)
        for h0 in range(0, h, hb):
            f = functools.partial(gdn_group, d=d, head0=h0, n_heads=h, tinv=ti_ref[h0 // hb])
            s = tuple(sh_ref[h0 + i] for i in range(hb))
            _, vjp = jax.vjp(f, s, _heads(q_ref, h0, hb), _heads(k_ref, h0, hb), _heads(v_ref, h0, hb), ba, al, dt)
            ds, dq, dk, dv, dba, dal, ddt = vjp((tuple(ds_scr[h0 + i] for i in range(hb)), _heads(do_ref, h0, hb)))
            for i in range(hb):
                ds_scr[h0 + i] = ds[i]
            cols = slice(h0 * HEAD_DIM, (h0 + hb) * HEAD_DIM)
            dqkv_ref[0, :, cols] = dq
            dqkv_ref[1, :, cols] = dk
            dqkv_ref[2, :, cols] = dv
            dba_sum = dba_sum + dba
            dal_sum = dal_sum + dal
            ddt_sum = ddt_sum + ddt
        dba_ref[...] = dba_sum
        dal_ref[...] += dal_sum
        ddt_ref[...] += ddt_sum

    blk = (CHUNK, wa)
    in_specs = [
        pl.BlockSpec(blk, lambda d, n: (cb(d, n), 0)),
        pl.BlockSpec(blk, lambda d, n: (cb(d, n), 1)),
        pl.BlockSpec(blk, lambda d, n: (cb(d, n), 2)),
        pl.BlockSpec((CHUNK, LANES), lambda d, n: (cb(d, n), ba_blk)),
        pl.BlockSpec((1, LANES), lambda d, n: (0, 0)),
        pl.BlockSpec((1, LANES), lambda d, n: (0, 0)),
        pl.BlockSpec((h, None, HEAD_DIM, HEAD_DIM), lambda d, n: (d, nc - 1 - n, 0, 0)),
        pl.BlockSpec(blk, lambda d, n: (cb(d, n), 0)),
        pl.BlockSpec((None, None, h // hb, hb * CHUNK, hb * CHUNK), lambda d, n: (d, nc - 1 - n, 0, 0, 0)),
    ]
    out_specs = [
        pl.BlockSpec((None, 3, CHUNK, wa), lambda d, n: (d, 0, cb(d, n), 0)),
        pl.BlockSpec((None, CHUNK, LANES), lambda d, n: (d, cb(d, n), 0)),
        pl.BlockSpec((None, 1, LANES), lambda d, n: (d, 0, 0)),
        pl.BlockSpec((None, 1, LANES), lambda d, n: (d, 0, 0)),
    ]
    out_shape = [jax.ShapeDtypeStruct((2, 3, nrow, wa), F32),
                 jax.ShapeDtypeStruct((2, nrow, LANES), F32),
                 jax.ShapeDtypeStruct((2, 1, LANES), F32),
                 jax.ShapeDtypeStruct((2, 1, LANES), F32)]
    return _call(body, grid=(2, nc), in_specs=in_specs, out_specs=out_specs, out_shape=out_shape,
                 scratch_shapes=[pltpu.VMEM((h, HEAD_DIM, HEAD_DIM), F32)], sem=("parallel", "arbitrary"),
                 name=name, args=(qkv, qkv, qkv, proj, alog_row, dtb_row, s_hist, do, t_hist), side=side)


def s5_mats(lam_re, lam_im, log_dt, b_re, b_im, c_re, c_im):
    g = lam_re.shape[1]
    nb = g // S5_BLOCK_GROUPS
    dt = jnp.exp(log_dt)[..., None]
    mag = jnp.exp(lam_re * dt)
    ar = mag * jnp.cos(lam_im * dt)
    ai = mag * jnp.sin(lam_im * dt)
    den = lam_re * lam_re + lam_im * lam_im
    fr = ((ar - 1.0) * lam_re + ai * lam_im) / den
    fi = (ai * lam_re - (ar - 1.0) * lam_im) / den
    bbr = fr[..., None] * b_re - fi[..., None] * b_im
    bbi = fr[..., None] * b_im + fi[..., None] * b_re
    eye = jnp.eye(S5_BLOCK_GROUPS, dtype=F32)
    shp = (2, nb, S5_BLOCK_GROUPS, S5_STATE, S5_GROUP_CH)
    w_r = jnp.einsum('dsjpc,jk->dsjckp', bbr.reshape(shp), eye).reshape(2, nb, LANES, S5_BLOCK_STATE)
    w_i = jnp.einsum('dsjpc,jk->dsjckp', bbi.reshape(shp), eye).reshape(2, nb, LANES, S5_BLOCK_STATE)
    w = jnp.concatenate([w_r, w_i], axis=-1)
    shc = (2, nb, S5_BLOCK_GROUPS, S5_GROUP_CH, S5_STATE)
    c_r = jnp.einsum('dsjcp,jk->dsjpkc', c_re.reshape(shc), eye).reshape(2, nb, S5_BLOCK_STATE, LANES)
    c_i = jnp.einsum('dsjcp,jk->dsjpkc', c_im.reshape(shc), eye).reshape(2, nb, S5_BLOCK_STATE, LANES)
    cm = jnp.concatenate([c_r, -c_i], axis=-2)
    return (ar.reshape(2, nb, 1, S5_BLOCK_STATE), ai.reshape(2, nb, 1, S5_BLOCK_STATE), w, cm)


_S5_ROWS = 512
S5_SEGMENTS = 8


def segment_major(a):
    n, c = a.shape
    return jnp.transpose(a.reshape(S5_SEGMENTS, n // S5_SEGMENTS, c), (1, 0, 2)).reshape(n, c)


def time_major(a):
    n, c = a.shape
    return jnp.transpose(a.reshape(n // S5_SEGMENTS, S5_SEGMENTS, c), (1, 0, 2)).reshape(n, c)


def _segmented_scan(x_ref, a_r, a_i, *, reverse, visit=None):
    nrow, two_hs = x_ref.shape
    hs = two_hs // 2
    steps = nrow // S5_SEGMENTS
    assert steps & (steps - 1) == 0
    b_r = jnp.broadcast_to(a_r, (S5_SEGMENTS, hs))
    b_i = jnp.broadcast_to(a_i, (S5_SEGMENTS, hs))
    rid = lax.broadcasted_iota(jnp.int32, (S5_SEGMENTS, hs), 0)

    def rows_of(i):
        return pl.ds(pl.multiple_of((steps - 1 - i if reverse else i) * S5_SEGMENTS, S5_SEGMENTS), S5_SEGMENTS)

    def local(i, carry):
        s_r, s_i = carry
        rows = rows_of(i)
        n_r = b_r * s_r - b_i * s_i + x_ref[rows, pl.ds(0, hs)]
        n_i = b_r * s_i + b_i * s_r + x_ref[rows, pl.ds(hs, hs)]
        x_ref[rows, pl.ds(0, hs)] = n_r
        x_ref[rows, pl.ds(hs, hs)] = n_i
        return n_r, n_i

    z = jnp.zeros((S5_SEGMENTS, hs), F32)
    e_r, e_i = lax.fori_loop(0, steps, local, (z, z))
    q_r, q_i = a_r, a_i
    for _ in range(steps.bit_length() - 1):
        q_r, q_i = q_r * q_r - q_i * q_i, 2.0 * q_r * q_i
    c_r = jnp.zeros((1, hs), F32)
    c_i = jnp.zeros((1, hs), F32)
    en_r, en_i = z, z
    for s in (reversed(range(S5_SEGMENTS)) if reverse else range(S5_SEGMENTS)):
        en_r = jnp.where(rid == s, c_r, en_r)
        en_i = jnp.where(rid == s, c_i, en_i)
        c_r, c_i = (e_r[s:s + 1] + q_r * c_r - q_i * c_i, e_i[s:s + 1] + q_r * c_i + q_i * c_r)

    def fix(i, carry):
        p_r, p_i = carry
        rows = rows_of(i)
        x_r = x_ref[rows, pl.ds(0, hs)] + (p_r * en_r - p_i * en_i)
        x_i = x_ref[rows, pl.ds(hs, hs)] + (p_r * en_i + p_i * en_r)
        x_ref[rows, pl.ds(0, hs)] = x_r
        x_ref[rows, pl.ds(hs, hs)] = x_i
        if visit is not None:
            visit(steps - 1 - i if reverse else i, rows, x_r, x_i)
        return p_r * b_r - p_i * b_i, p_r * b_i + p_i * b_r

    return lax.fori_loop(0, steps, fix, (b_r, b_i))


def s5_fwd(u_seg, ar, ai, w, cm, *, name):
    nrow = u_seg.shape[0]
    nb = w.shape[1]
    hs = S5_BLOCK_STATE
    rs = min(_S5_ROWS, nrow)

    def body(u_ref, ar_ref, ai_ref, w_ref, cm_ref, y_ref, x_ref):
        d = pl.program_id(0)
        wb = w_ref[...].astype(MXU_DT)
        for r0 in range(0, nrow, rs):
            x_ref[pl.ds(r0, rs), :] = _dot(u_ref[pl.ds(r0, rs), :].astype(MXU_DT), wb, _NN)
        @pl.when(d == 0)
        def _():
            _segmented_scan(x_ref, ar_ref[...], ai_ref[...], reverse=False)

        @pl.when(d == 1)
        def _():
            _segmented_scan(x_ref, ar_ref[...], ai_ref[...], reverse=True)

        cb = cm_ref[...].astype(MXU_DT)
        for r0 in range(0, nrow, rs):
            y_ref[pl.ds(r0, rs), :] = _dot(x_ref[pl.ds(r0, rs), :].astype(MXU_DT), cb, _NN)

    in_specs = [
        pl.BlockSpec((nrow, LANES), lambda d, s: (0, s)),
        pl.BlockSpec((None, None, 1, hs), lambda d, s: (d, s, 0, 0)),
        pl.BlockSpec((None, None, 1, hs), lambda d, s: (d, s, 0, 0)),
        pl.BlockSpec((None, None, LANES, 2 * hs), lambda d, s: (d, s, 0, 0)),
        pl.BlockSpec((None, None, 2 * hs, LANES), lambda d, s: (d, s, 0, 0)),
    ]
    out_specs = [
        pl.BlockSpec((None, nrow, LANES), lambda d, s: (d, 0, s)),
        pl.BlockSpec((None, nrow, 2 * hs), lambda d, s: (d, 0, s)),
    ]
    out_shape = [jax.ShapeDtypeStruct((2, nrow, nb * LANES), F32),
                 jax.ShapeDtypeStruct((2, nrow, nb * 2 * hs), F32)]
    return pl.pallas_call(
        body, grid=(2, nb), in_specs=in_specs, out_specs=out_specs, out_shape=out_shape,
        compiler_params=_params(("parallel", "parallel")), name=name)(u_seg, ar, ai, w, cm)


def s5_bwd(u_seg, ar, ai, w, cm, xs, dy, du_in, *, side=None, name):
    nrow = u_seg.shape[0]
    nb = w.shape[1]
    steps = nrow // S5_SEGMENTS
    hs = S5_BLOCK_STATE
    rs = min(_S5_ROWS, nrow)

    def body(u_ref, dy_ref, dui_ref, x_ref, ar_ref, ai_ref, w_ref, cm_ref,
             du_ref, dw_ref, dcm_ref, dar_ref, dai_ref, g_ref, acc_ref):
        d = pl.program_id(1)
        cb = cm_ref[...].astype(MXU_DT)
        for r0 in range(0, nrow, rs):
            g_ref[pl.ds(r0, rs), :] = _dot(dy_ref[pl.ds(r0, rs), :].astype(MXU_DT), cb, _NT)
        acc_ref[...] = jnp.zeros_like(acc_ref)
        rid = lax.broadcasted_iota(jnp.int32, (S5_SEGMENTS, 2 * hs), 0)

        def run(forward_recurrence):
            last = pl.ds((steps - 1) * S5_SEGMENTS, S5_SEGMENTS)
            first = pl.ds(0, S5_SEGMENTS)
            if forward_recurrence:
                wrap = jnp.where(rid == 0, 0.0, pltpu.roll(x_ref[last, :], 1, 0))
            else:
                wrap = jnp.where(rid == S5_SEGMENTS - 1, 0.0, pltpu.roll(x_ref[first, :], S5_SEGMENTS - 1, 0))

            def visit(step, rows, g_r, g_i):
                if forward_recurrence:
                    nbr = jnp.maximum(step - 1, 0)
                    edge = step == 0
                else:
                    nbr = jnp.minimum(step + 1, steps - 1)
                    edge = step == steps - 1
                prev = x_ref[pl.ds(pl.multiple_of(nbr * S5_SEGMENTS, S5_SEGMENTS), S5_SEGMENTS), :]
                prev = jnp.where(edge, wrap, prev)
                p_r, p_i = prev[:, :hs], prev[:, hs:]
                acc_ref[:, pl.ds(0, hs)] += g_r * p_r + g_i * p_i
                acc_ref[:, pl.ds(hs, hs)] += g_i * p_r - g_r * p_i

            _segmented_scan(g_ref, ar_ref[...], -ai_ref[...], reverse=forward_recurrence, visit=visit)

        @pl.when(d == 0)
        def _():
            run(True)

        @pl.when(d == 1)
        def _():
            run(False)

        dar_ref[...] = jnp.sum(acc_ref[:, pl.ds(0, hs)], axis=0, keepdims=True)
        dai_ref[...] = jnp.sum(acc_ref[:, pl.ds(hs, hs)], axis=0, keepdims=True)

        wb = w_ref[...].astype(MXU_DT)
        for r0 in range(0, nrow, rs):
            part = _dot(g_ref[pl.ds(r0, rs), :].astype(MXU_DT), wb, _NT)

            @pl.when(d == 0)
            def _(part=part, r0=r0):
                du_ref[pl.ds(r0, rs), :] = dui_ref[pl.ds(r0, rs), :] + part

            @pl.when(d == 1)
            def _(part=part, r0=r0):
                du_ref[pl.ds(r0, rs), :] += part

        dw_ref[...] = _dot(u_ref[...].astype(MXU_DT), g_ref[...].astype(MXU_DT), _TN)
        dcm_ref[...] = _dot(x_ref[...].astype(MXU_DT), dy_ref[...].astype(MXU_DT), _TN)

    in_specs = [
        pl.BlockSpec((nrow, LANES), lambda s, d: (0, s)),
        pl.BlockSpec((nrow, LANES), lambda s, d: (0, s)),
        pl.BlockSpec((nrow, LANES), lambda s, d: (0, s)),
        pl.BlockSpec((None, nrow, 2 * hs), lambda s, d: (d, 0, s)),
        pl.BlockSpec((None, None, 1, hs), lambda s, d: (d, s, 0, 0)),
        pl.BlockSpec((None, None, 1, hs), lambda s, d: (d, s, 0, 0)),
        pl.BlockSpec((None, None, LANES, 2 * hs), lambda s, d: (d, s, 0, 0)),
        pl.BlockSpec((None, None, 2 * hs, LANES), lambda s, d: (d, s, 0, 0)),
    ]
    out_specs = [
        pl.BlockSpec((nrow, LANES), lambda s, d: (0, s)),
        pl.BlockSpec((None, None, LANES, 2 * hs), lambda s, d: (d, s, 0, 0)),
        pl.BlockSpec((None, None, 2 * hs, LANES), lambda s, d: (d, s, 0, 0)),
        pl.BlockSpec((None, None, 1, hs), lambda s, d: (d, s, 0, 0)),
        pl.BlockSpec((None, None, 1, hs), lambda s, d: (d, s, 0, 0)),
    ]
    out_shape = [jax.ShapeDtypeStruct((nrow, nb * LANES), F32),
                 jax.ShapeDtypeStruct(w.shape, F32), jax.ShapeDtypeStruct(cm.shape, F32),
                 jax.ShapeDtypeStruct(ar.shape, F32), jax.ShapeDtypeStruct(ai.shape, F32)]
    return _call(body, grid=(nb, 2), in_specs=in_specs, out_specs=out_specs, out_shape=out_shape,
                 scratch_shapes=[pltpu.VMEM((nrow, 2 * hs), F32), pltpu.VMEM((S5_SEGMENTS, 2 * hs), F32)],
                 sem=("parallel", "arbitrary"), name=name, args=(u_seg, dy, du_in, xs, ar, ai, w, cm), side=side)


def _me():
    return lax.axis_index("x"), lax.axis_index("y"), lax.axis_index("c")


def all_gather(shards):
    na = len(shards)

    def plan(x_refs, out_refs, sems):
        send_sems, recv_sems, local_sems = sems
        x, y, c = _me()
        me, sibling = (x, y, c), (x, y, 1 - c)
        chips = [(1 - x, y), (x, 1 - y), (1 - x, 1 - y)]

        def slot(a, px, py, pc):
            return out_refs[a].at[4 * px + 2 * py + pc]

        def copy(a, k, block, to, src=None):
            return pltpu.make_async_remote_copy(
                src_ref=slot(a, *block) if src is None else src, dst_ref=slot(a, *block),
                send_sem=send_sems.at[a, k], recv_sem=recv_sems.at[a, k], device_id=to, device_id_type=MESH)

        mine = [pltpu.make_async_copy(x_refs[a], slot(a, *me), local_sems.at[a]) for a in range(na)]
        first = []
        for a in range(na):
            first.append(copy(a, 0, me, sibling, src=x_refs[a]))
            first += [copy(a, 1 + j, me, (*chip, c), src=x_refs[a]) for j, chip in enumerate(chips)]
        return me, sibling, chips, c, copy, mine, first

    def start(x_refs, out_refs, sems):
        _, _, _, _, _, mine, first = plan(x_refs, out_refs, sems)
        for cp in mine + first:
            cp.start()

    def finish(x_refs, out_refs, sems):
        me, sibling, chips, c, copy, mine, first = plan(x_refs, out_refs, sems)
        passed = []
        for j, chip in enumerate(chips):
            for a in range(na):
                copy(a, 1 + j, (*chip, c), me).wait_recv()
                fwd = copy(a, 4 + j, (*chip, c), sibling)
                fwd.start()
                passed.append(fwd)
        for a in range(na):
            copy(a, 0, sibling, me).wait_recv()
            for j, chip in enumerate(chips):
                copy(a, 4 + j, (*chip, 1 - c), me).wait_recv()
        for cp in first + passed:
            cp.wait_send()
        for cp in mine:
            cp.wait()

    return Side(shards, [jax.ShapeDtypeStruct((8,) + s.shape, s.dtype) for s in shards],
                [pltpu.SemaphoreType.DMA((na, 7)), pltpu.SemaphoreType.DMA((na, 7)), pltpu.SemaphoreType.DMA((na,))],
                start, finish)


_AXES = ("x", "y", "c")


def exchange(bufs, axes, *, half):
    na = len(bufs)
    if isinstance(half, bool):
        half = [half] * na

    def copies(in_refs, out_refs, sems):
        send_sems, recv_sems = sems
        me = _me()
        cps = []
        for a in range(na):
            bit = me[_AXES.index(axes[a])]
            peer = tuple(1 - v if ax == axes[a] else v for ax, v in zip(_AXES, me))
            cps.append(pltpu.make_async_remote_copy(
                src_ref=in_refs[a].at[1 - bit] if half[a] else in_refs[a], dst_ref=out_refs[a],
                send_sem=send_sems.at[a], recv_sem=recv_sems.at[a], device_id=peer, device_id_type=MESH))
        return cps

    def start(*refs):
        for cp in copies(*refs):
            cp.start()

    def finish(*refs):
        for cp in copies(*refs):
            cp.wait()

    return Side(bufs, [jax.ShapeDtypeStruct(b.shape[1:] if h else b.shape, b.dtype) for b, h in zip(bufs, half)],
                [pltpu.SemaphoreType.DMA((na,)), pltpu.SemaphoreType.DMA((na,))], start, finish)


def all_to_all(pieces, slot_fns):
    na = len(pieces)

    def copies(in_refs, out_refs, sems):
        send_sems, recv_sems, local_sems = sems
        x, y, c = _me()
        cps = []
        for a in range(na):
            mine = slot_fns[a](x, y, c)
            for k in range(8):
                tx, ty, tc = x ^ (k // 4), y ^ ((k // 2) % 2), c ^ (k % 2)
                src = in_refs[a].at[4 * tx + 2 * ty + tc]
                dst = out_refs[a].at[mine]
                if k == 0:
                    cps.append(pltpu.make_async_copy(src, dst, local_sems.at[a]))
                else:
                    cps.append(pltpu.make_async_remote_copy(
                        src_ref=src, dst_ref=dst, send_sem=send_sems.at[a, k - 1], recv_sem=recv_sems.at[a, k - 1],
                        device_id=(tx, ty, tc), device_id_type=MESH))
        return cps

    def start(*refs):
        for cp in copies(*refs):
            cp.start()

    def finish(*refs):
        for cp in copies(*refs):
            cp.wait()

    return Side(pieces, [jax.ShapeDtypeStruct(p.shape, p.dtype) for p in pieces],
                [pltpu.SemaphoreType.DMA((na, 7)), pltpu.SemaphoreType.DMA((na, 7)), pltpu.SemaphoreType.DMA((na,))],
                start, finish)


def add_half(buf, recv, bit, *, narrow, name):
    c = recv.shape[-1]
    r = math.prod(recv.shape[:-1])
    t = _tile(r, max(16, (ADD_BLOCK_BYTES // (4 * c)) // 16 * 16), 16)

    def body(bit_ref, a_ref, b_ref, o_ref, *tx_ref):
        s = a_ref[...] + b_ref[...].astype(F32)
        o_ref[...] = s
        if narrow:
            tx_ref[0][...] = s.astype(WIRE_DT)

    o_spec = pl.BlockSpec((t, c), lambda i, b: (i, 0))
    grid_spec = pltpu.PrefetchScalarGridSpec(
        num_scalar_prefetch=1, grid=(r // t,),
        in_specs=[pl.BlockSpec((None, t, c), lambda i, b: (b[0], i, 0)), o_spec],
        out_specs=[o_spec, o_spec] if narrow else [o_spec])
    out_shape = [jax.ShapeDtypeStruct((r, c), F32)] + ([jax.ShapeDtypeStruct((r, c), WIRE_DT)] if narrow else [])
    outs = pl.pallas_call(body, grid_spec=grid_spec, out_shape=out_shape,
                          compiler_params=_params(("parallel",)), name=name)(
                              bit, buf.reshape(2, r, c), recv.reshape(r, c))
    return [o.reshape(recv.shape) for o in outs]


def _pack_rows(n, cols):
    return -(-n // (8 * cols)) * 8


def _pack(arrs, cols, mult):
    parts = []
    for a in arrs:
        n = math.prod(a.shape)
        nr = _pack_rows(n, cols)
        parts.append(jnp.pad(a.reshape(-1), (0, nr * cols - n)).reshape(nr, cols))
    total = sum(p.shape[0] for p in parts)
    pad_rows = -(-total // mult) * mult - total
    if pad_rows:
        parts.append(jnp.zeros((pad_rows, cols), arrs[0].dtype))
    return jnp.concatenate(parts, axis=0)


def _unpack(buf, shapes):
    out, r = [], 0
    cols = buf.shape[1]
    for s in shapes:
        n = math.prod(s)
        nr = _pack_rows(n, cols)
        piece = lax.optimization_barrier(buf[r:r + nr])
        out.append(piece.reshape(-1)[:n].reshape(s))
        r += nr
    return out


def _elementwise(fn, arrs, nout, name):
    r, c = arrs[0].shape
    lanes = -(-c // LANES) * LANES
    t = _tile(r, max(ROW_TILE, (ELEMENTWISE_BLOCK_BYTES // (4 * lanes)) // 8 * 8), 8)
    return rw_fwd(fn, [rows(a, t) for a in arrs], [(c, False)] * nout, nrow=r, t=t, name=name)


def kernel(x, ln_g, w_in, conv_w, a_log, dt_bias, head_norm_g, lam_re, lam_im, log_dt, b_re, b_im, c_re, c_im, d_skip, w_glu, b_glu, w_pa, w_pb, b_gate, w_out, final_g, loss_target, m_ln_g, m_w_in, m_conv_w, m_a_log, m_dt_bias, m_head_norm_g, m_lam_re, m_lam_im, m_log_dt, m_b_re, m_b_im, m_c_re, m_c_im, m_d_skip, m_w_glu, m_b_glu, m_w_pa, m_w_pb, m_b_gate, m_w_out, m_final_g, v_ln_g, v_w_in, v_conv_w, v_a_log, v_dt_bias, v_head_norm_g, v_lam_re, v_lam_im, v_log_dt, v_b_re, v_b_im, v_c_re, v_c_im, v_d_skip, v_w_glu, v_b_glu, v_w_pa, v_w_pb, v_b_gate, v_w_out, v_final_g):
    env = dict(locals())
    wts = {n: env[n] for n in WEIGHTS}
    mom_m = {n: env["m_" + n] for n in WEIGHTS}
    mom_v = {n: env["v_" + n] for n in WEIGHTS}

    xin = x[0]
    tgt = loss_target[0]
    nrow, dm = xin.shape
    depth = ln_g.shape[0]
    nh = dm // (2 * HEAD_DIM)
    wa = nh * HEAD_DIM
    wb = dm // 2
    ngrp = wb // S5_GROUP_CH
    pw = w_in.shape[-1]
    t = min(ROW_TILE, nrow)

    o_ba = 4 * wa
    o_u = o_ba + 4 * nh
    n_main = 4 * wa + 2 * wb + 2 * dm
    projp = -(-(n_main + LANES) // 512) * 512
    blk_za = 3 * wa // HEAD_DIM
    blk_u = 4 * wa // LANES
    ba_blk = n_main // LANES
    cx, cy, cc = _me()

    jb = o_ba // pw
    assert (o_u - 1) // pw == jb
    cut_lo, cut = o_ba - jb * pw, o_u - o_ba
    wide = -(-pw // LANES) * LANES + LANES
    col = lax.broadcasted_iota(jnp.int32, (pw, wide), 1)
    row = lax.broadcasted_iota(jnp.int32, (pw, wide), 0)
    holds_cut = (4 * cx + 2 * cy + cc) == jb
    src_plain = jnp.where(col < pw, col, -1)
    src_cut = jnp.where(col < cut_lo, col, jnp.where(col < pw - cut, col + cut, -1))
    src_cut = jnp.where((col >= wide - LANES) & (col < wide - LANES + cut), col - (wide - LANES) + cut_lo, src_cut)
    select = (row == jnp.where(holds_cut, src_cut, src_plain)).astype(WIRE_DT)
    w_in_tx = mm(w_in.reshape(depth * dm, pw), select, name="w_in_prepare").astype(WIRE_DT).reshape(depth, dm, wide)

    def gather_side(l):
        return all_gather([w_in_tx[l], w_glu[l].astype(WIRE_DT), w_pa[l].astype(WIRE_DT),
                           w_pb[l].astype(WIRE_DT), w_out[l].astype(WIRE_DT), conv_w[l]])

    def cat(g):
        return jnp.concatenate([g[j] for j in range(8)], axis=1)

    def assemble(gathered):
        g_in, g_glu, g_pa, g_pb, g_out, g_conv = gathered
        w_perm = jnp.concatenate(
            [g_in[j][:, :pw - cut if j == jb else pw] for j in range(8)]
            + [g_in[jb][:, wide - LANES:], jnp.zeros((dm, projp - n_main - LANES), WIRE_DT)], axis=1)
        return dict(w_in=w_perm, w_glu=g_glu.reshape(wb, wb), w_pa=cat(g_pa), w_pb=cat(g_pb),
                    w_out=g_out.reshape(dm, dm), conv_w=cat(g_conv))

    full = [assemble(run_side(gather_side(0), name="gather_weights"))]

    def small(l):
        z = jnp.zeros((1, LANES - 4 * nh), F32)
        alog_row = jnp.concatenate([jnp.zeros((1, 2 * nh), F32), a_log[l].reshape(1, 2 * nh), z], axis=1)
        dtb_row = jnp.concatenate([jnp.zeros((1, 2 * nh), F32), dt_bias[l].reshape(1, 2 * nh), z], axis=1)
        return alog_row, dtb_row

    saved = []
    cur = xin
    for l in range(depth):
        fw = full[l]
        (hh,) = rw_fwd(fn_rms, [rows(cur, t), bcast(ln_g[l][None])], [(dm, False)], nrow=nrow, t=t, name="rms_fwd")
        proj = mm(hh, fw["w_in"], name="proj_fwd")
        (qkv,) = rw_fwd(make_fn_prep(nh), [rows(proj, nrow, HEAD_DIM, 0, True), bcast(fw["conv_w"], HEAD_DIM, 0, True)],
                        [(HEAD_DIM, True)], nrow=nrow, t=nrow, ncol=3 * nh, name="prep_fwd")
        alog_row, dtb_row = small(l)
        nxt_gather = gather_side(l + 1) if l + 1 < depth else None
        o_dir, s_hist, t_hist = gdn_fwd(qkv, proj, ba_blk, alog_row, dtb_row, n_heads=nh, side=nxt_gather,
                                        name="gdn_fwd")
        if nxt_gather is not None:
            full.append(assemble(nxt_gather.result))
        hn_ins = [rows3(o_dir, 0, t, HEAD_DIM), rows3(o_dir, 1, t, HEAD_DIM), rows(proj, t, HEAD_DIM, blk_za, True),
                  bcast(head_norm_g[l][None])]
        (ya_in,) = rw_fwd(fn_headnorm, hn_ins, [(HEAD_DIM, True)], nrow=nrow, t=t, ncol=nh, name="headnorm_fwd")
        mats, mats_vjp = jax.vjp(s5_mats, lam_re[l], lam_im[l], log_dt[l], b_re[l], b_im[l], c_re[l], c_im[l])
        u_seg = segment_major(proj[:, 4 * wa:4 * wa + wb])
        yd, xs = s5_fwd(u_seg, *mats, name="s5_fwd")
        ge_ins = [rows3(yd, 0, t, wb, False), rows3(yd, 1, t, wb, False), rows(u_seg, t), bcast(d_skip[l][None])]
        (ys_seg,) = rw_fwd(fn_gelu, ge_ins, [(wb, False)], nrow=nrow, t=t, name="gelu_fwd")
        ys = time_major(ys_seg)
        glu = mm(ys, fw["w_glu"], name="glu_fwd")
        gl_ins = [rows(ys, t), rows(glu, t), rows(proj, t, wb, (4 * wa + wb) // wb), bcast(b_glu[l][None])]
        (yb_in,) = rw_fwd(fn_glu, gl_ins, [(wb, False)], nrow=nrow, t=t, name="glugate_fwd")
        y_a = mm(ya_in, fw["w_pa"], name="pa_fwd")
        y_b = mm(yb_in, fw["w_pb"], name="pb_fwd")
        bg = b_gate[l][None]
        mg_ins = [rows(proj, t, dm, 3), rows(proj, t, dm, 4), rows(y_a, t), rows(y_b, t),
                  bcast(bg, dm, 0), bcast(bg, dm, 1)]
        (merged,) = rw_fwd(fn_merge, mg_ins, [(dm, False)], nrow=nrow, t=t, name="merge_fwd")
        nxt = mm(merged, fw["w_out"], add=cur, name="out_fwd")
        saved.append(dict(x=cur, h=hh, proj=proj, qkv=qkv, o_dir=o_dir, s_hist=s_hist, t_hist=t_hist, hn_ins=hn_ins,
                          mats=mats, mats_vjp=mats_vjp, xs=xs, ge_ins=ge_ins, ys=ys, gl_ins=gl_ins, u_seg=u_seg,
                          ya_in=ya_in, yb_in=yb_in, mg_ins=mg_ins, merged=merged,
                          alog_row=alog_row, dtb_row=dtb_row))
        cur = nxt

    loss_ins = [rows(cur, t), rows(tgt, t), bcast(final_g[None])]
    (row_loss,) = rw_fwd(fn_loss, loss_ins, [(1, False)], nrow=nrow, t=t, name="loss_fwd")
    ones = jnp.ones((nrow, 1), F32)
    dcur, dfinal = rw_bwd(fn_loss, loss_ins, [[rows(ones, t)]], nrow=nrow, t=t,
                          row_grads=[(0, False)], bc_grads=[2], name="loss_bwd")
    loss = lax.psum(jnp.sum(row_loss), ("x", "y", "c"))

    coord = {"x": cx, "y": cy, "c": cc}
    routes = {"cxy": ("c", "x", "y"), "cyx": ("c", "y", "x")}
    route = ["cxy", "cyx", "cxy", "cxy", "cyx", "cyx", "cxy"]
    conv_order = [4 * ((q // 2) % 2) + 2 * (q % 2) + q // 4 for q in range(8)]
    cw = 3 * wa // 8
    g_final = {n: [None] * depth for n in SHARDED}
    gsh = {n: [None] * depth for n in SHARDED}

    def own_columns(z):
        un = jnp.concatenate([z[:, :o_ba], z[:, n_main:n_main + 4 * nh], z[:, o_ba:n_main]], axis=1)
        return jnp.stack([un[:, d * pw:(d + 1) * pw] for d in range(8)])

    def rs_begin(l):
        in_a, in_b = gsh["w_in"][l]
        conv_buf = jnp.stack([gsh["conv_w"][l][:, d * cw:(d + 1) * cw] for d in conv_order])
        return dict(layer=l, stage=0, bufs=[
            in_a.reshape(8, dm // 16, projp), in_b.reshape(8, dm // 16, projp), gsh["w_out"][l],
            gsh["w_glu"][l], gsh["w_pa"][l], gsh["w_pb"][l], conv_buf])

    def rs_side(st):
        if st is None:
            return None
        if st["stage"] < 3:
            st["axes"] = [routes[r][st["stage"]] for r in route]

            def halves(b):
                return b.reshape((2, b.shape[0] // 2) + b.shape[1:])

            st["bufs"] = [halves(b) for b in st["bufs"]]
            send = [halves(b) for b in st["tx"]] if st.get("tx") else st["bufs"]
            if st.get("whole") is not None:
                return exchange(send + [st["whole"]], st["axes"] + [routes["cxy"][st["stage"]]],
                                half=[True] * len(send) + [False])
            return exchange(send, st["axes"], half=True)
        st["bufs"] = [b.reshape(b.shape[1:]) for b in st["bufs"]]
        return all_to_all([own_columns(st["bufs"][0]), own_columns(st["bufs"][1])],
                          [lambda x, y, c: 4 * c + 2 * x + y, lambda x, y, c: 4 * c + 2 * y + x])

    def rs_absorb(st, side):
        if st is None:
            return
        if st["stage"] < 3:
            if st.get("whole") is not None:
                (st["whole"],) = _elementwise(make_fn_sum(2), [st["whole"], side.result[-1]], 1, "ar_add")
            narrow = st["stage"] < 2
            sums = [add_half(b, r, coord[ax].astype(jnp.int32).reshape(1), narrow=narrow, name=f"rs_add_{st['stage']}")
                    for b, r, ax in zip(st["bufs"], side.result, st["axes"])]
            st["bufs"] = [s[0] for s in sums]
            st["tx"] = [s[1] for s in sums] if narrow else None
        else:
            got_a, got_b = side.result
            g_final["w_in"][st["layer"]] = jnp.concatenate(
                [got_a.reshape(dm // 2, pw), got_b.reshape(dm // 2, pw)], axis=0)
            for n, b in zip(["w_out", "w_glu", "w_pa", "w_pb", "conv_w"], st["bufs"][2:]):
                g_final[n][st["layer"]] = b
        st["stage"] += 1

    grep = {n: [None] * depth for n in REPLICATED if n != "final_g"}
    pending = None
    for l in reversed(range(depth)):
        fw, sv = full[l], saved[l]
        dmerged = mm(dcur, fw["w_out"], tb=True, name="out_bwd_x")
        gsh["w_out"][l] = mm(sv["merged"], dcur, ta=True, scatter=("rows", slot_cxy), name="out_bwd_w")
        side = rs_side(pending)
        dla, dlb, dya, dyb, dbga, dbgb = rw_bwd(
            fn_merge, sv["mg_ins"], [[rows(dmerged, t)]], nrow=nrow, t=t,
            row_grads=[(0, False), (1, False), (2, False), (3, False)], bc_grads=[4, 5], side=side, name="merge_bwd")
        rs_absorb(pending, side)
        grep["b_gate"][l] = jnp.concatenate([dbga.reshape(dm), dbgb.reshape(dm)])
        dya_in = mm(dya, fw["w_pa"], tb=True, name="pa_bwd_x")
        gsh["w_pa"][l] = mm(sv["ya_in"], dya, ta=True, scatter=("cols", slot_cyx), name="pa_bwd_w")
        dyb_in = mm(dyb, fw["w_pb"], tb=True, name="pb_bwd_x")
        gsh["w_pb"][l] = mm(sv["yb_in"], dyb, ta=True, scatter=("cols", slot_cyx), name="pb_bwd_w")
        dys1, dglu, dzb, dbglu = rw_bwd(
            fn_glu, sv["gl_ins"], [[rows(dyb_in, t)]], nrow=nrow, t=t,
            row_grads=[(0, False), (1, False), (2, False)], bc_grads=[3], name="glugate_bwd")
        grep["b_glu"][l] = dbglu.reshape(wb)
        dys2 = mm(dglu, fw["w_glu"], tb=True, name="glu_bwd_x")
        gsh["w_glu"][l] = mm(sv["ys"], dglu, ta=True, scatter=("rows", slot_cxy), name="glu_bwd_w")
        dyd, du1, ddskip = rw_bwd(
            fn_gelu, sv["ge_ins"], [[rows(segment_major(dys1), t), rows(segment_major(dys2), t)]], nrow=nrow, t=t,
            row_grads=[(0, False), (2, False)], bc_grads=[3], name="gelu_bwd")
        grep["d_skip"][l] = ddskip.reshape(wb)
        side = rs_side(pending)
        du_seg, dw_s5, dcm_s5, dar, dai = s5_bwd(sv["u_seg"], *sv["mats"], sv["xs"], dyd, du1, side=side,
                                                 name="s5_bwd")
        du = time_major(du_seg)
        rs_absorb(pending, side)
        g_lr, g_li, g_ldt, g_br, g_bi, g_cr, g_ci = sv["mats_vjp"]((dar, dai, dw_s5, dcm_s5))
        for nme, val in zip(["lam_re", "lam_im", "log_dt", "b_re", "b_im", "c_re", "c_im"],
                            [g_lr, g_li, g_ldt, g_br, g_bi, g_cr, g_ci]):
            grep[nme][l] = val
        do, dza, dhn = rw_bwd(fn_headnorm, sv["hn_ins"], [[rows(dya_in, t, HEAD_DIM, 0, True)]], nrow=nrow, t=t,
                              ncol=nh, row_grads=[(0, True), (2, True)], bc_grads=[3], name="headnorm_bwd")
        grep["head_norm_g"][l] = jnp.sum(dhn, axis=0).reshape(HEAD_DIM)
        side = rs_side(pending)
        dqkv, dba_all, dal, ddt = gdn_bwd(sv["qkv"], sv["proj"], ba_blk, sv["alog_row"], sv["dtb_row"],
                                          sv["s_hist"], sv["t_hist"], do, n_heads=nh, side=side, name="gdn_bwd")
        rs_absorb(pending, side)
        grep["a_log"][l] = jnp.sum(dal, axis=(0, 1))[2 * nh:4 * nh].reshape(2, nh)
        grep["dt_bias"][l] = jnp.sum(ddt, axis=(0, 1))[2 * nh:4 * nh].reshape(2, nh)
        (dba,) = rw_fwd(make_fn_sum(2), [rows3(dba_all, p, t, LANES, False) for p in range(2)],
                        [(LANES, False)], nrow=nrow, t=t, name="dba_sum")
        prep_ins = [rows(sv["proj"], nrow, HEAD_DIM, 0, True), bcast(fw["conv_w"], HEAD_DIM, 0, True)]
        dq_cots = [(dqkv.reshape(2, 3 * nrow, wa), (None, nrow, HEAD_DIM),
                    (lambda j, i, dd=dd: (dd, j // nh, j % nh))) for dd in range(2)]
        dqkv_raw, dconv = rw_bwd(make_fn_prep(nh), prep_ins, [dq_cots], nrow=nrow, t=nrow, ncol=3 * nh,
                                 row_grads=[(0, True)], bc_grads=[1], name="prep_bwd")
        gsh["conv_w"][l] = jnp.transpose(dconv, (1, 0, 2)).reshape(CONV_K, 3 * wa)
        dproj = jnp.concatenate([dqkv_raw, dza, du, dzb, dla, dlb, dba,
                                 jnp.zeros((nrow, projp - n_main - LANES), F32)], axis=1).astype(MXU_DT)
        side = rs_side(pending)
        dh = mm(dproj, fw["w_in"], tb=True, side=side, name="proj_bwd_x")
        rs_absorb(pending, side)
        gsh["w_in"][l] = [mm(sv["h"], dproj, ta=True, m_part=(part, 2), name="proj_bwd_w") for part in range(2)]
        dcur, dlng = rw_bwd(fn_rms, [rows(sv["x"], t), bcast(ln_g[l][None])], [[rows(dh, t)]], nrow=nrow, t=t,
                            row_grads=[(0, False)], bc_grads=[1], residual=rows(dcur, t), name="rms_bwd")
        grep["ln_g"][l] = dlng.reshape(dm)
        pending = rs_begin(l)
    grad_x = dcur[None]
    rep_list = [jnp.stack(grep[n]) for n in REPLICATED if n != "final_g"] + [dfinal.reshape(dm)]
    rep_shapes = [a.shape for a in rep_list]
    pending["whole"] = _pack(rep_list, COMM_COLS, ROW_TILE)
    for stage in range(RS_STAGES):
        side = rs_side(pending)
        run_side(side, name=f"rs_stage_{stage}")
        rs_absorb(pending, side)
    grads = {n: jnp.stack(g_final[n]) for n in SHARDED}
    for n, val in zip(REPLICATED, _unpack(pending["whole"], rep_shapes)):
        grads[n] = val

    deltas, new_m, new_v = {}, {}, {}
    for n in WEIGHTS:
        shp = wts[n].shape
        two = [a.reshape(-1, shp[-1]) for a in (wts[n], grads[n], mom_m[n], mom_v[n])]
        d_, m_, v_ = _elementwise(fn_adamw, two, 3, "adamw_" + n)
        deltas[n], new_m[n], new_v[n] = d_.reshape(shp), m_.reshape(shp), v_.reshape(shp)

    return (loss, grad_x, *[grads[n] for n in WEIGHTS], *[deltas[n] for n in WEIGHTS],
            *[new_m[n] for n in WEIGHTS], *[new_v[n] for n in WEIGHTS])
```

```python
import functools
import math

import jax
import jax.numpy as jnp
from jax import lax
from jax.experimental import pallas as pl
from jax.experimental.pallas import tpu as pltpu

F32 = jnp.float32
MXU_DT = jnp.bfloat16
WIRE_DT = jnp.bfloat16

HEAD_DIM = 128
CHUNK = 64
CONV_K = 5
S5_GROUP_CH = 16
S5_STATE = 64
S5_BLOCK_GROUPS = 8
S5_BLOCK_STATE = S5_BLOCK_GROUPS * S5_STATE
RMS_EPS = 1e-6
LANES = 128
VMEM_LIMIT = 56 * 1024 * 1024
ROW_TILE = 256
COMM_COLS = 1024
ADD_BLOCK_BYTES = 2 * 1024 * 1024
RS_STAGES = 4
ELEMENTWISE_BLOCK_BYTES = 1024 * 1024

ADAM_LR = 0.001
ADAM_B1 = 0.9
ADAM_B2 = 0.999
ADAM_EPS = 1e-08
ADAM_WD = 0.01
ADAM_STEP = 10

WEIGHTS = ['ln_g', 'w_in', 'conv_w', 'a_log', 'dt_bias', 'head_norm_g', 'lam_re', 'lam_im', 'log_dt',
           'b_re', 'b_im', 'c_re', 'c_im', 'd_skip', 'w_glu', 'b_glu', 'w_pa', 'w_pb', 'b_gate',
           'w_out', 'final_g']
SHARDED = ['w_in', 'w_glu', 'w_pa', 'w_pb', 'w_out', 'conv_w']
REPLICATED = [n for n in WEIGHTS if n not in SHARDED]
MESH = pl.DeviceIdType.MESH


def _params(sem=None):
    return pltpu.CompilerParams(dimension_semantics=sem, vmem_limit_bytes=VMEM_LIMIT)


def _tile(n, cap, q=LANES):
    t = (min(n, cap) // q) * q
    while t > q and n % t:
        t -= q
    return t if t > 0 and n % t == 0 else n


class Side:
    def __init__(self, ins, out_sd, sems, start, finish):
        self.ins, self.out_sd, self.sems, self.start, self.finish = list(ins), list(out_sd), list(sems), start, finish
        self.result = None


def _call(body, *, grid, in_specs, out_specs, out_shape, scratch_shapes=(), sem, name, args, side=None):
    in_specs, out_specs, out_shape = list(in_specs), list(out_specs), list(out_shape)
    scratch_shapes = list(scratch_shapes)
    if side is None:
        return pl.pallas_call(body, grid=grid, in_specs=in_specs, out_specs=out_specs, out_shape=out_shape,
                              scratch_shapes=scratch_shapes, compiler_params=_params(sem), name=name)(*args)
    hbm = pl.BlockSpec(memory_space=pl.ANY)
    n_in, n_out, n_scr = len(in_specs), len(out_specs), len(scratch_shapes)
    s_in, s_out = len(side.ins), len(side.out_sd)

    def hosted(*refs):
        main_in, rest = refs[:n_in], refs[n_in:]
        side_in, rest = rest[:s_in], rest[s_in:]
        main_out, rest = rest[:n_out], rest[n_out:]
        side_out, rest = rest[:s_out], rest[s_out:]
        main_scr, sems = rest[:n_scr], rest[n_scr:]
        ids = [pl.program_id(k) for k in range(len(grid))]
        first = functools.reduce(jnp.logical_and, [i == 0 for i in ids])
        last = functools.reduce(jnp.logical_and, [i == g - 1 for i, g in zip(ids, grid)])

        @pl.when(first)
        def _():
            side.start(side_in, side_out, sems)

        body(*main_in, *main_out, *main_scr)

        @pl.when(last)
        def _():
            side.finish(side_in, side_out, sems)

    outs = pl.pallas_call(
        hosted, grid=grid, in_specs=in_specs + [hbm] * s_in, out_specs=out_specs + [hbm] * s_out,
        out_shape=out_shape + side.out_sd, scratch_shapes=scratch_shapes + side.sems,
        compiler_params=_params(("arbitrary",) * len(grid)), name=name)(*args, *side.ins)
    side.result = list(outs[n_out:])
    return list(outs[:n_out])


def run_side(side, *, name):
    hbm = pl.BlockSpec(memory_space=pl.ANY)
    s_in, s_out = len(side.ins), len(side.out_sd)

    def body(*refs):
        side_in, side_out, sems = refs[:s_in], refs[s_in:s_in + s_out], refs[s_in + s_out:]
        side.start(side_in, side_out, sems)
        side.finish(side_in, side_out, sems)

    side.result = list(pl.pallas_call(body, out_shape=side.out_sd, in_specs=[hbm] * s_in, out_specs=[hbm] * s_out,
                                      scratch_shapes=side.sems, name=name)(*side.ins))
    return side.result


def slot_cxy(d):
    return 4 * (d % 2) + 2 * (d // 4) + (d // 2) % 2


def slot_cyx(d):
    return 4 * (d % 2) + 2 * ((d // 2) % 2) + d // 4


def mm(a, b, *, ta=False, tb=False, add=None, m_part=None, scatter=None, side=None, name):
    m = a.shape[1] if ta else a.shape[0]
    k = a.shape[0] if ta else a.shape[1]
    n = b.shape[0] if tb else b.shape[1]
    m_off = 0
    if m_part is not None:
        m = m // m_part[1]
        m_off = m_part[0]
    tm, tn, tk = _tile(m, 1024), _tile(n, 512), _tile(k, 2048)
    if scatter is not None and scatter[0] == "rows":
        tm = m // 8
    if scatter is not None and scatter[0] == "cols":
        tn = n // 8
    if m_part is not None:
        assert tm == m
    nk = k // tk
    dn = (((0 if ta else 1,), (1 if tb else 0,)), ((), ()))

    def body(*refs):
        if add is None:
            a_ref, b_ref, o_ref, acc = refs
        else:
            a_ref, b_ref, add_ref, o_ref, acc = refs
        kk = pl.program_id(2)

        @pl.when(kk == 0)
        def _():
            acc[...] = jnp.zeros_like(acc)

        acc[...] += lax.dot_general(a_ref[...].astype(MXU_DT), b_ref[...].astype(MXU_DT), dn,
                                    preferred_element_type=F32)

        @pl.when(kk == nk - 1)
        def _():
            r = acc[...]
            if add is not None:
                r = r + add_ref[...]
            o_ref[...] = r

    if ta:
        a_spec = pl.BlockSpec((tk, tm), lambda i, j, kk: (kk, i + m_off))
    else:
        a_spec = pl.BlockSpec((tm, tk), lambda i, j, kk: (i + m_off, kk))
    b_spec = pl.BlockSpec((tn, tk), lambda i, j, kk: (j, kk)) if tb else pl.BlockSpec((tk, tn), lambda i, j, kk: (kk, j))
    out_sd = jax.ShapeDtypeStruct((m, n), F32)
    if scatter is None:
        o_spec = pl.BlockSpec((tm, tn), lambda i, j, kk: (i, j))
    elif scatter[0] == "rows":
        o_spec = pl.BlockSpec((None, tm, tn), lambda i, j, kk: (scatter[1](i), 0, j))
        out_sd = jax.ShapeDtypeStruct((8, tm, n), F32)
    else:
        o_spec = pl.BlockSpec((None, tm, tn), lambda i, j, kk: (scatter[1](j), i, 0))
        out_sd = jax.ShapeDtypeStruct((8, m, tn), F32)
    ins, specs = [a, b], [a_spec, b_spec]
    if add is not None:
        ins.append(add)
        specs.append(o_spec)
    return _call(body, grid=(m // tm, n // tn, nk), in_specs=specs, out_specs=[o_spec], out_shape=[out_sd],
                 scratch_shapes=[pltpu.VMEM((tm, tn), F32)], sem=("parallel", "parallel", "arbitrary"),
                 name=name, args=ins, side=side)[0]


def rows(arr, t, width=None, base=0, per_j=False):
    width = arr.shape[1] if width is None else width
    return (arr, (t, width), lambda j, i: (i, base + (j if per_j else 0)))


def rows3(arr, lead, t, width, per_j=True):
    return (arr, (None, t, width), lambda j, i: (lead, i, j if per_j else 0))


def bcast(arr, width=None, base=0, per_j=False):
    width = arr.shape[1] if width is None else width
    return (arr, (arr.shape[0], width), lambda j, i: (0, base + (j if per_j else 0)))


def _specs(items):
    return [pl.BlockSpec(bs, im) for (_, bs, im) in items]


def rw_fwd(fn, ins, outs, *, nrow, t, ncol=1, name):
    out_specs = [pl.BlockSpec((t, w), (lambda j, i: (i, j)) if pj else (lambda j, i: (i, 0))) for (w, pj) in outs]
    out_shape = [jax.ShapeDtypeStruct((nrow, w * (ncol if pj else 1)), F32) for (w, pj) in outs]
    nin = len(ins)

    def body(*refs):
        j = pl.program_id(0)
        res = fn(j, *[r[...] for r in refs[:nin]])
        for o_ref, r in zip(refs[nin:], res):
            o_ref[...] = r

    return pl.pallas_call(
        body, grid=(ncol, nrow // t), in_specs=_specs(ins), out_specs=out_specs, out_shape=out_shape,
        compiler_params=_params(("parallel", "parallel")), name=name)(*[x[0] for x in ins])


def rw_bwd(fn, ins, cots, *, nrow, t, ncol=1, row_grads, bc_grads, residual=None, side=None, name):
    nin = len(ins)
    flat_cots = [c for group in cots for c in group]
    extra = [residual] if residual is not None else []
    out_specs, out_shape = [], []
    for idx, pj in row_grads:
        w = ins[idx][1][-1]
        out_specs.append(pl.BlockSpec((t, w), (lambda j, i: (i, j)) if pj else (lambda j, i: (i, 0))))
        out_shape.append(jax.ShapeDtypeStruct((nrow, w * (ncol if pj else 1)), F32))
    for idx in bc_grads:
        r, w = ins[idx][1]
        out_specs.append(pl.BlockSpec((None, r, w), lambda j, i: (j, 0, 0)))
        out_shape.append(jax.ShapeDtypeStruct((ncol, r, w), F32))

    def body(*refs):
        j = pl.program_id(0)
        i = pl.program_id(1)
        vals = [r[...] for r in refs[:nin]]
        pos = nin
        cts = []
        for group in cots:
            c = refs[pos][...]
            for q in range(1, len(group)):
                c = c + refs[pos + q][...]
            pos += len(group)
            cts.append(c)
        res_ref = refs[pos] if residual is not None else None
        pos += len(extra)
        outs = refs[pos:]
        _, vjp = jax.vjp(lambda *a: tuple(fn(j, *a)), *vals)
        grads = vjp(tuple(cts))
        for q, (idx, _) in enumerate(row_grads):
            g = grads[idx]
            if q == 0 and res_ref is not None:
                g = g + res_ref[...]
            outs[q][...] = g
        for q, idx in enumerate(bc_grads):
            o_ref = outs[len(row_grads) + q]

            @pl.when(i == 0)
            def _(o_ref=o_ref):
                o_ref[...] = jnp.zeros_like(o_ref)

            o_ref[...] += grads[idx]

    all_in = list(ins) + flat_cots + extra
    return _call(body, grid=(ncol, nrow // t), in_specs=_specs(all_in), out_specs=out_specs, out_shape=out_shape,
                 sem=("parallel", "arbitrary"), name=name, args=[x[0] for x in all_in], side=side)


def _silu(x):
    return x * jax.nn.sigmoid(x)


@jax.custom_vjp
def _softplus(x):
    return jnp.maximum(x, 0.0) + jnp.log1p(jnp.exp(-jnp.abs(x)))


def _softplus_fwd(x):
    return _softplus(x), x


def _softplus_bwd(x, ct):
    return (ct * jax.nn.sigmoid(x),)


_softplus.defvjp(_softplus_fwd, _softplus_bwd)


def _gelu(x):
    return 0.5 * x * (1.0 + jnp.tanh(math.sqrt(2.0 / math.pi) * (x + 0.044715 * (x * x * x))))


def _row_shift_impl(x, s):
    n = x.shape[0]
    if s == 0:
        return x
    rolled = pltpu.roll(x, (-s) % n, 0)
    t = lax.broadcasted_iota(jnp.int32, x.shape, 0)
    ok = (t + s >= 0) & (t + s < n)
    return jnp.where(ok, rolled, 0.0)


@functools.partial(jax.custom_vjp, nondiff_argnums=(1,))
def _row_shift(x, s):
    return _row_shift_impl(x, s)


def _row_shift_fwd(x, s):
    return _row_shift_impl(x, s), None


def _row_shift_bwd(s, _, ct):
    return (_row_shift_impl(ct, -s),)


_row_shift.defvjp(_row_shift_fwd, _row_shift_bwd)


def fn_rms(j, x, g):
    return (x * lax.rsqrt(jnp.mean(x * x, axis=-1, keepdims=True) + RMS_EPS) * g,)


def make_fn_prep(n_heads):
    pad = (CONV_K - 1) // 2

    def fn_prep(j, x, w):
        y = _row_shift(x, -pad) * w[0:1, :]
        for i in range(1, CONV_K):
            y = y + _row_shift(x, i - pad) * w[i:i + 1, :]
        a = _silu(y)
        scale = jnp.where(j < n_heads, HEAD_DIM ** -0.5, 1.0).astype(F32)
        nrm = a * lax.rsqrt(jnp.sum(a * a, axis=-1, keepdims=True) + RMS_EPS) * scale
        return (jnp.where(j < 2 * n_heads, nrm, a),)

    return fn_prep


def fn_headnorm(j, o0, o1, z, g):
    o = o0 + o1
    n = o * lax.rsqrt(jnp.mean(o * o, axis=-1, keepdims=True) + RMS_EPS) * g
    return (n * _silu(z),)


def fn_gelu(j, y0, y1, u, dsk):
    return (_gelu(y0 + y1 + u * dsk),)


def fn_glu(j, ys, logit, z, b):
    return (ys * jax.nn.sigmoid(logit + b) * _silu(z),)


def fn_merge(j, la, lb, ya, yb, ba, bb):
    return (jax.nn.sigmoid(la + ba) * ya + jax.nn.sigmoid(lb + bb) * yb,)


def fn_loss(j, x, t, g):
    y = x * lax.rsqrt(jnp.mean(x * x, axis=-1, keepdims=True) + RMS_EPS) * g
    e = y - t
    return (0.5 * jnp.mean(e * e, axis=-1, keepdims=True),)


def make_fn_sum(n):
    def fn_sum(j, *xs):
        s = xs[0]
        for q in range(1, n):
            s = s + xs[q]
        return (s,)

    return fn_sum


def fn_adamw(j, w, g, m, v):
    m2 = ADAM_B1 * m + (1.0 - ADAM_B1) * g
    v2 = ADAM_B2 * v + (1.0 - ADAM_B2) * (g * g)
    m_hat = m2 / (1.0 - ADAM_B1 ** ADAM_STEP)
    v_hat = v2 / (1.0 - ADAM_B2 ** ADAM_STEP)
    delta = -ADAM_LR * (m_hat / (jnp.sqrt(v_hat) + ADAM_EPS) + ADAM_WD * w)
    return delta, m2, v2


def _dot(a, b, dims, precision=None):
    return lax.dot_general(a, b, (dims, ((), ())), precision=precision, preferred_element_type=F32)


_NN = ((1,), (0,))
_NT = ((1,), (1,))
_TN = ((0,), (0,))
GDN_HEAD_BLOCK = 4
INVERSE_WIDE_FACTORS = 0


def _split(a):
    hi = a.astype(MXU_DT)
    return hi, (a - hi.astype(F32)).astype(MXU_DT)


def _dot3s(a, b, dims):
    (ah, al), (bh, bl) = a, b
    return _dot(ah, bh, dims) + (_dot(al, bh, dims) + _dot(ah, bl, dims))


def _dot3(a, b, dims):
    return _dot3s(_split(a), _split(b), dims)


def _unit_inverse(lmat):
    r = lmat.shape[0]
    eye = (lax.broadcasted_iota(jnp.int32, (r, r), 0) == lax.broadcasted_iota(jnp.int32, (r, r), 1)).astype(F32)
    pw = -lmat
    tinv = eye + pw
    pws = _split(pw)
    n_fact = int(math.ceil(math.log2(CHUNK))) - 1
    for k in range(n_fact):
        if k < INVERSE_WIDE_FACTORS:
            pws = _split(_dot3s(pws, pws, _NN))
            tinv = tinv + _dot3s(_split(tinv), pws, _NN)
        else:
            pw = _dot(pws[0], pws[0], _NN)
            pws = (pw.astype(MXU_DT), None)
            tinv = tinv + _dot(tinv.astype(MXU_DT), pws[0], _NN)
    return tinv


@jax.custom_vjp
def tri_apply(lmat, rhs, tinv):
    return _dot3(tinv, rhs, _NN)


def _tri_apply_fwd(lmat, rhs, tinv):
    x = _dot3(tinv, rhs, _NN)
    return x, (tinv, x)


def _tri_apply_bwd(res, dx):
    tinv, x = res
    drhs = _dot3(tinv, dx, _TN)
    return -_dot3(drhs, x, _NT), drhs, jnp.zeros_like(tinv)


tri_apply.defvjp(_tri_apply_fwd, _tri_apply_bwd)


def gdn_group(s, q, k, v, ba, alog, dtb, *, d, head0, n_heads, tinv=None):
    hb = len(s)
    c = q.shape[0]
    r = hb * c

    def stack(x):
        return jnp.concatenate([x[:, i * HEAD_DIM:(i + 1) * HEAD_DIM] for i in range(hb)], axis=0)

    def pick(x, lane0):
        lane = lax.broadcasted_iota(jnp.int32, x.shape, 1)
        return jnp.concatenate(
            [jnp.sum(jnp.where(lane == lane0 + i, x, 0.0), axis=1, keepdims=True) for i in range(hb)], axis=0)

    q4, k4, v4 = stack(q), stack(k), stack(v)
    beta = pick(jax.nn.sigmoid(ba), d * n_heads + head0)
    g = pick(-jnp.exp(alog) * _softplus(ba + dtb), 2 * n_heads + d * n_heads + head0)
    ii = lax.broadcasted_iota(jnp.int32, (r, r), 0)
    jj = lax.broadcasted_iota(jnp.int32, (r, r), 1)
    same = (ii // c) == (jj // c)
    rel = (ii - jj) * (1 - 2 * d)
    incl = same & (rel >= 0)
    strict = same & (rel > 0)
    incl_t = same & (rel <= 0)
    g_row = jnp.sum(jnp.where(ii == jj, g, 0.0), axis=0, keepdims=True)
    gc_col = jnp.sum(jnp.where(incl, g_row, 0.0), axis=1, keepdims=True)
    gc_row = jnp.sum(jnp.where(incl_t, g, 0.0), axis=0, keepdims=True)
    g_tot = jnp.sum(jnp.where(same, g_row, 0.0), axis=1, keepdims=True)
    decay = jnp.exp(jnp.where(incl, gc_col - gc_row, -1e30))
    kb = k4 * beta
    vb = v4 * beta
    lmat = jnp.where(strict, _dot(kb, k4, _NT) * decay, 0.0)
    fresh = tinv is None
    if fresh:
        tinv = _unit_inverse(lmat)
    uw = tri_apply(lmat, jnp.concatenate([vb, kb * jnp.exp(gc_col)], axis=1), tinv)
    u, w = uw[:, :HEAD_DIM], uw[:, HEAD_DIM:]
    qk = _dot(q4, k4, _NT) * decay
    qe = q4 * jnp.exp(gc_col)
    kd = k4 * jnp.exp(g_tot - gc_col)
    eg = jnp.exp(g_tot)
    v_new, o_s = [], []
    for i in range(hb):
        rs = slice(i * c, (i + 1) * c)
        ws = _dot(jnp.concatenate([w[rs], qe[rs]], axis=0), s[i], _NN)
        v_new.append(u[rs] - ws[:c])
        o_s.append(ws[c:])
    o4 = jnp.concatenate(o_s, axis=0) + _dot(qk, jnp.concatenate(v_new, axis=0), _NN)
    s_new = tuple(s[i] * eg[i * c:i * c + 1, :] + _dot(kd[i * c:(i + 1) * c], v_new[i], _TN) for i in range(hb))
    o = jnp.concatenate([o4[i * c:(i + 1) * c] for i in range(hb)], axis=1)
    return (s_new, o, tinv) if fresh else (s_new, o)


def _gdn_maps(n_chunks):
    def chunk_of(d, step):
        return step + d * (n_chunks - 1 - 2 * step)
    return chunk_of


def _heads(ref, h0, n):
    return ref[:, h0 * HEAD_DIM:(h0 + n) * HEAD_DIM]


def gdn_fwd(qkv, proj, ba_blk, alog_row, dtb_row, *, n_heads, side=None, name):
    nrow = qkv.shape[0]
    nc = nrow // CHUNK
    h = n_heads
    wa = h * HEAD_DIM
    hb = math.gcd(h, GDN_HEAD_BLOCK)
    chunk_of = _gdn_maps(nc)

    def body(q_ref, k_ref, v_ref, ba_ref, al_ref, dt_ref, o_ref, sh_ref, ti_ref, s_scr):
        d = pl.program_id(0)
        n = pl.program_id(1)

        @pl.when(n == 0)
        def _():
            s_scr[...] = jnp.zeros_like(s_scr)

        ba, al, dt = ba_ref[...], al_ref[...], dt_ref[...]
        for h0 in range(0, h, hb):
            s = tuple(s_scr[h0 + i] for i in range(hb))
            for i in range(hb):
                sh_ref[h0 + i] = s[i]
            s2, o, tinv = gdn_group(s, _heads(q_ref, h0, hb), _heads(k_ref, h0, hb), _heads(v_ref, h0, hb),
                                    ba, al, dt, d=d, head0=h0, n_heads=h)
            o_ref[:, h0 * HEAD_DIM:(h0 + hb) * HEAD_DIM] = o
            ti_ref[h0 // hb] = tinv
            for i in range(hb):
                s_scr[h0 + i] = s2[i]

    blk = (CHUNK, wa)
    in_specs = [
        pl.BlockSpec(blk, lambda d, n: (chunk_of(d, n), 0)),
        pl.BlockSpec(blk, lambda d, n: (chunk_of(d, n), 1)),
        pl.BlockSpec(blk, lambda d, n: (chunk_of(d, n), 2)),
        pl.BlockSpec((CHUNK, LANES), lambda d, n: (chunk_of(d, n), ba_blk)),
        pl.BlockSpec((1, LANES), lambda d, n: (0, 0)),
        pl.BlockSpec((1, LANES), lambda d, n: (0, 0)),
    ]
    ng, r = h // hb, hb * CHUNK
    out_specs = [
        pl.BlockSpec((None, CHUNK, wa), lambda d, n: (d, chunk_of(d, n), 0)),
        pl.BlockSpec((h, None, HEAD_DIM, HEAD_DIM), lambda d, n: (d, n, 0, 0)),
        pl.BlockSpec((None, None, ng, r, r), lambda d, n: (d, n, 0, 0, 0)),
    ]
    out_shape = [jax.ShapeDtypeStruct((2, nrow, wa), F32),
                 jax.ShapeDtypeStruct((2 * h, nc, HEAD_DIM, HEAD_DIM), F32),
                 jax.ShapeDtypeStruct((2, nc, ng, r, r), F32)]
    return _call(body, grid=(2, nc), in_specs=in_specs, out_specs=out_specs, out_shape=out_shape,
                 scratch_shapes=[pltpu.VMEM((h, HEAD_DIM, HEAD_DIM), F32)], sem=("parallel", "arbitrary"),
                 name=name, args=(qkv, qkv, qkv, proj, alog_row, dtb_row), side=side)


def gdn_bwd(qkv, proj, ba_blk, alog_row, dtb_row, s_hist, t_hist, do, *, n_heads, side=None, name):
    nrow = qkv.shape[0]
    nc = nrow // CHUNK
    h = n_heads
    wa = h * HEAD_DIM
    hb = math.gcd(h, GDN_HEAD_BLOCK)
    chunk_of = _gdn_maps(nc)

    def cb(d, n):
        return chunk_of(d, nc - 1 - n)

    def body(q_ref, k_ref, v_ref, ba_ref, al_ref, dt_ref, sh_ref, do_ref, ti_ref,
             dqkv_ref, dba_ref, dal_ref, ddt_ref, ds_scr):
        d = pl.program_id(0)
        n = pl.program_id(1)

        @pl.when(n == 0)
        def _():
            ds_scr[...] = jnp.zeros_like(ds_scr)
            dal_ref[...] = jnp.zeros_like(dal_ref)
            ddt_ref[...] = jnp.zeros_like(ddt_ref)

        ba, al, dt = ba_ref[...], al_ref[...], dt_ref[...]
        dba_sum = jnp.zeros_like(ba)
        dal_sum = jnp.zeros_like(al)
        ddt_sum = jnp.zeros_like(dt)
        for h0 in range(0, h, hb):
            f = functools.partial(gdn_group, d=d, head0=h0, n_heads=h, tinv=ti_ref[h0 // hb])
            s = tuple(sh_ref[h0 + i] for i in range(hb))
            _, vjp = jax.vjp(f, s, _heads(q_ref, h0, hb), _heads(k_ref, h0, hb), _heads(v_ref, h0, hb), ba, al, dt)
            ds, dq, dk, dv, dba, dal, ddt = vjp((tuple(ds_scr[h0 + i] for i in range(hb)), _heads(do_ref, h0, hb)))
            for i in range(hb):
                ds_scr[h0 + i] = ds[i]
            cols = slice(h0 * HEAD_DIM, (h0 + hb) * HEAD_DIM)
            dqkv_ref[0, :, cols] = dq
            dqkv_ref[1, :, cols] = dk
            dqkv_ref[2, :, cols] = dv
            dba_sum = dba_sum + dba
            dal_sum = dal_sum + dal
            ddt_sum = ddt_sum + ddt
        dba_ref[...] = dba_sum
        dal_ref[...] += dal_sum
        ddt_ref[...] += ddt_sum

    blk = (CHUNK, wa)
    in_specs = [
        pl.BlockSpec(blk, lambda d, n: (cb(d, n), 0)),
        pl.BlockSpec(blk, lambda d, n: (cb(d, n), 1)),
        pl.BlockSpec(blk, lambda d, n: (cb(d, n), 2)),
        pl.BlockSpec((CHUNK, LANES), lambda d, n: (cb(d, n), ba_blk)),
        pl.BlockSpec((1, LANES), lambda d, n: (0, 0)),
        pl.BlockSpec((1, LANES), lambda d, n: (0, 0)),
        pl.BlockSpec((h, None, HEAD_DIM, HEAD_DIM), lambda d, n: (d, nc - 1 - n, 0, 0)),
        pl.BlockSpec(blk, lambda d, n: (cb(d, n), 0)),
        pl.BlockSpec((None, None, h // hb, hb * CHUNK, hb * CHUNK), lambda d, n: (d, nc - 1 - n, 0, 0, 0)),
    ]
    out_specs = [
        pl.BlockSpec((None, 3, CHUNK, wa), lambda d, n: (d, 0, cb(d, n), 0)),
        pl.BlockSpec((None, CHUNK, LANES), lambda d, n: (d, cb(d, n), 0)),
        pl.BlockSpec((None, 1, LANES), lambda d, n: (d, 0, 0)),
        pl.BlockSpec((None, 1, LANES), lambda d, n: (d, 0, 0)),
    ]
    out_shape = [jax.ShapeDtypeStruct((2, 3, nrow, wa), F32),
                 jax.ShapeDtypeStruct((2, nrow, LANES), F32),
                 jax.ShapeDtypeStruct((2, 1, LANES), F32),
                 jax.ShapeDtypeStruct((2, 1, LANES), F32)]
    return _call(body, grid=(2, nc), in_specs=in_specs, out_specs=out_specs, out_shape=out_shape,
                 scratch_shapes=[pltpu.VMEM((h, HEAD_DIM, HEAD_DIM), F32)], sem=("parallel", "arbitrary"),
                 name=name, args=(qkv, qkv, qkv, proj, alog_row, dtb_row, s_hist, do, t_hist), side=side)


def s5_mats(lam_re, lam_im, log_dt, b_re, b_im, c_re, c_im):
    g = lam_re.shape[1]
    nb = g // S5_BLOCK_GROUPS
    dt = jnp.exp(log_dt)[..., None]
    mag = jnp.exp(lam_re * dt)
    ar = mag * jnp.cos(lam_im * dt)
    ai = mag * jnp.sin(lam_im * dt)
    den = lam_re * lam_re + lam_im * lam_im
    fr = ((ar - 1.0) * lam_re + ai * lam_im) / den
    fi = (ai * lam_re - (ar - 1.0) * lam_im) / den
    bbr = fr[..., None] * b_re - fi[..., None] * b_im
    bbi = fr[..., None] * b_im + fi[..., None] * b_re
    eye = jnp.eye(S5_BLOCK_GROUPS, dtype=F32)
    shp = (2, nb, S5_BLOCK_GROUPS, S5_STATE, S5_GROUP_CH)
    w_r = jnp.einsum('dsjpc,jk->dsjckp', bbr.reshape(shp), eye).reshape(2, nb, LANES, S5_BLOCK_STATE)
    w_i = jnp.einsum('dsjpc,jk->dsjckp', bbi.reshape(shp), eye).reshape(2, nb, LANES, S5_BLOCK_STATE)
    w = jnp.concatenate([w_r, w_i], axis=-1)
    shc = (2, nb, S5_BLOCK_GROUPS, S5_GROUP_CH, S5_STATE)
    c_r = jnp.einsum('dsjcp,jk->dsjpkc', c_re.reshape(shc), eye).reshape(2, nb, S5_BLOCK_STATE, LANES)
    c_i = jnp.einsum('dsjcp,jk->dsjpkc', c_im.reshape(shc), eye).reshape(2, nb, S5_BLOCK_STATE, LANES)
    cm = jnp.concatenate([c_r, -c_i], axis=-2)
    return (ar.reshape(2, nb, 1, S5_BLOCK_STATE), ai.reshape(2, nb, 1, S5_BLOCK_STATE), w, cm)


_S5_ROWS = 512
S5_SEGMENTS = 8


def segment_major(a):
    n, c = a.shape
    return jnp.transpose(a.reshape(S5_SEGMENTS, n // S5_SEGMENTS, c), (1, 0, 2)).reshape(n, c)


def time_major(a):
    n, c = a.shape
    return jnp.transpose(a.reshape(n // S5_SEGMENTS, S5_SEGMENTS, c), (1, 0, 2)).reshape(n, c)


def _segmented_scan(x_ref, a_r, a_i, *, reverse, visit=None):
    nrow, two_hs = x_ref.shape
    hs = two_hs // 2
    steps = nrow // S5_SEGMENTS
    assert steps & (steps - 1) == 0
    b_r = jnp.broadcast_to(a_r, (S5_SEGMENTS, hs))
    b_i = jnp.broadcast_to(a_i, (S5_SEGMENTS, hs))
    rid = lax.broadcasted_iota(jnp.int32, (S5_SEGMENTS, hs), 0)

    def rows_of(i):
        return pl.ds(pl.multiple_of((steps - 1 - i if reverse else i) * S5_SEGMENTS, S5_SEGMENTS), S5_SEGMENTS)

    def local(i, carry):
        s_r, s_i = carry
        rows = rows_of(i)
        n_r = b_r * s_r - b_i * s_i + x_ref[rows, pl.ds(0, hs)]
        n_i = b_r * s_i + b_i * s_r + x_ref[rows, pl.ds(hs, hs)]
        x_ref[rows, pl.ds(0, hs)] = n_r
        x_ref[rows, pl.ds(hs, hs)] = n_i
        return n_r, n_i

    z = jnp.zeros((S5_SEGMENTS, hs), F32)
    e_r, e_i = lax.fori_loop(0, steps, local, (z, z))
    q_r, q_i = a_r, a_i
    for _ in range(steps.bit_length() - 1):
        q_r, q_i = q_r * q_r - q_i * q_i, 2.0 * q_r * q_i
    c_r = jnp.zeros((1, hs), F32)
    c_i = jnp.zeros((1, hs), F32)
    en_r, en_i = z, z
    for s in (reversed(range(S5_SEGMENTS)) if reverse else range(S5_SEGMENTS)):
        en_r = jnp.where(rid == s, c_r, en_r)
        en_i = jnp.where(rid == s, c_i, en_i)
        c_r, c_i = (e_r[s:s + 1] + q_r * c_r - q_i * c_i, e_i[s:s + 1] + q_r * c_i + q_i * c_r)

    def fix(i, carry):
        p_r, p_i = carry
        rows = rows_of(i)
        x_r = x_ref[rows, pl.ds(0, hs)] + (p_r * en_r - p_i * en_i)
        x_i = x_ref[rows, pl.ds(hs, hs)] + (p_r * en_i + p_i * en_r)
        x_ref[rows, pl.ds(0, hs)] = x_r
        x_ref[rows, pl.ds(hs, hs)] = x_i
        if visit is not None:
            visit(steps - 1 - i if reverse else i, rows, x_r, x_i)
        return p_r * b_r - p_i * b_i, p_r * b_i + p_i * b_r

    return lax.fori_loop(0, steps, fix, (b_r, b_i))


def s5_fwd(u_seg, ar, ai, w, cm, *, side=None, name):
    nrow = u_seg.shape[0]
    nb = w.shape[1]
    hs = S5_BLOCK_STATE
    rs = min(_S5_ROWS, nrow)

    def body(u_ref, ar_ref, ai_ref, w_ref, cm_ref, y_ref, x_ref):
        d = pl.program_id(0)
        wb = w_ref[...].astype(MXU_DT)
        for r0 in range(0, nrow, rs):
            x_ref[pl.ds(r0, rs), :] = _dot(u_ref[pl.ds(r0, rs), :].astype(MXU_DT), wb, _NN)
        @pl.when(d == 0)
        def _():
            _segmented_scan(x_ref, ar_ref[...], ai_ref[...], reverse=False)

        @pl.when(d == 1)
        def _():
            _segmented_scan(x_ref, ar_ref[...], ai_ref[...], reverse=True)

        cb = cm_ref[...].astype(MXU_DT)
        for r0 in range(0, nrow, rs):
            y_ref[pl.ds(r0, rs), :] = _dot(x_ref[pl.ds(r0, rs), :].astype(MXU_DT), cb, _NN)

    in_specs = [
        pl.BlockSpec((nrow, LANES), lambda d, s: (0, s)),
        pl.BlockSpec((None, None, 1, hs), lambda d, s: (d, s, 0, 0)),
        pl.BlockSpec((None, None, 1, hs), lambda d, s: (d, s, 0, 0)),
        pl.BlockSpec((None, None, LANES, 2 * hs), lambda d, s: (d, s, 0, 0)),
        pl.BlockSpec((None, None, 2 * hs, LANES), lambda d, s: (d, s, 0, 0)),
    ]
    out_specs = [
        pl.BlockSpec((None, nrow, LANES), lambda d, s: (d, 0, s)),
        pl.BlockSpec((None, nrow, 2 * hs), lambda d, s: (d, 0, s)),
    ]
    out_shape = [jax.ShapeDtypeStruct((2, nrow, nb * LANES), F32),
                 jax.ShapeDtypeStruct((2, nrow, nb * 2 * hs), F32)]
    return _call(body, grid=(2, nb), in_specs=in_specs, out_specs=out_specs, out_shape=out_shape,
                 sem=("parallel", "parallel"), name=name, args=(u_seg, ar, ai, w, cm), side=side)


def s5_bwd(u_seg, ar, ai, w, cm, xs, dy, du_in, *, side=None, name):
    nrow = u_seg.shape[0]
    nb = w.shape[1]
    steps = nrow // S5_SEGMENTS
    hs = S5_BLOCK_STATE
    rs = min(_S5_ROWS, nrow)

    def body(u_ref, dy_ref, dui_ref, x_ref, ar_ref, ai_ref, w_ref, cm_ref,
             du_ref, dw_ref, dcm_ref, dar_ref, dai_ref, g_ref, acc_ref):
        d = pl.program_id(1)
        cb = cm_ref[...].astype(MXU_DT)
        for r0 in range(0, nrow, rs):
            g_ref[pl.ds(r0, rs), :] = _dot(dy_ref[pl.ds(r0, rs), :].astype(MXU_DT), cb, _NT)
        acc_ref[...] = jnp.zeros_like(acc_ref)
        rid = lax.broadcasted_iota(jnp.int32, (S5_SEGMENTS, 2 * hs), 0)

        def run(forward_recurrence):
            last = pl.ds((steps - 1) * S5_SEGMENTS, S5_SEGMENTS)
            first = pl.ds(0, S5_SEGMENTS)
            if forward_recurrence:
                wrap = jnp.where(rid == 0, 0.0, pltpu.roll(x_ref[last, :], 1, 0))
            else:
                wrap = jnp.where(rid == S5_SEGMENTS - 1, 0.0, pltpu.roll(x_ref[first, :], S5_SEGMENTS - 1, 0))

            def visit(step, rows, g_r, g_i):
                if forward_recurrence:
                    nbr = jnp.maximum(step - 1, 0)
                    edge = step == 0
                else:
                    nbr = jnp.minimum(step + 1, steps - 1)
                    edge = step == steps - 1
                prev = x_ref[pl.ds(pl.multiple_of(nbr * S5_SEGMENTS, S5_SEGMENTS), S5_SEGMENTS), :]
                prev = jnp.where(edge, wrap, prev)
                p_r, p_i = prev[:, :hs], prev[:, hs:]
                acc_ref[:, pl.ds(0, hs)] += g_r * p_r + g_i * p_i
                acc_ref[:, pl.ds(hs, hs)] += g_i * p_r - g_r * p_i

            _segmented_scan(g_ref, ar_ref[...], -ai_ref[...], reverse=forward_recurrence, visit=visit)

        @pl.when(d == 0)
        def _():
            run(True)

        @pl.when(d == 1)
        def _():
            run(False)

        dar_ref[...] = jnp.sum(acc_ref[:, pl.ds(0, hs)], axis=0, keepdims=True)
        dai_ref[...] = jnp.sum(acc_ref[:, pl.ds(hs, hs)], axis=0, keepdims=True)

        wb = w_ref[...].astype(MXU_DT)
        for r0 in range(0, nrow, rs):
            part = _dot(g_ref[pl.ds(r0, rs), :].astype(MXU_DT), wb, _NT)

            @pl.when(d == 0)
            def _(part=part, r0=r0):
                du_ref[pl.ds(r0, rs), :] = dui_ref[pl.ds(r0, rs), :] + part

            @pl.when(d == 1)
            def _(part=part, r0=r0):
                du_ref[pl.ds(r0, rs), :] += part

        dw_ref[...] = _dot(u_ref[...].astype(MXU_DT), g_ref[...].astype(MXU_DT), _TN)
        dcm_ref[...] = _dot(x_ref[...].astype(MXU_DT), dy_ref[...].astype(MXU_DT), _TN)

    in_specs = [
        pl.BlockSpec((nrow, LANES), lambda s, d: (0, s)),
        pl.BlockSpec((nrow, LANES), lambda s, d: (0, s)),
        pl.BlockSpec((nrow, LANES), lambda s, d: (0, s)),
        pl.BlockSpec((None, nrow, 2 * hs), lambda s, d: (d, 0, s)),
        pl.BlockSpec((None, None, 1, hs), lambda s, d: (d, s, 0, 0)),
        pl.BlockSpec((None, None, 1, hs), lambda s, d: (d, s, 0, 0)),
        pl.BlockSpec((None, None, LANES, 2 * hs), lambda s, d: (d, s, 0, 0)),
        pl.BlockSpec((None, None, 2 * hs, LANES), lambda s, d: (d, s, 0, 0)),
    ]
    out_specs = [
        pl.BlockSpec((nrow, LANES), lambda s, d: (0, s)),
        pl.BlockSpec((None, None, LANES, 2 * hs), lambda s, d: (d, s, 0, 0)),
        pl.BlockSpec((None, None, 2 * hs, LANES), lambda s, d: (d, s, 0, 0)),
        pl.BlockSpec((None, None, 1, hs), lambda s, d: (d, s, 0, 0)),
        pl.BlockSpec((None, None, 1, hs), lambda s, d: (d, s, 0, 0)),
    ]
    out_shape = [jax.ShapeDtypeStruct((nrow, nb * LANES), F32),
                 jax.ShapeDtypeStruct(w.shape, F32), jax.ShapeDtypeStruct(cm.shape, F32),
                 jax.ShapeDtypeStruct(ar.shape, F32), jax.ShapeDtypeStruct(ai.shape, F32)]
    return _call(body, grid=(nb, 2), in_specs=in_specs, out_specs=out_specs, out_shape=out_shape,
                 scratch_shapes=[pltpu.VMEM((nrow, 2 * hs), F32), pltpu.VMEM((S5_SEGMENTS, 2 * hs), F32)],
                 sem=("parallel", "arbitrary"), name=name, args=(u_seg, dy, du_in, xs, ar, ai, w, cm), side=side)


def _me():
    return lax.axis_index("x"), lax.axis_index("y"), lax.axis_index("c")


def all_gather(shards):
    na = len(shards)

    def plan(x_refs, out_refs, sems):
        send_sems, recv_sems, local_sems = sems
        x, y, c = _me()
        me, sibling = (x, y, c), (x, y, 1 - c)
        chips = [(1 - x, y), (x, 1 - y), (1 - x, 1 - y)]

        def slot(a, px, py, pc):
            return out_refs[a].at[4 * px + 2 * py + pc]

        def copy(a, k, block, to, src=None):
            return pltpu.make_async_remote_copy(
                src_ref=slot(a, *block) if src is None else src, dst_ref=slot(a, *block),
                send_sem=send_sems.at[a, k], recv_sem=recv_sems.at[a, k], device_id=to, device_id_type=MESH)

        mine = [pltpu.make_async_copy(x_refs[a], slot(a, *me), local_sems.at[a]) for a in range(na)]
        first = []
        for a in range(na):
            first.append(copy(a, 0, me, sibling, src=x_refs[a]))
            first += [copy(a, 1 + j, me, (*chip, c), src=x_refs[a]) for j, chip in enumerate(chips)]
        return me, sibling, chips, c, copy, mine, first

    def start(x_refs, out_refs, sems):
        _, _, _, _, _, mine, first = plan(x_refs, out_refs, sems)
        for cp in mine + first:
            cp.start()

    def finish(x_refs, out_refs, sems):
        me, sibling, chips, c, copy, mine, first = plan(x_refs, out_refs, sems)
        passed = []
        for j, chip in enumerate(chips):
            for a in range(na):
                copy(a, 1 + j, (*chip, c), me).wait_recv()
                fwd = copy(a, 4 + j, (*chip, c), sibling)
                fwd.start()
                passed.append(fwd)
        for a in range(na):
            copy(a, 0, sibling, me).wait_recv()
            for j, chip in enumerate(chips):
                copy(a, 4 + j, (*chip, 1 - c), me).wait_recv()
        for cp in first + passed:
            cp.wait_send()
        for cp in mine:
            cp.wait()

    return Side(shards, [jax.ShapeDtypeStruct((8,) + s.shape, s.dtype) for s in shards],
                [pltpu.SemaphoreType.DMA((na, 7)), pltpu.SemaphoreType.DMA((na, 7)), pltpu.SemaphoreType.DMA((na,))],
                start, finish)


_AXES = ("x", "y", "c")


def exchange(bufs, axes, *, half):
    na = len(bufs)
    if isinstance(half, bool):
        half = [half] * na

    def copies(in_refs, out_refs, sems):
        send_sems, recv_sems = sems
        me = _me()
        cps = []
        for a in range(na):
            bit = me[_AXES.index(axes[a])]
            peer = tuple(1 - v if ax == axes[a] else v for ax, v in zip(_AXES, me))
            cps.append(pltpu.make_async_remote_copy(
                src_ref=in_refs[a].at[1 - bit] if half[a] else in_refs[a], dst_ref=out_refs[a],
                send_sem=send_sems.at[a], recv_sem=recv_sems.at[a], device_id=peer, device_id_type=MESH))
        return cps

    def start(*refs):
        for cp in copies(*refs):
            cp.start()

    def finish(*refs):
        for cp in copies(*refs):
            cp.wait()

    return Side(bufs, [jax.ShapeDtypeStruct(b.shape[1:] if h else b.shape, b.dtype) for b, h in zip(bufs, half)],
                [pltpu.SemaphoreType.DMA((na,)), pltpu.SemaphoreType.DMA((na,))], start, finish)


def all_to_all(pieces, slot_fns):
    na = len(pieces)

    def copies(in_refs, out_refs, sems):
        send_sems, recv_sems, local_sems = sems
        x, y, c = _me()
        cps = []
        for a in range(na):
            mine = slot_fns[a](x, y, c)
            for k in range(8):
                tx, ty, tc = x ^ (k // 4), y ^ ((k // 2) % 2), c ^ (k % 2)
                src = in_refs[a].at[4 * tx + 2 * ty + tc]
                dst = out_refs[a].at[mine]
                if k == 0:
                    cps.append(pltpu.make_async_copy(src, dst, local_sems.at[a]))
                else:
                    cps.append(pltpu.make_async_remote_copy(
                        src_ref=src, dst_ref=dst, send_sem=send_sems.at[a, k - 1], recv_sem=recv_sems.at[a, k - 1],
                        device_id=(tx, ty, tc), device_id_type=MESH))
        return cps

    def start(*refs):
        for cp in copies(*refs):
            cp.start()

    def finish(*refs):
        for cp in copies(*refs):
            cp.wait()

    return Side(pieces, [jax.ShapeDtypeStruct(p.shape, p.dtype) for p in pieces],
                [pltpu.SemaphoreType.DMA((na, 7)), pltpu.SemaphoreType.DMA((na, 7)), pltpu.SemaphoreType.DMA((na,))],
                start, finish)


def add_half(buf, recv, bit, *, narrow, name):
    c = recv.shape[-1]
    r = math.prod(recv.shape[:-1])
    t = _tile(r, max(16, (ADD_BLOCK_BYTES // (4 * c)) // 16 * 16), 16)

    def body(bit_ref, a_ref, b_ref, o_ref, *tx_ref):
        s = a_ref[...] + b_ref[...].astype(F32)
        o_ref[...] = s
        if narrow:
            tx_ref[0][...] = s.astype(WIRE_DT)

    o_spec = pl.BlockSpec((t, c), lambda i, b: (i, 0))
    grid_spec = pltpu.PrefetchScalarGridSpec(
        num_scalar_prefetch=1, grid=(r // t,),
        in_specs=[pl.BlockSpec((None, t, c), lambda i, b: (b[0], i, 0)), o_spec],
        out_specs=[o_spec, o_spec] if narrow else [o_spec])
    out_shape = [jax.ShapeDtypeStruct((r, c), F32)] + ([jax.ShapeDtypeStruct((r, c), WIRE_DT)] if narrow else [])
    outs = pl.pallas_call(body, grid_spec=grid_spec, out_shape=out_shape,
                          compiler_params=_params(("parallel",)), name=name)(
                              bit, buf.reshape(2, r, c), recv.reshape(r, c))
    return [o.reshape(recv.shape) for o in outs]


def _pack_rows(n, cols):
    return -(-n // (8 * cols)) * 8


def _pack(arrs, cols, mult):
    parts = []
    for a in arrs:
        n = math.prod(a.shape)
        nr = _pack_rows(n, cols)
        parts.append(jnp.pad(a.reshape(-1), (0, nr * cols - n)).reshape(nr, cols))
    total = sum(p.shape[0] for p in parts)
    pad_rows = -(-total // mult) * mult - total
    if pad_rows:
        parts.append(jnp.zeros((pad_rows, cols), arrs[0].dtype))
    return jnp.concatenate(parts, axis=0)


def _unpack(buf, shapes):
    out, r = [], 0
    cols = buf.shape[1]
    for s in shapes:
        n = math.prod(s)
        nr = _pack_rows(n, cols)
        piece = lax.optimization_barrier(buf[r:r + nr])
        out.append(piece.reshape(-1)[:n].reshape(s))
        r += nr
    return out


def _elementwise(fn, arrs, nout, name):
    r, c = arrs[0].shape
    lanes = -(-c // LANES) * LANES
    t = _tile(r, max(ROW_TILE, (ELEMENTWISE_BLOCK_BYTES // (4 * lanes)) // 8 * 8), 8)
    return rw_fwd(fn, [rows(a, t) for a in arrs], [(c, False)] * nout, nrow=r, t=t, name=name)


def kernel(x, ln_g, w_in, conv_w, a_log, dt_bias, head_norm_g, lam_re, lam_im, log_dt, b_re, b_im, c_re, c_im, d_skip, w_glu, b_glu, w_pa, w_pb, b_gate, w_out, final_g, loss_target, m_ln_g, m_w_in, m_conv_w, m_a_log, m_dt_bias, m_head_norm_g, m_lam_re, m_lam_im, m_log_dt, m_b_re, m_b_im, m_c_re, m_c_im, m_d_skip, m_w_glu, m_b_glu, m_w_pa, m_w_pb, m_b_gate, m_w_out, m_final_g, v_ln_g, v_w_in, v_conv_w, v_a_log, v_dt_bias, v_head_norm_g, v_lam_re, v_lam_im, v_log_dt, v_b_re, v_b_im, v_c_re, v_c_im, v_d_skip, v_w_glu, v_b_glu, v_w_pa, v_w_pb, v_b_gate, v_w_out, v_final_g):
    env = dict(locals())
    wts = {n: env[n] for n in WEIGHTS}
    mom_m = {n: env["m_" + n] for n in WEIGHTS}
    mom_v = {n: env["v_" + n] for n in WEIGHTS}

    xin = x[0]
    tgt = loss_target[0]
    nrow, dm = xin.shape
    depth = ln_g.shape[0]
    nh = dm // (2 * HEAD_DIM)
    wa = nh * HEAD_DIM
    wb = dm // 2
    ngrp = wb // S5_GROUP_CH
    pw = w_in.shape[-1]
    t = min(ROW_TILE, nrow)

    o_ba = 4 * wa
    o_u = o_ba + 4 * nh
    n_main = 4 * wa + 2 * wb + 2 * dm
    projp = -(-(n_main + LANES) // 512) * 512
    blk_za = 3 * wa // HEAD_DIM
    blk_u = 4 * wa // LANES
    ba_blk = n_main // LANES
    cx, cy, cc = _me()

    jb = o_ba // pw
    assert (o_u - 1) // pw == jb
    cut_lo, cut = o_ba - jb * pw, o_u - o_ba
    wide = -(-pw // LANES) * LANES + LANES
    col = lax.broadcasted_iota(jnp.int32, (pw, wide), 1)
    row = lax.broadcasted_iota(jnp.int32, (pw, wide), 0)
    holds_cut = (4 * cx + 2 * cy + cc) == jb
    src_plain = jnp.where(col < pw, col, -1)
    src_cut = jnp.where(col < cut_lo, col, jnp.where(col < pw - cut, col + cut, -1))
    src_cut = jnp.where((col >= wide - LANES) & (col < wide - LANES + cut), col - (wide - LANES) + cut_lo, src_cut)
    select = (row == jnp.where(holds_cut, src_cut, src_plain)).astype(WIRE_DT)
    w_in_tx = mm(w_in.reshape(depth * dm, pw), select, name="w_in_prepare").astype(WIRE_DT).reshape(depth, dm, wide)

    split = (2 * wide // 3) // LANES * LANES

    def gather_parts(l):
        return ([w_in_tx[l][:, :split]],
                [w_in_tx[l][:, split:], w_out[l].astype(WIRE_DT)],
                [w_glu[l].astype(WIRE_DT), w_pa[l].astype(WIRE_DT), w_pb[l].astype(WIRE_DT), conv_w[l]])

    def cat(g):
        return jnp.concatenate([g[j] for j in range(8)], axis=1)

    def assemble(part_a, part_b, part_c):
        (g_a,), (g_b, g_out), (g_glu, g_pa, g_pb, g_conv) = part_a, part_b, part_c
        pieces = []
        for j in range(8):
            valid = pw - cut if j == jb else pw
            pieces += [g_a[j], g_b[j][:, :valid - split]]
        w_perm = jnp.concatenate(
            pieces + [g_b[jb][:, wide - LANES - split:], jnp.zeros((dm, projp - n_main - LANES), WIRE_DT)], axis=1)
        return dict(w_in=w_perm, w_glu=g_glu.reshape(wb, wb), w_pa=cat(g_pa), w_pb=cat(g_pb),
                    w_out=g_out.reshape(dm, dm), conv_w=cat(g_conv))

    parts0 = gather_parts(0)
    first = run_side(all_gather(parts0[0] + parts0[1] + parts0[2]), name="gather_weights")
    full = [assemble(first[:1], first[1:3], first[3:])]

    def small(l):
        z = jnp.zeros((1, LANES - 4 * nh), F32)
        alog_row = jnp.concatenate([jnp.zeros((1, 2 * nh), F32), a_log[l].reshape(1, 2 * nh), z], axis=1)
        dtb_row = jnp.concatenate([jnp.zeros((1, 2 * nh), F32), dt_bias[l].reshape(1, 2 * nh), z], axis=1)
        return alog_row, dtb_row

    saved = []
    cur = xin
    for l in range(depth):
        fw = full[l]
        (hh,) = rw_fwd(fn_rms, [rows(cur, t), bcast(ln_g[l][None])], [(dm, False)], nrow=nrow, t=t, name="rms_fwd")
        nxt = [all_gather(p) for p in gather_parts(l + 1)] if l + 1 < depth else [None] * 3
        proj = mm(hh, fw["w_in"], side=nxt[2], name="proj_fwd")
        (qkv,) = rw_fwd(make_fn_prep(nh), [rows(proj, nrow, HEAD_DIM, 0, True), bcast(fw["conv_w"], HEAD_DIM, 0, True)],
                        [(HEAD_DIM, True)], nrow=nrow, t=nrow, ncol=3 * nh, name="prep_fwd")
        alog_row, dtb_row = small(l)
        o_dir, s_hist, t_hist = gdn_fwd(qkv, proj, ba_blk, alog_row, dtb_row, n_heads=nh, side=nxt[0],
                                        name="gdn_fwd")
        hn_ins = [rows3(o_dir, 0, t, HEAD_DIM), rows3(o_dir, 1, t, HEAD_DIM), rows(proj, t, HEAD_DIM, blk_za, True),
                  bcast(head_norm_g[l][None])]
        (ya_in,) = rw_fwd(fn_headnorm, hn_ins, [(HEAD_DIM, True)], nrow=nrow, t=t, ncol=nh, name="headnorm_fwd")
        mats, mats_vjp = jax.vjp(s5_mats, lam_re[l], lam_im[l], log_dt[l], b_re[l], b_im[l], c_re[l], c_im[l])
        u_seg = segment_major(proj[:, 4 * wa:4 * wa + wb])
        yd, xs = s5_fwd(u_seg, *mats, side=nxt[1], name="s5_fwd")
        if nxt[0] is not None:
            full.append(assemble(nxt[0].result, nxt[1].result, nxt[2].result))
        ge_ins = [rows3(yd, 0, t, wb, False), rows3(yd, 1, t, wb, False), rows(u_seg, t), bcast(d_skip[l][None])]
        (ys_seg,) = rw_fwd(fn_gelu, ge_ins, [(wb, False)], nrow=nrow, t=t, name="gelu_fwd")
        ys = time_major(ys_seg)
        glu = mm(ys, fw["w_glu"], name="glu_fwd")
        gl_ins = [rows(ys, t), rows(glu, t), rows(proj, t, wb, (4 * wa + wb) // wb), bcast(b_glu[l][None])]
        (yb_in,) = rw_fwd(fn_glu, gl_ins, [(wb, False)], nrow=nrow, t=t, name="glugate_fwd")
        y_a = mm(ya_in, fw["w_pa"], name="pa_fwd")
        y_b = mm(yb_in, fw["w_pb"], name="pb_fwd")
        bg = b_gate[l][None]
        mg_ins = [rows(proj, t, dm, 3), rows(proj, t, dm, 4), rows(y_a, t), rows(y_b, t),
                  bcast(bg, dm, 0), bcast(bg, dm, 1)]
        (merged,) = rw_fwd(fn_merge, mg_ins, [(dm, False)], nrow=nrow, t=t, name="merge_fwd")
        nxt = mm(merged, fw["w_out"], add=cur, name="out_fwd")
        saved.append(dict(x=cur, h=hh, proj=proj, qkv=qkv, o_dir=o_dir, s_hist=s_hist, t_hist=t_hist, hn_ins=hn_ins,
                          mats=mats, mats_vjp=mats_vjp, xs=xs, ge_ins=ge_ins, ys=ys, gl_ins=gl_ins, u_seg=u_seg,
                          ya_in=ya_in, yb_in=yb_in, mg_ins=mg_ins, merged=merged,
                          alog_row=alog_row, dtb_row=dtb_row))
        cur = nxt

    loss_ins = [rows(cur, t), rows(tgt, t), bcast(final_g[None])]
    (row_loss,) = rw_fwd(fn_loss, loss_ins, [(1, False)], nrow=nrow, t=t, name="loss_fwd")
    ones = jnp.ones((nrow, 1), F32)
    dcur, dfinal = rw_bwd(fn_loss, loss_ins, [[rows(ones, t)]], nrow=nrow, t=t,
                          row_grads=[(0, False)], bc_grads=[2], name="loss_bwd")
    loss = lax.psum(jnp.sum(row_loss), ("x", "y", "c"))

    coord = {"x": cx, "y": cy, "c": cc}
    routes = {"cxy": ("c", "x", "y"), "cyx": ("c", "y", "x")}
    route = ["cxy", "cyx", "cxy", "cxy", "cyx", "cyx", "cxy"]
    conv_order = [4 * ((q // 2) % 2) + 2 * (q % 2) + q // 4 for q in range(8)]
    cw = 3 * wa // 8
    g_final = {n: [None] * depth for n in SHARDED}
    gsh = {n: [None] * depth for n in SHARDED}

    def own_columns(z):
        un = jnp.concatenate([z[:, :o_ba], z[:, n_main:n_main + 4 * nh], z[:, o_ba:n_main]], axis=1)
        return jnp.stack([un[:, d * pw:(d + 1) * pw] for d in range(8)])

    def rs_begin(l):
        in_a, in_b = gsh["w_in"][l]
        conv_buf = jnp.stack([gsh["conv_w"][l][:, d * cw:(d + 1) * cw] for d in conv_order])
        return dict(layer=l, stage=0, bufs=[
            in_a.reshape(8, dm // 16, projp), in_b.reshape(8, dm // 16, projp), gsh["w_out"][l],
            gsh["w_glu"][l], gsh["w_pa"][l], gsh["w_pb"][l], conv_buf])

    def rs_side(st):
        if st is None:
            return None
        if st["stage"] < 3:
            st["axes"] = [routes[r][st["stage"]] for r in route]

            def halves(b):
                return b.reshape((2, b.shape[0] // 2) + b.shape[1:])

            st["bufs"] = [halves(b) for b in st["bufs"]]
            send = [halves(b) for b in st["tx"]] if st.get("tx") else st["bufs"]
            if st.get("whole") is not None:
                return exchange(send + [st["whole"]], st["axes"] + [routes["cxy"][st["stage"]]],
                                half=[True] * len(send) + [False])
            return exchange(send, st["axes"], half=True)
        st["bufs"] = [b.reshape(b.shape[1:]) for b in st["bufs"]]
        return all_to_all([own_columns(st["bufs"][0]), own_columns(st["bufs"][1])],
                          [lambda x, y, c: 4 * c + 2 * x + y, lambda x, y, c: 4 * c + 2 * y + x])

    def rs_absorb(st, side):
        if st is None:
            return
        if st["stage"] < 3:
            if st.get("whole") is not None:
                (st["whole"],) = _elementwise(make_fn_sum(2), [st["whole"], side.result[-1]], 1, "ar_add")
            narrow = st["stage"] < 2
            sums = [add_half(b, r, coord[ax].astype(jnp.int32).reshape(1), narrow=narrow, name=f"rs_add_{st['stage']}")
                    for b, r, ax in zip(st["bufs"], side.result, st["axes"])]
            st["bufs"] = [s[0] for s in sums]
            st["tx"] = [s[1] for s in sums] if narrow else None
        else:
            got_a, got_b = side.result
            g_final["w_in"][st["layer"]] = jnp.concatenate(
                [got_a.reshape(dm // 2, pw), got_b.reshape(dm // 2, pw)], axis=0)
            for n, b in zip(["w_out", "w_glu", "w_pa", "w_pb", "conv_w"], st["bufs"][2:]):
                g_final[n][st["layer"]] = b
        st["stage"] += 1

    grep = {n: [None] * depth for n in REPLICATED if n != "final_g"}
    pending = None
    for l in reversed(range(depth)):
        fw, sv = full[l], saved[l]
        dmerged = mm(dcur, fw["w_out"], tb=True, name="out_bwd_x")
        gsh["w_out"][l] = mm(sv["merged"], dcur, ta=True, scatter=("rows", slot_cxy), name="out_bwd_w")
        side = rs_side(pending)
        dla, dlb, dya, dyb, dbga, dbgb = rw_bwd(
            fn_merge, sv["mg_ins"], [[rows(dmerged, t)]], nrow=nrow, t=t,
            row_grads=[(0, False), (1, False), (2, False), (3, False)], bc_grads=[4, 5], side=side, name="merge_bwd")
        rs_absorb(pending, side)
        grep["b_gate"][l] = jnp.concatenate([dbga.reshape(dm), dbgb.reshape(dm)])
        dya_in = mm(dya, fw["w_pa"], tb=True, name="pa_bwd_x")
        gsh["w_pa"][l] = mm(sv["ya_in"], dya, ta=True, scatter=("cols", slot_cyx), name="pa_bwd_w")
        dyb_in = mm(dyb, fw["w_pb"], tb=True, name="pb_bwd_x")
        gsh["w_pb"][l] = mm(sv["yb_in"], dyb, ta=True, scatter=("cols", slot_cyx), name="pb_bwd_w")
        dys1, dglu, dzb, dbglu = rw_bwd(
            fn_glu, sv["gl_ins"], [[rows(dyb_in, t)]], nrow=nrow, t=t,
            row_grads=[(0, False), (1, False), (2, False)], bc_grads=[3], name="glugate_bwd")
        grep["b_glu"][l] = dbglu.reshape(wb)
        dys2 = mm(dglu, fw["w_glu"], tb=True, name="glu_bwd_x")
        gsh["w_glu"][l] = mm(sv["ys"], dglu, ta=True, scatter=("rows", slot_cxy), name="glu_bwd_w")
        dyd, du1, ddskip = rw_bwd(
            fn_gelu, sv["ge_ins"], [[rows(segment_major(dys1), t), rows(segment_major(dys2), t)]], nrow=nrow, t=t,
            row_grads=[(0, False), (2, False)], bc_grads=[3], name="gelu_bwd")
        grep["d_skip"][l] = ddskip.reshape(wb)
        side = rs_side(pending)
        du_seg, dw_s5, dcm_s5, dar, dai = s5_bwd(sv["u_seg"], *sv["mats"], sv["xs"], dyd, du1, side=side,
                                                 name="s5_bwd")
        du = time_major(du_seg)
        rs_absorb(pending, side)
        g_lr, g_li, g_ldt, g_br, g_bi, g_cr, g_ci = sv["mats_vjp"]((dar, dai, dw_s5, dcm_s5))
        for nme, val in zip(["lam_re", "lam_im", "log_dt", "b_re", "b_im", "c_re", "c_im"],
                            [g_lr, g_li, g_ldt, g_br, g_bi, g_cr, g_ci]):
            grep[nme][l] = val
        do, dza, dhn = rw_bwd(fn_headnorm, sv["hn_ins"], [[rows(dya_in, t, HEAD_DIM, 0, True)]], nrow=nrow, t=t,
                              ncol=nh, row_grads=[(0, True), (2, True)], bc_grads=[3], name="headnorm_bwd")
        grep["head_norm_g"][l] = jnp.sum(dhn, axis=0).reshape(HEAD_DIM)
        side = rs_side(pending)
        dqkv, dba_all, dal, ddt = gdn_bwd(sv["qkv"], sv["proj"], ba_blk, sv["alog_row"], sv["dtb_row"],
                                          sv["s_hist"], sv["t_hist"], do, n_heads=nh, side=side, name="gdn_bwd")
        rs_absorb(pending, side)
        grep["a_log"][l] = jnp.sum(dal, axis=(0, 1))[2 * nh:4 * nh].reshape(2, nh)
        grep["dt_bias"][l] = jnp.sum(ddt, axis=(0, 1))[2 * nh:4 * nh].reshape(2, nh)
        (dba,) = rw_fwd(make_fn_sum(2), [rows3(dba_all, p, t, LANES, False) for p in range(2)],
                        [(LANES, False)], nrow=nrow, t=t, name="dba_sum")
        prep_ins = [rows(sv["proj"], nrow, HEAD_DIM, 0, True), bcast(fw["conv_w"], HEAD_DIM, 0, True)]
        dq_cots = [(dqkv.reshape(2, 3 * nrow, wa), (None, nrow, HEAD_DIM),
                    (lambda j, i, dd=dd: (dd, j // nh, j % nh))) for dd in range(2)]
        dqkv_raw, dconv = rw_bwd(make_fn_prep(nh), prep_ins, [dq_cots], nrow=nrow, t=nrow, ncol=3 * nh,
                                 row_grads=[(0, True)], bc_grads=[1], name="prep_bwd")
        gsh["conv_w"][l] = jnp.transpose(dconv, (1, 0, 2)).reshape(CONV_K, 3 * wa)
        dproj = jnp.concatenate([dqkv_raw, dza, du, dzb, dla, dlb, dba,
                                 jnp.zeros((nrow, projp - n_main - LANES), F32)], axis=1).astype(MXU_DT)
        side = rs_side(pending)
        dh = mm(dproj, fw["w_in"], tb=True, side=side, name="proj_bwd_x")
        rs_absorb(pending, side)
        gsh["w_in"][l] = [mm(sv["h"], dproj, ta=True, m_part=(part, 2), name="proj_bwd_w") for part in range(2)]
        dcur, dlng = rw_bwd(fn_rms, [rows(sv["x"], t), bcast(ln_g[l][None])], [[rows(dh, t)]], nrow=nrow, t=t,
                            row_grads=[(0, False)], bc_grads=[1], residual=rows(dcur, t), name="rms_bwd")
        grep["ln_g"][l] = dlng.reshape(dm)
        pending = rs_begin(l)
    grad_x = dcur[None]
    rep_list = [jnp.stack(grep[n]) for n in REPLICATED if n != "final_g"] + [dfinal.reshape(dm)]
    rep_shapes = [a.shape for a in rep_list]
    pending["whole"] = _pack(rep_list, COMM_COLS, ROW_TILE)
    for stage in range(RS_STAGES):
        side = rs_side(pending)
        run_side(side, name=f"rs_stage_{stage}")
        rs_absorb(pending, side)
    grads = {n: jnp.stack(g_final[n]) for n in SHARDED}
    for n, val in zip(REPLICATED, _unpack(pending["whole"], rep_shapes)):
        grads[n] = val

    deltas, new_m, new_v = {}, {}, {}
    for n in WEIGHTS:
        shp = wts[n].shape
        two = [a.reshape(-1, shp[-1]) for a in (wts[n], grads[n], mom_m[n], mom_v[n])]
        d_, m_, v_ = _elementwise(fn_adamw, two, 3, "adamw_" + n)
        deltas[n], new_m[n], new_v[n] = d_.reshape(shp), m_.reshape(shp), v_.reshape(shp)

    return (loss, grad_x, *[grads[n] for n in WEIGHTS], *[deltas[n] for n in WEIGHTS],
            *[new_m[n] for n in WEIGHTS], *[new_v[n] for n in WEIGHTS])
```

```python
import functools
import math

import jax
import jax.numpy as jnp
from jax import lax
from jax.experimental import pallas as pl
from jax.experimental.pallas import tpu as pltpu

F32 = jnp.float32
MXU_DT = jnp.bfloat16
WIRE_DT = jnp.bfloat16

HEAD_DIM = 128
CHUNK = 64
CONV_K = 5
S5_GROUP_CH = 16
S5_STATE = 64
S5_BLOCK_GROUPS = 8
S5_BLOCK_STATE = S5_BLOCK_GROUPS * S5_STATE
RMS_EPS = 1e-6
LANES = 128
VMEM_LIMIT = 56 * 1024 * 1024
ROW_TILE = 256
COMM_COLS = 1024
ADD_BLOCK_BYTES = 2 * 1024 * 1024
RS_STAGES = 4
ELEMENTWISE_BLOCK_BYTES = 1024 * 1024

ADAM_LR = 0.001
ADAM_B1 = 0.9
ADAM_B2 = 0.999
ADAM_EPS = 1e-08
ADAM_WD = 0.01
ADAM_STEP = 10

WEIGHTS = ['ln_g', 'w_in', 'conv_w', 'a_log', 'dt_bias', 'head_norm_g', 'lam_re', 'lam_im', 'log_dt',
           'b_re', 'b_im', 'c_re', 'c_im', 'd_skip', 'w_glu', 'b_glu', 'w_pa', 'w_pb', 'b_gate',
           'w_out', 'final_g']
SHARDED = ['w_in', 'w_glu', 'w_pa', 'w_pb', 'w_out', 'conv_w']
REPLICATED = [n for n in WEIGHTS if n not in SHARDED]
MESH = pl.DeviceIdType.MESH


def _params(sem=None):
    return pltpu.CompilerParams(dimension_semantics=sem, vmem_limit_bytes=VMEM_LIMIT)


def _tile(n, cap, q=LANES):
    t = (min(n, cap) // q) * q
    while t > q and n % t:
        t -= q
    return t if t > 0 and n % t == 0 else n


class Side:
    def __init__(self, ins, out_sd, sems, start, finish):
        self.ins, self.out_sd, self.sems, self.start, self.finish = list(ins), list(out_sd), list(sems), start, finish
        self.result = None


def _call(body, *, grid, in_specs, out_specs, out_shape, scratch_shapes=(), sem, name, args, side=None):
    in_specs, out_specs, out_shape = list(in_specs), list(out_specs), list(out_shape)
    scratch_shapes = list(scratch_shapes)
    if side is None:
        return pl.pallas_call(body, grid=grid, in_specs=in_specs, out_specs=out_specs, out_shape=out_shape,
                              scratch_shapes=scratch_shapes, compiler_params=_params(sem), name=name)(*args)
    hbm = pl.BlockSpec(memory_space=pl.ANY)
    n_in, n_out, n_scr = len(in_specs), len(out_specs), len(scratch_shapes)
    s_in, s_out = len(side.ins), len(side.out_sd)

    def hosted(*refs):
        main_in, rest = refs[:n_in], refs[n_in:]
        side_in, rest = rest[:s_in], rest[s_in:]
        main_out, rest = rest[:n_out], rest[n_out:]
        side_out, rest = rest[:s_out], rest[s_out:]
        main_scr, sems = rest[:n_scr], rest[n_scr:]
        ids = [pl.program_id(k) for k in range(len(grid))]
        first = functools.reduce(jnp.logical_and, [i == 0 for i in ids])
        last = functools.reduce(jnp.logical_and, [i == g - 1 for i, g in zip(ids, grid)])

        @pl.when(first)
        def _():
            side.start(side_in, side_out, sems)

        body(*main_in, *main_out, *main_scr)

        @pl.when(last)
        def _():
            side.finish(side_in, side_out, sems)

    outs = pl.pallas_call(
        hosted, grid=grid, in_specs=in_specs + [hbm] * s_in, out_specs=out_specs + [hbm] * s_out,
        out_shape=out_shape + side.out_sd, scratch_shapes=scratch_shapes + side.sems,
        compiler_params=_params(("arbitrary",) * len(grid)), name=name)(*args, *side.ins)
    side.result = list(outs[n_out:])
    return list(outs[:n_out])


def run_side(side, *, name):
    hbm = pl.BlockSpec(memory_space=pl.ANY)
    s_in, s_out = len(side.ins), len(side.out_sd)

    def body(*refs):
        side_in, side_out, sems = refs[:s_in], refs[s_in:s_in + s_out], refs[s_in + s_out:]
        side.start(side_in, side_out, sems)
        side.finish(side_in, side_out, sems)

    side.result = list(pl.pallas_call(body, out_shape=side.out_sd, in_specs=[hbm] * s_in, out_specs=[hbm] * s_out,
                                      scratch_shapes=side.sems, name=name)(*side.ins))
    return side.result


def slot_cxy(d):
    return 4 * (d % 2) + 2 * (d // 4) + (d // 2) % 2


def slot_cyx(d):
    return 4 * (d % 2) + 2 * ((d // 2) % 2) + d // 4


def mm(a, b, *, ta=False, tb=False, add=None, m_part=None, scatter=None, side=None, name):
    m = a.shape[1] if ta else a.shape[0]
    k = a.shape[0] if ta else a.shape[1]
    n = b.shape[0] if tb else b.shape[1]
    m_off = 0
    if m_part is not None:
        m = m // m_part[1]
        m_off = m_part[0]
    tm, tn, tk = _tile(m, 1024), _tile(n, 512), _tile(k, 2048)
    if scatter is not None and scatter[0] == "rows":
        tm = m // 8
    if scatter is not None and scatter[0] == "cols":
        tn = n // 8
    if m_part is not None:
        assert tm == m
    nk = k // tk
    dn = (((0 if ta else 1,), (1 if tb else 0,)), ((), ()))

    def body(*refs):
        if add is None:
            a_ref, b_ref, o_ref, acc = refs
        else:
            a_ref, b_ref, add_ref, o_ref, acc = refs
        kk = pl.program_id(2)

        @pl.when(kk == 0)
        def _():
            acc[...] = jnp.zeros_like(acc)

        acc[...] += lax.dot_general(a_ref[...].astype(MXU_DT), b_ref[...].astype(MXU_DT), dn,
                                    preferred_element_type=F32)

        @pl.when(kk == nk - 1)
        def _():
            r = acc[...]
            if add is not None:
                r = r + add_ref[...]
            o_ref[...] = r

    if ta:
        a_spec = pl.BlockSpec((tk, tm), lambda i, j, kk: (kk, i + m_off))
    else:
        a_spec = pl.BlockSpec((tm, tk), lambda i, j, kk: (i + m_off, kk))
    b_spec = pl.BlockSpec((tn, tk), lambda i, j, kk: (j, kk)) if tb else pl.BlockSpec((tk, tn), lambda i, j, kk: (kk, j))
    out_sd = jax.ShapeDtypeStruct((m, n), F32)
    if scatter is None:
        o_spec = pl.BlockSpec((tm, tn), lambda i, j, kk: (i, j))
    elif scatter[0] == "rows":
        o_spec = pl.BlockSpec((None, tm, tn), lambda i, j, kk: (scatter[1](i), 0, j))
        out_sd = jax.ShapeDtypeStruct((8, tm, n), F32)
    else:
        o_spec = pl.BlockSpec((None, tm, tn), lambda i, j, kk: (scatter[1](j), i, 0))
        out_sd = jax.ShapeDtypeStruct((8, m, tn), F32)
    ins, specs = [a, b], [a_spec, b_spec]
    if add is not None:
        ins.append(add)
        specs.append(o_spec)
    return _call(body, grid=(m // tm, n // tn, nk), in_specs=specs, out_specs=[o_spec], out_shape=[out_sd],
                 scratch_shapes=[pltpu.VMEM((tm, tn), F32)], sem=("parallel", "parallel", "arbitrary"),
                 name=name, args=ins, side=side)[0]


def rows(arr, t, width=None, base=0, per_j=False):
    width = arr.shape[1] if width is None else width
    return (arr, (t, width), lambda j, i: (i, base + (j if per_j else 0)))


def rows3(arr, lead, t, width, per_j=True):
    return (arr, (None, t, width), lambda j, i: (lead, i, j if per_j else 0))


def bcast(arr, width=None, base=0, per_j=False):
    width = arr.shape[1] if width is None else width
    return (arr, (arr.shape[0], width), lambda j, i: (0, base + (j if per_j else 0)))


def _specs(items):
    return [pl.BlockSpec(bs, im) for (_, bs, im) in items]


def rw_fwd(fn, ins, outs, *, nrow, t, ncol=1, name):
    out_specs = [pl.BlockSpec((t, w), (lambda j, i: (i, j)) if pj else (lambda j, i: (i, 0))) for (w, pj) in outs]
    out_shape = [jax.ShapeDtypeStruct((nrow, w * (ncol if pj else 1)), F32) for (w, pj) in outs]
    nin = len(ins)

    def body(*refs):
        j = pl.program_id(0)
        res = fn(j, *[r[...] for r in refs[:nin]])
        for o_ref, r in zip(refs[nin:], res):
            o_ref[...] = r

    return pl.pallas_call(
        body, grid=(ncol, nrow // t), in_specs=_specs(ins), out_specs=out_specs, out_shape=out_shape,
        compiler_params=_params(("parallel", "parallel")), name=name)(*[x[0] for x in ins])


def rw_bwd(fn, ins, cots, *, nrow, t, ncol=1, row_grads, bc_grads, residual=None, side=None, name):
    nin = len(ins)
    flat_cots = [c for group in cots for c in group]
    extra = [residual] if residual is not None else []
    out_specs, out_shape = [], []
    for idx, pj in row_grads:
        w = ins[idx][1][-1]
        out_specs.append(pl.BlockSpec((t, w), (lambda j, i: (i, j)) if pj else (lambda j, i: (i, 0))))
        out_shape.append(jax.ShapeDtypeStruct((nrow, w * (ncol if pj else 1)), F32))
    for idx in bc_grads:
        r, w = ins[idx][1]
        out_specs.append(pl.BlockSpec((None, r, w), lambda j, i: (j, 0, 0)))
        out_shape.append(jax.ShapeDtypeStruct((ncol, r, w), F32))

    def body(*refs):
        j = pl.program_id(0)
        i = pl.program_id(1)
        vals = [r[...] for r in refs[:nin]]
        pos = nin
        cts = []
        for group in cots:
            c = refs[pos][...]
            for q in range(1, len(group)):
                c = c + refs[pos + q][...]
            pos += len(group)
            cts.append(c)
        res_ref = refs[pos] if residual is not None else None
        pos += len(extra)
        outs = refs[pos:]
        _, vjp = jax.vjp(lambda *a: tuple(fn(j, *a)), *vals)
        grads = vjp(tuple(cts))
        for q, (idx, _) in enumerate(row_grads):
            g = grads[idx]
            if q == 0 and res_ref is not None:
                g = g + res_ref[...]
            outs[q][...] = g
        for q, idx in enumerate(bc_grads):
            o_ref = outs[len(row_grads) + q]

            @pl.when(i == 0)
            def _(o_ref=o_ref):
                o_ref[...] = jnp.zeros_like(o_ref)

            o_ref[...] += grads[idx]

    all_in = list(ins) + flat_cots + extra
    return _call(body, grid=(ncol, nrow // t), in_specs=_specs(all_in), out_specs=out_specs, out_shape=out_shape,
                 sem=("parallel", "arbitrary"), name=name, args=[x[0] for x in all_in], side=side)


def _silu(x):
    return x * jax.nn.sigmoid(x)


@jax.custom_vjp
def _softplus(x):
    return jnp.maximum(x, 0.0) + jnp.log1p(jnp.exp(-jnp.abs(x)))


def _softplus_fwd(x):
    return _softplus(x), x


def _softplus_bwd(x, ct):
    return (ct * jax.nn.sigmoid(x),)


_softplus.defvjp(_softplus_fwd, _softplus_bwd)


def _gelu(x):
    return 0.5 * x * (1.0 + jnp.tanh(math.sqrt(2.0 / math.pi) * (x + 0.044715 * (x * x * x))))


def _row_shift_impl(x, s):
    n = x.shape[0]
    if s == 0:
        return x
    rolled = pltpu.roll(x, (-s) % n, 0)
    t = lax.broadcasted_iota(jnp.int32, x.shape, 0)
    ok = (t + s >= 0) & (t + s < n)
    return jnp.where(ok, rolled, 0.0)


@functools.partial(jax.custom_vjp, nondiff_argnums=(1,))
def _row_shift(x, s):
    return _row_shift_impl(x, s)


def _row_shift_fwd(x, s):
    return _row_shift_impl(x, s), None


def _row_shift_bwd(s, _, ct):
    return (_row_shift_impl(ct, -s),)


_row_shift.defvjp(_row_shift_fwd, _row_shift_bwd)


def fn_rms(j, x, g):
    return (x * lax.rsqrt(jnp.mean(x * x, axis=-1, keepdims=True) + RMS_EPS) * g,)


def make_fn_prep(n_heads):
    pad = (CONV_K - 1) // 2

    def fn_prep(j, x, w):
        y = _row_shift(x, -pad) * w[0:1, :]
        for i in range(1, CONV_K):
            y = y + _row_shift(x, i - pad) * w[i:i + 1, :]
        a = _silu(y)
        scale = jnp.where(j < n_heads, HEAD_DIM ** -0.5, 1.0).astype(F32)
        nrm = a * lax.rsqrt(jnp.sum(a * a, axis=-1, keepdims=True) + RMS_EPS) * scale
        return (jnp.where(j < 2 * n_heads, nrm, a),)

    return fn_prep


def fn_headnorm(j, o0, o1, z, g):
    o = o0 + o1
    n = o * lax.rsqrt(jnp.mean(o * o, axis=-1, keepdims=True) + RMS_EPS) * g
    return (n * _silu(z),)


def fn_gelu(j, y0, y1, u, dsk):
    return (_gelu(y0 + y1 + u * dsk),)


def fn_glu(j, ys, logit, z, b):
    return (ys * jax.nn.sigmoid(logit + b) * _silu(z),)


def fn_merge(j, la, lb, ya, yb, ba, bb):
    return (jax.nn.sigmoid(la + ba) * ya + jax.nn.sigmoid(lb + bb) * yb,)


def fn_loss(j, x, t, g):
    y = x * lax.rsqrt(jnp.mean(x * x, axis=-1, keepdims=True) + RMS_EPS) * g
    e = y - t
    return (0.5 * jnp.mean(e * e, axis=-1, keepdims=True),)


def make_fn_sum(n):
    def fn_sum(j, *xs):
        s = xs[0]
        for q in range(1, n):
            s = s + xs[q]
        return (s,)

    return fn_sum


def fn_adamw(j, w, g, m, v):
    m2 = ADAM_B1 * m + (1.0 - ADAM_B1) * g
    v2 = ADAM_B2 * v + (1.0 - ADAM_B2) * (g * g)
    m_hat = m2 / (1.0 - ADAM_B1 ** ADAM_STEP)
    v_hat = v2 / (1.0 - ADAM_B2 ** ADAM_STEP)
    delta = -ADAM_LR * (m_hat / (jnp.sqrt(v_hat) + ADAM_EPS) + ADAM_WD * w)
    return delta, m2, v2


def _dot(a, b, dims, precision=None):
    return lax.dot_general(a, b, (dims, ((), ())), precision=precision, preferred_element_type=F32)


_NN = ((1,), (0,))
_NT = ((1,), (1,))
_TN = ((0,), (0,))
GDN_HEAD_BLOCK = 4
INVERSE_WIDE_FACTORS = 0


def _split(a):
    hi = a.astype(MXU_DT)
    return hi, (a - hi.astype(F32)).astype(MXU_DT)


def _dot3s(a, b, dims):
    (ah, al), (bh, bl) = a, b
    return _dot(ah, bh, dims) + (_dot(al, bh, dims) + _dot(ah, bl, dims))


def _dot3(a, b, dims):
    return _dot3s(_split(a), _split(b), dims)


def _unit_inverse(lmat):
    r = lmat.shape[0]
    eye = (lax.broadcasted_iota(jnp.int32, (r, r), 0) == lax.broadcasted_iota(jnp.int32, (r, r), 1)).astype(F32)
    pw = -lmat
    tinv = eye + pw
    pws = _split(pw)
    n_fact = int(math.ceil(math.log2(CHUNK))) - 1
    for k in range(n_fact):
        if k < INVERSE_WIDE_FACTORS:
            pws = _split(_dot3s(pws, pws, _NN))
            tinv = tinv + _dot3s(_split(tinv), pws, _NN)
        else:
            pw = _dot(pws[0], pws[0], _NN)
            pws = (pw.astype(MXU_DT), None)
            tinv = tinv + _dot(tinv.astype(MXU_DT), pws[0], _NN)
    return tinv


@jax.custom_vjp
def tri_apply(lmat, rhs, tinv):
    return _dot3(tinv, rhs, _NN)


def _tri_apply_fwd(lmat, rhs, tinv):
    x = _dot3(tinv, rhs, _NN)
    return x, (tinv, x)


def _tri_apply_bwd(res, dx):
    tinv, x = res
    drhs = _dot3(tinv, dx, _TN)
    return -_dot3(drhs, x, _NT), drhs, jnp.zeros_like(tinv)


tri_apply.defvjp(_tri_apply_fwd, _tri_apply_bwd)


def gdn_group(s, q, k, v, ba, alog, dtb, *, d, head0, n_heads, tinv=None):
    hb = len(s)
    c = q.shape[0]
    r = hb * c

    def stack(x):
        return jnp.concatenate([x[:, i * HEAD_DIM:(i + 1) * HEAD_DIM] for i in range(hb)], axis=0)

    def pick(x, lane0):
        lane = lax.broadcasted_iota(jnp.int32, x.shape, 1)
        return jnp.concatenate(
            [jnp.sum(jnp.where(lane == lane0 + i, x, 0.0), axis=1, keepdims=True) for i in range(hb)], axis=0)

    q4, k4, v4 = stack(q), stack(k), stack(v)
    beta = pick(jax.nn.sigmoid(ba), d * n_heads + head0)
    g = pick(-jnp.exp(alog) * _softplus(ba + dtb), 2 * n_heads + d * n_heads + head0)
    ii = lax.broadcasted_iota(jnp.int32, (r, r), 0)
    jj = lax.broadcasted_iota(jnp.int32, (r, r), 1)
    same = (ii // c) == (jj // c)
    rel = (ii - jj) * (1 - 2 * d)
    incl = same & (rel >= 0)
    strict = same & (rel > 0)
    incl_t = same & (rel <= 0)
    g_row = jnp.sum(jnp.where(ii == jj, g, 0.0), axis=0, keepdims=True)
    gc_col = jnp.sum(jnp.where(incl, g_row, 0.0), axis=1, keepdims=True)
    gc_row = jnp.sum(jnp.where(incl_t, g, 0.0), axis=0, keepdims=True)
    g_tot = jnp.sum(jnp.where(same, g_row, 0.0), axis=1, keepdims=True)
    decay = jnp.exp(jnp.where(incl, gc_col - gc_row, -1e30))
    kb = k4 * beta
    vb = v4 * beta
    lmat = jnp.where(strict, _dot(kb, k4, _NT) * decay, 0.0)
    fresh = tinv is None
    if fresh:
        tinv = _unit_inverse(lmat)
    uw = tri_apply(lmat, jnp.concatenate([vb, kb * jnp.exp(gc_col)], axis=1), tinv)
    u, w = uw[:, :HEAD_DIM], uw[:, HEAD_DIM:]
    qk = _dot(q4, k4, _NT) * decay
    qe = q4 * jnp.exp(gc_col)
    kd = k4 * jnp.exp(g_tot - gc_col)
    eg = jnp.exp(g_tot)
    v_new, o_s = [], []
    for i in range(hb):
        rs = slice(i * c, (i + 1) * c)
        ws = _dot(jnp.concatenate([w[rs], qe[rs]], axis=0), s[i], _NN)
        v_new.append(u[rs] - ws[:c])
        o_s.append(ws[c:])
    o4 = jnp.concatenate(o_s, axis=0) + _dot(qk, jnp.concatenate(v_new, axis=0), _NN)
    s_new = tuple(s[i] * eg[i * c:i * c + 1, :] + _dot(kd[i * c:(i + 1) * c], v_new[i], _TN) for i in range(hb))
    o = jnp.concatenate([o4[i * c:(i + 1) * c] for i in range(hb)], axis=1)
    return (s_new, o, tinv) if fresh else (s_new, o)


def _gdn_maps(n_chunks):
    def chunk_of(d, step):
        return step + d * (n_chunks - 1 - 2 * step)
    return chunk_of


def _heads(ref, h0, n):
    return ref[:, h0 * HEAD_DIM:(h0 + n) * HEAD_DIM]


def gdn_fwd(qkv, proj, ba_blk, alog_row, dtb_row, *, n_heads, side=None, name):
    nrow = qkv.shape[0]
    nc = nrow // CHUNK
    h = n_heads
    wa = h * HEAD_DIM
    hb = math.gcd(h, GDN_HEAD_BLOCK)
    chunk_of = _gdn_maps(nc)

    def body(q_ref, k_ref, v_ref, ba_ref, al_ref, dt_ref, o_ref, sh_ref, ti_ref, s_scr):
        d = pl.program_id(0)
        n = pl.program_id(1)

        @pl.when(n == 0)
        def _():
            s_scr[...] = jnp.zeros_like(s_scr)

        ba, al, dt = ba_ref[...], al_ref[...], dt_ref[...]
        for h0 in range(0, h, hb):
            s = tuple(s_scr[h0 + i] for i in range(hb))
            for i in range(hb):
                sh_ref[h0 + i] = s[i]
            s2, o, tinv = gdn_group(s, _heads(q_ref, h0, hb), _heads(k_ref, h0, hb), _heads(v_ref, h0, hb),
                                    ba, al, dt, d=d, head0=h0, n_heads=h)
            o_ref[:, h0 * HEAD_DIM:(h0 + hb) * HEAD_DIM] = o
            ti_ref[h0 // hb] = tinv
            for i in range(hb):
                s_scr[h0 + i] = s2[i]

    blk = (CHUNK, wa)
    in_specs = [
        pl.BlockSpec(blk, lambda d, n: (chunk_of(d, n), 0)),
        pl.BlockSpec(blk, lambda d, n: (chunk_of(d, n), 1)),
        pl.BlockSpec(blk, lambda d, n: (chunk_of(d, n), 2)),
        pl.BlockSpec((CHUNK, LANES), lambda d, n: (chunk_of(d, n), ba_blk)),
        pl.BlockSpec((1, LANES), lambda d, n: (0, 0)),
        pl.BlockSpec((1, LANES), lambda d, n: (0, 0)),
    ]
    ng, r = h // hb, hb * CHUNK
    out_specs = [
        pl.BlockSpec((None, CHUNK, wa), lambda d, n: (d, chunk_of(d, n), 0)),
        pl.BlockSpec((h, None, HEAD_DIM, HEAD_DIM), lambda d, n: (d, n, 0, 0)),
        pl.BlockSpec((None, None, ng, r, r), lambda d, n: (d, n, 0, 0, 0)),
    ]
    out_shape = [jax.ShapeDtypeStruct((2, nrow, wa), F32),
                 jax.ShapeDtypeStruct((2 * h, nc, HEAD_DIM, HEAD_DIM), F32),
                 jax.ShapeDtypeStruct((2, nc, ng, r, r), F32)]
    return _call(body, grid=(2, nc), in_specs=in_specs, out_specs=out_specs, out_shape=out_shape,
                 scratch_shapes=[pltpu.VMEM((h, HEAD_DIM, HEAD_DIM), F32)], sem=("parallel", "arbitrary"),
                 name=name, args=(qkv, qkv, qkv, proj, alog_row, dtb_row), side=side)


def gdn_bwd(qkv, proj, ba_blk, alog_row, dtb_row, s_hist, t_hist, do, *, n_heads, side=None, name):
    nrow = qkv.shape[0]
    nc = nrow // CHUNK
    h = n_heads
    wa = h * HEAD_DIM
    hb = math.gcd(h, GDN_HEAD_BLOCK)
    chunk_of = _gdn_maps(nc)

    def cb(d, n):
        return chunk_of(d, nc - 1 - n)

    def body(q_ref, k_ref, v_ref, ba_ref, al_ref, dt_ref, sh_ref, do_ref, ti_ref,
             dqkv_ref, dba_ref, dal_ref, ddt_ref, ds_scr):
        d = pl.program_id(0)
        n = pl.program_id(1)

        @pl.when(n == 0)
        def _():
            ds_scr[...] = jnp.zeros_like(ds_scr)
            dal_ref[...] = jnp.zeros_like(dal_ref)
            ddt_ref[...] = jnp.zeros_like(ddt_ref)

        ba, al, dt = ba_ref[...], al_ref[...], dt_ref[...]
        dba_sum = jnp.zeros_like(ba)
        dal_sum = jnp.zeros_like(al)
        ddt_sum = jnp.zeros_like(dt)
        for h0 in range(0, h, hb):
            f = functools.partial(gdn_group, d=d, head0=h0, n_heads=h, tinv=ti_ref[h0 // hb])
            s = tuple(sh_ref[h0 + i] for i in range(hb))
            _, vjp = jax.vjp(f, s, _heads(q_ref, h0, hb), _heads(k_ref, h0, hb), _heads(v_ref, h0, hb), ba, al, dt)
            ds, dq, dk, dv, dba, dal, ddt = vjp((tuple(ds_scr[h0 + i] for i in range(hb)), _heads(do_ref, h0, hb)))
            for i in range(hb):
                ds_scr[h0 + i] = ds[i]
            cols = slice(h0 * HEAD_DIM, (h0 + hb) * HEAD_DIM)
            dqkv_ref[0, :, cols] = dq
            dqkv_ref[1, :, cols] = dk
            dqkv_ref[2, :, cols] = dv
            dba_sum = dba_sum + dba
            dal_sum = dal_sum + dal
            ddt_sum = ddt_sum + ddt
        dba_ref[...] = dba_sum
        dal_ref[...] += dal_sum
        ddt_ref[...] += ddt_sum

    blk = (CHUNK, wa)
    in_specs = [
        pl.BlockSpec(blk, lambda d, n: (cb(d, n), 0)),
        pl.BlockSpec(blk, lambda d, n: (cb(d, n), 1)),
        pl.BlockSpec(blk, lambda d, n: (cb(d, n), 2)),
        pl.BlockSpec((CHUNK, LANES), lambda d, n: (cb(d, n), ba_blk)),
        pl.BlockSpec((1, LANES), lambda d, n: (0, 0)),
        pl.BlockSpec((1, LANES), lambda d, n: (0, 0)),
        pl.BlockSpec((h, None, HEAD_DIM, HEAD_DIM), lambda d, n: (d, nc - 1 - n, 0, 0)),
        pl.BlockSpec(blk, lambda d, n: (cb(d, n), 0)),
        pl.BlockSpec((None, None, h // hb, hb * CHUNK, hb * CHUNK), lambda d, n: (d, nc - 1 - n, 0, 0, 0)),
    ]
    out_specs = [
        pl.BlockSpec((None, 3, CHUNK, wa), lambda d, n: (d, 0, cb(d, n), 0)),
        pl.BlockSpec((None, CHUNK, LANES), lambda d, n: (d, cb(d, n), 0)),
        pl.BlockSpec((None, 1, LANES), lambda d, n: (d, 0, 0)),
        pl.BlockSpec((None, 1, LANES), lambda d, n: (d, 0, 0)),
    ]
    out_shape = [jax.ShapeDtypeStruct((2, 3, nrow, wa), F32),
                 jax.ShapeDtypeStruct((2, nrow, LANES), F32),
                 jax.ShapeDtypeStruct((2, 1, LANES), F32),
                 jax.ShapeDtypeStruct((2, 1, LANES), F32)]
    return _call(body, grid=(2, nc), in_specs=in_specs, out_specs=out_specs, out_shape=out_shape,
                 scratch_shapes=[pltpu.VMEM((h, HEAD_DIM, HEAD_DIM), F32)], sem=("parallel", "arbitrary"),
                 name=name, args=(qkv, qkv, qkv, proj, alog_row, dtb_row, s_hist, do, t_hist), side=side)


def s5_mats(lam_re, lam_im, log_dt, b_re, b_im, c_re, c_im):
    g = lam_re.shape[1]
    nb = g // S5_BLOCK_GROUPS
    dt = jnp.exp(log_dt)[..., None]
    mag = jnp.exp(lam_re * dt)
    ar = mag * jnp.cos(lam_im * dt)
    ai = mag * jnp.sin(lam_im * dt)
    den = lam_re * lam_re + lam_im * lam_im
    fr = ((ar - 1.0) * lam_re + ai * lam_im) / den
    fi = (ai * lam_re - (ar - 1.0) * lam_im) / den
    bbr = fr[..., None] * b_re - fi[..., None] * b_im
    bbi = fr[..., None] * b_im + fi[..., None] * b_re
    eye = jnp.eye(S5_BLOCK_GROUPS, dtype=F32)
    shp = (2, nb, S5_BLOCK_GROUPS, S5_STATE, S5_GROUP_CH)
    w_r = jnp.einsum('dsjpc,jk->dsjckp', bbr.reshape(shp), eye).reshape(2, nb, LANES, S5_BLOCK_STATE)
    w_i = jnp.einsum('dsjpc,jk->dsjckp', bbi.reshape(shp), eye).reshape(2, nb, LANES, S5_BLOCK_STATE)
    w = jnp.concatenate([w_r, w_i], axis=-1)
    shc = (2, nb, S5_BLOCK_GROUPS, S5_GROUP_CH, S5_STATE)
    c_r = jnp.einsum('dsjcp,jk->dsjpkc', c_re.reshape(shc), eye).reshape(2, nb, S5_BLOCK_STATE, LANES)
    c_i = jnp.einsum('dsjcp,jk->dsjpkc', c_im.reshape(shc), eye).reshape(2, nb, S5_BLOCK_STATE, LANES)
    cm = jnp.concatenate([c_r, -c_i], axis=-2)
    return (ar.reshape(2, nb, 1, S5_BLOCK_STATE), ai.reshape(2, nb, 1, S5_BLOCK_STATE), w, cm)


_S5_ROWS = 512
S5_SEGMENTS = 8


def segment_major(a):
    n, c = a.shape
    return jnp.transpose(a.reshape(S5_SEGMENTS, n // S5_SEGMENTS, c), (1, 0, 2)).reshape(n, c)


def time_major(a):
    n, c = a.shape
    return jnp.transpose(a.reshape(n // S5_SEGMENTS, S5_SEGMENTS, c), (1, 0, 2)).reshape(n, c)


def _segmented_scan(x_ref, a_r, a_i, *, reverse, visit=None):
    nrow, two_hs = x_ref.shape
    hs = two_hs // 2
    steps = nrow // S5_SEGMENTS
    assert steps & (steps - 1) == 0
    b_r = jnp.broadcast_to(a_r, (S5_SEGMENTS, hs))
    b_i = jnp.broadcast_to(a_i, (S5_SEGMENTS, hs))
    rid = lax.broadcasted_iota(jnp.int32, (S5_SEGMENTS, hs), 0)

    def rows_of(i):
        return pl.ds(pl.multiple_of((steps - 1 - i if reverse else i) * S5_SEGMENTS, S5_SEGMENTS), S5_SEGMENTS)

    def local(i, carry):
        s_r, s_i = carry
        rows = rows_of(i)
        n_r = b_r * s_r - b_i * s_i + x_ref[rows, pl.ds(0, hs)]
        n_i = b_r * s_i + b_i * s_r + x_ref[rows, pl.ds(hs, hs)]
        x_ref[rows, pl.ds(0, hs)] = n_r
        x_ref[rows, pl.ds(hs, hs)] = n_i
        return n_r, n_i

    z = jnp.zeros((S5_SEGMENTS, hs), F32)
    e_r, e_i = lax.fori_loop(0, steps, local, (z, z))
    q_r, q_i = a_r, a_i
    for _ in range(steps.bit_length() - 1):
        q_r, q_i = q_r * q_r - q_i * q_i, 2.0 * q_r * q_i
    c_r = jnp.zeros((1, hs), F32)
    c_i = jnp.zeros((1, hs), F32)
    en_r, en_i = z, z
    for s in (reversed(range(S5_SEGMENTS)) if reverse else range(S5_SEGMENTS)):
        en_r = jnp.where(rid == s, c_r, en_r)
        en_i = jnp.where(rid == s, c_i, en_i)
        c_r, c_i = (e_r[s:s + 1] + q_r * c_r - q_i * c_i, e_i[s:s + 1] + q_r * c_i + q_i * c_r)

    def fix(i, carry):
        p_r, p_i = carry
        rows = rows_of(i)
        x_r = x_ref[rows, pl.ds(0, hs)] + (p_r * en_r - p_i * en_i)
        x_i = x_ref[rows, pl.ds(hs, hs)] + (p_r * en_i + p_i * en_r)
        x_ref[rows, pl.ds(0, hs)] = x_r
        x_ref[rows, pl.ds(hs, hs)] = x_i
        if visit is not None:
            visit(steps - 1 - i if reverse else i, rows, x_r, x_i)
        return p_r * b_r - p_i * b_i, p_r * b_i + p_i * b_r

    return lax.fori_loop(0, steps, fix, (b_r, b_i))


def s5_fwd(u_seg, ar, ai, w, cm, *, side=None, name):
    nrow = u_seg.shape[0]
    nb = w.shape[1]
    hs = S5_BLOCK_STATE
    rs = min(_S5_ROWS, nrow)

    def body(u_ref, ar_ref, ai_ref, w_ref, cm_ref, y_ref, x_ref):
        d = pl.program_id(0)
        wb = w_ref[...].astype(MXU_DT)
        for r0 in range(0, nrow, rs):
            x_ref[pl.ds(r0, rs), :] = _dot(u_ref[pl.ds(r0, rs), :].astype(MXU_DT), wb, _NN)
        @pl.when(d == 0)
        def _():
            _segmented_scan(x_ref, ar_ref[...], ai_ref[...], reverse=False)

        @pl.when(d == 1)
        def _():
            _segmented_scan(x_ref, ar_ref[...], ai_ref[...], reverse=True)

        cb = cm_ref[...].astype(MXU_DT)
        for r0 in range(0, nrow, rs):
            y_ref[pl.ds(r0, rs), :] = _dot(x_ref[pl.ds(r0, rs), :].astype(MXU_DT), cb, _NN)

    in_specs = [
        pl.BlockSpec((nrow, LANES), lambda d, s: (0, s)),
        pl.BlockSpec((None, None, 1, hs), lambda d, s: (d, s, 0, 0)),
        pl.BlockSpec((None, None, 1, hs), lambda d, s: (d, s, 0, 0)),
        pl.BlockSpec((None, None, LANES, 2 * hs), lambda d, s: (d, s, 0, 0)),
        pl.BlockSpec((None, None, 2 * hs, LANES), lambda d, s: (d, s, 0, 0)),
    ]
    out_specs = [
        pl.BlockSpec((None, nrow, LANES), lambda d, s: (d, 0, s)),
        pl.BlockSpec((None, nrow, 2 * hs), lambda d, s: (d, 0, s)),
    ]
    out_shape = [jax.ShapeDtypeStruct((2, nrow, nb * LANES), F32),
                 jax.ShapeDtypeStruct((2, nrow, nb * 2 * hs), F32)]
    return _call(body, grid=(2, nb), in_specs=in_specs, out_specs=out_specs, out_shape=out_shape,
                 sem=("parallel", "parallel"), name=name, args=(u_seg, ar, ai, w, cm), side=side)


def s5_bwd(u_seg, ar, ai, w, cm, xs, dy, du_in, *, side=None, name):
    nrow = u_seg.shape[0]
    nb = w.shape[1]
    steps = nrow // S5_SEGMENTS
    hs = S5_BLOCK_STATE
    rs = min(_S5_ROWS, nrow)

    def body(u_ref, dy_ref, dui_ref, x_ref, ar_ref, ai_ref, w_ref, cm_ref,
             du_ref, dw_ref, dcm_ref, dar_ref, dai_ref, g_ref, acc_ref):
        d = pl.program_id(1)
        cb = cm_ref[...].astype(MXU_DT)
        for r0 in range(0, nrow, rs):
            g_ref[pl.ds(r0, rs), :] = _dot(dy_ref[pl.ds(r0, rs), :].astype(MXU_DT), cb, _NT)
        acc_ref[...] = jnp.zeros_like(acc_ref)
        rid = lax.broadcasted_iota(jnp.int32, (S5_SEGMENTS, 2 * hs), 0)

        def run(forward_recurrence):
            last = pl.ds((steps - 1) * S5_SEGMENTS, S5_SEGMENTS)
            first = pl.ds(0, S5_SEGMENTS)
            if forward_recurrence:
                wrap = jnp.where(rid == 0, 0.0, pltpu.roll(x_ref[last, :], 1, 0))
            else:
                wrap = jnp.where(rid == S5_SEGMENTS - 1, 0.0, pltpu.roll(x_ref[first, :], S5_SEGMENTS - 1, 0))

            def visit(step, rows, g_r, g_i):
                if forward_recurrence:
                    nbr = jnp.maximum(step - 1, 0)
                    edge = step == 0
                else:
                    nbr = jnp.minimum(step + 1, steps - 1)
                    edge = step == steps - 1
                prev = x_ref[pl.ds(pl.multiple_of(nbr * S5_SEGMENTS, S5_SEGMENTS), S5_SEGMENTS), :]
                prev = jnp.where(edge, wrap, prev)
                p_r, p_i = prev[:, :hs], prev[:, hs:]
                acc_ref[:, pl.ds(0, hs)] += g_r * p_r + g_i * p_i
                acc_ref[:, pl.ds(hs, hs)] += g_i * p_r - g_r * p_i

            _segmented_scan(g_ref, ar_ref[...], -ai_ref[...], reverse=forward_recurrence, visit=visit)

        @pl.when(d == 0)
        def _():
            run(True)

        @pl.when(d == 1)
        def _():
            run(False)

        dar_ref[...] = jnp.sum(acc_ref[:, pl.ds(0, hs)], axis=0, keepdims=True)
        dai_ref[...] = jnp.sum(acc_ref[:, pl.ds(hs, hs)], axis=0, keepdims=True)

        wb = w_ref[...].astype(MXU_DT)
        for r0 in range(0, nrow, rs):
            part = _dot(g_ref[pl.ds(r0, rs), :].astype(MXU_DT), wb, _NT)

            @pl.when(d == 0)
            def _(part=part, r0=r0):
                du_ref[pl.ds(r0, rs), :] = dui_ref[pl.ds(r0, rs), :] + part

            @pl.when(d == 1)
            def _(part=part, r0=r0):
                du_ref[pl.ds(r0, rs), :] += part

        dw_ref[...] = _dot(u_ref[...].astype(MXU_DT), g_ref[...].astype(MXU_DT), _TN)
        dcm_ref[...] = _dot(x_ref[...].astype(MXU_DT), dy_ref[...].astype(MXU_DT), _TN)

    in_specs = [
        pl.BlockSpec((nrow, LANES), lambda s, d: (0, s)),
        pl.BlockSpec((nrow, LANES), lambda s, d: (0, s)),
        pl.BlockSpec((nrow, LANES), lambda s, d: (0, s)),
        pl.BlockSpec((None, nrow, 2 * hs), lambda s, d: (d, 0, s)),
        pl.BlockSpec((None, None, 1, hs), lambda s, d: (d, s, 0, 0)),
        pl.BlockSpec((None, None, 1, hs), lambda s, d: (d, s, 0, 0)),
        pl.BlockSpec((None, None, LANES, 2 * hs), lambda s, d: (d, s, 0, 0)),
        pl.BlockSpec((None, None, 2 * hs, LANES), lambda s, d: (d, s, 0, 0)),
    ]
    out_specs = [
        pl.BlockSpec((nrow, LANES), lambda s, d: (0, s)),
        pl.BlockSpec((None, None, LANES, 2 * hs), lambda s, d: (d, s, 0, 0)),
        pl.BlockSpec((None, None, 2 * hs, LANES), lambda s, d: (d, s, 0, 0)),
        pl.BlockSpec((None, None, 1, hs), lambda s, d: (d, s, 0, 0)),
        pl.BlockSpec((None, None, 1, hs), lambda s, d: (d, s, 0, 0)),
    ]
    out_shape = [jax.ShapeDtypeStruct((nrow, nb * LANES), F32),
                 jax.ShapeDtypeStruct(w.shape, F32), jax.ShapeDtypeStruct(cm.shape, F32),
                 jax.ShapeDtypeStruct(ar.shape, F32), jax.ShapeDtypeStruct(ai.shape, F32)]
    return _call(body, grid=(nb, 2), in_specs=in_specs, out_specs=out_specs, out_shape=out_shape,
                 scratch_shapes=[pltpu.VMEM((nrow, 2 * hs), F32), pltpu.VMEM((S5_SEGMENTS, 2 * hs), F32)],
                 sem=("parallel", "arbitrary"), name=name, args=(u_seg, dy, du_in, xs, ar, ai, w, cm), side=side)


def _me():
    return lax.axis_index("x"), lax.axis_index("y"), lax.axis_index("c")


def all_gather(shards):
    na = len(shards)

    def plan(x_refs, out_refs, sems):
        send_sems, recv_sems, local_sems = sems
        x, y, c = _me()
        me, sibling = (x, y, c), (x, y, 1 - c)
        chips = [(1 - x, y), (x, 1 - y), (1 - x, 1 - y)]

        def slot(a, px, py, pc):
            return out_refs[a].at[4 * px + 2 * py + pc]

        def copy(a, k, block, to, src=None):
            return pltpu.make_async_remote_copy(
                src_ref=slot(a, *block) if src is None else src, dst_ref=slot(a, *block),
                send_sem=send_sems.at[a, k], recv_sem=recv_sems.at[a, k], device_id=to, device_id_type=MESH)

        mine = [pltpu.make_async_copy(x_refs[a], slot(a, *me), local_sems.at[a]) for a in range(na)]
        first = []
        for a in range(na):
            first.append(copy(a, 0, me, sibling, src=x_refs[a]))
            first += [copy(a, 1 + j, me, (*chip, c), src=x_refs[a]) for j, chip in enumerate(chips)]
        return me, sibling, chips, c, copy, mine, first

    def start(x_refs, out_refs, sems):
        _, _, _, _, _, mine, first = plan(x_refs, out_refs, sems)
        for cp in mine + first:
            cp.start()

    def finish(x_refs, out_refs, sems):
        me, sibling, chips, c, copy, mine, first = plan(x_refs, out_refs, sems)
        passed = []
        for j, chip in enumerate(chips):
            for a in range(na):
                copy(a, 1 + j, (*chip, c), me).wait_recv()
                fwd = copy(a, 4 + j, (*chip, c), sibling)
                fwd.start()
                passed.append(fwd)
        for a in range(na):
            copy(a, 0, sibling, me).wait_recv()
            for j, chip in enumerate(chips):
                copy(a, 4 + j, (*chip, 1 - c), me).wait_recv()
        for cp in first + passed:
            cp.wait_send()
        for cp in mine:
            cp.wait()

    return Side(shards, [jax.ShapeDtypeStruct((8,) + s.shape, s.dtype) for s in shards],
                [pltpu.SemaphoreType.DMA((na, 7)), pltpu.SemaphoreType.DMA((na, 7)), pltpu.SemaphoreType.DMA((na,))],
                start, finish)


_AXES = ("x", "y", "c")


def exchange(bufs, axes, *, half):
    na = len(bufs)
    if isinstance(half, bool):
        half = [half] * na

    def copies(in_refs, out_refs, sems):
        send_sems, recv_sems = sems
        me = _me()
        cps = []
        for a in range(na):
            bit = me[_AXES.index(axes[a])]
            peer = tuple(1 - v if ax == axes[a] else v for ax, v in zip(_AXES, me))
            cps.append(pltpu.make_async_remote_copy(
                src_ref=in_refs[a].at[1 - bit] if half[a] else in_refs[a], dst_ref=out_refs[a],
                send_sem=send_sems.at[a], recv_sem=recv_sems.at[a], device_id=peer, device_id_type=MESH))
        return cps

    def start(*refs):
        for cp in copies(*refs):
            cp.start()

    def finish(*refs):
        for cp in copies(*refs):
            cp.wait()

    return Side(bufs, [jax.ShapeDtypeStruct(b.shape[1:] if h else b.shape, b.dtype) for b, h in zip(bufs, half)],
                [pltpu.SemaphoreType.DMA((na,)), pltpu.SemaphoreType.DMA((na,))], start, finish)


def all_to_all(pieces, slot_fns):
    na = len(pieces)

    def copies(in_refs, out_refs, sems):
        send_sems, recv_sems, local_sems = sems
        x, y, c = _me()
        cps = []
        for a in range(na):
            mine = slot_fns[a](x, y, c)
            for k in range(8):
                tx, ty, tc = x ^ (k // 4), y ^ ((k // 2) % 2), c ^ (k % 2)
                src = in_refs[a].at[4 * tx + 2 * ty + tc]
                dst = out_refs[a].at[mine]
                if k == 0:
                    cps.append(pltpu.make_async_copy(src, dst, local_sems.at[a]))
                else:
                    cps.append(pltpu.make_async_remote_copy(
                        src_ref=src, dst_ref=dst, send_sem=send_sems.at[a, k - 1], recv_sem=recv_sems.at[a, k - 1],
                        device_id=(tx, ty, tc), device_id_type=MESH))
        return cps

    def start(*refs):
        for cp in copies(*refs):
            cp.start()

    def finish(*refs):
        for cp in copies(*refs):
            cp.wait()

    return Side(pieces, [jax.ShapeDtypeStruct(p.shape, p.dtype) for p in pieces],
                [pltpu.SemaphoreType.DMA((na, 7)), pltpu.SemaphoreType.DMA((na, 7)), pltpu.SemaphoreType.DMA((na,))],
                start, finish)


def add_half(buf, recv, bit, *, narrow, name):
    c = recv.shape[-1]
    r = math.prod(recv.shape[:-1])
    t = _tile(r, max(16, (ADD_BLOCK_BYTES // (4 * c)) // 16 * 16), 16)

    def body(bit_ref, a_ref, b_ref, o_ref, *tx_ref):
        s = a_ref[...] + b_ref[...].astype(F32)
        o_ref[...] = s
        if narrow:
            tx_ref[0][...] = s.astype(WIRE_DT)

    o_spec = pl.BlockSpec((t, c), lambda i, b: (i, 0))
    grid_spec = pltpu.PrefetchScalarGridSpec(
        num_scalar_prefetch=1, grid=(r // t,),
        in_specs=[pl.BlockSpec((None, t, c), lambda i, b: (b[0], i, 0)), o_spec],
        out_specs=[o_spec, o_spec] if narrow else [o_spec])
    out_shape = [jax.ShapeDtypeStruct((r, c), F32)] + ([jax.ShapeDtypeStruct((r, c), WIRE_DT)] if narrow else [])
    outs = pl.pallas_call(body, grid_spec=grid_spec, out_shape=out_shape,
                          compiler_params=_params(("parallel",)), name=name)(
                              bit, buf.reshape(2, r, c), recv.reshape(r, c))
    return [o.reshape(recv.shape) for o in outs]


def _pack_rows(n, cols):
    return -(-n // (8 * cols)) * 8


def _pack(arrs, cols, mult):
    parts = []
    for a in arrs:
        n = math.prod(a.shape)
        nr = _pack_rows(n, cols)
        parts.append(jnp.pad(a.reshape(-1), (0, nr * cols - n)).reshape(nr, cols))
    total = sum(p.shape[0] for p in parts)
    pad_rows = -(-total // mult) * mult - total
    if pad_rows:
        parts.append(jnp.zeros((pad_rows, cols), arrs[0].dtype))
    return jnp.concatenate(parts, axis=0)


def _unpack(buf, shapes):
    out, r = [], 0
    cols = buf.shape[1]
    for s in shapes:
        n = math.prod(s)
        nr = _pack_rows(n, cols)
        piece = lax.optimization_barrier(buf[r:r + nr])
        out.append(piece.reshape(-1)[:n].reshape(s))
        r += nr
    return out


def _elementwise(fn, arrs, nout, name):
    r, c = arrs[0].shape
    lanes = -(-c // LANES) * LANES
    t = _tile(r, max(ROW_TILE, (ELEMENTWISE_BLOCK_BYTES // (4 * lanes)) // 8 * 8), 8)
    return rw_fwd(fn, [rows(a, t) for a in arrs], [(c, False)] * nout, nrow=r, t=t, name=name)


def kernel(x, ln_g, w_in, conv_w, a_log, dt_bias, head_norm_g, lam_re, lam_im, log_dt, b_re, b_im, c_re, c_im, d_skip, w_glu, b_glu, w_pa, w_pb, b_gate, w_out, final_g, loss_target, m_ln_g, m_w_in, m_conv_w, m_a_log, m_dt_bias, m_head_norm_g, m_lam_re, m_lam_im, m_log_dt, m_b_re, m_b_im, m_c_re, m_c_im, m_d_skip, m_w_glu, m_b_glu, m_w_pa, m_w_pb, m_b_gate, m_w_out, m_final_g, v_ln_g, v_w_in, v_conv_w, v_a_log, v_dt_bias, v_head_norm_g, v_lam_re, v_lam_im, v_log_dt, v_b_re, v_b_im, v_c_re, v_c_im, v_d_skip, v_w_glu, v_b_glu, v_w_pa, v_w_pb, v_b_gate, v_w_out, v_final_g):
    env = dict(locals())
    wts = {n: env[n] for n in WEIGHTS}
    mom_m = {n: env["m_" + n] for n in WEIGHTS}
    mom_v = {n: env["v_" + n] for n in WEIGHTS}

    xin = x[0]
    tgt = loss_target[0]
    nrow, dm = xin.shape
    depth = ln_g.shape[0]
    nh = dm // (2 * HEAD_DIM)
    wa = nh * HEAD_DIM
    wb = dm // 2
    ngrp = wb // S5_GROUP_CH
    pw = w_in.shape[-1]
    t = min(ROW_TILE, nrow)

    o_ba = 4 * wa
    o_u = o_ba + 4 * nh
    n_main = 4 * wa + 2 * wb + 2 * dm
    projp = -(-(n_main + LANES) // 512) * 512
    blk_za = 3 * wa // HEAD_DIM
    blk_u = 4 * wa // LANES
    ba_blk = n_main // LANES
    cx, cy, cc = _me()

    jb = o_ba // pw
    assert (o_u - 1) // pw == jb
    cut_lo, cut = o_ba - jb * pw, o_u - o_ba
    wide = -(-pw // LANES) * LANES + LANES
    col = lax.broadcasted_iota(jnp.int32, (pw, wide), 1)
    row = lax.broadcasted_iota(jnp.int32, (pw, wide), 0)
    holds_cut = (4 * cx + 2 * cy + cc) == jb
    src_plain = jnp.where(col < pw, col, -1)
    src_cut = jnp.where(col < cut_lo, col, jnp.where(col < pw - cut, col + cut, -1))
    src_cut = jnp.where((col >= wide - LANES) & (col < wide - LANES + cut), col - (wide - LANES) + cut_lo, src_cut)
    select = (row == jnp.where(holds_cut, src_cut, src_plain)).astype(WIRE_DT)
    w_in_tx = mm(w_in.reshape(depth * dm, pw), select, name="w_in_prepare").astype(WIRE_DT).reshape(depth, dm, wide)

    split = (2 * wide // 3) // LANES * LANES

    def gather_parts(l):
        return ([w_in_tx[l][:, :split]],
                [w_in_tx[l][:, split:], w_out[l].astype(WIRE_DT)],
                [w_glu[l].astype(WIRE_DT), w_pa[l].astype(WIRE_DT), w_pb[l].astype(WIRE_DT), conv_w[l]])

    def cat(g):
        return jnp.concatenate([g[j] for j in range(8)], axis=1)

    def assemble(part_a, part_b, part_c):
        (g_a,), (g_b, g_out), (g_glu, g_pa, g_pb, g_conv) = part_a, part_b, part_c
        pieces = []
        for j in range(8):
            valid = pw - cut if j == jb else pw
            pieces += [g_a[j], g_b[j][:, :valid - split]]
        w_perm = jnp.concatenate(
            pieces + [g_b[jb][:, wide - LANES - split:], jnp.zeros((dm, projp - n_main - LANES), WIRE_DT)], axis=1)
        return dict(w_in=w_perm, w_glu=g_glu.reshape(wb, wb), w_pa=cat(g_pa), w_pb=cat(g_pb),
                    w_out=g_out.reshape(dm, dm), conv_w=cat(g_conv))

    parts0 = gather_parts(0)
    first = run_side(all_gather(parts0[0] + parts0[1] + parts0[2]), name="gather_weights")
    full = [assemble(first[:1], first[1:3], first[3:])]

    def small(l):
        z = jnp.zeros((1, LANES - 4 * nh), F32)
        alog_row = jnp.concatenate([jnp.zeros((1, 2 * nh), F32), a_log[l].reshape(1, 2 * nh), z], axis=1)
        dtb_row = jnp.concatenate([jnp.zeros((1, 2 * nh), F32), dt_bias[l].reshape(1, 2 * nh), z], axis=1)
        return alog_row, dtb_row

    saved = []
    cur = xin
    for l in range(depth):
        fw = full[l]
        (hh,) = rw_fwd(fn_rms, [rows(cur, t), bcast(ln_g[l][None])], [(dm, False)], nrow=nrow, t=t, name="rms_fwd")
        nxt = [all_gather(p) for p in gather_parts(l + 1)] if l + 1 < depth else [None] * 3
        proj = mm(hh, fw["w_in"], side=nxt[2], name="proj_fwd")
        (qkv,) = rw_fwd(make_fn_prep(nh), [rows(proj, nrow, HEAD_DIM, 0, True), bcast(fw["conv_w"], HEAD_DIM, 0, True)],
                        [(HEAD_DIM, True)], nrow=nrow, t=nrow, ncol=3 * nh, name="prep_fwd")
        alog_row, dtb_row = small(l)
        o_dir, s_hist, t_hist = gdn_fwd(qkv, proj, ba_blk, alog_row, dtb_row, n_heads=nh, side=nxt[0],
                                        name="gdn_fwd")
        hn_ins = [rows3(o_dir, 0, t, HEAD_DIM), rows3(o_dir, 1, t, HEAD_DIM), rows(proj, t, HEAD_DIM, blk_za, True),
                  bcast(head_norm_g[l][None])]
        (ya_in,) = rw_fwd(fn_headnorm, hn_ins, [(HEAD_DIM, True)], nrow=nrow, t=t, ncol=nh, name="headnorm_fwd")
        mats, mats_vjp = jax.vjp(s5_mats, lam_re[l], lam_im[l], log_dt[l], b_re[l], b_im[l], c_re[l], c_im[l])
        u_seg = segment_major(proj[:, 4 * wa:4 * wa + wb])
        yd, xs = s5_fwd(u_seg, *mats, side=nxt[1], name="s5_fwd")
        if nxt[0] is not None:
            full.append(assemble(nxt[0].result, nxt[1].result, nxt[2].result))
        ge_ins = [rows3(yd, 0, t, wb, False), rows3(yd, 1, t, wb, False), rows(u_seg, t), bcast(d_skip[l][None])]
        (ys_seg,) = rw_fwd(fn_gelu, ge_ins, [(wb, False)], nrow=nrow, t=t, name="gelu_fwd")
        ys = time_major(ys_seg)
        glu = mm(ys, fw["w_glu"], name="glu_fwd")
        gl_ins = [rows(ys, t), rows(glu, t), rows(proj, t, wb, (4 * wa + wb) // wb), bcast(b_glu[l][None])]
        (yb_in,) = rw_fwd(fn_glu, gl_ins, [(wb, False)], nrow=nrow, t=t, name="glugate_fwd")
        y_a = mm(ya_in, fw["w_pa"], name="pa_fwd")
        y_b = mm(yb_in, fw["w_pb"], name="pb_fwd")
        bg = b_gate[l][None]
        mg_ins = [rows(proj, t, dm, 3), rows(proj, t, dm, 4), rows(y_a, t), rows(y_b, t),
                  bcast(bg, dm, 0), bcast(bg, dm, 1)]
        (merged,) = rw_fwd(fn_merge, mg_ins, [(dm, False)], nrow=nrow, t=t, name="merge_fwd")
        nxt = mm(merged, fw["w_out"], add=cur, name="out_fwd")
        saved.append(dict(x=cur, h=hh, proj=proj, qkv=qkv, o_dir=o_dir, s_hist=s_hist, t_hist=t_hist, hn_ins=hn_ins,
                          mats=mats, mats_vjp=mats_vjp, xs=xs, ge_ins=ge_ins, ys=ys, gl_ins=gl_ins, u_seg=u_seg,
                          ya_in=ya_in, yb_in=yb_in, mg_ins=mg_ins, merged=merged,
                          alog_row=alog_row, dtb_row=dtb_row))
        cur = nxt

    loss_ins = [rows(cur, t), rows(tgt, t), bcast(final_g[None])]
    (row_loss,) = rw_fwd(fn_loss, loss_ins, [(1, False)], nrow=nrow, t=t, name="loss_fwd")
    ones = jnp.ones((nrow, 1), F32)
    dcur, dfinal = rw_bwd(fn_loss, loss_ins, [[rows(ones, t)]], nrow=nrow, t=t,
                          row_grads=[(0, False)], bc_grads=[2], name="loss_bwd")
    loss = lax.psum(jnp.sum(row_loss), ("x", "y", "c"))

    coord = {"x": cx, "y": cy, "c": cc}
    routes = {"cxy": ("c", "x", "y"), "cyx": ("c", "y", "x")}
    route = ["cxy", "cyx", "cxy", "cxy", "cyx", "cyx", "cxy"]
    conv_order = [4 * ((q // 2) % 2) + 2 * (q % 2) + q // 4 for q in range(8)]
    cw = 3 * wa // 8
    g_final = {n: [None] * depth for n in SHARDED}
    gsh = {n: [None] * depth for n in SHARDED}

    def own_columns(z):
        un = jnp.concatenate([z[:, :o_ba], z[:, n_main:n_main + 4 * nh], z[:, o_ba:n_main]], axis=1)
        return jnp.stack([un[:, d * pw:(d + 1) * pw] for d in range(8)])

    def rs_begin(l):
        in_a, in_b = gsh["w_in"][l]
        conv_buf = jnp.stack([gsh["conv_w"][l][:, d * cw:(d + 1) * cw] for d in conv_order])
        return dict(layer=l, stage=0, bufs=[
            in_a.reshape(8, dm // 16, projp), in_b.reshape(8, dm // 16, projp), gsh["w_out"][l],
            gsh["w_glu"][l], gsh["w_pa"][l], gsh["w_pb"][l], conv_buf])

    def rs_side(st):
        if st is None:
            return None
        if st["stage"] < 3:
            st["axes"] = [routes[r][st["stage"]] for r in route]

            def halves(b):
                return b.reshape((2, b.shape[0] // 2) + b.shape[1:])

            st["bufs"] = [halves(b) for b in st["bufs"]]
            send = [halves(b) for b in st["tx"]] if st.get("tx") else st["bufs"]
            if st.get("whole") is not None:
                return exchange(send + st["whole"],
                                st["axes"] + [routes["cxy"][st["stage"]], routes["cyx"][st["stage"]]],
                                half=[True] * len(send) + [False, False])
            return exchange(send, st["axes"], half=True)
        st["bufs"] = [b.reshape(b.shape[1:]) for b in st["bufs"]]
        return all_to_all([own_columns(st["bufs"][0]), own_columns(st["bufs"][1])],
                          [lambda x, y, c: 4 * c + 2 * x + y, lambda x, y, c: 4 * c + 2 * y + x])

    def rs_absorb(st, side):
        if st is None:
            return
        if st["stage"] < 3:
            if st.get("whole") is not None:
                st["whole"] = [_elementwise(make_fn_sum(2), [w, r], 1, "ar_add")[0]
                               for w, r in zip(st["whole"], side.result[-2:])]
            narrow = st["stage"] < 2
            sums = [add_half(b, r, coord[ax].astype(jnp.int32).reshape(1), narrow=narrow, name=f"rs_add_{st['stage']}")
                    for b, r, ax in zip(st["bufs"], side.result, st["axes"])]
            st["bufs"] = [s[0] for s in sums]
            st["tx"] = [s[1] for s in sums] if narrow else None
        else:
            got_a, got_b = side.result
            g_final["w_in"][st["layer"]] = jnp.concatenate(
                [got_a.reshape(dm // 2, pw), got_b.reshape(dm // 2, pw)], axis=0)
            for n, b in zip(["w_out", "w_glu", "w_pa", "w_pb", "conv_w"], st["bufs"][2:]):
                g_final[n][st["layer"]] = b
        st["stage"] += 1

    grep = {n: [None] * depth for n in REPLICATED if n != "final_g"}
    pending = None
    for l in reversed(range(depth)):
        fw, sv = full[l], saved[l]
        dmerged = mm(dcur, fw["w_out"], tb=True, name="out_bwd_x")
        gsh["w_out"][l] = mm(sv["merged"], dcur, ta=True, scatter=("rows", slot_cxy), name="out_bwd_w")
        side = rs_side(pending)
        dla, dlb, dya, dyb, dbga, dbgb = rw_bwd(
            fn_merge, sv["mg_ins"], [[rows(dmerged, t)]], nrow=nrow, t=t,
            row_grads=[(0, False), (1, False), (2, False), (3, False)], bc_grads=[4, 5], side=side, name="merge_bwd")
        rs_absorb(pending, side)
        grep["b_gate"][l] = jnp.concatenate([dbga.reshape(dm), dbgb.reshape(dm)])
        dya_in = mm(dya, fw["w_pa"], tb=True, name="pa_bwd_x")
        gsh["w_pa"][l] = mm(sv["ya_in"], dya, ta=True, scatter=("cols", slot_cyx), name="pa_bwd_w")
        dyb_in = mm(dyb, fw["w_pb"], tb=True, name="pb_bwd_x")
        gsh["w_pb"][l] = mm(sv["yb_in"], dyb, ta=True, scatter=("cols", slot_cyx), name="pb_bwd_w")
        dys1, dglu, dzb, dbglu = rw_bwd(
            fn_glu, sv["gl_ins"], [[rows(dyb_in, t)]], nrow=nrow, t=t,
            row_grads=[(0, False), (1, False), (2, False)], bc_grads=[3], name="glugate_bwd")
        grep["b_glu"][l] = dbglu.reshape(wb)
        dys2 = mm(dglu, fw["w_glu"], tb=True, name="glu_bwd_x")
        gsh["w_glu"][l] = mm(sv["ys"], dglu, ta=True, scatter=("rows", slot_cxy), name="glu_bwd_w")
        dyd, du1, ddskip = rw_bwd(
            fn_gelu, sv["ge_ins"], [[rows(segment_major(dys1), t), rows(segment_major(dys2), t)]], nrow=nrow, t=t,
            row_grads=[(0, False), (2, False)], bc_grads=[3], name="gelu_bwd")
        grep["d_skip"][l] = ddskip.reshape(wb)
        side = rs_side(pending)
        du_seg, dw_s5, dcm_s5, dar, dai = s5_bwd(sv["u_seg"], *sv["mats"], sv["xs"], dyd, du1, side=side,
                                                 name="s5_bwd")
        du = time_major(du_seg)
        rs_absorb(pending, side)
        g_lr, g_li, g_ldt, g_br, g_bi, g_cr, g_ci = sv["mats_vjp"]((dar, dai, dw_s5, dcm_s5))
        for nme, val in zip(["lam_re", "lam_im", "log_dt", "b_re", "b_im", "c_re", "c_im"],
                            [g_lr, g_li, g_ldt, g_br, g_bi, g_cr, g_ci]):
            grep[nme][l] = val
        do, dza, dhn = rw_bwd(fn_headnorm, sv["hn_ins"], [[rows(dya_in, t, HEAD_DIM, 0, True)]], nrow=nrow, t=t,
                              ncol=nh, row_grads=[(0, True), (2, True)], bc_grads=[3], name="headnorm_bwd")
        grep["head_norm_g"][l] = jnp.sum(dhn, axis=0).reshape(HEAD_DIM)
        side = rs_side(pending)
        dqkv, dba_all, dal, ddt = gdn_bwd(sv["qkv"], sv["proj"], ba_blk, sv["alog_row"], sv["dtb_row"],
                                          sv["s_hist"], sv["t_hist"], do, n_heads=nh, side=side, name="gdn_bwd")
        rs_absorb(pending, side)
        grep["a_log"][l] = jnp.sum(dal, axis=(0, 1))[2 * nh:4 * nh].reshape(2, nh)
        grep["dt_bias"][l] = jnp.sum(ddt, axis=(0, 1))[2 * nh:4 * nh].reshape(2, nh)
        (dba,) = rw_fwd(make_fn_sum(2), [rows3(dba_all, p, t, LANES, False) for p in range(2)],
                        [(LANES, False)], nrow=nrow, t=t, name="dba_sum")
        prep_ins = [rows(sv["proj"], nrow, HEAD_DIM, 0, True), bcast(fw["conv_w"], HEAD_DIM, 0, True)]
        dq_cots = [(dqkv.reshape(2, 3 * nrow, wa), (None, nrow, HEAD_DIM),
                    (lambda j, i, dd=dd: (dd, j // nh, j % nh))) for dd in range(2)]
        dqkv_raw, dconv = rw_bwd(make_fn_prep(nh), prep_ins, [dq_cots], nrow=nrow, t=nrow, ncol=3 * nh,
                                 row_grads=[(0, True)], bc_grads=[1], name="prep_bwd")
        gsh["conv_w"][l] = jnp.transpose(dconv, (1, 0, 2)).reshape(CONV_K, 3 * wa)
        dproj = jnp.concatenate([dqkv_raw, dza, du, dzb, dla, dlb, dba,
                                 jnp.zeros((nrow, projp - n_main - LANES), F32)], axis=1).astype(MXU_DT)
        side = rs_side(pending)
        dh = mm(dproj, fw["w_in"], tb=True, side=side, name="proj_bwd_x")
        rs_absorb(pending, side)
        gsh["w_in"][l] = [mm(sv["h"], dproj, ta=True, m_part=(part, 2), name="proj_bwd_w") for part in range(2)]
        dcur, dlng = rw_bwd(fn_rms, [rows(sv["x"], t), bcast(ln_g[l][None])], [[rows(dh, t)]], nrow=nrow, t=t,
                            row_grads=[(0, False)], bc_grads=[1], residual=rows(dcur, t), name="rms_bwd")
        grep["ln_g"][l] = dlng.reshape(dm)
        pending = rs_begin(l)
    grad_x = dcur[None]
    rep_list = [jnp.stack(grep[n]) for n in REPLICATED if n != "final_g"] + [dfinal.reshape(dm)]
    rep_shapes = [a.shape for a in rep_list]
    rep_buf = _pack(rep_list, COMM_COLS, 2 * ROW_TILE)
    pending["whole"] = [rep_buf[:rep_buf.shape[0] // 2], rep_buf[rep_buf.shape[0] // 2:]]
    for stage in range(RS_STAGES):
        side = rs_side(pending)
        run_side(side, name=f"rs_stage_{stage}")
        rs_absorb(pending, side)
    grads = {n: jnp.stack(g_final[n]) for n in SHARDED}
    for n, val in zip(REPLICATED, _unpack(jnp.concatenate(pending["whole"], axis=0), rep_shapes)):
        grads[n] = val

    deltas, new_m, new_v = {}, {}, {}
    for n in WEIGHTS:
        shp = wts[n].shape
        two = [a.reshape(-1, shp[-1]) for a in (wts[n], grads[n], mom_m[n], mom_v[n])]
        d_, m_, v_ = _elementwise(fn_adamw, two, 3, "adamw_" + n)
        deltas[n], new_m[n], new_v[n] = d_.reshape(shp), m_.reshape(shp), v_.reshape(shp)

    return (loss, grad_x, *[grads[n] for n in WEIGHTS], *[deltas[n] for n in WEIGHTS],
            *[new_m[n] for n in WEIGHTS], *[new_v[n] for n in WEIGHTS])
```

```python
import functools
import math

import jax
import jax.numpy as jnp
from jax import lax
from jax.experimental import pallas as pl
from jax.experimental.pallas import tpu as pltpu

F32 = jnp.float32
MXU_DT = jnp.bfloat16
WIRE_DT = jnp.bfloat16

HEAD_DIM = 128
CHUNK = 64
CONV_K = 5
S5_GROUP_CH = 16
S5_STATE = 64
S5_BLOCK_GROUPS = 8
S5_BLOCK_STATE = S5_BLOCK_GROUPS * S5_STATE
RMS_EPS = 1e-6
LANES = 128
VMEM_LIMIT = 56 * 1024 * 1024
ROW_TILE = 256
COMM_COLS = 1024
ADD_BLOCK_BYTES = 2 * 1024 * 1024
RS_STAGES = 4
ELEMENTWISE_BLOCK_BYTES = 1024 * 1024

ADAM_LR = 0.001
ADAM_B1 = 0.9
ADAM_B2 = 0.999
ADAM_EPS = 1e-08
ADAM_WD = 0.01
ADAM_STEP = 10

WEIGHTS = ['ln_g', 'w_in', 'conv_w', 'a_log', 'dt_bias', 'head_norm_g', 'lam_re', 'lam_im', 'log_dt',
           'b_re', 'b_im', 'c_re', 'c_im', 'd_skip', 'w_glu', 'b_glu', 'w_pa', 'w_pb', 'b_gate',
           'w_out', 'final_g']
SHARDED = ['w_in', 'w_glu', 'w_pa', 'w_pb', 'w_out', 'conv_w']
REPLICATED = [n for n in WEIGHTS if n not in SHARDED]
MESH = pl.DeviceIdType.MESH


def _params(sem=None):
    return pltpu.CompilerParams(dimension_semantics=sem, vmem_limit_bytes=VMEM_LIMIT)


def _tile(n, cap, q=LANES):
    t = (min(n, cap) // q) * q
    while t > q and n % t:
        t -= q
    return t if t > 0 and n % t == 0 else n


class Side:
    def __init__(self, ins, out_sd, sems, start, finish):
        self.ins, self.out_sd, self.sems, self.start, self.finish = list(ins), list(out_sd), list(sems), start, finish
        self.result = None


def _call(body, *, grid, in_specs, out_specs, out_shape, scratch_shapes=(), sem, name, args, side=None):
    in_specs, out_specs, out_shape = list(in_specs), list(out_specs), list(out_shape)
    scratch_shapes = list(scratch_shapes)
    if side is None:
        return pl.pallas_call(body, grid=grid, in_specs=in_specs, out_specs=out_specs, out_shape=out_shape,
                              scratch_shapes=scratch_shapes, compiler_params=_params(sem), name=name)(*args)
    hbm = pl.BlockSpec(memory_space=pl.ANY)
    n_in, n_out, n_scr = len(in_specs), len(out_specs), len(scratch_shapes)
    s_in, s_out = len(side.ins), len(side.out_sd)

    def hosted(*refs):
        main_in, rest = refs[:n_in], refs[n_in:]
        side_in, rest = rest[:s_in], rest[s_in:]
        main_out, rest = rest[:n_out], rest[n_out:]
        side_out, rest = rest[:s_out], rest[s_out:]
        main_scr, sems = rest[:n_scr], rest[n_scr:]
        ids = [pl.program_id(k) for k in range(len(grid))]
        first = functools.reduce(jnp.logical_and, [i == 0 for i in ids])
        last = functools.reduce(jnp.logical_and, [i == g - 1 for i, g in zip(ids, grid)])

        @pl.when(first)
        def _():
            side.start(side_in, side_out, sems)

        body(*main_in, *main_out, *main_scr)

        @pl.when(last)
        def _():
            side.finish(side_in, side_out, sems)

    outs = pl.pallas_call(
        hosted, grid=grid, in_specs=in_specs + [hbm] * s_in, out_specs=out_specs + [hbm] * s_out,
        out_shape=out_shape + side.out_sd, scratch_shapes=scratch_shapes + side.sems,
        compiler_params=_params(("arbitrary",) * len(grid)), name=name)(*args, *side.ins)
    side.result = list(outs[n_out:])
    return list(outs[:n_out])


def run_side(side, *, name):
    hbm = pl.BlockSpec(memory_space=pl.ANY)
    s_in, s_out = len(side.ins), len(side.out_sd)

    def body(*refs):
        side_in, side_out, sems = refs[:s_in], refs[s_in:s_in + s_out], refs[s_in + s_out:]
        side.start(side_in, side_out, sems)
        side.finish(side_in, side_out, sems)

    side.result = list(pl.pallas_call(body, out_shape=side.out_sd, in_specs=[hbm] * s_in, out_specs=[hbm] * s_out,
                                      scratch_shapes=side.sems, name=name)(*side.ins))
    return side.result


def slot_cxy(d):
    return 4 * (d % 2) + 2 * (d // 4) + (d // 2) % 2


def slot_cyx(d):
    return 4 * (d % 2) + 2 * ((d // 2) % 2) + d // 4


def mm(a, b, *, ta=False, tb=False, add=None, m_part=None, scatter=None, side=None, name):
    m = a.shape[1] if ta else a.shape[0]
    k = a.shape[0] if ta else a.shape[1]
    n = b.shape[0] if tb else b.shape[1]
    m_off = 0
    if m_part is not None:
        m = m // m_part[1]
        m_off = m_part[0]
    tm, tn, tk = _tile(m, 1024), _tile(n, 512), _tile(k, 2048)
    if scatter is not None and scatter[0] == "rows":
        tm = m // 8
    if scatter is not None and scatter[0] == "cols":
        tn = n // 8
    if m_part is not None:
        assert tm == m
    nk = k // tk
    dn = (((0 if ta else 1,), (1 if tb else 0,)), ((), ()))

    def body(*refs):
        if add is None:
            a_ref, b_ref, o_ref, acc = refs
        else:
            a_ref, b_ref, add_ref, o_ref, acc = refs
        kk = pl.program_id(2)

        @pl.when(kk == 0)
        def _():
            acc[...] = jnp.zeros_like(acc)

        acc[...] += lax.dot_general(a_ref[...].astype(MXU_DT), b_ref[...].astype(MXU_DT), dn,
                                    preferred_element_type=F32)

        @pl.when(kk == nk - 1)
        def _():
            r = acc[...]
            if add is not None:
                r = r + add_ref[...]
            o_ref[...] = r

    if ta:
        a_spec = pl.BlockSpec((tk, tm), lambda i, j, kk: (kk, i + m_off))
    else:
        a_spec = pl.BlockSpec((tm, tk), lambda i, j, kk: (i + m_off, kk))
    b_spec = pl.BlockSpec((tn, tk), lambda i, j, kk: (j, kk)) if tb else pl.BlockSpec((tk, tn), lambda i, j, kk: (kk, j))
    out_sd = jax.ShapeDtypeStruct((m, n), F32)
    if scatter is None:
        o_spec = pl.BlockSpec((tm, tn), lambda i, j, kk: (i, j))
    elif scatter[0] == "rows":
        o_spec = pl.BlockSpec((None, tm, tn), lambda i, j, kk: (scatter[1](i), 0, j))
        out_sd = jax.ShapeDtypeStruct((8, tm, n), F32)
    else:
        o_spec = pl.BlockSpec((None, tm, tn), lambda i, j, kk: (scatter[1](j), i, 0))
        out_sd = jax.ShapeDtypeStruct((8, m, tn), F32)
    ins, specs = [a, b], [a_spec, b_spec]
    if add is not None:
        ins.append(add)
        specs.append(o_spec)
    return _call(body, grid=(m // tm, n // tn, nk), in_specs=specs, out_specs=[o_spec], out_shape=[out_sd],
                 scratch_shapes=[pltpu.VMEM((tm, tn), F32)], sem=("parallel", "parallel", "arbitrary"),
                 name=name, args=ins, side=side)[0]


def rows(arr, t, width=None, base=0, per_j=False):
    width = arr.shape[1] if width is None else width
    return (arr, (t, width), lambda j, i: (i, base + (j if per_j else 0)))


def rows3(arr, lead, t, width, per_j=True):
    return (arr, (None, t, width), lambda j, i: (lead, i, j if per_j else 0))


def bcast(arr, width=None, base=0, per_j=False):
    width = arr.shape[1] if width is None else width
    return (arr, (arr.shape[0], width), lambda j, i: (0, base + (j if per_j else 0)))


def _specs(items):
    return [pl.BlockSpec(bs, im) for (_, bs, im) in items]


def rw_fwd(fn, ins, outs, *, nrow, t, ncol=1, name):
    out_specs = [pl.BlockSpec((t, w), (lambda j, i: (i, j)) if pj else (lambda j, i: (i, 0))) for (w, pj) in outs]
    out_shape = [jax.ShapeDtypeStruct((nrow, w * (ncol if pj else 1)), F32) for (w, pj) in outs]
    nin = len(ins)

    def body(*refs):
        j = pl.program_id(0)
        res = fn(j, *[r[...] for r in refs[:nin]])
        for o_ref, r in zip(refs[nin:], res):
            o_ref[...] = r

    return pl.pallas_call(
        body, grid=(ncol, nrow // t), in_specs=_specs(ins), out_specs=out_specs, out_shape=out_shape,
        compiler_params=_params(("parallel", "parallel")), name=name)(*[x[0] for x in ins])


def rw_bwd(fn, ins, cots, *, nrow, t, ncol=1, row_grads, bc_grads, residual=None, side=None, name):
    nin = len(ins)
    flat_cots = [c for group in cots for c in group]
    extra = [residual] if residual is not None else []
    out_specs, out_shape = [], []
    for idx, pj in row_grads:
        w = ins[idx][1][-1]
        out_specs.append(pl.BlockSpec((t, w), (lambda j, i: (i, j)) if pj else (lambda j, i: (i, 0))))
        out_shape.append(jax.ShapeDtypeStruct((nrow, w * (ncol if pj else 1)), F32))
    for idx in bc_grads:
        r, w = ins[idx][1]
        out_specs.append(pl.BlockSpec((None, r, w), lambda j, i: (j, 0, 0)))
        out_shape.append(jax.ShapeDtypeStruct((ncol, r, w), F32))

    def body(*refs):
        j = pl.program_id(0)
        i = pl.program_id(1)
        vals = [r[...] for r in refs[:nin]]
        pos = nin
        cts = []
        for group in cots:
            c = refs[pos][...]
            for q in range(1, len(group)):
                c = c + refs[pos + q][...]
            pos += len(group)
            cts.append(c)
        res_ref = refs[pos] if residual is not None else None
        pos += len(extra)
        outs = refs[pos:]
        _, vjp = jax.vjp(lambda *a: tuple(fn(j, *a)), *vals)
        grads = vjp(tuple(cts))
        for q, (idx, _) in enumerate(row_grads):
            g = grads[idx]
            if q == 0 and res_ref is not None:
                g = g + res_ref[...]
            outs[q][...] = g
        for q, idx in enumerate(bc_grads):
            o_ref = outs[len(row_grads) + q]

            @pl.when(i == 0)
            def _(o_ref=o_ref):
                o_ref[...] = jnp.zeros_like(o_ref)

            o_ref[...] += grads[idx]

    all_in = list(ins) + flat_cots + extra
    return _call(body, grid=(ncol, nrow // t), in_specs=_specs(all_in), out_specs=out_specs, out_shape=out_shape,
                 sem=("parallel", "arbitrary"), name=name, args=[x[0] for x in all_in], side=side)


def _silu(x):
    return x * jax.nn.sigmoid(x)


@jax.custom_vjp
def _softplus(x):
    return jnp.maximum(x, 0.0) + jnp.log1p(jnp.exp(-jnp.abs(x)))


def _softplus_fwd(x):
    return _softplus(x), x


def _softplus_bwd(x, ct):
    return (ct * jax.nn.sigmoid(x),)


_softplus.defvjp(_softplus_fwd, _softplus_bwd)


def _gelu(x):
    return 0.5 * x * (1.0 + jnp.tanh(math.sqrt(2.0 / math.pi) * (x + 0.044715 * (x * x * x))))


def _row_shift_impl(x, s):
    n = x.shape[0]
    if s == 0:
        return x
    rolled = pltpu.roll(x, (-s) % n, 0)
    t = lax.broadcasted_iota(jnp.int32, x.shape, 0)
    ok = (t + s >= 0) & (t + s < n)
    return jnp.where(ok, rolled, 0.0)


@functools.partial(jax.custom_vjp, nondiff_argnums=(1,))
def _row_shift(x, s):
    return _row_shift_impl(x, s)


def _row_shift_fwd(x, s):
    return _row_shift_impl(x, s), None


def _row_shift_bwd(s, _, ct):
    return (_row_shift_impl(ct, -s),)


_row_shift.defvjp(_row_shift_fwd, _row_shift_bwd)


def fn_rms(j, x, g):
    return (x * lax.rsqrt(jnp.mean(x * x, axis=-1, keepdims=True) + RMS_EPS) * g,)


def make_fn_prep(n_heads):
    pad = (CONV_K - 1) // 2

    def fn_prep(j, x, w):
        y = _row_shift(x, -pad) * w[0:1, :]
        for i in range(1, CONV_K):
            y = y + _row_shift(x, i - pad) * w[i:i + 1, :]
        a = _silu(y)
        scale = jnp.where(j < n_heads, HEAD_DIM ** -0.5, 1.0).astype(F32)
        nrm = a * lax.rsqrt(jnp.sum(a * a, axis=-1, keepdims=True) + RMS_EPS) * scale
        return (jnp.where(j < 2 * n_heads, nrm, a),)

    return fn_prep


def fn_headnorm(j, o0, o1, z, g):
    o = o0 + o1
    n = o * lax.rsqrt(jnp.mean(o * o, axis=-1, keepdims=True) + RMS_EPS) * g
    return (n * _silu(z),)


def fn_gelu(j, y0, y1, u, dsk):
    return (_gelu(y0 + y1 + u * dsk),)


def fn_glu(j, ys, logit, z, b):
    return (ys * jax.nn.sigmoid(logit + b) * _silu(z),)


def fn_merge(j, la, lb, ya, yb, ba, bb):
    return (jax.nn.sigmoid(la + ba) * ya + jax.nn.sigmoid(lb + bb) * yb,)


def fn_loss(j, x, t, g):
    y = x * lax.rsqrt(jnp.mean(x * x, axis=-1, keepdims=True) + RMS_EPS) * g
    e = y - t
    return (0.5 * jnp.mean(e * e, axis=-1, keepdims=True),)


def make_fn_sum(n):
    def fn_sum(j, *xs):
        s = xs[0]
        for q in range(1, n):
            s = s + xs[q]
        return (s,)

    return fn_sum


def fn_adamw(j, w, g, m, v):
    m2 = ADAM_B1 * m + (1.0 - ADAM_B1) * g
    v2 = ADAM_B2 * v + (1.0 - ADAM_B2) * (g * g)
    m_hat = m2 / (1.0 - ADAM_B1 ** ADAM_STEP)
    v_hat = v2 / (1.0 - ADAM_B2 ** ADAM_STEP)
    delta = -ADAM_LR * (m_hat / (jnp.sqrt(v_hat) + ADAM_EPS) + ADAM_WD * w)
    return delta, m2, v2


def _dot(a, b, dims, precision=None):
    return lax.dot_general(a, b, (dims, ((), ())), precision=precision, preferred_element_type=F32)


_NN = ((1,), (0,))
_NT = ((1,), (1,))
_TN = ((0,), (0,))
GDN_HEAD_BLOCK = 4
INVERSE_WIDE_FACTORS = 0


def _split(a):
    hi = a.astype(MXU_DT)
    return hi, (a - hi.astype(F32)).astype(MXU_DT)


def _dot3s(a, b, dims):
    (ah, al), (bh, bl) = a, b
    return _dot(ah, bh, dims) + (_dot(al, bh, dims) + _dot(ah, bl, dims))


def _dot3(a, b, dims):
    return _dot3s(_split(a), _split(b), dims)


def _unit_inverse(lmat):
    r = lmat.shape[0]
    eye = (lax.broadcasted_iota(jnp.int32, (r, r), 0) == lax.broadcasted_iota(jnp.int32, (r, r), 1)).astype(F32)
    pw = -lmat
    tinv = eye + pw
    pws = _split(pw)
    n_fact = int(math.ceil(math.log2(CHUNK))) - 1
    for k in range(n_fact):
        if k < INVERSE_WIDE_FACTORS:
            pws = _split(_dot3s(pws, pws, _NN))
            tinv = tinv + _dot3s(_split(tinv), pws, _NN)
        else:
            pw = _dot(pws[0], pws[0], _NN)
            pws = (pw.astype(MXU_DT), None)
            tinv = tinv + _dot(tinv.astype(MXU_DT), pws[0], _NN)
    return tinv


@jax.custom_vjp
def tri_apply(lmat, rhs, tinv):
    return _dot(tinv, rhs, _NN)


def _tri_apply_fwd(lmat, rhs, tinv):
    x = _dot(tinv, rhs, _NN)
    return x, (tinv, x)


def _tri_apply_bwd(res, dx):
    tinv, x = res
    drhs = _dot(tinv, dx, _TN)
    return -_dot(drhs, x, _NT), drhs, jnp.zeros_like(tinv)


tri_apply.defvjp(_tri_apply_fwd, _tri_apply_bwd)


def gdn_group(s, q, k, v, ba, alog, dtb, *, d, head0, n_heads, tinv=None):
    hb = len(s)
    c = q.shape[0]
    r = hb * c

    def stack(x):
        return jnp.concatenate([x[:, i * HEAD_DIM:(i + 1) * HEAD_DIM] for i in range(hb)], axis=0)

    def pick(x, lane0):
        lane = lax.broadcasted_iota(jnp.int32, x.shape, 1)
        return jnp.concatenate(
            [jnp.sum(jnp.where(lane == lane0 + i, x, 0.0), axis=1, keepdims=True) for i in range(hb)], axis=0)

    q4, k4, v4 = stack(q), stack(k), stack(v)
    beta = pick(jax.nn.sigmoid(ba), d * n_heads + head0)
    g = pick(-jnp.exp(alog) * _softplus(ba + dtb), 2 * n_heads + d * n_heads + head0)
    ii = lax.broadcasted_iota(jnp.int32, (r, r), 0)
    jj = lax.broadcasted_iota(jnp.int32, (r, r), 1)
    same = (ii // c) == (jj // c)
    rel = (ii - jj) * (1 - 2 * d)
    incl = same & (rel >= 0)
    strict = same & (rel > 0)
    incl_t = same & (rel <= 0)
    g_row = jnp.sum(jnp.where(ii == jj, g, 0.0), axis=0, keepdims=True)
    gc_col = jnp.sum(jnp.where(incl, g_row, 0.0), axis=1, keepdims=True)
    gc_row = jnp.sum(jnp.where(incl_t, g, 0.0), axis=0, keepdims=True)
    g_tot = jnp.sum(jnp.where(same, g_row, 0.0), axis=1, keepdims=True)
    decay = jnp.exp(jnp.where(incl, gc_col - gc_row, -1e30))
    kb = k4 * beta
    vb = v4 * beta
    lmat = jnp.where(strict, _dot(kb, k4, _NT) * decay, 0.0)
    fresh = tinv is None
    if fresh:
        tinv = _unit_inverse(lmat)
    uw = tri_apply(lmat, jnp.concatenate([vb, kb * jnp.exp(gc_col)], axis=1), tinv)
    u, w = uw[:, :HEAD_DIM], uw[:, HEAD_DIM:]
    qk = _dot(q4, k4, _NT) * decay
    qe = q4 * jnp.exp(gc_col)
    kd = k4 * jnp.exp(g_tot - gc_col)
    eg = jnp.exp(g_tot)
    v_new, o_s = [], []
    for i in range(hb):
        rs = slice(i * c, (i + 1) * c)
        ws = _dot(jnp.concatenate([w[rs], qe[rs]], axis=0), s[i], _NN)
        v_new.append(u[rs] - ws[:c])
        o_s.append(ws[c:])
    o4 = jnp.concatenate(o_s, axis=0) + _dot(qk, jnp.concatenate(v_new, axis=0), _NN)
    s_new = tuple(s[i] * eg[i * c:i * c + 1, :] + _dot(kd[i * c:(i + 1) * c], v_new[i], _TN) for i in range(hb))
    o = jnp.concatenate([o4[i * c:(i + 1) * c] for i in range(hb)], axis=1)
    return (s_new, o, tinv) if fresh else (s_new, o)


def _gdn_maps(n_chunks):
    def chunk_of(d, step):
        return step + d * (n_chunks - 1 - 2 * step)
    return chunk_of


def _heads(ref, h0, n):
    return ref[:, h0 * HEAD_DIM:(h0 + n) * HEAD_DIM]


def gdn_fwd(qkv, proj, ba_blk, alog_row, dtb_row, *, n_heads, side=None, name):
    nrow = qkv.shape[0]
    nc = nrow // CHUNK
    h = n_heads
    wa = h * HEAD_DIM
    hb = math.gcd(h, GDN_HEAD_BLOCK)
    chunk_of = _gdn_maps(nc)

    def body(q_ref, k_ref, v_ref, ba_ref, al_ref, dt_ref, o_ref, sh_ref, ti_ref, s_scr):
        d = pl.program_id(0)
        n = pl.program_id(1)

        @pl.when(n == 0)
        def _():
            s_scr[...] = jnp.zeros_like(s_scr)

        ba, al, dt = ba_ref[...], al_ref[...], dt_ref[...]
        for h0 in range(0, h, hb):
            s = tuple(s_scr[h0 + i] for i in range(hb))
            for i in range(hb):
                sh_ref[h0 + i] = s[i]
            s2, o, tinv = gdn_group(s, _heads(q_ref, h0, hb), _heads(k_ref, h0, hb), _heads(v_ref, h0, hb),
                                    ba, al, dt, d=d, head0=h0, n_heads=h)
            o_ref[:, h0 * HEAD_DIM:(h0 + hb) * HEAD_DIM] = o
            ti_ref[h0 // hb] = tinv
            for i in range(hb):
                s_scr[h0 + i] = s2[i]

    blk = (CHUNK, wa)
    in_specs = [
        pl.BlockSpec(blk, lambda d, n: (chunk_of(d, n), 0)),
        pl.BlockSpec(blk, lambda d, n: (chunk_of(d, n), 1)),
        pl.BlockSpec(blk, lambda d, n: (chunk_of(d, n), 2)),
        pl.BlockSpec((CHUNK, LANES), lambda d, n: (chunk_of(d, n), ba_blk)),
        pl.BlockSpec((1, LANES), lambda d, n: (0, 0)),
        pl.BlockSpec((1, LANES), lambda d, n: (0, 0)),
    ]
    ng, r = h // hb, hb * CHUNK
    out_specs = [
        pl.BlockSpec((None, CHUNK, wa), lambda d, n: (d, chunk_of(d, n), 0)),
        pl.BlockSpec((h, None, HEAD_DIM, HEAD_DIM), lambda d, n: (d, n, 0, 0)),
        pl.BlockSpec((None, None, ng, r, r), lambda d, n: (d, n, 0, 0, 0)),
    ]
    out_shape = [jax.ShapeDtypeStruct((2, nrow, wa), F32),
                 jax.ShapeDtypeStruct((2 * h, nc, HEAD_DIM, HEAD_DIM), F32),
                 jax.ShapeDtypeStruct((2, nc, ng, r, r), F32)]
    return _call(body, grid=(2, nc), in_specs=in_specs, out_specs=out_specs, out_shape=out_shape,
                 scratch_shapes=[pltpu.VMEM((h, HEAD_DIM, HEAD_DIM), F32)], sem=("parallel", "arbitrary"),
                 name=name, args=(qkv, qkv, qkv, proj, alog_row, dtb_row), side=side)


def gdn_bwd(qkv, proj, ba_blk, alog_row, dtb_row, s_hist, t_hist, do, *, n_heads, side=None, name):
    nrow = qkv.shape[0]
    nc = nrow // CHUNK
    h = n_heads
    wa = h * HEAD_DIM
    hb = math.gcd(h, GDN_HEAD_BLOCK)
    chunk_of = _gdn_maps(nc)

    def cb(d, n):
        return chunk_of(d, nc - 1 - n)

    def body(q_ref, k_ref, v_ref, ba_ref, al_ref, dt_ref, sh_ref, do_ref, ti_ref,
             dqkv_ref, dba_ref, dal_ref, ddt_ref, ds_scr):
        d = pl.program_id(0)
        n = pl.program_id(1)

        @pl.when(n == 0)
        def _():
            ds_scr[...] = jnp.zeros_like(ds_scr)
            dal_ref[...] = jnp.zeros_like(dal_ref)
            ddt_ref[...] = jnp.zeros_like(ddt_ref)

        ba, al, dt = ba_ref[...], al_ref[...], dt_ref[...]
        dba_sum = jnp.zeros_like(ba)
        dal_sum = jnp.zeros_like(al)
        ddt_sum = jnp.zeros_like(dt)
        for h0 in range(0, h, hb):
            f = functools.partial(gdn_group, d=d, head0=h0, n_heads=h, tinv=ti_ref[h0 // hb])
            s = tuple(sh_ref[h0 + i] for i in range(hb))
            _, vjp = jax.vjp(f, s, _heads(q_ref, h0, hb), _heads(k_ref, h0, hb), _heads(v_ref, h0, hb), ba, al, dt)
            ds, dq, dk, dv, dba, dal, ddt = vjp((tuple(ds_scr[h0 + i] for i in range(hb)), _heads(do_ref, h0, hb)))
            for i in range(hb):
                ds_scr[h0 + i] = ds[i]
            cols = slice(h0 * HEAD_DIM, (h0 + hb) * HEAD_DIM)
            dqkv_ref[0, :, cols] = dq
            dqkv_ref[1, :, cols] = dk
            dqkv_ref[2, :, cols] = dv
            dba_sum = dba_sum + dba
            dal_sum = dal_sum + dal
            ddt_sum = ddt_sum + ddt
        dba_ref[...] = dba_sum
        dal_ref[...] += dal_sum
        ddt_ref[...] += ddt_sum

    blk = (CHUNK, wa)
    in_specs = [
        pl.BlockSpec(blk, lambda d, n: (cb(d, n), 0)),
        pl.BlockSpec(blk, lambda d, n: (cb(d, n), 1)),
        pl.BlockSpec(blk, lambda d, n: (cb(d, n), 2)),
        pl.BlockSpec((CHUNK, LANES), lambda d, n: (cb(d, n), ba_blk)),
        pl.BlockSpec((1, LANES), lambda d, n: (0, 0)),
        pl.BlockSpec((1, LANES), lambda d, n: (0, 0)),
        pl.BlockSpec((h, None, HEAD_DIM, HEAD_DIM), lambda d, n: (d, nc - 1 - n, 0, 0)),
        pl.BlockSpec(blk, lambda d, n: (cb(d, n), 0)),
        pl.BlockSpec((None, None, h // hb, hb * CHUNK, hb * CHUNK), lambda d, n: (d, nc - 1 - n, 0, 0, 0)),
    ]
    out_specs = [
        pl.BlockSpec((None, 3, CHUNK, wa), lambda d, n: (d, 0, cb(d, n), 0)),
        pl.BlockSpec((None, CHUNK, LANES), lambda d, n: (d, cb(d, n), 0)),
        pl.BlockSpec((None, 1, LANES), lambda d, n: (d, 0, 0)),
        pl.BlockSpec((None, 1, LANES), lambda d, n: (d, 0, 0)),
    ]
    out_shape = [jax.ShapeDtypeStruct((2, 3, nrow, wa), F32),
                 jax.ShapeDtypeStruct((2, nrow, LANES), F32),
                 jax.ShapeDtypeStruct((2, 1, LANES), F32),
                 jax.ShapeDtypeStruct((2, 1, LANES), F32)]
    return _call(body, grid=(2, nc), in_specs=in_specs, out_specs=out_specs, out_shape=out_shape,
                 scratch_shapes=[pltpu.VMEM((h, HEAD_DIM, HEAD_DIM), F32)], sem=("parallel", "arbitrary"),
                 name=name, args=(qkv, qkv, qkv, proj, alog_row, dtb_row, s_hist, do, t_hist), side=side)


def s5_mats(lam_re, lam_im, log_dt, b_re, b_im, c_re, c_im):
    g = lam_re.shape[1]
    nb = g // S5_BLOCK_GROUPS
    dt = jnp.exp(log_dt)[..., None]
    mag = jnp.exp(lam_re * dt)
    ar = mag * jnp.cos(lam_im * dt)
    ai = mag * jnp.sin(lam_im * dt)
    den = lam_re * lam_re + lam_im * lam_im
    fr = ((ar - 1.0) * lam_re + ai * lam_im) / den
    fi = (ai * lam_re - (ar - 1.0) * lam_im) / den
    bbr = fr[..., None] * b_re - fi[..., None] * b_im
    bbi = fr[..., None] * b_im + fi[..., None] * b_re
    eye = jnp.eye(S5_BLOCK_GROUPS, dtype=F32)
    shp = (2, nb, S5_BLOCK_GROUPS, S5_STATE, S5_GROUP_CH)
    w_r = jnp.einsum('dsjpc,jk->dsjckp', bbr.reshape(shp), eye).reshape(2, nb, LANES, S5_BLOCK_STATE)
    w_i = jnp.einsum('dsjpc,jk->dsjckp', bbi.reshape(shp), eye).reshape(2, nb, LANES, S5_BLOCK_STATE)
    w = jnp.concatenate([w_r, w_i], axis=-1)
    shc = (2, nb, S5_BLOCK_GROUPS, S5_GROUP_CH, S5_STATE)
    c_r = jnp.einsum('dsjcp,jk->dsjpkc', c_re.reshape(shc), eye).reshape(2, nb, S5_BLOCK_STATE, LANES)
    c_i = jnp.einsum('dsjcp,jk->dsjpkc', c_im.reshape(shc), eye).reshape(2, nb, S5_BLOCK_STATE, LANES)
    cm = jnp.concatenate([c_r, -c_i], axis=-2)
    return (ar.reshape(2, nb, 1, S5_BLOCK_STATE), ai.reshape(2, nb, 1, S5_BLOCK_STATE), w, cm)


_S5_ROWS = 512
S5_SEGMENTS = 8


def segment_major(a):
    n, c = a.shape
    return jnp.transpose(a.reshape(S5_SEGMENTS, n // S5_SEGMENTS, c), (1, 0, 2)).reshape(n, c)


def time_major(a):
    n, c = a.shape
    return jnp.transpose(a.reshape(n // S5_SEGMENTS, S5_SEGMENTS, c), (1, 0, 2)).reshape(n, c)


def _segmented_scan(x_ref, a_r, a_i, *, reverse, visit=None):
    nrow, two_hs = x_ref.shape
    hs = two_hs // 2
    steps = nrow // S5_SEGMENTS
    assert steps & (steps - 1) == 0
    b_r = jnp.broadcast_to(a_r, (S5_SEGMENTS, hs))
    b_i = jnp.broadcast_to(a_i, (S5_SEGMENTS, hs))
    rid = lax.broadcasted_iota(jnp.int32, (S5_SEGMENTS, hs), 0)

    def rows_of(i):
        return pl.ds(pl.multiple_of((steps - 1 - i if reverse else i) * S5_SEGMENTS, S5_SEGMENTS), S5_SEGMENTS)

    def local(i, carry):
        s_r, s_i = carry
        rows = rows_of(i)
        n_r = b_r * s_r - b_i * s_i + x_ref[rows, pl.ds(0, hs)]
        n_i = b_r * s_i + b_i * s_r + x_ref[rows, pl.ds(hs, hs)]
        x_ref[rows, pl.ds(0, hs)] = n_r
        x_ref[rows, pl.ds(hs, hs)] = n_i
        return n_r, n_i

    z = jnp.zeros((S5_SEGMENTS, hs), F32)
    e_r, e_i = lax.fori_loop(0, steps, local, (z, z))
    q_r, q_i = a_r, a_i
    for _ in range(steps.bit_length() - 1):
        q_r, q_i = q_r * q_r - q_i * q_i, 2.0 * q_r * q_i
    c_r = jnp.zeros((1, hs), F32)
    c_i = jnp.zeros((1, hs), F32)
    en_r, en_i = z, z
    for s in (reversed(range(S5_SEGMENTS)) if reverse else range(S5_SEGMENTS)):
        en_r = jnp.where(rid == s, c_r, en_r)
        en_i = jnp.where(rid == s, c_i, en_i)
        c_r, c_i = (e_r[s:s + 1] + q_r * c_r - q_i * c_i, e_i[s:s + 1] + q_r * c_i + q_i * c_r)

    def fix(i, carry):
        p_r, p_i = carry
        rows = rows_of(i)
        x_r = x_ref[rows, pl.ds(0, hs)] + (p_r * en_r - p_i * en_i)
        x_i = x_ref[rows, pl.ds(hs, hs)] + (p_r * en_i + p_i * en_r)
        x_ref[rows, pl.ds(0, hs)] = x_r
        x_ref[rows, pl.ds(hs, hs)] = x_i
        if visit is not None:
            visit(steps - 1 - i if reverse else i, rows, x_r, x_i)
        return p_r * b_r - p_i * b_i, p_r * b_i + p_i * b_r

    return lax.fori_loop(0, steps, fix, (b_r, b_i))


def s5_fwd(u_seg, ar, ai, w, cm, *, side=None, name):
    nrow = u_seg.shape[0]
    nb = w.shape[1]
    hs = S5_BLOCK_STATE
    rs = min(_S5_ROWS, nrow)

    def body(u_ref, ar_ref, ai_ref, w_ref, cm_ref, y_ref, x_ref):
        d = pl.program_id(0)
        wb = w_ref[...].astype(MXU_DT)
        for r0 in range(0, nrow, rs):
            x_ref[pl.ds(r0, rs), :] = _dot(u_ref[pl.ds(r0, rs), :].astype(MXU_DT), wb, _NN)
        @pl.when(d == 0)
        def _():
            _segmented_scan(x_ref, ar_ref[...], ai_ref[...], reverse=False)

        @pl.when(d == 1)
        def _():
            _segmented_scan(x_ref, ar_ref[...], ai_ref[...], reverse=True)

        cb = cm_ref[...].astype(MXU_DT)
        for r0 in range(0, nrow, rs):
            y_ref[pl.ds(r0, rs), :] = _dot(x_ref[pl.ds(r0, rs), :].astype(MXU_DT), cb, _NN)

    in_specs = [
        pl.BlockSpec((nrow, LANES), lambda d, s: (0, s)),
        pl.BlockSpec((None, None, 1, hs), lambda d, s: (d, s, 0, 0)),
        pl.BlockSpec((None, None, 1, hs), lambda d, s: (d, s, 0, 0)),
        pl.BlockSpec((None, None, LANES, 2 * hs), lambda d, s: (d, s, 0, 0)),
        pl.BlockSpec((None, None, 2 * hs, LANES), lambda d, s: (d, s, 0, 0)),
    ]
    out_specs = [
        pl.BlockSpec((None, nrow, LANES), lambda d, s: (d, 0, s)),
        pl.BlockSpec((None, nrow, 2 * hs), lambda d, s: (d, 0, s)),
    ]
    out_shape = [jax.ShapeDtypeStruct((2, nrow, nb * LANES), F32),
                 jax.ShapeDtypeStruct((2, nrow, nb * 2 * hs), F32)]
    return _call(body, grid=(2, nb), in_specs=in_specs, out_specs=out_specs, out_shape=out_shape,
                 sem=("parallel", "parallel"), name=name, args=(u_seg, ar, ai, w, cm), side=side)


def s5_bwd(u_seg, ar, ai, w, cm, xs, dy, du_in, *, side=None, name):
    nrow = u_seg.shape[0]
    nb = w.shape[1]
    steps = nrow // S5_SEGMENTS
    hs = S5_BLOCK_STATE
    rs = min(_S5_ROWS, nrow)

    def body(u_ref, dy_ref, dui_ref, x_ref, ar_ref, ai_ref, w_ref, cm_ref,
             du_ref, dw_ref, dcm_ref, dar_ref, dai_ref, g_ref, acc_ref):
        d = pl.program_id(1)
        cb = cm_ref[...].astype(MXU_DT)
        for r0 in range(0, nrow, rs):
            g_ref[pl.ds(r0, rs), :] = _dot(dy_ref[pl.ds(r0, rs), :].astype(MXU_DT), cb, _NT)
        acc_ref[...] = jnp.zeros_like(acc_ref)
        rid = lax.broadcasted_iota(jnp.int32, (S5_SEGMENTS, 2 * hs), 0)

        def run(forward_recurrence):
            last = pl.ds((steps - 1) * S5_SEGMENTS, S5_SEGMENTS)
            first = pl.ds(0, S5_SEGMENTS)
            if forward_recurrence:
                wrap = jnp.where(rid == 0, 0.0, pltpu.roll(x_ref[last, :], 1, 0))
            else:
                wrap = jnp.where(rid == S5_SEGMENTS - 1, 0.0, pltpu.roll(x_ref[first, :], S5_SEGMENTS - 1, 0))

            def visit(step, rows, g_r, g_i):
                if forward_recurrence:
                    nbr = jnp.maximum(step - 1, 0)
                    edge = step == 0
                else:
                    nbr = jnp.minimum(step + 1, steps - 1)
                    edge = step == steps - 1
                prev = x_ref[pl.ds(pl.multiple_of(nbr * S5_SEGMENTS, S5_SEGMENTS), S5_SEGMENTS), :]
                prev = jnp.where(edge, wrap, prev)
                p_r, p_i = prev[:, :hs], prev[:, hs:]
                acc_ref[:, pl.ds(0, hs)] += g_r * p_r + g_i * p_i
                acc_ref[:, pl.ds(hs, hs)] += g_i * p_r - g_r * p_i

            _segmented_scan(g_ref, ar_ref[...], -ai_ref[...], reverse=forward_recurrence, visit=visit)

        @pl.when(d == 0)
        def _():
            run(True)

        @pl.when(d == 1)
        def _():
            run(False)

        dar_ref[...] = jnp.sum(acc_ref[:, pl.ds(0, hs)], axis=0, keepdims=True)
        dai_ref[...] = jnp.sum(acc_ref[:, pl.ds(hs, hs)], axis=0, keepdims=True)

        wb = w_ref[...].astype(MXU_DT)
        for r0 in range(0, nrow, rs):
            part = _dot(g_ref[pl.ds(r0, rs), :].astype(MXU_DT), wb, _NT)

            @pl.when(d == 0)
            def _(part=part, r0=r0):
                du_ref[pl.ds(r0, rs), :] = dui_ref[pl.ds(r0, rs), :] + part

            @pl.when(d == 1)
            def _(part=part, r0=r0):
                du_ref[pl.ds(r0, rs), :] += part

        dw_ref[...] = _dot(u_ref[...].astype(MXU_DT), g_ref[...].astype(MXU_DT), _TN)
        dcm_ref[...] = _dot(x_ref[...].astype(MXU_DT), dy_ref[...].astype(MXU_DT), _TN)

    in_specs = [
        pl.BlockSpec((nrow, LANES), lambda s, d: (0, s)),
        pl.BlockSpec((nrow, LANES), lambda s, d: (0, s)),
        pl.BlockSpec((nrow, LANES), lambda s, d: (0, s)),
        pl.BlockSpec((None, nrow, 2 * hs), lambda s, d: (d, 0, s)),
        pl.BlockSpec((None, None, 1, hs), lambda s, d: (d, s, 0, 0)),
        pl.BlockSpec((None, None, 1, hs), lambda s, d: (d, s, 0, 0)),
        pl.BlockSpec((None, None, LANES, 2 * hs), lambda s, d: (d, s, 0, 0)),
        pl.BlockSpec((None, None, 2 * hs, LANES), lambda s, d: (d, s, 0, 0)),
    ]
    out_specs = [
        pl.BlockSpec((nrow, LANES), lambda s, d: (0, s)),
        pl.BlockSpec((None, None, LANES, 2 * hs), lambda s, d: (d, s, 0, 0)),
        pl.BlockSpec((None, None, 2 * hs, LANES), lambda s, d: (d, s, 0, 0)),
        pl.BlockSpec((None, None, 1, hs), lambda s, d: (d, s, 0, 0)),
        pl.BlockSpec((None, None, 1, hs), lambda s, d: (d, s, 0, 0)),
    ]
    out_shape = [jax.ShapeDtypeStruct((nrow, nb * LANES), F32),
                 jax.ShapeDtypeStruct(w.shape, F32), jax.ShapeDtypeStruct(cm.shape, F32),
                 jax.ShapeDtypeStruct(ar.shape, F32), jax.ShapeDtypeStruct(ai.shape, F32)]
    return _call(body, grid=(nb, 2), in_specs=in_specs, out_specs=out_specs, out_shape=out_shape,
                 scratch_shapes=[pltpu.VMEM((nrow, 2 * hs), F32), pltpu.VMEM((S5_SEGMENTS, 2 * hs), F32)],
                 sem=("parallel", "arbitrary"), name=name, args=(u_seg, dy, du_in, xs, ar, ai, w, cm), side=side)


def _me():
    return lax.axis_index("x"), lax.axis_index("y"), lax.axis_index("c")


def all_gather(shards):
    na = len(shards)

    def plan(x_refs, out_refs, sems):
        send_sems, recv_sems, local_sems = sems
        x, y, c = _me()
        me, sibling = (x, y, c), (x, y, 1 - c)
        chips = [(1 - x, y), (x, 1 - y), (1 - x, 1 - y)]

        def slot(a, px, py, pc):
            return out_refs[a].at[4 * px + 2 * py + pc]

        def copy(a, k, block, to, src=None):
            return pltpu.make_async_remote_copy(
                src_ref=slot(a, *block) if src is None else src, dst_ref=slot(a, *block),
                send_sem=send_sems.at[a, k], recv_sem=recv_sems.at[a, k], device_id=to, device_id_type=MESH)

        mine = [pltpu.make_async_copy(x_refs[a], slot(a, *me), local_sems.at[a]) for a in range(na)]
        first = []
        for a in range(na):
            first.append(copy(a, 0, me, sibling, src=x_refs[a]))
            first += [copy(a, 1 + j, me, (*chip, c), src=x_refs[a]) for j, chip in enumerate(chips)]
        return me, sibling, chips, c, copy, mine, first

    def start(x_refs, out_refs, sems):
        _, _, _, _, _, mine, first = plan(x_refs, out_refs, sems)
        for cp in mine + first:
            cp.start()

    def finish(x_refs, out_refs, sems):
        me, sibling, chips, c, copy, mine, first = plan(x_refs, out_refs, sems)
        passed = []
        for j, chip in enumerate(chips):
            for a in range(na):
                copy(a, 1 + j, (*chip, c), me).wait_recv()
                fwd = copy(a, 4 + j, (*chip, c), sibling)
                fwd.start()
                passed.append(fwd)
        for a in range(na):
            copy(a, 0, sibling, me).wait_recv()
            for j, chip in enumerate(chips):
                copy(a, 4 + j, (*chip, 1 - c), me).wait_recv()
        for cp in first + passed:
            cp.wait_send()
        for cp in mine:
            cp.wait()

    return Side(shards, [jax.ShapeDtypeStruct((8,) + s.shape, s.dtype) for s in shards],
                [pltpu.SemaphoreType.DMA((na, 7)), pltpu.SemaphoreType.DMA((na, 7)), pltpu.SemaphoreType.DMA((na,))],
                start, finish)


_AXES = ("x", "y", "c")


def exchange(bufs, axes, *, half):
    na = len(bufs)
    if isinstance(half, bool):
        half = [half] * na

    def copies(in_refs, out_refs, sems):
        send_sems, recv_sems = sems
        me = _me()
        cps = []
        for a in range(na):
            bit = me[_AXES.index(axes[a])]
            peer = tuple(1 - v if ax == axes[a] else v for ax, v in zip(_AXES, me))
            cps.append(pltpu.make_async_remote_copy(
                src_ref=in_refs[a].at[1 - bit] if half[a] else in_refs[a], dst_ref=out_refs[a],
                send_sem=send_sems.at[a], recv_sem=recv_sems.at[a], device_id=peer, device_id_type=MESH))
        return cps

    def start(*refs):
        for cp in copies(*refs):
            cp.start()

    def finish(*refs):
        for cp in copies(*refs):
            cp.wait()

    return Side(bufs, [jax.ShapeDtypeStruct(b.shape[1:] if h else b.shape, b.dtype) for b, h in zip(bufs, half)],
                [pltpu.SemaphoreType.DMA((na,)), pltpu.SemaphoreType.DMA((na,))], start, finish)


def all_to_all(pieces, slot_fns):
    na = len(pieces)

    def copies(in_refs, out_refs, sems):
        send_sems, recv_sems, local_sems = sems
        x, y, c = _me()
        cps = []
        for a in range(na):
            mine = slot_fns[a](x, y, c)
            for k in range(8):
                tx, ty, tc = x ^ (k // 4), y ^ ((k // 2) % 2), c ^ (k % 2)
                src = in_refs[a].at[4 * tx + 2 * ty + tc]
                dst = out_refs[a].at[mine]
                if k == 0:
                    cps.append(pltpu.make_async_copy(src, dst, local_sems.at[a]))
                else:
                    cps.append(pltpu.make_async_remote_copy(
                        src_ref=src, dst_ref=dst, send_sem=send_sems.at[a, k - 1], recv_sem=recv_sems.at[a, k - 1],
                        device_id=(tx, ty, tc), device_id_type=MESH))
        return cps

    def start(*refs):
        for cp in copies(*refs):
            cp.start()

    def finish(*refs):
        for cp in copies(*refs):
            cp.wait()

    return Side(pieces, [jax.ShapeDtypeStruct(p.shape, p.dtype) for p in pieces],
                [pltpu.SemaphoreType.DMA((na, 7)), pltpu.SemaphoreType.DMA((na, 7)), pltpu.SemaphoreType.DMA((na,))],
                start, finish)


def add_half(buf, recv, bit, *, narrow, name):
    c = recv.shape[-1]
    r = math.prod(recv.shape[:-1])
    t = _tile(r, max(16, (ADD_BLOCK_BYTES // (4 * c)) // 16 * 16), 16)

    def body(bit_ref, a_ref, b_ref, o_ref, *tx_ref):
        s = a_ref[...] + b_ref[...].astype(F32)
        o_ref[...] = s
        if narrow:
            tx_ref[0][...] = s.astype(WIRE_DT)

    o_spec = pl.BlockSpec((t, c), lambda i, b: (i, 0))
    grid_spec = pltpu.PrefetchScalarGridSpec(
        num_scalar_prefetch=1, grid=(r // t,),
        in_specs=[pl.BlockSpec((None, t, c), lambda i, b: (b[0], i, 0)), o_spec],
        out_specs=[o_spec, o_spec] if narrow else [o_spec])
    out_shape = [jax.ShapeDtypeStruct((r, c), F32)] + ([jax.ShapeDtypeStruct((r, c), WIRE_DT)] if narrow else [])
    outs = pl.pallas_call(body, grid_spec=grid_spec, out_shape=out_shape,
                          compiler_params=_params(("parallel",)), name=name)(
                              bit, buf.reshape(2, r, c), recv.reshape(r, c))
    return [o.reshape(recv.shape) for o in outs]


def _pack_rows(n, cols):
    return -(-n // (8 * cols)) * 8


def _pack(arrs, cols, mult):
    parts = []
    for a in arrs:
        n = math.prod(a.shape)
        nr = _pack_rows(n, cols)
        parts.append(jnp.pad(a.reshape(-1), (0, nr * cols - n)).reshape(nr, cols))
    total = sum(p.shape[0] for p in parts)
    pad_rows = -(-total // mult) * mult - total
    if pad_rows:
        parts.append(jnp.zeros((pad_rows, cols), arrs[0].dtype))
    return jnp.concatenate(parts, axis=0)


def _unpack(buf, shapes):
    out, r = [], 0
    cols = buf.shape[1]
    for s in shapes:
        n = math.prod(s)
        nr = _pack_rows(n, cols)
        piece = lax.optimization_barrier(buf[r:r + nr])
        out.append(piece.reshape(-1)[:n].reshape(s))
        r += nr
    return out


def _elementwise(fn, arrs, nout, name):
    r, c = arrs[0].shape
    lanes = -(-c // LANES) * LANES
    t = _tile(r, max(ROW_TILE, (ELEMENTWISE_BLOCK_BYTES // (4 * lanes)) // 8 * 8), 8)
    return rw_fwd(fn, [rows(a, t) for a in arrs], [(c, False)] * nout, nrow=r, t=t, name=name)


def kernel(x, ln_g, w_in, conv_w, a_log, dt_bias, head_norm_g, lam_re, lam_im, log_dt, b_re, b_im, c_re, c_im, d_skip, w_glu, b_glu, w_pa, w_pb, b_gate, w_out, final_g, loss_target, m_ln_g, m_w_in, m_conv_w, m_a_log, m_dt_bias, m_head_norm_g, m_lam_re, m_lam_im, m_log_dt, m_b_re, m_b_im, m_c_re, m_c_im, m_d_skip, m_w_glu, m_b_glu, m_w_pa, m_w_pb, m_b_gate, m_w_out, m_final_g, v_ln_g, v_w_in, v_conv_w, v_a_log, v_dt_bias, v_head_norm_g, v_lam_re, v_lam_im, v_log_dt, v_b_re, v_b_im, v_c_re, v_c_im, v_d_skip, v_w_glu, v_b_glu, v_w_pa, v_w_pb, v_b_gate, v_w_out, v_final_g):
    env = dict(locals())
    wts = {n: env[n] for n in WEIGHTS}
    mom_m = {n: env["m_" + n] for n in WEIGHTS}
    mom_v = {n: env["v_" + n] for n in WEIGHTS}

    xin = x[0]
    tgt = loss_target[0]
    nrow, dm = xin.shape
    depth = ln_g.shape[0]
    nh = dm // (2 * HEAD_DIM)
    wa = nh * HEAD_DIM
    wb = dm // 2
    ngrp = wb // S5_GROUP_CH
    pw = w_in.shape[-1]
    t = min(ROW_TILE, nrow)

    o_ba = 4 * wa
    o_u = o_ba + 4 * nh
    n_main = 4 * wa + 2 * wb + 2 * dm
    projp = -(-(n_main + LANES) // 512) * 512
    blk_za = 3 * wa // HEAD_DIM
    blk_u = 4 * wa // LANES
    ba_blk = n_main // LANES
    cx, cy, cc = _me()

    jb = o_ba // pw
    assert (o_u - 1) // pw == jb
    cut_lo, cut = o_ba - jb * pw, o_u - o_ba
    wide = -(-pw // LANES) * LANES + LANES
    col = lax.broadcasted_iota(jnp.int32, (pw, wide), 1)
    row = lax.broadcasted_iota(jnp.int32, (pw, wide), 0)
    holds_cut = (4 * cx + 2 * cy + cc) == jb
    src_plain = jnp.where(col < pw, col, -1)
    src_cut = jnp.where(col < cut_lo, col, jnp.where(col < pw - cut, col + cut, -1))
    src_cut = jnp.where((col >= wide - LANES) & (col < wide - LANES + cut), col - (wide - LANES) + cut_lo, src_cut)
    select = (row == jnp.where(holds_cut, src_cut, src_plain)).astype(WIRE_DT)
    w_in_tx = mm(w_in.reshape(depth * dm, pw), select, name="w_in_prepare").astype(WIRE_DT).reshape(depth, dm, wide)

    split = (2 * wide // 3) // LANES * LANES

    def gather_parts(l):
        return ([w_in_tx[l][:, :split]],
                [w_in_tx[l][:, split:], w_out[l].astype(WIRE_DT)],
                [w_glu[l].astype(WIRE_DT), w_pa[l].astype(WIRE_DT), w_pb[l].astype(WIRE_DT), conv_w[l]])

    def cat(g):
        return jnp.concatenate([g[j] for j in range(8)], axis=1)

    def assemble(part_a, part_b, part_c):
        (g_a,), (g_b, g_out), (g_glu, g_pa, g_pb, g_conv) = part_a, part_b, part_c
        pieces = []
        for j in range(8):
            valid = pw - cut if j == jb else pw
            pieces += [g_a[j], g_b[j][:, :valid - split]]
        w_perm = jnp.concatenate(
            pieces + [g_b[jb][:, wide - LANES - split:], jnp.zeros((dm, projp - n_main - LANES), WIRE_DT)], axis=1)
        return dict(w_in=w_perm, w_glu=g_glu.reshape(wb, wb), w_pa=cat(g_pa), w_pb=cat(g_pb),
                    w_out=g_out.reshape(dm, dm), conv_w=cat(g_conv))

    parts0 = gather_parts(0)
    first = run_side(all_gather(parts0[0] + parts0[1] + parts0[2]), name="gather_weights")
    full = [assemble(first[:1], first[1:3], first[3:])]

    def small(l):
        z = jnp.zeros((1, LANES - 4 * nh), F32)
        alog_row = jnp.concatenate([jnp.zeros((1, 2 * nh), F32), a_log[l].reshape(1, 2 * nh), z], axis=1)
        dtb_row = jnp.concatenate([jnp.zeros((1, 2 * nh), F32), dt_bias[l].reshape(1, 2 * nh), z], axis=1)
        return alog_row, dtb_row

    saved = []
    cur = xin
    for l in range(depth):
        fw = full[l]
        (hh,) = rw_fwd(fn_rms, [rows(cur, t), bcast(ln_g[l][None])], [(dm, False)], nrow=nrow, t=t, name="rms_fwd")
        nxt = [all_gather(p) for p in gather_parts(l + 1)] if l + 1 < depth else [None] * 3
        proj = mm(hh, fw["w_in"], side=nxt[2], name="proj_fwd")
        (qkv,) = rw_fwd(make_fn_prep(nh), [rows(proj, nrow, HEAD_DIM, 0, True), bcast(fw["conv_w"], HEAD_DIM, 0, True)],
                        [(HEAD_DIM, True)], nrow=nrow, t=nrow, ncol=3 * nh, name="prep_fwd")
        alog_row, dtb_row = small(l)
        o_dir, s_hist, t_hist = gdn_fwd(qkv, proj, ba_blk, alog_row, dtb_row, n_heads=nh, side=nxt[0],
                                        name="gdn_fwd")
        hn_ins = [rows3(o_dir, 0, t, HEAD_DIM), rows3(o_dir, 1, t, HEAD_DIM), rows(proj, t, HEAD_DIM, blk_za, True),
                  bcast(head_norm_g[l][None])]
        (ya_in,) = rw_fwd(fn_headnorm, hn_ins, [(HEAD_DIM, True)], nrow=nrow, t=t, ncol=nh, name="headnorm_fwd")
        mats, mats_vjp = jax.vjp(s5_mats, lam_re[l], lam_im[l], log_dt[l], b_re[l], b_im[l], c_re[l], c_im[l])
        u_seg = segment_major(proj[:, 4 * wa:4 * wa + wb])
        yd, xs = s5_fwd(u_seg, *mats, side=nxt[1], name="s5_fwd")
        if nxt[0] is not None:
            full.append(assemble(nxt[0].result, nxt[1].result, nxt[2].result))
        ge_ins = [rows3(yd, 0, t, wb, False), rows3(yd, 1, t, wb, False), rows(u_seg, t), bcast(d_skip[l][None])]
        (ys_seg,) = rw_fwd(fn_gelu, ge_ins, [(wb, False)], nrow=nrow, t=t, name="gelu_fwd")
        ys = time_major(ys_seg)
        glu = mm(ys, fw["w_glu"], name="glu_fwd")
        gl_ins = [rows(ys, t), rows(glu, t), rows(proj, t, wb, (4 * wa + wb) // wb), bcast(b_glu[l][None])]
        (yb_in,) = rw_fwd(fn_glu, gl_ins, [(wb, False)], nrow=nrow, t=t, name="glugate_fwd")
        y_a = mm(ya_in, fw["w_pa"], name="pa_fwd")
        y_b = mm(yb_in, fw["w_pb"], name="pb_fwd")
        bg = b_gate[l][None]
        mg_ins = [rows(proj, t, dm, 3), rows(proj, t, dm, 4), rows(y_a, t), rows(y_b, t),
                  bcast(bg, dm, 0), bcast(bg, dm, 1)]
        (merged,) = rw_fwd(fn_merge, mg_ins, [(dm, False)], nrow=nrow, t=t, name="merge_fwd")
        nxt = mm(merged, fw["w_out"], add=cur, name="out_fwd")
        saved.append(dict(x=cur, h=hh, proj=proj, qkv=qkv, o_dir=o_dir, s_hist=s_hist, t_hist=t_hist, hn_ins=hn_ins,
                          mats=mats, mats_vjp=mats_vjp, xs=xs, ge_ins=ge_ins, ys=ys, gl_ins=gl_ins, u_seg=u_seg,
                          ya_in=ya_in, yb_in=yb_in, mg_ins=mg_ins, merged=merged,
                          alog_row=alog_row, dtb_row=dtb_row))
        cur = nxt

    loss_ins = [rows(cur, t), rows(tgt, t), bcast(final_g[None])]
    (row_loss,) = rw_fwd(fn_loss, loss_ins, [(1, False)], nrow=nrow, t=t, name="loss_fwd")
    ones = jnp.ones((nrow, 1), F32)
    dcur, dfinal = rw_bwd(fn_loss, loss_ins, [[rows(ones, t)]], nrow=nrow, t=t,
                          row_grads=[(0, False)], bc_grads=[2], name="loss_bwd")
    loss = lax.psum(jnp.sum(row_loss), ("x", "y", "c"))

    coord = {"x": cx, "y": cy, "c": cc}
    routes = {"cxy": ("c", "x", "y"), "cyx": ("c", "y", "x")}
    route = ["cxy", "cyx", "cxy", "cxy", "cyx", "cyx", "cxy"]
    conv_order = [4 * ((q // 2) % 2) + 2 * (q % 2) + q // 4 for q in range(8)]
    cw = 3 * wa // 8
    g_final = {n: [None] * depth for n in SHARDED}
    gsh = {n: [None] * depth for n in SHARDED}

    def own_columns(z):
        un = jnp.concatenate([z[:, :o_ba], z[:, n_main:n_main + 4 * nh], z[:, o_ba:n_main]], axis=1)
        return jnp.stack([un[:, d * pw:(d + 1) * pw] for d in range(8)])

    def rs_begin(l):
        in_a, in_b = gsh["w_in"][l]
        conv_buf = jnp.stack([gsh["conv_w"][l][:, d * cw:(d + 1) * cw] for d in conv_order])
        return dict(layer=l, stage=0, bufs=[
            in_a.reshape(8, dm // 16, projp), in_b.reshape(8, dm // 16, projp), gsh["w_out"][l],
            gsh["w_glu"][l], gsh["w_pa"][l], gsh["w_pb"][l], conv_buf])

    def rs_side(st):
        if st is None:
            return None
        if st["stage"] < 3:
            st["axes"] = [routes[r][st["stage"]] for r in route]

            def halves(b):
                return b.reshape((2, b.shape[0] // 2) + b.shape[1:])

            st["bufs"] = [halves(b) for b in st["bufs"]]
            send = [halves(b) for b in st["tx"]] if st.get("tx") else st["bufs"]
            if st.get("whole") is not None:
                return exchange(send + st["whole"],
                                st["axes"] + [routes["cxy"][st["stage"]], routes["cyx"][st["stage"]]],
                                half=[True] * len(send) + [False, False])
            return exchange(send, st["axes"], half=True)
        st["bufs"] = [b.reshape(b.shape[1:]) for b in st["bufs"]]
        return all_to_all([own_columns(st["bufs"][0]), own_columns(st["bufs"][1])],
                          [lambda x, y, c: 4 * c + 2 * x + y, lambda x, y, c: 4 * c + 2 * y + x])

    def rs_absorb(st, side):
        if st is None:
            return
        if st["stage"] < 3:
            if st.get("whole") is not None:
                st["whole"] = [_elementwise(make_fn_sum(2), [w, r], 1, "ar_add")[0]
                               for w, r in zip(st["whole"], side.result[-2:])]
            narrow = st["stage"] < 2
            sums = [add_half(b, r, coord[ax].astype(jnp.int32).reshape(1), narrow=narrow, name=f"rs_add_{st['stage']}")
                    for b, r, ax in zip(st["bufs"], side.result, st["axes"])]
            st["bufs"] = [s[0] for s in sums]
            st["tx"] = [s[1] for s in sums] if narrow else None
        else:
            got_a, got_b = side.result
            g_final["w_in"][st["layer"]] = jnp.concatenate(
                [got_a.reshape(dm // 2, pw), got_b.reshape(dm // 2, pw)], axis=0)
            for n, b in zip(["w_out", "w_glu", "w_pa", "w_pb", "conv_w"], st["bufs"][2:]):
                g_final[n][st["layer"]] = b
        st["stage"] += 1

    grep = {n: [None] * depth for n in REPLICATED if n != "final_g"}
    pending = None
    for l in reversed(range(depth)):
        fw, sv = full[l], saved[l]
        dmerged = mm(dcur, fw["w_out"], tb=True, name="out_bwd_x")
        gsh["w_out"][l] = mm(sv["merged"], dcur, ta=True, scatter=("rows", slot_cxy), name="out_bwd_w")
        side = rs_side(pending)
        dla, dlb, dya, dyb, dbga, dbgb = rw_bwd(
            fn_merge, sv["mg_ins"], [[rows(dmerged, t)]], nrow=nrow, t=t,
            row_grads=[(0, False), (1, False), (2, False), (3, False)], bc_grads=[4, 5], side=side, name="merge_bwd")
        rs_absorb(pending, side)
        grep["b_gate"][l] = jnp.concatenate([dbga.reshape(dm), dbgb.reshape(dm)])
        dya_in = mm(dya, fw["w_pa"], tb=True, name="pa_bwd_x")
        gsh["w_pa"][l] = mm(sv["ya_in"], dya, ta=True, scatter=("cols", slot_cyx), name="pa_bwd_w")
        dyb_in = mm(dyb, fw["w_pb"], tb=True, name="pb_bwd_x")
        gsh["w_pb"][l] = mm(sv["yb_in"], dyb, ta=True, scatter=("cols", slot_cyx), name="pb_bwd_w")
        dys1, dglu, dzb, dbglu = rw_bwd(
            fn_glu, sv["gl_ins"], [[rows(dyb_in, t)]], nrow=nrow, t=t,
            row_grads=[(0, False), (1, False), (2, False)], bc_grads=[3], name="glugate_bwd")
        grep["b_glu"][l] = dbglu.reshape(wb)
        dys2 = mm(dglu, fw["w_glu"], tb=True, name="glu_bwd_x")
        gsh["w_glu"][l] = mm(sv["ys"], dglu, ta=True, scatter=("rows", slot_cxy), name="glu_bwd_w")
        dyd, du1, ddskip = rw_bwd(
            fn_gelu, sv["ge_ins"], [[rows(segment_major(dys1), t), rows(segment_major(dys2), t)]], nrow=nrow, t=t,
            row_grads=[(0, False), (2, False)], bc_grads=[3], name="gelu_bwd")
        grep["d_skip"][l] = ddskip.reshape(wb)
        side = rs_side(pending)
        du_seg, dw_s5, dcm_s5, dar, dai = s5_bwd(sv["u_seg"], *sv["mats"], sv["xs"], dyd, du1, side=side,
                                                 name="s5_bwd")
        du = time_major(du_seg)
        rs_absorb(pending, side)
        g_lr, g_li, g_ldt, g_br, g_bi, g_cr, g_ci = sv["mats_vjp"]((dar, dai, dw_s5, dcm_s5))
        for nme, val in zip(["lam_re", "lam_im", "log_dt", "b_re", "b_im", "c_re", "c_im"],
                            [g_lr, g_li, g_ldt, g_br, g_bi, g_cr, g_ci]):
            grep[nme][l] = val
        do, dza, dhn = rw_bwd(fn_headnorm, sv["hn_ins"], [[rows(dya_in, t, HEAD_DIM, 0, True)]], nrow=nrow, t=t,
                              ncol=nh, row_grads=[(0, True), (2, True)], bc_grads=[3], name="headnorm_bwd")
        grep["head_norm_g"][l] = jnp.sum(dhn, axis=0).reshape(HEAD_DIM)
        side = rs_side(pending)
        dqkv, dba_all, dal, ddt = gdn_bwd(sv["qkv"], sv["proj"], ba_blk, sv["alog_row"], sv["dtb_row"],
                                          sv["s_hist"], sv["t_hist"], do, n_heads=nh, side=side, name="gdn_bwd")
        rs_absorb(pending, side)
        grep["a_log"][l] = jnp.sum(dal, axis=(0, 1))[2 * nh:4 * nh].reshape(2, nh)
        grep["dt_bias"][l] = jnp.sum(ddt, axis=(0, 1))[2 * nh:4 * nh].reshape(2, nh)
        (dba,) = rw_fwd(make_fn_sum(2), [rows3(dba_all, p, t, LANES, False) for p in range(2)],
                        [(LANES, False)], nrow=nrow, t=t, name="dba_sum")
        prep_ins = [rows(sv["proj"], nrow, HEAD_DIM, 0, True), bcast(fw["conv_w"], HEAD_DIM, 0, True)]
        dq_cots = [(dqkv.reshape(2, 3 * nrow, wa), (None, nrow, HEAD_DIM),
                    (lambda j, i, dd=dd: (dd, j // nh, j % nh))) for dd in range(2)]
        dqkv_raw, dconv = rw_bwd(make_fn_prep(nh), prep_ins, [dq_cots], nrow=nrow, t=nrow, ncol=3 * nh,
                                 row_grads=[(0, True)], bc_grads=[1], name="prep_bwd")
        gsh["conv_w"][l] = jnp.transpose(dconv, (1, 0, 2)).reshape(CONV_K, 3 * wa)
        dproj = jnp.concatenate([dqkv_raw, dza, du, dzb, dla, dlb, dba,
                                 jnp.zeros((nrow, projp - n_main - LANES), F32)], axis=1).astype(MXU_DT)
        side = rs_side(pending)
        dh = mm(dproj, fw["w_in"], tb=True, side=side, name="proj_bwd_x")
        rs_absorb(pending, side)
        gsh["w_in"][l] = [mm(sv["h"], dproj, ta=True, m_part=(part, 2), name="proj_bwd_w") for part in range(2)]
        dcur, dlng = rw_bwd(fn_rms, [rows(sv["x"], t), bcast(ln_g[l][None])], [[rows(dh, t)]], nrow=nrow, t=t,
                            row_grads=[(0, False)], bc_grads=[1], residual=rows(dcur, t), name="rms_bwd")
        grep["ln_g"][l] = dlng.reshape(dm)
        pending = rs_begin(l)
    grad_x = dcur[None]
    rep_list = [jnp.stack(grep[n]) for n in REPLICATED if n != "final_g"] + [dfinal.reshape(dm)]
    rep_shapes = [a.shape for a in rep_list]
    rep_buf = _pack(rep_list, COMM_COLS, 2 * ROW_TILE)
    pending["whole"] = [rep_buf[:rep_buf.shape[0] // 2], rep_buf[rep_buf.shape[0] // 2:]]
    for stage in range(RS_STAGES):
        side = rs_side(pending)
        run_side(side, name=f"rs_stage_{stage}")
        rs_absorb(pending, side)
    grads = {n: jnp.stack(g_final[n]) for n in SHARDED}
    for n, val in zip(REPLICATED, _unpack(jnp.concatenate(pending["whole"], axis=0), rep_shapes)):
        grads[n] = val

    deltas, new_m, new_v = {}, {}, {}
    for n in WEIGHTS:
        shp = wts[n].shape
        two = [a.reshape(-1, shp[-1]) for a in (wts[n], grads[n], mom_m[n], mom_v[n])]
        d_, m_, v_ = _elementwise(fn_adamw, two, 3, "adamw_" + n)
        deltas[n], new_m[n], new_v[n] = d_.reshape(shp), m_.reshape(shp), v_.reshape(shp)

    return (loss, grad_x, *[grads[n] for n in WEIGHTS], *[deltas[n] for n in WEIGHTS],
            *[new_m[n] for n in WEIGHTS], *[new_v[n] for n in WEIGHTS])
```
